```python
import math
import jax, jax.numpy as jnp
from jax import lax
import numpy as np

D_MODEL = 1024
BATCH = 8
SEQ = 4096
DEPTH = 4

GDN_HEADS = 4
GDN_HEAD_DIM = 128
CONV_WIDTH = 4
GDN_CHUNK = 64
DIFF_HEADS = 4
DIFF_QK_DIM = 64
DIFF_V_DIM = 2 * DIFF_QK_DIM
FOX_HEADS = 8
FOX_HEAD_DIM = D_MODEL // FOX_HEADS
D_FF = 4 * D_MODEL
PLE_DIM = 256
ROPE_THETA = 10000.0
Q_BLOCK = 128
EPS = 1e-6
NEG_INF = -1e30

N_EVEN = (DEPTH + 1) // 2
N_ODD = DEPTH // 2

GDN_QK = GDN_HEADS * GDN_HEAD_DIM
GDN_V = GDN_HEADS * GDN_HEAD_DIM
DIFF_Q = DIFF_HEADS * 2 * DIFF_QK_DIM
DIFF_V = DIFF_HEADS * DIFF_V_DIM
EVEN_SPLITS = [3 * GDN_QK, GDN_V, GDN_HEADS, GDN_HEADS, DIFF_Q, DIFF_Q, DIFF_V]
EVEN_IN = sum(EVEN_SPLITS)
EVEN_MIX = GDN_V + DIFF_V
ODD_MIX = FOX_HEADS * FOX_HEAD_DIM
ODD_SPLITS = [ODD_MIX, ODD_MIX, ODD_MIX, ODD_MIX, FOX_HEADS]
ODD_IN = sum(ODD_SPLITS)

kernel_name = 'hybrid_gdn_diffattn_fox_trunk'


def split_cols(t, sizes):
    offs = np.concatenate([[0], np.cumsum(sizes)]).tolist()
    return [t[..., offs[i]:offs[i + 1]] for i in range(len(sizes))]


def rms_norm(x, w):
    xf = x.astype(jnp.float32)
    y = xf * lax.rsqrt(jnp.mean(xf * xf, axis=-1, keepdims=True) + EPS)
    return (y * w.astype(jnp.float32)).astype(x.dtype)


def l2_norm(x):
    return x * lax.rsqrt(jnp.sum(x * x, axis=-1, keepdims=True) + EPS)


def rope_tables(positions, dim):
    inv_freq = ROPE_THETA ** (-jnp.arange(0, dim, 2, dtype=jnp.float32) / dim)
    ang = positions.astype(jnp.float32)[..., None] * inv_freq
    return jnp.cos(ang), jnp.sin(ang)


def apply_rope(x, cos, sin):
    shape = cos.shape[:2] + (1,) * (x.ndim - 3) + cos.shape[2:]
    c, s = cos.reshape(shape), sin.reshape(shape)
    x1, x2 = jnp.split(x.astype(jnp.float32), 2, axis=-1)
    return jnp.concatenate([x1 * c - x2 * s, x2 * c + x1 * s], axis=-1)


def causal_conv(x, w):
    K = w.shape[0]
    T = x.shape[1]
    xp = jnp.pad(x, ((0, 0), (K - 1, 0), (0, 0)))
    return sum(xp[:, i:i + T] * w[i] for i in range(K))


def causal_mask(start, T):
    return (start + jnp.arange(Q_BLOCK))[:, None] >= jnp.arange(T)[None, :]


def block_starts(T):
    return jnp.arange(T // Q_BLOCK) * Q_BLOCK


def gated_delta_rule(q, k, v, g, beta):
    B, H, T, dk = q.shape
    dv = v.shape[-1]
    C = GDN_CHUNK
    N = T // C
    chunk = lambda t: t.reshape(B, H, N, C, *t.shape[3:])
    q = chunk(q * dk ** -0.5)
    k = chunk(k)
    v = chunk(v)
    beta = chunk(beta)
    gc = jnp.cumsum(chunk(g), axis=-1)
    incl = jnp.tril(jnp.ones((C, C), dtype=bool))
    strict = jnp.tril(jnp.ones((C, C), dtype=bool), -1)
    decay = jnp.where(incl, jnp.exp(jnp.where(incl, gc[..., :, None] - gc[..., None, :], 0.0)), 0.0)
    kb = k * beta[..., None]
    kk = jnp.einsum('bhncd,bhnsd->bhncs', kb, k) * decay
    a_mat = jnp.where(strict, kk, 0.0) + jnp.eye(C, dtype=kk.dtype)
    rhs = jnp.concatenate([v * beta[..., None], kb * jnp.exp(gc)[..., None]], axis=-1)
    sol = lax.linalg.triangular_solve(a_mat, rhs, left_side=True, lower=True, unit_diagonal=True)
    u, w = sol[..., :dv], sol[..., dv:]
    qk = jnp.where(incl, jnp.einsum('bhncd,bhnsd->bhncs', q, k) * decay, 0.0)
    q_dec = q * jnp.exp(gc)[..., None]
    k_dec = k * jnp.exp(gc[..., -1:] - gc)[..., None]
    g_last = jnp.exp(gc[..., -1])

    def step(S, xs):
        qd, kd, uc, wc, qkc, gl = xs
        v_new = uc - jnp.einsum('bhcd,bhde->bhce', wc, S)
        o = jnp.einsum('bhcd,bhde->bhce', qd, S) + jnp.einsum('bhcs,bhse->bhce', qkc, v_new)
        S = S * gl[..., None, None] + jnp.einsum('bhcd,bhce->bhde', kd, v_new)
        return S, o

    xs = tuple(jnp.moveaxis(t, 2, 0) for t in (q_dec, k_dec, u, w, qk, g_last))
    S0 = jnp.zeros((B, H, dk, dv), jnp.float32)
    _, o = lax.scan(step, S0, xs)
    return jnp.moveaxis(o, 0, 2).reshape(B, H, T, dv)


def diff_attention(q, k, v, lam):
    B, T, H, _, d = q.shape
    dv = v.shape[-1]
    qh = q.transpose(0, 2, 3, 1, 4).astype(jnp.float32) * d ** -0.5
    kh = k.transpose(0, 2, 3, 1, 4).astype(jnp.float32)
    vh = v.transpose(0, 2, 1, 3).astype(jnp.float32)

    def block(start):
        qb = lax.dynamic_slice_in_dim(qh, start, Q_BLOCK, axis=3)
        s = jnp.einsum('bhmqd,bhmkd->bhmqk', qb, kh)
        a = jax.nn.softmax(jnp.where(causal_mask(start, T), s, NEG_INF), axis=-1)
        a = a[:, :, 0] - lam * a[:, :, 1]
        return jnp.einsum('bhqk,bhkd->bhqd', a, vh)

    o = lax.map(block, block_starts(T))
    return o.transpose(1, 0, 3, 2, 4).reshape(B, T, H, dv)


def even_mixer(h, cos, sin, w_in, conv_w, a_log, dt_bias, gdn_norm_w,
               lam_q1, lam_k1, lam_q2, lam_k2, diff_norm_w, w_out, lambda_init):
    B, T, _ = h.shape
    qkv_a, z_a, b_a, a_a, q_b, k_b, v_b = split_cols(h @ w_in, EVEN_SPLITS)
    qkv_a = jax.nn.silu(causal_conv(qkv_a, conv_w))
    q_a, k_a, v_a = split_cols(qkv_a, [GDN_QK, GDN_QK, GDN_V])
    to_heads = lambda t: t.reshape(B, T, GDN_HEADS, GDN_HEAD_DIM).transpose(0, 2, 1, 3).astype(jnp.float32)
    q_a = l2_norm(to_heads(q_a))
    k_a = l2_norm(to_heads(k_a))
    v_a = to_heads(v_a)
    beta = jax.nn.sigmoid(b_a.astype(jnp.float32)).transpose(0, 2, 1)
    g = (-jnp.exp(a_log.astype(jnp.float32))
         * jax.nn.softplus(a_a.astype(jnp.float32) + dt_bias.astype(jnp.float32))).transpose(0, 2, 1)
    o_a = gated_delta_rule(q_a, k_a, v_a, g, beta).transpose(0, 2, 1, 3)
    z = z_a.reshape(B, T, GDN_HEADS, GDN_HEAD_DIM).astype(jnp.float32)
    o_a = rms_norm(o_a, gdn_norm_w) * jax.nn.silu(z)
    q_b = apply_rope(q_b.reshape(B, T, DIFF_HEADS, 2, DIFF_QK_DIM), cos, sin)
    k_b = apply_rope(k_b.reshape(B, T, DIFF_HEADS, 2, DIFF_QK_DIM), cos, sin)
    v_b = v_b.reshape(B, T, DIFF_HEADS, DIFF_V_DIM)
    lam = (jnp.exp(jnp.sum(lam_q1.astype(jnp.float32) * lam_k1.astype(jnp.float32)))
           - jnp.exp(jnp.sum(lam_q2.astype(jnp.float32) * lam_k2.astype(jnp.float32))) + lambda_init)
    o_b = diff_attention(q_b, k_b, v_b, lam)
    o_b = rms_norm(o_b, diff_norm_w) * (1.0 - lambda_init)
    o = jnp.concatenate([o_a.reshape(B, T, GDN_V), o_b.reshape(B, T, DIFF_V)], axis=-1)
    return o.astype(h.dtype) @ w_out


def fox_mixer(h, w_in, b_forget, w_out):
    B, T, _ = h.shape
    q, k, v, gate, f = split_cols(h @ w_in, ODD_SPLITS)
    heads = lambda t: t.reshape(B, T, FOX_HEADS, FOX_HEAD_DIM).transpose(0, 2, 1, 3).astype(jnp.float32)
    qh = heads(q) * FOX_HEAD_DIM ** -0.5
    kh = heads(k)
    vh = heads(v)
    log_f = jax.nn.log_sigmoid(f.astype(jnp.float32) + b_forget.astype(jnp.float32))
    cum = jnp.cumsum(log_f, axis=1).transpose(0, 2, 1)

    def block(start):
        qb = lax.dynamic_slice_in_dim(qh, start, Q_BLOCK, axis=2)
        cb = lax.dynamic_slice_in_dim(cum, start, Q_BLOCK, axis=2)
        s = jnp.einsum('bhqd,bhkd->bhqk', qb, kh) + cb[..., :, None] - cum[:, :, None, :]
        a = jax.nn.softmax(jnp.where(causal_mask(start, T), s, NEG_INF), axis=-1)
        return jnp.einsum('bhqk,bhkd->bhqd', a, vh)

    o = lax.map(block, block_starts(T))
    o = o.transpose(1, 0, 3, 2, 4).reshape(B, T, ODD_MIX)
    o = o * jax.nn.sigmoid(gate.astype(jnp.float32))
    return o.astype(h.dtype) @ w_out


def setup_inputs(seed: int = 0) -> dict:
    key = jax.random.key(seed)
    ks = jax.random.split(key, 32)
    nrm = lambda k, shape, scale: jax.random.normal(k, shape, jnp.float32) * scale
    gain = lambda k, shape: 1.0 + 0.02 * jax.random.normal(k, shape, jnp.float32)
    dt = jnp.exp(jax.random.uniform(ks[7], (N_EVEN, GDN_HEADS), jnp.float32,
                                    math.log(0.001), math.log(0.1)))
    res_scale = (2.0 * DEPTH) ** -0.5
    return {
        'x': nrm(ks[0], (BATCH, SEQ, D_MODEL), 1.0),
        'p': nrm(ks[1], (DEPTH, BATCH, SEQ, PLE_DIM), 1.0),
        'positions': jnp.broadcast_to(jnp.arange(SEQ, dtype=jnp.int32), (BATCH, SEQ)),
        'norm_mix': gain(ks[2], (DEPTH, D_MODEL)),
        'norm_mlp': gain(ks[3], (DEPTH, D_MODEL)),
        'norm_final': gain(ks[4], (D_MODEL,)),
        'w_in_even': nrm(ks[5], (N_EVEN, D_MODEL, EVEN_IN), D_MODEL ** -0.5),
        'conv_w': nrm(ks[6], (N_EVEN, CONV_WIDTH, 3 * GDN_QK), CONV_WIDTH ** -0.5),
        'a_log': jnp.log(jax.random.uniform(ks[8], (N_EVEN, GDN_HEADS), jnp.float32, 1.0, 16.0)),
        'dt_bias': dt + jnp.log(-jnp.expm1(-dt)),
        'gdn_norm': gain(ks[9], (N_EVEN, GDN_HEAD_DIM)),
        'lam_q1': nrm(ks[10], (N_EVEN, DIFF_QK_DIM), 0.1),
        'lam_k1': nrm(ks[11], (N_EVEN, DIFF_QK_DIM), 0.1),
        'lam_q2': nrm(ks[12], (N_EVEN, DIFF_QK_DIM), 0.1),
        'lam_k2': nrm(ks[13], (N_EVEN, DIFF_QK_DIM), 0.1),
        'diff_norm': gain(ks[14], (N_EVEN, DIFF_V_DIM)),
        'w_out_even': nrm(ks[15], (N_EVEN, EVEN_MIX, D_MODEL), EVEN_MIX ** -0.5 * res_scale),
        'w_in_odd': nrm(ks[16], (N_ODD, D_MODEL, ODD_IN), D_MODEL ** -0.5),
        'b_forget': jax.random.uniform(ks[17], (N_ODD, FOX_HEADS), jnp.float32, 1.0, 4.0),
        'w_out_odd': nrm(ks[18], (N_ODD, ODD_MIX, D_MODEL), ODD_MIX ** -0.5 * res_scale),
        'w_mlp_up': nrm(ks[19], (DEPTH, D_MODEL, D_FF), D_MODEL ** -0.5),
        'w_mlp_down': nrm(ks[20], (DEPTH, D_FF, D_MODEL), D_FF ** -0.5 * res_scale),
        'w_ple_proj': nrm(ks[21], (DEPTH, PLE_DIM, D_MODEL), PLE_DIM ** -0.5 * res_scale),
        'w_ple_gate': nrm(ks[22], (DEPTH, D_MODEL, D_MODEL), D_MODEL ** -0.5),
    }


def reference(x, p, positions, norm_mix, norm_mlp, norm_final, w_in_even, conv_w, a_log, dt_bias,
              gdn_norm, lam_q1, lam_k1, lam_q2, lam_k2, diff_norm, w_out_even, w_in_odd, b_forget,
              w_out_odd, w_mlp_up, w_mlp_down, w_ple_proj, w_ple_gate):
    cos, sin = rope_tables(positions, DIFF_QK_DIM)
    for i in range(DEPTH):
        j = i // 2
        h = rms_norm(x, norm_mix[i])
        if i % 2 == 0:
            lambda_init = 0.8 - 0.6 * math.exp(-0.3 * i)
            y = even_mixer(h, cos, sin, w_in_even[j], conv_w[j], a_log[j], dt_bias[j], gdn_norm[j],
                           lam_q1[j], lam_k1[j], lam_q2[j], lam_k2[j], diff_norm[j], w_out_even[j],
                           lambda_init)
        else:
            y = fox_mixer(h, w_in_odd[j], b_forget[j], w_out_odd[j])
        x = x + y
        h = rms_norm(x, norm_mlp[i])
        x = x + jnp.square(jax.nn.relu(h @ w_mlp_up[i])) @ w_mlp_down[i]
        x = x + (p[i] @ w_ple_proj[i]) * jax.nn.sigmoid(x @ w_ple_gate[i])
    return rms_norm(x, norm_final)
```

```python
import functools
import math

import jax
import jax.numpy as jnp
from jax import lax
from jax.experimental import pallas as pl
from jax.experimental.pallas import tpu as pltpu

F32 = jnp.float32
BF16 = jnp.bfloat16

GDN_HEADS = 4
GDN_HEAD_DIM = 128
GDN_CHUNK = 64
CONV_WIDTH = 4
DIFF_HEADS = 4
DIFF_QK_DIM = 64
FOX_HEADS = 8
HEAD_LANES = 128
ROPE_THETA = 10000.0
EPS = 1e-6
NEG_INF = -1e30
LANES = 128
SUBLANES = 8
VMEM_LIMIT_BYTES = 56 * 1024 * 1024

TOKEN_TILE = 512
PROJ_SEG = 512
FF_SEG = 1024


def _dot(a, b):
    return jnp.dot(a, b, preferred_element_type=F32)


def _dot_exact(a, b):
    return jnp.dot(a, b, preferred_element_type=F32, precision=lax.Precision.HIGHEST)


def _dot_nt(a, b):
    return lax.dot_general(a, b, (((1,), (1,)), ((), ())), preferred_element_type=F32)


def _dot_tn(a, b):
    return lax.dot_general(a, b, (((0,), (0,)), ((), ())), preferred_element_type=F32)


def _rms(x, g):
    return x * lax.rsqrt(jnp.mean(x * x, axis=-1, keepdims=True) + EPS) * g


def _sigmoid(x):
    return 1.0 / (1.0 + jnp.exp(-x))


def _softplus(x):
    return jnp.maximum(x, 0.0) + jnp.log1p(jnp.exp(-jnp.abs(x)))


def _row_scan(x, period):
    rows = lax.broadcasted_iota(jnp.int32, x.shape, 0) % period
    s = 1
    while s < period:
        x = x + jnp.where(rows >= s, pltpu.roll(x, s, 0), 0.0)
        s *= 2
    return x


def _const_spec(shape):
    return pl.BlockSpec(shape, lambda *_: (0,) * len(shape))


def _params(sem):
    return pltpu.CompilerParams(dimension_semantics=sem, vmem_limit_bytes=VMEM_LIMIT_BYTES)


def _even_in_kernel(x_ref, g_ref, wm_ref, wg_ref, conv_ref, alog_ref, dt_ref, cos_ref, sin_ref,
                    qkv_ref, z_ref, qkb_ref, vb_ref, gates_ref, h_ref, carry_ref, *, tiles_per_seq):
    tm = x_ref.shape[0]
    i = pl.program_id(0)
    h_ref[...] = _rms(x_ref[...], g_ref[...]).astype(BF16)
    seq_start = (i % tiles_per_seq) == 0

    row8 = lax.broadcasted_iota(jnp.int32, (SUBLANES, PROJ_SEG), 0)
    for s in range(3):
        cols = slice(s * PROJ_SEG, (s + 1) * PROJ_SEG)
        y = _dot(h_ref[...], wm_ref[:, cols])
        prev8 = jnp.where(seq_start, 0.0, carry_ref[:, cols])
        carry_ref[:, cols] = y[tm - SUBLANES:, :]
        w = conv_ref[:, cols]
        acc = y * w[CONV_WIDTH - 1:CONV_WIDTH, :]
        top = y[:SUBLANES, :] * w[CONV_WIDTH - 1:CONV_WIDTH, :]
        for k in range(1, CONV_WIDTH):
            wk = w[CONV_WIDTH - 1 - k:CONV_WIDTH - k, :]
            rolled = pltpu.roll(y, k, 0)
            acc = acc + rolled * wk
            top = top + jnp.where(row8 < k, pltpu.roll(prev8, k, 0), rolled[:SUBLANES, :]) * wk
        for part, rows in ((acc, slice(0, tm)), (top, slice(0, SUBLANES))):
            a = part * _sigmoid(part)
            if s < 2:
                outs = []
                for hd in range(GDN_HEADS):
                    seg = a[:, hd * HEAD_LANES:(hd + 1) * HEAD_LANES]
                    n = seg * lax.rsqrt(jnp.sum(seg * seg, axis=-1, keepdims=True) + EPS)
                    outs.append(n * (GDN_HEAD_DIM ** -0.5) if s == 0 else n)
                a = jnp.concatenate(outs, axis=1)
            qkv_ref[rows, cols] = a

    z_ref[...] = _dot(h_ref[...], wm_ref[:, 3 * PROJ_SEG:4 * PROJ_SEG]).astype(BF16)

    cos = jnp.concatenate([cos_ref[...]] * (PROJ_SEG // LANES), axis=1)
    sin = jnp.concatenate([sin_ref[...]] * (PROJ_SEG // LANES), axis=1)
    lane = lax.broadcasted_iota(jnp.int32, (tm, PROJ_SEG), 1)
    first_half = (lane % DIFF_QK_DIM) < (DIFF_QK_DIM // 2)
    for s, scale in ((4, DIFF_QK_DIM ** -0.5), (5, 1.0)):
        y = _dot(h_ref[...], wm_ref[:, s * PROJ_SEG:(s + 1) * PROJ_SEG])
        swapped = jnp.where(first_half, pltpu.roll(y, PROJ_SEG - DIFF_QK_DIM // 2, 1),
                            pltpu.roll(y, DIFF_QK_DIM // 2, 1))
        r = y * cos + swapped * sin
        qkb_ref[:, (s - 4) * PROJ_SEG:(s - 3) * PROJ_SEG] = (r * scale).astype(BF16)
    vb_ref[...] = _dot(h_ref[...], wm_ref[:, 6 * PROJ_SEG:7 * PROJ_SEG]).astype(BF16)

    graw = _dot(h_ref[...], wg_ref[...])
    beta = _sigmoid(graw)
    g = -jnp.exp(alog_ref[...]) * _softplus(graw + dt_ref[...])
    gc = _row_scan(g, GDN_CHUNK)
    lane_g = lax.broadcasted_iota(jnp.int32, (tm, LANES), 1)
    gates_ref[...] = jnp.where(lane_g < GDN_HEADS, beta, gc)


def _even_in(x2d, g, wm, wg, conv_w, alog_row, dt_row, cos_t, sin_t, seq_len):
    m, d = x2d.shape
    tm = TOKEN_TILE
    n_main = wm.shape[1]
    kern = functools.partial(_even_in_kernel, tiles_per_seq=seq_len // tm)
    row = lambda i: (i, 0)
    return pl.pallas_call(
        kern,
        grid=(m // tm,),
        in_specs=[
            pl.BlockSpec((tm, d), row),
            _const_spec((1, d)),
            _const_spec((d, n_main)),
            _const_spec((d, LANES)),
            _const_spec(conv_w.shape),
            _const_spec((1, LANES)),
            _const_spec((1, LANES)),
            pl.BlockSpec((tm, LANES), row),
            pl.BlockSpec((tm, LANES), row),
        ],
        out_specs=[
            pl.BlockSpec((tm, 3 * PROJ_SEG), row),
            pl.BlockSpec((tm, PROJ_SEG), row),
            pl.BlockSpec((tm, 2 * PROJ_SEG), row),
            pl.BlockSpec((tm, PROJ_SEG), row),
            pl.BlockSpec((tm, LANES), row),
        ],
        out_shape=[
            jax.ShapeDtypeStruct((m, 3 * PROJ_SEG), F32),
            jax.ShapeDtypeStruct((m, PROJ_SEG), BF16),
            jax.ShapeDtypeStruct((m, 2 * PROJ_SEG), BF16),
            jax.ShapeDtypeStruct((m, PROJ_SEG), BF16),
            jax.ShapeDtypeStruct((m, LANES), F32),
        ],
        scratch_shapes=[pltpu.VMEM((tm, d), BF16), pltpu.VMEM((SUBLANES, 3 * PROJ_SEG), F32)],
        compiler_params=_params(("arbitrary",)),
        name="even_in_proj",
    )(x2d, g, wm, wg, conv_w, alog_row, dt_row, cos_t, sin_t)


def _odd_in_kernel(x_ref, g_ref, wm_ref, wf_ref, bf_ref, q_ref, k_ref, v_ref, gate_ref, cum_ref,
                   h_ref, carry_ref, *, tiles_per_seq, d_mix):
    tm = x_ref.shape[0]
    i = pl.program_id(0)
    h_ref[...] = _rms(x_ref[...], g_ref[...]).astype(BF16)
    head_dim = d_mix // FOX_HEADS
    for o_ref, base, scale in ((q_ref, 0, head_dim ** -0.5), (k_ref, d_mix, 1.0),
                               (v_ref, 2 * d_mix, 1.0), (gate_ref, 3 * d_mix, 1.0)):
        for s in range(d_mix // PROJ_SEG):
            cols = slice(s * PROJ_SEG, (s + 1) * PROJ_SEG)
            y = _dot(h_ref[...], wm_ref[:, base + s * PROJ_SEG:base + (s + 1) * PROJ_SEG])
            o_ref[:, cols] = (y * scale).astype(BF16)
    f = _dot(h_ref[...], wf_ref[...]) + bf_ref[...]
    log_f = jnp.minimum(f, 0.0) - jnp.log1p(jnp.exp(-jnp.abs(f)))
    prev = jnp.where((i % tiles_per_seq) == 0, 0.0, carry_ref[0:1, :])
    cum = _row_scan(log_f, tm) + prev
    cum_ref[...] = cum
    carry_ref[...] = jnp.broadcast_to(cum[tm - 1:tm, :], carry_ref.shape)


def _odd_in(x2d, g, wm, wf, bf_row, seq_len):
    m, d = x2d.shape
    tm = TOKEN_TILE
    d_mix = wm.shape[1] // 4
    kern = functools.partial(_odd_in_kernel, tiles_per_seq=seq_len // tm, d_mix=d_mix)
    row = lambda i: (i, 0)
    return pl.pallas_call(
        kern,
        grid=(m // tm,),
        in_specs=[
            pl.BlockSpec((tm, d), row),
            _const_spec((1, d)),
            _const_spec(wm.shape),
            _const_spec((d, LANES)),
            _const_spec((1, LANES)),
        ],
        out_specs=[pl.BlockSpec((tm, d_mix), row)] * 4 + [pl.BlockSpec((tm, LANES), row)],
        out_shape=[jax.ShapeDtypeStruct((m, d_mix), BF16)] * 4 + [jax.ShapeDtypeStruct((m, LANES), F32)],
        scratch_shapes=[pltpu.VMEM((tm, d), BF16), pltpu.VMEM((SUBLANES, LANES), F32)],
        compiler_params=_params(("arbitrary",)),
        name="odd_in_proj",
    )(x2d, g, wm, wf, bf_row)


def _attn_kernel(*refs, tq, tk, n_maps, fox, lambda_init):
    if fox:
        q_ref, k_ref, v_ref, gate_ref, cumc_ref, cumr_ref, o_ref = refs
    else:
        q_ref, k_ref, v_ref, lam_ref, nw_ref, o_ref = refs
    hd = pl.program_id(1)
    qi = pl.program_id(2)
    q = q_ref[0]
    if n_maps == 2:
        lane = lax.broadcasted_iota(jnp.int32, q.shape, 1)
        zero = jnp.zeros_like(q)
        q = jnp.concatenate([jnp.where(lane < DIFF_QK_DIM, q, zero),
                             jnp.where(lane >= DIFF_QK_DIM, q, zero)], axis=0)
    rows = n_maps * tq
    if fox:
        lane_c = lax.broadcasted_iota(jnp.int32, (tq, LANES), 1)
        cq = jnp.sum(jnp.where(lane_c == hd, cumc_ref[0], 0.0), axis=1, keepdims=True)

    def block(j, carry, masked):
        m, l, acc = carry
        k0 = pl.multiple_of(j * tk, tk)
        kj = k_ref[0, pl.ds(k0, tk), :]
        vj = v_ref[0, pl.ds(k0, tk), :]
        s = _dot_nt(q, kj)
        if fox:
            s = s + cq - cumr_ref[0, pl.ds(hd, 1), pl.ds(k0, tk)]
        if masked:
            qpos = qi * tq + lax.broadcasted_iota(jnp.int32, (rows, tk), 0) % tq
            kpos = k0 + lax.broadcasted_iota(jnp.int32, (rows, tk), 1)
            s = jnp.where(qpos >= kpos, s, NEG_INF)
        m_new = jnp.maximum(m, jnp.max(s, axis=1, keepdims=True))
        alpha = jnp.exp(m - m_new)
        p = jnp.exp(s - m_new)
        l = alpha * l + jnp.sum(p, axis=1, keepdims=True)
        acc = alpha * acc + _dot(p.astype(BF16), vj)
        return m_new, l, acc

    carry = (jnp.full((rows, 1), NEG_INF, F32), jnp.zeros((rows, 1), F32), jnp.zeros((rows, HEAD_LANES), F32))
    per_q = tq // tk
    carry = lax.fori_loop(0, qi * per_q, lambda j, c: block(j, c, False), carry)
    for d in range(per_q):
        carry = block(qi * per_q + d, carry, True)
    _, l, acc = carry
    o = acc / l
    if fox:
        o = o * _sigmoid(gate_ref[0].astype(F32))
    else:
        lam_p = lam_ref[...]
        lam = (jnp.exp(jnp.sum(lam_p[0:1] * lam_p[1:2], axis=1, keepdims=True))
               - jnp.exp(jnp.sum(lam_p[2:3] * lam_p[3:4], axis=1, keepdims=True)) + lambda_init)
        o = o[:tq] - lam * o[tq:]
        o = _rms(o, nw_ref[...]) * (1.0 - lambda_init)
    o_ref[0] = o.astype(o_ref.dtype)


def _fox_attention(q, k, v, gate, cum_col, cum_row, *, tq=512, tk=512):
    b, t, dm = q.shape
    nh = dm // HEAD_LANES
    kern = functools.partial(_attn_kernel, tq=tq, tk=tk, n_maps=1, fox=True, lambda_init=0.0)
    qblk = pl.BlockSpec((1, tq, HEAD_LANES), lambda bi, h, i: (bi, i, h))
    kvblk = pl.BlockSpec((1, t, HEAD_LANES), lambda bi, h, i: (bi, 0, h))
    return pl.pallas_call(
        kern,
        grid=(b, nh, t // tq),
        in_specs=[qblk, kvblk, kvblk, qblk,
                  pl.BlockSpec((1, tq, LANES), lambda bi, h, i: (bi, i, 0)),
                  pl.BlockSpec((1, SUBLANES, t), lambda bi, h, i: (bi, 0, 0))],
        out_specs=qblk,
        out_shape=jax.ShapeDtypeStruct((b, t, dm), BF16),
        compiler_params=_params(("arbitrary", "arbitrary", "arbitrary")),
        name="fox_attention",
    )(q, k, v, gate, cum_col, cum_row)


def _diff_attention(qk, v, lam_params, norm_w, lambda_init, *, tq=256, tk=256):
    b, t, _ = qk.shape
    nh = DIFF_HEADS
    kern = functools.partial(_attn_kernel, tq=tq, tk=tk, n_maps=2, fox=False, lambda_init=lambda_init)
    qblk = pl.BlockSpec((1, tq, HEAD_LANES), lambda bi, h, i: (bi, i, h))
    kblk = pl.BlockSpec((1, t, HEAD_LANES), lambda bi, h, i: (bi, 0, nh + h))
    vblk = pl.BlockSpec((1, t, HEAD_LANES), lambda bi, h, i: (bi, 0, h))
    return pl.pallas_call(
        kern,
        grid=(b, nh, t // tq),
        in_specs=[qblk, kblk, vblk, _const_spec(lam_params.shape), _const_spec((1, HEAD_LANES))],
        out_specs=qblk,
        out_shape=jax.ShapeDtypeStruct((b, t, nh * HEAD_LANES), BF16),
        compiler_params=_params(("arbitrary", "arbitrary", "arbitrary")),
        name="diff_attention",
    )(qk, qk, v, lam_params, norm_w)


def _gdn_prep_kernel(q_ref, k_ref, v_ref, gates_ref, u_ref, w_ref, qd_ref, kd_ref, qk_ref):
    c = GDN_CHUNK
    hd = pl.program_id(1)
    tt = q_ref.shape[1]
    ri = lax.broadcasted_iota(jnp.int32, (c, c), 0)
    ci = lax.broadcasted_iota(jnp.int32, (c, c), 1)
    incl, strict, eye = ri >= ci, ri > ci, ri == ci
    lane = lax.broadcasted_iota(jnp.int32, (c, LANES), 1)
    ones = jnp.ones((c, c), F32)
    ident = eye.astype(F32)
    for n in range(tt // c):
        rows = slice(n * c, (n + 1) * c)
        qc, kc, vc, gt = q_ref[0, rows, :], k_ref[0, rows, :], v_ref[0, rows, :], gates_ref[0, rows, :]
        beta = jnp.sum(jnp.where(lane == hd, gt, 0.0), axis=1, keepdims=True)
        gcc = jnp.sum(jnp.where(lane == GDN_HEADS + hd, gt, 0.0), axis=1, keepdims=True)
        gc_row = _dot_exact(ones, jnp.where(eye, gcc, 0.0))
        decay = jnp.where(incl, jnp.exp(jnp.where(incl, gcc - gc_row, 0.0)), 0.0)
        kb = kc * beta
        lower = jnp.where(strict, _dot_nt(kb.astype(BF16), kc.astype(BF16)) * decay, 0.0)
        inv = ident - lower
        power = lower
        for _ in range(int(math.log2(c)) - 1):
            power = _dot_exact(power, power)
            inv = inv + _dot_exact(inv, power)
        eg = jnp.exp(gcc)
        u_ref[0, rows, :] = _dot_exact(inv, vc * beta)
        w_ref[0, rows, :] = _dot_exact(inv, kb * eg).astype(BF16)
        qk = jnp.where(incl, _dot_nt(qc.astype(BF16), kc.astype(BF16)) * decay, 0.0)
        qk_ref[0, 0, rows, :] = qk.astype(BF16)
        qd_ref[0, rows, :] = (qc * eg).astype(BF16)
        kd_ref[0, rows, :] = (kc * jnp.exp(gcc[c - 1:c, :] - gcc)).astype(BF16)


def _gdn_prep(qkv, gates, *, tt=256):
    b, t, _ = qkv.shape
    nh = GDN_HEADS
    blk = lambda off: pl.BlockSpec((1, tt, HEAD_LANES), lambda bi, h, i: (bi, i, off + h))
    out_blk = pl.BlockSpec((1, tt, HEAD_LANES), lambda bi, h, i: (bi, i, h))
    return pl.pallas_call(
        _gdn_prep_kernel,
        grid=(b, nh, t // tt),
        in_specs=[blk(0), blk(nh), blk(2 * nh), pl.BlockSpec((1, tt, LANES), lambda bi, h, i: (bi, i, 0))],
        out_specs=[out_blk] * 4 + [pl.BlockSpec((1, 1, tt, GDN_CHUNK), lambda bi, h, i: (bi, h, i, 0))],
        out_shape=[jax.ShapeDtypeStruct((b, t, nh * HEAD_LANES), F32)]
        + [jax.ShapeDtypeStruct((b, t, nh * HEAD_LANES), BF16)] * 3
        + [jax.ShapeDtypeStruct((b, nh, t, GDN_CHUNK), BF16)],
        compiler_params=_params(("arbitrary", "arbitrary", "arbitrary")),
        name="gdn_prep",
    )(qkv, qkv, qkv, gates)


def _gdn_scan_kernel(u_ref, w_ref, qd_ref, kd_ref, qk_ref, gates_ref, z_ref, nw_ref, o_ref, s_ref):
    c = GDN_CHUNK
    tt = u_ref.shape[1]

    @pl.when(pl.program_id(1) == 0)
    def _():
        s_ref[...] = jnp.zeros_like(s_ref)

    def chunk(n, _):
        r0 = pl.multiple_of(n * c, c)
        rows = pl.ds(r0, c)
        g_last = gates_ref[0, pl.ds(r0 + c - SUBLANES, SUBLANES), :]
        for hd in range(GDN_HEADS):
            cols = slice(hd * HEAD_LANES, (hd + 1) * HEAD_LANES)
            state = s_ref[hd]
            wq = jnp.concatenate([w_ref[0, rows, cols], qd_ref[0, rows, cols]], axis=0)
            r = _dot(wq, state.astype(BF16))
            v_new = (u_ref[0, rows, cols] - r[:c]).astype(BF16)
            o = r[c:] + _dot(qk_ref[0, hd, rows, :], v_new)
            decay_last = jnp.exp(g_last[SUBLANES - 1:SUBLANES, GDN_HEADS + hd:GDN_HEADS + hd + 1])
            s_ref[hd] = state * decay_last + _dot_tn(kd_ref[0, rows, cols], v_new)
            zt = z_ref[0, rows, cols].astype(F32)
            o_ref[0, rows, cols] = (_rms(o, nw_ref[...]) * (zt * _sigmoid(zt))).astype(o_ref.dtype)
        return 0

    lax.fori_loop(0, tt // c, chunk, 0)


def _gdn_scan(u, w, qd, kd, qk, gates, z, norm_w, *, tt=512):
    b, t, dm = u.shape
    nh = GDN_HEADS
    blk = pl.BlockSpec((1, tt, dm), lambda bi, i: (bi, i, 0))
    return pl.pallas_call(
        _gdn_scan_kernel,
        grid=(b, t // tt),
        in_specs=[blk, blk, blk, blk,
                  pl.BlockSpec((1, nh, tt, GDN_CHUNK), lambda bi, i: (bi, 0, i, 0)),
                  pl.BlockSpec((1, tt, LANES), lambda bi, i: (bi, i, 0)),
                  blk, _const_spec((1, HEAD_LANES))],
        out_specs=blk,
        out_shape=jax.ShapeDtypeStruct((b, t, dm), BF16),
        scratch_shapes=[pltpu.VMEM((nh, GDN_HEAD_DIM, GDN_HEAD_DIM), F32)],
        compiler_params=_params(("arbitrary", "arbitrary")),
        name="gdn_scan",
    )(u, w, qd, kd, qk, gates, z, norm_w)


def _post_kernel(*refs, n_mix, final_norm):
    x_ref = refs[0]
    mix_refs = refs[1:1 + n_mix]
    wout_ref, g_ref, wup_ref, wdn_ref, p_ref, wpp_ref, wpg_ref = refs[1 + n_mix:8 + n_mix]
    rest = refs[8 + n_mix:]
    if final_norm:
        gf_ref, o_ref = rest
    else:
        (o_ref,) = rest
    mix = mix_refs[0][...] if n_mix == 1 else jnp.concatenate([r[...] for r in mix_refs], axis=1)
    x = x_ref[...] + _dot(mix, wout_ref[...])
    h = _rms(x, g_ref[...]).astype(BF16)
    d_ff = wup_ref.shape[1]
    acc = x
    for s in range(d_ff // FF_SEG):
        a = jnp.maximum(_dot(h, wup_ref[:, s * FF_SEG:(s + 1) * FF_SEG]), 0.0)
        acc = acc + _dot((a * a).astype(BF16), wdn_ref[s * FF_SEG:(s + 1) * FF_SEG, :])
    x = acc
    gate = _sigmoid(_dot(x.astype(BF16), wpg_ref[...]))
    x = x + _dot(p_ref[...].astype(BF16), wpp_ref[...]) * gate
    if final_norm:
        x = _rms(x, gf_ref[...])
    o_ref[...] = x


def _post(x2d, mixes, wout, g, wup, wdn, p2d, wpp, wpg, gf=None):
    m, d = x2d.shape
    tm = TOKEN_TILE
    row = lambda i: (i, 0)
    single = pl.Buffered(1)
    const = lambda a: pl.BlockSpec(a.shape, lambda i: (0, 0), pipeline_mode=single)
    args = [x2d, *mixes, wout, g, wup, wdn, p2d, wpp, wpg]
    in_specs = ([pl.BlockSpec((tm, d), row)]
                + [pl.BlockSpec((tm, a.shape[1]), row) for a in mixes]
                + [const(wout), const(g), const(wup), const(wdn), pl.BlockSpec((tm, p2d.shape[1]), row), const(wpp), const(wpg)])
    if gf is not None:
        args.append(gf)
        in_specs.append(const(gf))
    kern = functools.partial(_post_kernel, n_mix=len(mixes), final_norm=gf is not None)
    return pl.pallas_call(
        kern,
        grid=(m // tm,),
        in_specs=in_specs,
        out_specs=pl.BlockSpec((tm, d), row),
        out_shape=jax.ShapeDtypeStruct((m, d), F32),
        compiler_params=_params(("arbitrary",)),
        name="out_proj_mlp_ple",
    )(*args)


def _pad_lanes(a):
    return jnp.pad(a, ((0, 0), (0, LANES - a.shape[1])))


def kernel(x, p, positions, norm_mix, norm_mlp, norm_final, w_in_even, conv_w, a_log, dt_bias, gdn_norm,
           lam_q1, lam_k1, lam_q2, lam_k2, diff_norm, w_out_even, w_in_odd, b_forget, w_out_odd,
           w_mlp_up, w_mlp_down, w_ple_proj, w_ple_gate):
    b, t, d = x.shape
    depth = p.shape[0]
    m = b * t
    assert t % TOKEN_TILE == 0 and d % PROJ_SEG == 0
    nh = GDN_HEADS
    gdn_w = 3 * nh * GDN_HEAD_DIM + nh * GDN_HEAD_DIM
    assert w_in_even.shape[2] == gdn_w + 2 * nh + 3 * DIFF_HEADS * 2 * DIFF_QK_DIM

    inv_freq = ROPE_THETA ** (-jnp.arange(0, DIFF_QK_DIM, 2, dtype=F32) / DIFF_QK_DIM)
    ang = positions.astype(F32)[..., None] * inv_freq
    cos, sin = jnp.cos(ang), jnp.sin(ang)
    cos_t = jnp.concatenate([cos, cos, cos, cos], axis=-1).reshape(m, LANES)
    sin_t = jnp.concatenate([-sin, sin, -sin, sin], axis=-1).reshape(m, LANES)

    x2d = x.reshape(m, d)
    for i in range(depth):
        j = i // 2
        g_mix = norm_mix[i].reshape(1, d)
        if i % 2 == 0:
            lambda_init = 0.8 - 0.6 * math.exp(-0.3 * i)
            w = w_in_even[j]
            wm = jnp.concatenate([w[:, :gdn_w], w[:, gdn_w + 2 * nh:]], axis=1).astype(BF16)
            wg = _pad_lanes(w[:, gdn_w:gdn_w + 2 * nh]).astype(BF16)
            alog_row = _pad_lanes(jnp.concatenate([jnp.zeros((nh,), F32), a_log[j]]).reshape(1, 2 * nh))
            dt_row = _pad_lanes(jnp.concatenate([jnp.zeros((nh,), F32), dt_bias[j]]).reshape(1, 2 * nh))
            qkv, z, qkb, vb, gates = _even_in(x2d, g_mix, wm, wg, conv_w[j], alog_row, dt_row, cos_t, sin_t, t)
            qkv, z, qkb, vb, gates = (a.reshape(b, t, -1) for a in (qkv, z, qkb, vb, gates))
            u, wy, qd, kd, qk = _gdn_prep(qkv, gates)
            o_a = _gdn_scan(u, wy, qd, kd, qk, gates, z, gdn_norm[j].reshape(1, HEAD_LANES))
            lam_params = jnp.stack([lam_q1[j], lam_k1[j], lam_q2[j], lam_k2[j]])
            o_b = _diff_attention(qkb, vb, lam_params, diff_norm[j].reshape(1, HEAD_LANES), lambda_init)
            mixes = [o_a.reshape(m, -1), o_b.reshape(m, -1)]
            wout = w_out_even[j].astype(BF16)
        else:
            w = w_in_odd[j]
            d_mix = (w.shape[1] - FOX_HEADS) // 4
            wm = w[:, :4 * d_mix].astype(BF16)
            wf = _pad_lanes(w[:, 4 * d_mix:]).astype(BF16)
            bf_row = _pad_lanes(b_forget[j].reshape(1, FOX_HEADS))
            q, k, v, gate, cum = _odd_in(x2d, g_mix, wm, wf, bf_row, t)
            q, k, v, gate, cum = (a.reshape(b, t, -1) for a in (q, k, v, gate, cum))
            cum_row = jnp.swapaxes(cum[:, :, :SUBLANES], 1, 2)
            o = _fox_attention(q, k, v, gate, cum, cum_row)
            mixes = [o.reshape(m, -1)]
            wout = w_out_odd[j].astype(BF16)
        x2d = _post(x2d, mixes, wout, norm_mlp[i].reshape(1, d), w_mlp_up[i].astype(BF16),
                    w_mlp_down[i].astype(BF16), p[i].reshape(m, -1), w_ple_proj[i].astype(BF16),
                    w_ple_gate[i].astype(BF16), norm_final.reshape(1, d) if i == depth - 1 else None)
    return x2d.reshape(b, t, d)
```

```python
import functools
import math

import jax
import jax.numpy as jnp
from jax import lax
from jax.experimental import pallas as pl
from jax.experimental.pallas import tpu as pltpu

F32 = jnp.float32
BF16 = jnp.bfloat16

GDN_HEADS = 4
GDN_HEAD_DIM = 128
GDN_CHUNK = 64
CONV_WIDTH = 4
DIFF_HEADS = 4
DIFF_QK_DIM = 64
FOX_HEADS = 8
HEAD_LANES = 128
ROPE_THETA = 10000.0
EPS = 1e-6
NEG_INF = -1e30
LANES = 128
SUBLANES = 8
VMEM_LIMIT_BYTES = 56 * 1024 * 1024

TOKEN_TILE = 512
PROJ_SEG = 512
FF_SEG = 1024
GDN_REFINE = True


def _dot(a, b):
    return jnp.dot(a, b, preferred_element_type=F32)


def _dot_exact(a, b):
    return jnp.dot(a, b, preferred_element_type=F32, precision=lax.Precision.HIGHEST)


def _dot_nt(a, b):
    return lax.dot_general(a, b, (((1,), (1,)), ((), ())), preferred_element_type=F32)


def _dot_tn(a, b):
    return lax.dot_general(a, b, (((0,), (0,)), ((), ())), preferred_element_type=F32)


def _rms(x, g):
    return x * lax.rsqrt(jnp.mean(x * x, axis=-1, keepdims=True) + EPS) * g


def _sigmoid(x):
    return 1.0 / (1.0 + jnp.exp(-x))


def _softplus(x):
    return jnp.maximum(x, 0.0) + jnp.log1p(jnp.exp(-jnp.abs(x)))


def _row_scan(x, period):
    rows = lax.broadcasted_iota(jnp.int32, x.shape, 0) % period
    s = 1
    while s < period:
        x = x + jnp.where(rows >= s, pltpu.roll(x, s, 0), 0.0)
        s *= 2
    return x


def _const_spec(shape):
    return pl.BlockSpec(shape, lambda *_: (0,) * len(shape))


def _params(sem):
    return pltpu.CompilerParams(dimension_semantics=sem, vmem_limit_bytes=VMEM_LIMIT_BYTES)


def _even_in_kernel(x_ref, g_ref, wm_ref, wg_ref, conv_ref, alog_ref, dt_ref, cos_ref, sin_ref,
                    qkv_ref, z_ref, qkb_ref, vb_ref, gates_ref, h_ref, carry_ref, *, tiles_per_seq):
    tm = x_ref.shape[0]
    i = pl.program_id(0)
    h_ref[...] = _rms(x_ref[...], g_ref[...]).astype(BF16)
    seq_start = (i % tiles_per_seq) == 0

    row8 = lax.broadcasted_iota(jnp.int32, (SUBLANES, PROJ_SEG), 0)
    for s in range(3):
        cols = slice(s * PROJ_SEG, (s + 1) * PROJ_SEG)
        y = _dot(h_ref[...], wm_ref[:, cols])
        prev8 = jnp.where(seq_start, 0.0, carry_ref[:, cols])
        carry_ref[:, cols] = y[tm - SUBLANES:, :]
        w = conv_ref[:, cols]
        acc = y * w[CONV_WIDTH - 1:CONV_WIDTH, :]
        top = y[:SUBLANES, :] * w[CONV_WIDTH - 1:CONV_WIDTH, :]
        for k in range(1, CONV_WIDTH):
            wk = w[CONV_WIDTH - 1 - k:CONV_WIDTH - k, :]
            rolled = pltpu.roll(y, k, 0)
            acc = acc + rolled * wk
            top = top + jnp.where(row8 < k, pltpu.roll(prev8, k, 0), rolled[:SUBLANES, :]) * wk
        for part, rows in ((acc, slice(0, tm)), (top, slice(0, SUBLANES))):
            a = part * _sigmoid(part)
            if s < 2:
                outs = []
                for hd in range(GDN_HEADS):
                    seg = a[:, hd * HEAD_LANES:(hd + 1) * HEAD_LANES]
                    n = seg * lax.rsqrt(jnp.sum(seg * seg, axis=-1, keepdims=True) + EPS)
                    outs.append(n * (GDN_HEAD_DIM ** -0.5) if s == 0 else n)
                a = jnp.concatenate(outs, axis=1)
            qkv_ref[rows, cols] = a

    z_ref[...] = _dot(h_ref[...], wm_ref[:, 3 * PROJ_SEG:4 * PROJ_SEG]).astype(BF16)

    cos = jnp.concatenate([cos_ref[...]] * (PROJ_SEG // LANES), axis=1)
    sin = jnp.concatenate([sin_ref[...]] * (PROJ_SEG // LANES), axis=1)
    lane = lax.broadcasted_iota(jnp.int32, (tm, PROJ_SEG), 1)
    first_half = (lane % DIFF_QK_DIM) < (DIFF_QK_DIM // 2)
    for s, scale in ((4, DIFF_QK_DIM ** -0.5), (5, 1.0)):
        y = _dot(h_ref[...], wm_ref[:, s * PROJ_SEG:(s + 1) * PROJ_SEG])
        swapped = jnp.where(first_half, pltpu.roll(y, PROJ_SEG - DIFF_QK_DIM // 2, 1),
                            pltpu.roll(y, DIFF_QK_DIM // 2, 1))
        r = y * cos + swapped * sin
        qkb_ref[:, (s - 4) * PROJ_SEG:(s - 3) * PROJ_SEG] = (r * scale).astype(BF16)
    vb_ref[...] = _dot(h_ref[...], wm_ref[:, 6 * PROJ_SEG:7 * PROJ_SEG]).astype(BF16)

    graw = _dot(h_ref[...], wg_ref[...])
    beta = _sigmoid(graw)
    g = -jnp.exp(alog_ref[...]) * _softplus(graw + dt_ref[...])
    gc = _row_scan(g, GDN_CHUNK)
    lane_g = lax.broadcasted_iota(jnp.int32, (tm, LANES), 1)
    gates_ref[...] = jnp.where(lane_g < GDN_HEADS, beta, gc)


def _even_in(x2d, g, wm, wg, conv_w, alog_row, dt_row, cos_t, sin_t, seq_len):
    m, d = x2d.shape
    tm = TOKEN_TILE
    n_main = wm.shape[1]
    kern = functools.partial(_even_in_kernel, tiles_per_seq=seq_len // tm)
    row = lambda i: (i, 0)
    return pl.pallas_call(
        kern,
        grid=(m // tm,),
        in_specs=[
            pl.BlockSpec((tm, d), row),
            _const_spec((1, d)),
            _const_spec((d, n_main)),
            _const_spec((d, LANES)),
            _const_spec(conv_w.shape),
            _const_spec((1, LANES)),
            _const_spec((1, LANES)),
            pl.BlockSpec((tm, LANES), row),
            pl.BlockSpec((tm, LANES), row),
        ],
        out_specs=[
            pl.BlockSpec((tm, 3 * PROJ_SEG), row),
            pl.BlockSpec((tm, PROJ_SEG), row),
            pl.BlockSpec((tm, 2 * PROJ_SEG), row),
            pl.BlockSpec((tm, PROJ_SEG), row),
            pl.BlockSpec((tm, LANES), row),
        ],
        out_shape=[
            jax.ShapeDtypeStruct((m, 3 * PROJ_SEG), F32),
            jax.ShapeDtypeStruct((m, PROJ_SEG), BF16),
            jax.ShapeDtypeStruct((m, 2 * PROJ_SEG), BF16),
            jax.ShapeDtypeStruct((m, PROJ_SEG), BF16),
            jax.ShapeDtypeStruct((m, LANES), F32),
        ],
        scratch_shapes=[pltpu.VMEM((tm, d), BF16), pltpu.VMEM((SUBLANES, 3 * PROJ_SEG), F32)],
        compiler_params=_params(("arbitrary",)),
        name="even_in_proj",
    )(x2d, g, wm, wg, conv_w, alog_row, dt_row, cos_t, sin_t)


def _odd_in_kernel(x_ref, g_ref, wm_ref, wf_ref, bf_ref, q_ref, k_ref, v_ref, gate_ref, cum_ref,
                   h_ref, carry_ref, *, tiles_per_seq, d_mix):
    tm = x_ref.shape[0]
    i = pl.program_id(0)
    h_ref[...] = _rms(x_ref[...], g_ref[...]).astype(BF16)
    head_dim = d_mix // FOX_HEADS
    for o_ref, base, scale in ((q_ref, 0, head_dim ** -0.5), (k_ref, d_mix, 1.0),
                               (v_ref, 2 * d_mix, 1.0), (gate_ref, 3 * d_mix, 1.0)):
        for s in range(d_mix // PROJ_SEG):
            cols = slice(s * PROJ_SEG, (s + 1) * PROJ_SEG)
            y = _dot(h_ref[...], wm_ref[:, base + s * PROJ_SEG:base + (s + 1) * PROJ_SEG])
            o_ref[:, cols] = (y * scale).astype(BF16)
    f = _dot(h_ref[...], wf_ref[...]) + bf_ref[...]
    log_f = jnp.minimum(f, 0.0) - jnp.log1p(jnp.exp(-jnp.abs(f)))
    prev = jnp.where((i % tiles_per_seq) == 0, 0.0, carry_ref[0:1, :])
    cum = _row_scan(log_f, tm) + prev
    cum_ref[...] = cum
    carry_ref[...] = jnp.broadcast_to(cum[tm - 1:tm, :], carry_ref.shape)


def _odd_in(x2d, g, wm, wf, bf_row, seq_len):
    m, d = x2d.shape
    tm = TOKEN_TILE
    d_mix = wm.shape[1] // 4
    kern = functools.partial(_odd_in_kernel, tiles_per_seq=seq_len // tm, d_mix=d_mix)
    row = lambda i: (i, 0)
    return pl.pallas_call(
        kern,
        grid=(m // tm,),
        in_specs=[
            pl.BlockSpec((tm, d), row),
            _const_spec((1, d)),
            _const_spec(wm.shape),
            _const_spec((d, LANES)),
            _const_spec((1, LANES)),
        ],
        out_specs=[pl.BlockSpec((tm, d_mix), row)] * 4 + [pl.BlockSpec((tm, LANES), row)],
        out_shape=[jax.ShapeDtypeStruct((m, d_mix), BF16)] * 4 + [jax.ShapeDtypeStruct((m, LANES), F32)],
        scratch_shapes=[pltpu.VMEM((tm, d), BF16), pltpu.VMEM((SUBLANES, LANES), F32)],
        compiler_params=_params(("arbitrary",)),
        name="odd_in_proj",
    )(x2d, g, wm, wf, bf_row)


def _attn_kernel(*refs, tq, tk, n_maps, fox, lambda_init):
    if fox:
        q_ref, k_ref, v_ref, gate_ref, cumc_ref, cumr_ref, o_ref = refs
    else:
        q_ref, k_ref, v_ref, lam_ref, nw_ref, o_ref = refs
    hd = pl.program_id(1)
    qi = pl.program_id(2)
    q = q_ref[0]
    if n_maps == 2:
        lane = lax.broadcasted_iota(jnp.int32, q.shape, 1)
        zero = jnp.zeros_like(q)
        q = jnp.concatenate([jnp.where(lane < DIFF_QK_DIM, q, zero),
                             jnp.where(lane >= DIFF_QK_DIM, q, zero)], axis=0)
    rows = n_maps * tq
    if fox:
        lane_c = lax.broadcasted_iota(jnp.int32, (tq, LANES), 1)
        cq = jnp.sum(jnp.where(lane_c == hd, cumc_ref[0], 0.0), axis=1, keepdims=True)

    def block(j, carry, masked):
        m, l, acc = carry
        k0 = pl.multiple_of(j * tk, tk)
        kj = k_ref[0, pl.ds(k0, tk), :]
        vj = v_ref[0, pl.ds(k0, tk), :]
        s = _dot_nt(q, kj)
        if fox:
            s = s + cq - cumr_ref[0, pl.ds(hd, 1), pl.ds(k0, tk)]
        if masked:
            qpos = qi * tq + lax.broadcasted_iota(jnp.int32, (rows, tk), 0) % tq
            kpos = k0 + lax.broadcasted_iota(jnp.int32, (rows, tk), 1)
            s = jnp.where(qpos >= kpos, s, NEG_INF)
        m_new = jnp.maximum(m, jnp.max(s, axis=1, keepdims=True))
        alpha = jnp.exp(m - m_new)
        p = jnp.exp(s - m_new)
        l = alpha * l + jnp.sum(p, axis=1, keepdims=True)
        acc = alpha * acc + _dot(p.astype(BF16), vj)
        return m_new, l, acc

    carry = (jnp.full((rows, 1), NEG_INF, F32), jnp.zeros((rows, 1), F32), jnp.zeros((rows, HEAD_LANES), F32))
    per_q = tq // tk
    carry = lax.fori_loop(0, qi * per_q, lambda j, c: block(j, c, False), carry)
    for d in range(per_q):
        carry = block(qi * per_q + d, carry, True)
    _, l, acc = carry
    o = acc / l
    if fox:
        o = o * _sigmoid(gate_ref[0].astype(F32))
    else:
        lam_p = lam_ref[...]
        lam = (jnp.exp(jnp.sum(lam_p[0:1] * lam_p[1:2], axis=1, keepdims=True))
               - jnp.exp(jnp.sum(lam_p[2:3] * lam_p[3:4], axis=1, keepdims=True)) + lambda_init)
        o = o[:tq] - lam * o[tq:]
        o = _rms(o, nw_ref[...]) * (1.0 - lambda_init)
    o_ref[0] = o.astype(o_ref.dtype)


def _fox_attention(q, k, v, gate, cum_col, cum_row, *, tq=512, tk=512):
    b, t, dm = q.shape
    nh = dm // HEAD_LANES
    kern = functools.partial(_attn_kernel, tq=tq, tk=tk, n_maps=1, fox=True, lambda_init=0.0)
    qblk = pl.BlockSpec((1, tq, HEAD_LANES), lambda bi, h, i: (bi, i, h))
    kvblk = pl.BlockSpec((1, t, HEAD_LANES), lambda bi, h, i: (bi, 0, h))
    return pl.pallas_call(
        kern,
        grid=(b, nh, t // tq),
        in_specs=[qblk, kvblk, kvblk, qblk,
                  pl.BlockSpec((1, tq, LANES), lambda bi, h, i: (bi, i, 0)),
                  pl.BlockSpec((1, SUBLANES, t), lambda bi, h, i: (bi, 0, 0))],
        out_specs=qblk,
        out_shape=jax.ShapeDtypeStruct((b, t, dm), BF16),
        compiler_params=_params(("arbitrary", "arbitrary", "arbitrary")),
        name="fox_attention",
    )(q, k, v, gate, cum_col, cum_row)


def _diff_attention(qk, v, lam_params, norm_w, lambda_init, *, tq=256, tk=256):
    b, t, _ = qk.shape
    nh = DIFF_HEADS
    kern = functools.partial(_attn_kernel, tq=tq, tk=tk, n_maps=2, fox=False, lambda_init=lambda_init)
    qblk = pl.BlockSpec((1, tq, HEAD_LANES), lambda bi, h, i: (bi, i, h))
    kblk = pl.BlockSpec((1, t, HEAD_LANES), lambda bi, h, i: (bi, 0, nh + h))
    vblk = pl.BlockSpec((1, t, HEAD_LANES), lambda bi, h, i: (bi, 0, h))
    return pl.pallas_call(
        kern,
        grid=(b, nh, t // tq),
        in_specs=[qblk, kblk, vblk, _const_spec(lam_params.shape), _const_spec((1, HEAD_LANES))],
        out_specs=qblk,
        out_shape=jax.ShapeDtypeStruct((b, t, nh * HEAD_LANES), BF16),
        compiler_params=_params(("arbitrary", "arbitrary", "arbitrary")),
        name="diff_attention",
    )(qk, qk, v, lam_params, norm_w)


def _split_bf16(x, parts):
    out = []
    for _ in range(parts):
        piece = x.astype(BF16)
        out.append(piece)
        x = x - piece.astype(F32)
    return out


def _dot_split(a, b):
    a_hi, a_lo = _split_bf16(a, 2)
    b_hi, b_lo = _split_bf16(b, 2)
    return _dot(a_hi, b_hi) + (_dot(a_hi, b_lo) + _dot(a_lo, b_hi))


def _gdn_prep_kernel(q_ref, k_ref, v_ref, gates_ref, u_ref, w_ref, qd_ref, kd_ref, qk_ref, *, refine):
    c = GDN_CHUNK
    hd = pl.program_id(1)
    tt = q_ref.shape[1]
    qt, kt, vt, gt = q_ref[0], k_ref[0], v_ref[0], gates_ref[0]
    lane = lax.broadcasted_iota(jnp.int32, (tt, LANES), 1)
    beta = jnp.sum(jnp.where(lane == hd, gt, 0.0), axis=1, keepdims=True)
    gcc = jnp.sum(jnp.where(lane == GDN_HEADS + hd, gt, 0.0), axis=1, keepdims=True)
    g_hi, g_mid, g_lo = (piece.astype(F32) for piece in _split_bf16(gcc, 3))
    pieces = jnp.where(lane == 0, g_hi, jnp.where(lane == 1, g_mid, jnp.where(lane == 2, g_lo, 0.0)))
    gc_row = _dot_nt(jnp.ones((tt, LANES), BF16), pieces.astype(BF16))

    ri = lax.broadcasted_iota(jnp.int32, (tt, tt), 0)
    ci = lax.broadcasted_iota(jnp.int32, (tt, tt), 1)
    chunk_start = ri - ri % c
    incl = lambda a: jnp.where(ci <= ri, jnp.where(ci >= chunk_start, a, 0.0), 0.0)
    strict = lambda a: jnp.where(ci < ri, jnp.where(ci >= chunk_start, a, 0.0), 0.0)
    ident = jnp.where(ri == ci, 1.0, 0.0)
    decay = incl(jnp.exp(incl(gcc - gc_row)))
    kb = kt * beta
    k16 = kt.astype(BF16)
    lower = strict(_dot_nt(kb.astype(BF16), k16) * decay)
    inv = ident - lower
    l16 = lower.astype(BF16)
    power = _dot(l16, l16).astype(BF16)
    n_sq = int(math.log2(c)) - 1
    for step in range(n_sq):
        if step < n_sq - 1:
            r = _dot(jnp.concatenate([power, inv.astype(BF16)], axis=0), power)
            power, inv = r[:tt].astype(BF16), inv + r[tt:]
        else:
            inv = inv + _dot(inv.astype(BF16), power)
    eg = jnp.exp(gcc)
    rhs = jnp.concatenate([vt * beta, kb * eg], axis=1)
    inv16 = inv.astype(BF16)
    sol = _dot(inv16, rhs.astype(BF16))
    if refine:
        resid = rhs - _dot_split(ident + lower, sol)
        sol = sol + _dot(inv16, resid.astype(BF16))
    u_ref[0] = sol[:, :HEAD_LANES]
    w_ref[0] = sol[:, HEAD_LANES:].astype(BF16)
    qk = incl(_dot_nt(qt.astype(BF16), k16) * decay)
    qd_ref[0] = (qt * eg).astype(BF16)
    for n in range(tt // c):
        rows = slice(n * c, (n + 1) * c)
        qk_ref[0, 0, rows, :] = qk[rows, rows].astype(BF16)
        kd_ref[0, rows, :] = (kt[rows] * jnp.exp(gcc[(n + 1) * c - 1:(n + 1) * c, :] - gcc[rows])).astype(BF16)


def _gdn_prep(qkv, gates, *, tt=256):
    b, t, _ = qkv.shape
    nh = GDN_HEADS
    blk = lambda off: pl.BlockSpec((1, tt, HEAD_LANES), lambda bi, h, i: (bi, i, off + h))
    out_blk = pl.BlockSpec((1, tt, HEAD_LANES), lambda bi, h, i: (bi, i, h))
    return pl.pallas_call(
        functools.partial(_gdn_prep_kernel, refine=GDN_REFINE),
        grid=(b, nh, t // tt),
        in_specs=[blk(0), blk(nh), blk(2 * nh), pl.BlockSpec((1, tt, LANES), lambda bi, h, i: (bi, i, 0))],
        out_specs=[out_blk] * 4 + [pl.BlockSpec((1, 1, tt, GDN_CHUNK), lambda bi, h, i: (bi, h, i, 0))],
        out_shape=[jax.ShapeDtypeStruct((b, t, nh * HEAD_LANES), F32)]
        + [jax.ShapeDtypeStruct((b, t, nh * HEAD_LANES), BF16)] * 3
        + [jax.ShapeDtypeStruct((b, nh, t, GDN_CHUNK), BF16)],
        compiler_params=_params(("arbitrary", "arbitrary", "arbitrary")),
        name="gdn_prep",
    )(qkv, qkv, qkv, gates)


def _gdn_scan_kernel(u_ref, w_ref, qd_ref, kd_ref, qk_ref, gates_ref, z_ref, nw_ref, o_ref, s_ref):
    c = GDN_CHUNK
    tt = u_ref.shape[1]

    @pl.when(pl.program_id(1) == 0)
    def _():
        s_ref[...] = jnp.zeros_like(s_ref)

    def chunk(n, _):
        r0 = pl.multiple_of(n * c, c)
        rows = pl.ds(r0, c)
        g_last = gates_ref[0, pl.ds(r0 + c - SUBLANES, SUBLANES), :]
        for hd in range(GDN_HEADS):
            cols = slice(hd * HEAD_LANES, (hd + 1) * HEAD_LANES)
            state = s_ref[hd]
            wq = jnp.concatenate([w_ref[0, rows, cols], qd_ref[0, rows, cols]], axis=0)
            r = _dot(wq, state.astype(BF16))
            v_new = (u_ref[0, rows, cols] - r[:c]).astype(BF16)
            o = r[c:] + _dot(qk_ref[0, hd, rows, :], v_new)
            decay_last = jnp.exp(g_last[SUBLANES - 1:SUBLANES, GDN_HEADS + hd:GDN_HEADS + hd + 1])
            s_ref[hd] = state * decay_last + _dot_tn(kd_ref[0, rows, cols], v_new)
            zt = z_ref[0, rows, cols].astype(F32)
            o_ref[0, rows, cols] = (_rms(o, nw_ref[...]) * (zt * _sigmoid(zt))).astype(o_ref.dtype)
        return 0

    lax.fori_loop(0, tt // c, chunk, 0)


def _gdn_scan(u, w, qd, kd, qk, gates, z, norm_w, *, tt=512):
    b, t, dm = u.shape
    nh = GDN_HEADS
    blk = pl.BlockSpec((1, tt, dm), lambda bi, i: (bi, i, 0))
    return pl.pallas_call(
        _gdn_scan_kernel,
        grid=(b, t // tt),
        in_specs=[blk, blk, blk, blk,
                  pl.BlockSpec((1, nh, tt, GDN_CHUNK), lambda bi, i: (bi, 0, i, 0)),
                  pl.BlockSpec((1, tt, LANES), lambda bi, i: (bi, i, 0)),
                  blk, _const_spec((1, HEAD_LANES))],
        out_specs=blk,
        out_shape=jax.ShapeDtypeStruct((b, t, dm), BF16),
        scratch_shapes=[pltpu.VMEM((nh, GDN_HEAD_DIM, GDN_HEAD_DIM), F32)],
        compiler_params=_params(("arbitrary", "arbitrary")),
        name="gdn_scan",
    )(u, w, qd, kd, qk, gates, z, norm_w)


def _post_kernel(*refs, n_mix, final_norm):
    x_ref = refs[0]
    mix_refs = refs[1:1 + n_mix]
    wout_ref, g_ref, wup_ref, wdn_ref, p_ref, wpp_ref, wpg_ref = refs[1 + n_mix:8 + n_mix]
    rest = refs[8 + n_mix:]
    if final_norm:
        gf_ref, o_ref = rest
    else:
        (o_ref,) = rest
    mix = mix_refs[0][...] if n_mix == 1 else jnp.concatenate([r[...] for r in mix_refs], axis=1)
    x = x_ref[...] + _dot(mix, wout_ref[...])
    h = _rms(x, g_ref[...]).astype(BF16)
    d_ff = wup_ref.shape[1]
    acc = x
    for s in range(d_ff // FF_SEG):
        a = jnp.maximum(_dot(h, wup_ref[:, s * FF_SEG:(s + 1) * FF_SEG]), 0.0)
        acc = acc + _dot((a * a).astype(BF16), wdn_ref[s * FF_SEG:(s + 1) * FF_SEG, :])
    x = acc
    gate = _sigmoid(_dot(x.astype(BF16), wpg_ref[...]))
    x = x + _dot(p_ref[...].astype(BF16), wpp_ref[...]) * gate
    if final_norm:
        x = _rms(x, gf_ref[...])
    o_ref[...] = x


def _post(x2d, mixes, wout, g, wup, wdn, p2d, wpp, wpg, gf=None):
    m, d = x2d.shape
    tm = TOKEN_TILE
    row = lambda i: (i, 0)
    single = pl.Buffered(1)
    const = lambda a: pl.BlockSpec(a.shape, lambda i: (0, 0), pipeline_mode=single)
    args = [x2d, *mixes, wout, g, wup, wdn, p2d, wpp, wpg]
    in_specs = ([pl.BlockSpec((tm, d), row)]
                + [pl.BlockSpec((tm, a.shape[1]), row) for a in mixes]
                + [const(wout), const(g), const(wup), const(wdn), pl.BlockSpec((tm, p2d.shape[1]), row), const(wpp), const(wpg)])
    if gf is not None:
        args.append(gf)
        in_specs.append(const(gf))
    kern = functools.partial(_post_kernel, n_mix=len(mixes), final_norm=gf is not None)
    return pl.pallas_call(
        kern,
        grid=(m // tm,),
        in_specs=in_specs,
        out_specs=pl.BlockSpec((tm, d), row),
        out_shape=jax.ShapeDtypeStruct((m, d), F32),
        compiler_params=_params(("arbitrary",)),
        name="out_proj_mlp_ple",
    )(*args)


def _pad_lanes(a):
    return jnp.pad(a, ((0, 0), (0, LANES - a.shape[1])))


def kernel(x, p, positions, norm_mix, norm_mlp, norm_final, w_in_even, conv_w, a_log, dt_bias, gdn_norm,
           lam_q1, lam_k1, lam_q2, lam_k2, diff_norm, w_out_even, w_in_odd, b_forget, w_out_odd,
           w_mlp_up, w_mlp_down, w_ple_proj, w_ple_gate):
    b, t, d = x.shape
    depth = p.shape[0]
    m = b * t
    assert t % TOKEN_TILE == 0 and d % PROJ_SEG == 0
    nh = GDN_HEADS
    gdn_w = 3 * nh * GDN_HEAD_DIM + nh * GDN_HEAD_DIM
    assert w_in_even.shape[2] == gdn_w + 2 * nh + 3 * DIFF_HEADS * 2 * DIFF_QK_DIM

    inv_freq = ROPE_THETA ** (-jnp.arange(0, DIFF_QK_DIM, 2, dtype=F32) / DIFF_QK_DIM)
    ang = positions.astype(F32)[..., None] * inv_freq
    cos, sin = jnp.cos(ang), jnp.sin(ang)
    cos_t = jnp.concatenate([cos, cos, cos, cos], axis=-1).reshape(m, LANES)
    sin_t = jnp.concatenate([-sin, sin, -sin, sin], axis=-1).reshape(m, LANES)

    x2d = x.reshape(m, d)
    for i in range(depth):
        j = i // 2
        g_mix = norm_mix[i].reshape(1, d)
        if i % 2 == 0:
            lambda_init = 0.8 - 0.6 * math.exp(-0.3 * i)
            w = w_in_even[j]
            wm = jnp.concatenate([w[:, :gdn_w], w[:, gdn_w + 2 * nh:]], axis=1).astype(BF16)
            wg = _pad_lanes(w[:, gdn_w:gdn_w + 2 * nh]).astype(BF16)
            alog_row = _pad_lanes(jnp.concatenate([jnp.zeros((nh,), F32), a_log[j]]).reshape(1, 2 * nh))
            dt_row = _pad_lanes(jnp.concatenate([jnp.zeros((nh,), F32), dt_bias[j]]).reshape(1, 2 * nh))
            qkv, z, qkb, vb, gates = _even_in(x2d, g_mix, wm, wg, conv_w[j], alog_row, dt_row, cos_t, sin_t, t)
            qkv, z, qkb, vb, gates = (a.reshape(b, t, -1) for a in (qkv, z, qkb, vb, gates))
            u, wy, qd, kd, qk = _gdn_prep(qkv, gates)
            o_a = _gdn_scan(u, wy, qd, kd, qk, gates, z, gdn_norm[j].reshape(1, HEAD_LANES))
            lam_params = jnp.stack([lam_q1[j], lam_k1[j], lam_q2[j], lam_k2[j]])
            o_b = _diff_attention(qkb, vb, lam_params, diff_norm[j].reshape(1, HEAD_LANES), lambda_init)
            mixes = [o_a.reshape(m, -1), o_b.reshape(m, -1)]
            wout = w_out_even[j].astype(BF16)
        else:
            w = w_in_odd[j]
            d_mix = (w.shape[1] - FOX_HEADS) // 4
            wm = w[:, :4 * d_mix].astype(BF16)
            wf = _pad_lanes(w[:, 4 * d_mix:]).astype(BF16)
            bf_row = _pad_lanes(b_forget[j].reshape(1, FOX_HEADS))
            q, k, v, gate, cum = _odd_in(x2d, g_mix, wm, wf, bf_row, t)
            q, k, v, gate, cum = (a.reshape(b, t, -1) for a in (q, k, v, gate, cum))
            cum_row = jnp.swapaxes(cum[:, :, :SUBLANES], 1, 2)
            o = _fox_attention(q, k, v, gate, cum, cum_row)
            mixes = [o.reshape(m, -1)]
            wout = w_out_odd[j].astype(BF16)
        x2d = _post(x2d, mixes, wout, norm_mlp[i].reshape(1, d), w_mlp_up[i].astype(BF16),
                    w_mlp_down[i].astype(BF16), p[i].reshape(m, -1), w_ple_proj[i].astype(BF16),
                    w_ple_gate[i].astype(BF16), norm_final.reshape(1, d) if i == depth - 1 else None)
    return x2d.reshape(b, t, d)
```

```python
import functools
import math

import jax
import jax.numpy as jnp
from jax import lax
from jax.experimental import pallas as pl
from jax.experimental.pallas import tpu as pltpu

F32 = jnp.float32
BF16 = jnp.bfloat16

GDN_HEADS = 4
GDN_HEAD_DIM = 128
GDN_CHUNK = 64
CONV_WIDTH = 4
DIFF_HEADS = 4
DIFF_QK_DIM = 64
FOX_HEADS = 8
HEAD_LANES = 128
ROPE_THETA = 10000.0
EPS = 1e-6
NEG_INF = -1e30
LOG2E = 1.4426950408889634
LANES = 128
SUBLANES = 8
VMEM_LIMIT_BYTES = 56 * 1024 * 1024

TOKEN_TILE = 512
PROJ_SEG = 512
FF_SEG = 1024
GDN_REFINE = True
ATTN_HEADS_PER_STEP = 2


def _dot(a, b):
    return jnp.dot(a, b, preferred_element_type=F32)


def _dot_exact(a, b):
    return jnp.dot(a, b, preferred_element_type=F32, precision=lax.Precision.HIGHEST)


def _dot_nt(a, b):
    return lax.dot_general(a, b, (((1,), (1,)), ((), ())), preferred_element_type=F32)


def _dot_tn(a, b):
    return lax.dot_general(a, b, (((0,), (0,)), ((), ())), preferred_element_type=F32)


def _rms(x, g):
    return x * lax.rsqrt(jnp.mean(x * x, axis=-1, keepdims=True) + EPS) * g


def _sigmoid(x):
    return 1.0 / (1.0 + jnp.exp(-x))


def _softplus(x):
    return jnp.maximum(x, 0.0) + jnp.log1p(jnp.exp(-jnp.abs(x)))


def _row_scan(x, period):
    rows = lax.broadcasted_iota(jnp.int32, x.shape, 0) % period
    s = 1
    while s < period:
        x = x + jnp.where(rows >= s, pltpu.roll(x, s, 0), 0.0)
        s *= 2
    return x


def _const_spec(shape):
    return pl.BlockSpec(shape, lambda *_: (0,) * len(shape))


def _params(sem):
    return pltpu.CompilerParams(dimension_semantics=sem, vmem_limit_bytes=VMEM_LIMIT_BYTES)


def _even_in_kernel(x_ref, g_ref, wm_ref, wg_ref, conv_ref, alog_ref, dt_ref, cos_ref, sin_ref,
                    qkv_ref, z_ref, qkb_ref, vb_ref, gates_ref, h_ref, carry_ref, *, tiles_per_seq):
    tm = x_ref.shape[0]
    i = pl.program_id(0)
    h_ref[...] = _rms(x_ref[...], g_ref[...]).astype(BF16)
    seq_start = (i % tiles_per_seq) == 0

    row8 = lax.broadcasted_iota(jnp.int32, (SUBLANES, PROJ_SEG), 0)
    for s in range(3):
        cols = slice(s * PROJ_SEG, (s + 1) * PROJ_SEG)
        y = _dot(h_ref[...], wm_ref[:, cols])
        prev8 = jnp.where(seq_start, 0.0, carry_ref[:, cols])
        carry_ref[:, cols] = y[tm - SUBLANES:, :]
        w = conv_ref[:, cols]
        acc = y * w[CONV_WIDTH - 1:CONV_WIDTH, :]
        top = y[:SUBLANES, :] * w[CONV_WIDTH - 1:CONV_WIDTH, :]
        for k in range(1, CONV_WIDTH):
            wk = w[CONV_WIDTH - 1 - k:CONV_WIDTH - k, :]
            rolled = pltpu.roll(y, k, 0)
            acc = acc + rolled * wk
            top = top + jnp.where(row8 < k, pltpu.roll(prev8, k, 0), rolled[:SUBLANES, :]) * wk
        for part, rows in ((acc, slice(0, tm)), (top, slice(0, SUBLANES))):
            a = part * _sigmoid(part)
            if s < 2:
                outs = []
                for hd in range(GDN_HEADS):
                    seg = a[:, hd * HEAD_LANES:(hd + 1) * HEAD_LANES]
                    n = seg * lax.rsqrt(jnp.sum(seg * seg, axis=-1, keepdims=True) + EPS)
                    outs.append(n * (GDN_HEAD_DIM ** -0.5) if s == 0 else n)
                a = jnp.concatenate(outs, axis=1)
            qkv_ref[rows, cols] = a

    z_ref[...] = _dot(h_ref[...], wm_ref[:, 3 * PROJ_SEG:4 * PROJ_SEG]).astype(BF16)

    cos = jnp.concatenate([cos_ref[...]] * (PROJ_SEG // LANES), axis=1)
    sin = jnp.concatenate([sin_ref[...]] * (PROJ_SEG // LANES), axis=1)
    lane = lax.broadcasted_iota(jnp.int32, (tm, PROJ_SEG), 1)
    first_half = (lane % DIFF_QK_DIM) < (DIFF_QK_DIM // 2)
    for s, scale in ((4, DIFF_QK_DIM ** -0.5 * LOG2E), (5, 1.0)):
        y = _dot(h_ref[...], wm_ref[:, s * PROJ_SEG:(s + 1) * PROJ_SEG])
        swapped = jnp.where(first_half, pltpu.roll(y, PROJ_SEG - DIFF_QK_DIM // 2, 1),
                            pltpu.roll(y, DIFF_QK_DIM // 2, 1))
        r = y * cos + swapped * sin
        qkb_ref[:, (s - 4) * PROJ_SEG:(s - 3) * PROJ_SEG] = (r * scale).astype(BF16)
    vb_ref[...] = _dot(h_ref[...], wm_ref[:, 6 * PROJ_SEG:7 * PROJ_SEG]).astype(BF16)

    graw = _dot(h_ref[...], wg_ref[...])
    beta = _sigmoid(graw)
    g = -jnp.exp(alog_ref[...]) * _softplus(graw + dt_ref[...])
    gc = _row_scan(g, GDN_CHUNK)
    lane_g = lax.broadcasted_iota(jnp.int32, (tm, LANES), 1)
    gates_ref[...] = jnp.where(lane_g < GDN_HEADS, beta, gc)


def _even_in(x2d, g, wm, wg, conv_w, alog_row, dt_row, cos_t, sin_t, seq_len):
    m, d = x2d.shape
    tm = TOKEN_TILE
    n_main = wm.shape[1]
    kern = functools.partial(_even_in_kernel, tiles_per_seq=seq_len // tm)
    row = lambda i: (i, 0)
    return pl.pallas_call(
        kern,
        grid=(m // tm,),
        in_specs=[
            pl.BlockSpec((tm, d), row),
            _const_spec((1, d)),
            _const_spec((d, n_main)),
            _const_spec((d, LANES)),
            _const_spec(conv_w.shape),
            _const_spec((1, LANES)),
            _const_spec((1, LANES)),
            pl.BlockSpec((tm, LANES), row),
            pl.BlockSpec((tm, LANES), row),
        ],
        out_specs=[
            pl.BlockSpec((tm, 3 * PROJ_SEG), row),
            pl.BlockSpec((tm, PROJ_SEG), row),
            pl.BlockSpec((tm, 2 * PROJ_SEG), row),
            pl.BlockSpec((tm, PROJ_SEG), row),
            pl.BlockSpec((tm, LANES), row),
        ],
        out_shape=[
            jax.ShapeDtypeStruct((m, 3 * PROJ_SEG), F32),
            jax.ShapeDtypeStruct((m, PROJ_SEG), BF16),
            jax.ShapeDtypeStruct((m, 2 * PROJ_SEG), BF16),
            jax.ShapeDtypeStruct((m, PROJ_SEG), BF16),
            jax.ShapeDtypeStruct((m, LANES), F32),
        ],
        scratch_shapes=[pltpu.VMEM((tm, d), BF16), pltpu.VMEM((SUBLANES, 3 * PROJ_SEG), F32)],
        compiler_params=_params(("arbitrary",)),
        name="even_in_proj",
    )(x2d, g, wm, wg, conv_w, alog_row, dt_row, cos_t, sin_t)


def _odd_in_kernel(x_ref, g_ref, wm_ref, wf_ref, bf_ref, q_ref, k_ref, v_ref, gate_ref, cum_ref,
                   h_ref, carry_ref, *, tiles_per_seq, d_mix):
    tm = x_ref.shape[0]
    i = pl.program_id(0)
    h_ref[...] = _rms(x_ref[...], g_ref[...]).astype(BF16)
    head_dim = d_mix // FOX_HEADS
    for o_ref, base, scale in ((q_ref, 0, head_dim ** -0.5 * LOG2E), (k_ref, d_mix, 1.0),
                               (v_ref, 2 * d_mix, 1.0), (gate_ref, 3 * d_mix, 1.0)):
        for s in range(d_mix // PROJ_SEG):
            cols = slice(s * PROJ_SEG, (s + 1) * PROJ_SEG)
            y = _dot(h_ref[...], wm_ref[:, base + s * PROJ_SEG:base + (s + 1) * PROJ_SEG])
            o_ref[:, cols] = (y * scale).astype(BF16)
    f = _dot(h_ref[...], wf_ref[...]) + bf_ref[...]
    log_f = jnp.minimum(f, 0.0) - jnp.log1p(jnp.exp(-jnp.abs(f)))
    prev = jnp.where((i % tiles_per_seq) == 0, 0.0, carry_ref[0:1, :])
    cum = _row_scan(log_f, tm) + prev
    cum_ref[...] = cum
    carry_ref[...] = jnp.broadcast_to(cum[tm - 1:tm, :], carry_ref.shape)


def _odd_in(x2d, g, wm, wf, bf_row, seq_len):
    m, d = x2d.shape
    tm = TOKEN_TILE
    d_mix = wm.shape[1] // 4
    kern = functools.partial(_odd_in_kernel, tiles_per_seq=seq_len // tm, d_mix=d_mix)
    row = lambda i: (i, 0)
    return pl.pallas_call(
        kern,
        grid=(m // tm,),
        in_specs=[
            pl.BlockSpec((tm, d), row),
            _const_spec((1, d)),
            _const_spec(wm.shape),
            _const_spec((d, LANES)),
            _const_spec((1, LANES)),
        ],
        out_specs=[pl.BlockSpec((tm, d_mix), row)] * 4 + [pl.BlockSpec((tm, LANES), row)],
        out_shape=[jax.ShapeDtypeStruct((m, d_mix), BF16)] * 4 + [jax.ShapeDtypeStruct((m, LANES), F32)],
        scratch_shapes=[pltpu.VMEM((tm, d), BF16), pltpu.VMEM((SUBLANES, LANES), F32)],
        compiler_params=_params(("arbitrary",)),
        name="odd_in_proj",
    )(x2d, g, wm, wf, bf_row)


def _bias_lanes(c, pieces_first):
    hi, mid, lo = (piece.astype(F32) for piece in _split_bf16(c, 3))
    lane = lax.broadcasted_iota(jnp.int32, (c.shape[0], LANES), 1)
    first = 0 if pieces_first else 3
    vals = jnp.where(lane == first, hi, jnp.where(lane == first + 1, mid, jnp.where(lane == first + 2, lo, 0.0)))
    ones_at = 3 - first
    vals = jnp.where((lane >= ones_at) & (lane < ones_at + 3), 1.0, vals)
    return vals.astype(BF16)


def _attn_kernel(*refs, tq, tk, n_maps, fox, lambda_init, hp):
    if fox:
        q_ref, k_ref, v_ref, gate_ref, cumq_ref, cumk_ref, o_ref, vt_ref, kb_ref = refs
    else:
        q_ref, k_ref, v_ref, lam_ref, nw_ref, o_ref, vt_ref = refs
    assert tq == tk and hp == 2
    hg = pl.program_id(1)
    qi = pl.program_id(2)
    t = k_ref.shape[1]
    head = lambda g: slice(g * HEAD_LANES, (g + 1) * HEAD_LANES)

    def head_col(x, hd):
        lane = lax.broadcasted_iota(jnp.int32, x.shape, 1)
        return jnp.sum(jnp.where(lane == hd, x, 0.0), axis=1, keepdims=True)

    @pl.when(qi == 0)
    def _():
        for g in range(hp):
            for r in range(t // LANES):
                rows = slice(r * LANES, (r + 1) * LANES)
                vt_ref[head(g), rows] = v_ref[0, rows, head(g)].astype(F32).T.astype(BF16)
            if fox:
                kb_ref[:, head(g)] = _bias_lanes(-LOG2E * head_col(cumk_ref[0], hg * hp + g), True)

    qs = []
    for g in range(hp):
        q = q_ref[0, :, head(g)]
        if fox:
            q = jnp.concatenate([q, _bias_lanes(LOG2E * head_col(cumq_ref[0], hg * hp + g), False)], axis=1)
        if n_maps == 2:
            lane = lax.broadcasted_iota(jnp.int32, q.shape, 1)
            zero = jnp.zeros_like(q)
            q = jnp.concatenate([jnp.where(lane < DIFF_QK_DIM, q, zero),
                                 jnp.where(lane >= DIFF_QK_DIM, q, zero)], axis=0)
        qs.append(q)
    cols = n_maps * tq

    def scores(g, j):
        k0 = pl.multiple_of(j * tk, tk)
        kj = k_ref[0, pl.ds(k0, tk), head(g)]
        if fox:
            kj = jnp.concatenate([kj, kb_ref[pl.ds(k0, tk), head(g)]], axis=1)
        return _dot_nt(kj, qs[g])

    def absorb(g, j, st, state, masked):
        m, l, acc = state
        k0 = pl.multiple_of(j * tk, tk)
        if masked:
            kpos = k0 + lax.broadcasted_iota(jnp.int32, (tk, cols), 0)
            qpos = qi * tq + lax.broadcasted_iota(jnp.int32, (tk, cols), 1) % tq
            st = jnp.where(qpos >= kpos, st, NEG_INF)
        m_new = jnp.maximum(m, jnp.max(st, axis=0, keepdims=True))
        alpha = jnp.exp2(m - m_new)
        p = jnp.exp2(st - m_new)
        l = alpha * l + jnp.sum(p, axis=0, keepdims=True)
        acc = alpha * acc + _dot(vt_ref[head(g), pl.ds(k0, tk)], p.astype(BF16))
        return m_new, l, acc

    def step(j, carry):
        st0, s0, s1 = carry
        st1 = scores(1, j)
        s0 = absorb(0, j, st0, s0, False)
        st0 = scores(0, j + 1)
        s1 = absorb(1, j, st1, s1, False)
        return st0, s0, s1

    init = (jnp.full((1, cols), NEG_INF, F32), jnp.zeros((1, cols), F32), jnp.zeros((HEAD_LANES, cols), F32))
    st0, s0, s1 = lax.fori_loop(0, qi, step, (scores(0, 0), init, init))
    st1 = scores(1, qi)
    carry = (absorb(0, qi, st0, s0, True), absorb(1, qi, st1, s1, True))
    if not fox:
        lam_p = lam_ref[...]
        lam = (jnp.exp(jnp.sum(lam_p[0:1] * lam_p[1:2], axis=1, keepdims=True))
               - jnp.exp(jnp.sum(lam_p[2:3] * lam_p[3:4], axis=1, keepdims=True)) + lambda_init)
    for g in range(hp):
        _, l, acc = carry[g]
        ot = acc / l
        if fox:
            o = ot.T * _sigmoid(gate_ref[0, :, head(g)].astype(F32))
        else:
            o = (ot[:, :tq] - lam * ot[:, tq:]).T
            o = _rms(o, nw_ref[...]) * (1.0 - lambda_init)
        o_ref[0, :, head(g)] = o.astype(o_ref.dtype)


def _fox_attention(q, k, v, gate, cum, *, tq=512, tk=512, hp=ATTN_HEADS_PER_STEP):
    b, t, dm = q.shape
    nh = dm // HEAD_LANES
    kern = functools.partial(_attn_kernel, tq=tq, tk=tk, n_maps=1, fox=True, lambda_init=0.0, hp=hp)
    width = hp * HEAD_LANES
    qblk = pl.BlockSpec((1, tq, width), lambda bi, h, i: (bi, i, h))
    kvblk = pl.BlockSpec((1, t, width), lambda bi, h, i: (bi, 0, h))
    return pl.pallas_call(
        kern,
        grid=(b, nh // hp, t // tq),
        in_specs=[qblk, kvblk, kvblk, qblk,
                  pl.BlockSpec((1, tq, LANES), lambda bi, h, i: (bi, i, 0)),
                  pl.BlockSpec((1, t, LANES), lambda bi, h, i: (bi, 0, 0))],
        out_specs=qblk,
        out_shape=jax.ShapeDtypeStruct((b, t, dm), BF16),
        scratch_shapes=[pltpu.VMEM((width, t), BF16), pltpu.VMEM((t, width), BF16)],
        compiler_params=_params(("arbitrary", "arbitrary", "arbitrary")),
        name="fox_attention",
    )(q, k, v, gate, cum, cum)


def _diff_attention(qk, v, lam_params, norm_w, lambda_init, *, tq=256, tk=256, hp=ATTN_HEADS_PER_STEP):
    b, t, _ = qk.shape
    nh = DIFF_HEADS
    groups = nh // hp
    kern = functools.partial(_attn_kernel, tq=tq, tk=tk, n_maps=2, fox=False, lambda_init=lambda_init, hp=hp)
    width = hp * HEAD_LANES
    qblk = pl.BlockSpec((1, tq, width), lambda bi, h, i: (bi, i, h))
    kblk = pl.BlockSpec((1, t, width), lambda bi, h, i: (bi, 0, groups + h))
    vblk = pl.BlockSpec((1, t, width), lambda bi, h, i: (bi, 0, h))
    return pl.pallas_call(
        kern,
        grid=(b, groups, t // tq),
        in_specs=[qblk, kblk, vblk, _const_spec(lam_params.shape), _const_spec((1, HEAD_LANES))],
        out_specs=qblk,
        out_shape=jax.ShapeDtypeStruct((b, t, nh * HEAD_LANES), BF16),
        scratch_shapes=[pltpu.VMEM((width, t), BF16)],
        compiler_params=_params(("arbitrary", "arbitrary", "arbitrary")),
        name="diff_attention",
    )(qk, qk, v, lam_params, norm_w)


def _split_bf16(x, parts):
    out = []
    for _ in range(parts):
        piece = x.astype(BF16)
        out.append(piece)
        x = x - piece.astype(F32)
    return out


def _dot_split(a, b):
    a_hi, a_lo = _split_bf16(a, 2)
    b_hi, b_lo = _split_bf16(b, 2)
    return _dot(a_hi, b_hi) + (_dot(a_hi, b_lo) + _dot(a_lo, b_hi))


def _gdn_prep_kernel(q_ref, k_ref, v_ref, gates_ref, u_ref, w_ref, qd_ref, kd_ref, qk_ref, *, refine):
    c = GDN_CHUNK
    tt = q_ref.shape[1]
    gt = gates_ref[0]
    lane = lax.broadcasted_iota(jnp.int32, (tt, LANES), 1)
    ri = lax.broadcasted_iota(jnp.int32, (tt, tt), 0)
    ci = lax.broadcasted_iota(jnp.int32, (tt, tt), 1)
    chunk_start = ri - ri % c
    incl = lambda a: jnp.where(ci <= ri, jnp.where(ci >= chunk_start, a, 0.0), 0.0)
    strict = lambda a: jnp.where(ci < ri, jnp.where(ci >= chunk_start, a, 0.0), 0.0)
    ident = jnp.where(ri == ci, 1.0, 0.0)
    ones = jnp.ones((tt, LANES), BF16)
    n_sq = int(math.log2(c)) - 1
    for hd in range(GDN_HEADS):
        cols = slice(hd * HEAD_LANES, (hd + 1) * HEAD_LANES)
        qt, kt, vt = q_ref[0, :, cols], k_ref[0, :, cols], v_ref[0, :, cols]
        beta = gt[:, hd:hd + 1]
        gcc = gt[:, GDN_HEADS + hd:GDN_HEADS + hd + 1]
        g_hi, g_mid, g_lo = (piece.astype(F32) for piece in _split_bf16(gcc, 3))
        pieces = jnp.where(lane == 0, g_hi, jnp.where(lane == 1, g_mid, jnp.where(lane == 2, g_lo, 0.0)))
        gc_row = _dot_nt(ones, pieces.astype(BF16))
        decay = incl(jnp.exp(incl(gcc - gc_row)))
        kb = kt * beta
        k16 = kt.astype(BF16)
        lower = strict(_dot_nt(kb.astype(BF16), k16) * decay)
        inv = ident - lower
        l16 = lower.astype(BF16)
        power = _dot(l16, l16).astype(BF16)
        for step in range(n_sq):
            if step < n_sq - 1:
                r = _dot(jnp.concatenate([power, inv.astype(BF16)], axis=0), power)
                power, inv = r[:tt].astype(BF16), inv + r[tt:]
            else:
                inv = inv + _dot(inv.astype(BF16), power)
        eg = jnp.exp(gcc)
        rhs = jnp.concatenate([vt * beta, kb * eg], axis=1)
        inv16 = inv.astype(BF16)
        sol = _dot(inv16, rhs.astype(BF16))
        if refine:
            resid = rhs - _dot_split(ident + lower, sol)
            sol = sol + _dot(inv16, resid.astype(BF16))
        u_ref[0, :, cols] = sol[:, :HEAD_LANES]
        w_ref[0, :, cols] = sol[:, HEAD_LANES:].astype(BF16)
        qk = incl(_dot_nt(qt.astype(BF16), k16) * decay)
        qd_ref[0, :, cols] = (qt * eg).astype(BF16)
        for n in range(tt // c):
            rows = slice(n * c, (n + 1) * c)
            qk_ref[0, hd, rows, :] = qk[rows, rows].astype(BF16)
            kd_ref[0, rows, cols] = (kt[rows] * jnp.exp(gcc[(n + 1) * c - 1:(n + 1) * c, :] - gcc[rows])).astype(BF16)


def _gdn_prep(qkv, gates, *, tt=256):
    b, t, _ = qkv.shape
    nh = GDN_HEADS
    dm = nh * HEAD_LANES
    blk = lambda part: pl.BlockSpec((1, tt, dm), lambda bi, i: (bi, i, part))
    return pl.pallas_call(
        functools.partial(_gdn_prep_kernel, refine=GDN_REFINE),
        grid=(b, t // tt),
        in_specs=[blk(0), blk(1), blk(2), pl.BlockSpec((1, tt, LANES), lambda bi, i: (bi, i, 0))],
        out_specs=[blk(0)] * 4 + [pl.BlockSpec((1, nh, tt, GDN_CHUNK), lambda bi, i: (bi, 0, i, 0))],
        out_shape=[jax.ShapeDtypeStruct((b, t, dm), F32)]
        + [jax.ShapeDtypeStruct((b, t, dm), BF16)] * 3
        + [jax.ShapeDtypeStruct((b, nh, t, GDN_CHUNK), BF16)],
        compiler_params=_params(("arbitrary", "arbitrary")),
        name="gdn_prep",
    )(qkv, qkv, qkv, gates)


def _gdn_scan_kernel(u_ref, w_ref, qd_ref, kd_ref, qk_ref, gates_ref, z_ref, nw_ref, o_ref, s_ref):
    c = GDN_CHUNK
    tt = u_ref.shape[1]

    @pl.when(pl.program_id(1) == 0)
    def _():
        s_ref[...] = jnp.zeros_like(s_ref)

    def chunk(n, _):
        r0 = pl.multiple_of(n * c, c)
        rows = pl.ds(r0, c)
        g_last = gates_ref[0, pl.ds(r0 + c - SUBLANES, SUBLANES), :]
        for hd in range(GDN_HEADS):
            cols = slice(hd * HEAD_LANES, (hd + 1) * HEAD_LANES)
            state = s_ref[hd]
            wq = jnp.concatenate([w_ref[0, rows, cols], qd_ref[0, rows, cols]], axis=0)
            r = _dot(wq, state.astype(BF16))
            v_new = (u_ref[0, rows, cols] - r[:c]).astype(BF16)
            o = r[c:] + _dot(qk_ref[0, hd, rows, :], v_new)
            decay_last = jnp.exp(g_last[SUBLANES - 1:SUBLANES, GDN_HEADS + hd:GDN_HEADS + hd + 1])
            s_ref[hd] = state * decay_last + _dot_tn(kd_ref[0, rows, cols], v_new)
            zt = z_ref[0, rows, cols].astype(F32)
            o_ref[0, rows, cols] = (_rms(o, nw_ref[...]) * (zt * _sigmoid(zt))).astype(o_ref.dtype)
        return 0

    lax.fori_loop(0, tt // c, chunk, 0)


def _gdn_scan(u, w, qd, kd, qk, gates, z, norm_w, *, tt=512):
    b, t, dm = u.shape
    nh = GDN_HEADS
    blk = pl.BlockSpec((1, tt, dm), lambda bi, i: (bi, i, 0))
    return pl.pallas_call(
        _gdn_scan_kernel,
        grid=(b, t // tt),
        in_specs=[blk, blk, blk, blk,
                  pl.BlockSpec((1, nh, tt, GDN_CHUNK), lambda bi, i: (bi, 0, i, 0)),
                  pl.BlockSpec((1, tt, LANES), lambda bi, i: (bi, i, 0)),
                  blk, _const_spec((1, HEAD_LANES))],
        out_specs=blk,
        out_shape=jax.ShapeDtypeStruct((b, t, dm), BF16),
        scratch_shapes=[pltpu.VMEM((nh, GDN_HEAD_DIM, GDN_HEAD_DIM), F32)],
        compiler_params=_params(("arbitrary", "arbitrary")),
        name="gdn_scan",
    )(u, w, qd, kd, qk, gates, z, norm_w)


def _post_kernel(*refs, n_mix, final_norm):
    x_ref = refs[0]
    mix_refs = refs[1:1 + n_mix]
    wout_ref, g_ref, wup_ref, wdn_ref, p_ref, wpp_ref, wpg_ref = refs[1 + n_mix:8 + n_mix]
    rest = refs[8 + n_mix:]
    if final_norm:
        gf_ref, o_ref = rest
    else:
        (o_ref,) = rest
    mix = mix_refs[0][...] if n_mix == 1 else jnp.concatenate([r[...] for r in mix_refs], axis=1)
    x = x_ref[...] + _dot(mix, wout_ref[...])
    h = _rms(x, g_ref[...]).astype(BF16)
    d_ff = wup_ref.shape[1]
    acc = x
    for s in range(d_ff // FF_SEG):
        a = jnp.maximum(_dot(h, wup_ref[:, s * FF_SEG:(s + 1) * FF_SEG]), 0.0)
        acc = acc + _dot((a * a).astype(BF16), wdn_ref[s * FF_SEG:(s + 1) * FF_SEG, :])
    x = acc
    gate = _sigmoid(_dot(x.astype(BF16), wpg_ref[...]))
    x = x + _dot(p_ref[...].astype(BF16), wpp_ref[...]) * gate
    if final_norm:
        x = _rms(x, gf_ref[...])
    o_ref[...] = x


def _post(x2d, mixes, wout, g, wup, wdn, p2d, wpp, wpg, gf=None):
    m, d = x2d.shape
    tm = TOKEN_TILE
    row = lambda i: (i, 0)
    single = pl.Buffered(1)
    const = lambda a: pl.BlockSpec(a.shape, lambda i: (0, 0), pipeline_mode=single)
    args = [x2d, *mixes, wout, g, wup, wdn, p2d, wpp, wpg]
    in_specs = ([pl.BlockSpec((tm, d), row)]
                + [pl.BlockSpec((tm, a.shape[1]), row) for a in mixes]
                + [const(wout), const(g), const(wup), const(wdn), pl.BlockSpec((tm, p2d.shape[1]), row), const(wpp), const(wpg)])
    if gf is not None:
        args.append(gf)
        in_specs.append(const(gf))
    kern = functools.partial(_post_kernel, n_mix=len(mixes), final_norm=gf is not None)
    return pl.pallas_call(
        kern,
        grid=(m // tm,),
        in_specs=in_specs,
        out_specs=pl.BlockSpec((tm, d), row),
        out_shape=jax.ShapeDtypeStruct((m, d), F32),
        compiler_params=_params(("arbitrary",)),
        name="out_proj_mlp_ple",
    )(*args)


def _pad_lanes(a):
    return jnp.pad(a, ((0, 0), (0, LANES - a.shape[1])))


def kernel(x, p, positions, norm_mix, norm_mlp, norm_final, w_in_even, conv_w, a_log, dt_bias, gdn_norm,
           lam_q1, lam_k1, lam_q2, lam_k2, diff_norm, w_out_even, w_in_odd, b_forget, w_out_odd,
           w_mlp_up, w_mlp_down, w_ple_proj, w_ple_gate):
    b, t, d = x.shape
    depth = p.shape[0]
    m = b * t
    assert t % TOKEN_TILE == 0 and d % PROJ_SEG == 0
    nh = GDN_HEADS
    gdn_w = 3 * nh * GDN_HEAD_DIM + nh * GDN_HEAD_DIM
    assert w_in_even.shape[2] == gdn_w + 2 * nh + 3 * DIFF_HEADS * 2 * DIFF_QK_DIM

    inv_freq = ROPE_THETA ** (-jnp.arange(0, DIFF_QK_DIM, 2, dtype=F32) / DIFF_QK_DIM)
    ang = positions.astype(F32)[..., None] * inv_freq
    cos, sin = jnp.cos(ang), jnp.sin(ang)
    cos_t = jnp.concatenate([cos, cos, cos, cos], axis=-1).reshape(m, LANES)
    sin_t = jnp.concatenate([-sin, sin, -sin, sin], axis=-1).reshape(m, LANES)

    x2d = x.reshape(m, d)
    for i in range(depth):
        j = i // 2
        g_mix = norm_mix[i].reshape(1, d)
        if i % 2 == 0:
            lambda_init = 0.8 - 0.6 * math.exp(-0.3 * i)
            w = w_in_even[j]
            wm = jnp.concatenate([w[:, :gdn_w], w[:, gdn_w + 2 * nh:]], axis=1).astype(BF16)
            wg = _pad_lanes(w[:, gdn_w:gdn_w + 2 * nh]).astype(BF16)
            alog_row = _pad_lanes(jnp.concatenate([jnp.zeros((nh,), F32), a_log[j]]).reshape(1, 2 * nh))
            dt_row = _pad_lanes(jnp.concatenate([jnp.zeros((nh,), F32), dt_bias[j]]).reshape(1, 2 * nh))
            qkv, z, qkb, vb, gates = _even_in(x2d, g_mix, wm, wg, conv_w[j], alog_row, dt_row, cos_t, sin_t, t)
            qkv, z, qkb, vb, gates = (a.reshape(b, t, -1) for a in (qkv, z, qkb, vb, gates))
            u, wy, qd, kd, qk = _gdn_prep(qkv, gates)
            o_a = _gdn_scan(u, wy, qd, kd, qk, gates, z, gdn_norm[j].reshape(1, HEAD_LANES))
            lam_params = jnp.stack([lam_q1[j], lam_k1[j], lam_q2[j], lam_k2[j]])
            o_b = _diff_attention(qkb, vb, lam_params, diff_norm[j].reshape(1, HEAD_LANES), lambda_init)
            mixes = [o_a.reshape(m, -1), o_b.reshape(m, -1)]
            wout = w_out_even[j].astype(BF16)
        else:
            w = w_in_odd[j]
            d_mix = (w.shape[1] - FOX_HEADS) // 4
            wm = w[:, :4 * d_mix].astype(BF16)
            wf = _pad_lanes(w[:, 4 * d_mix:]).astype(BF16)
            bf_row = _pad_lanes(b_forget[j].reshape(1, FOX_HEADS))
            q, k, v, gate, cum = _odd_in(x2d, g_mix, wm, wf, bf_row, t)
            q, k, v, gate, cum = (a.reshape(b, t, -1) for a in (q, k, v, gate, cum))
            o = _fox_attention(q, k, v, gate, cum)
            mixes = [o.reshape(m, -1)]
            wout = w_out_odd[j].astype(BF16)
        x2d = _post(x2d, mixes, wout, norm_mlp[i].reshape(1, d), w_mlp_up[i].astype(BF16),
                    w_mlp_down[i].astype(BF16), p[i].reshape(m, -1), w_ple_proj[i].astype(BF16),
                    w_ple_gate[i].astype(BF16), norm_final.reshape(1, d) if i == depth - 1 else None)
    return x2d.reshape(b, t, d)
```

```python
import functools
import math

import jax
import jax.numpy as jnp
from jax import lax
from jax.experimental import pallas as pl
from jax.experimental.pallas import tpu as pltpu

F32 = jnp.float32
BF16 = jnp.bfloat16

GDN_HEADS = 4
GDN_HEAD_DIM = 128
GDN_CHUNK = 64
CONV_WIDTH = 4
DIFF_HEADS = 4
DIFF_QK_DIM = 64
FOX_HEADS = 8
HEAD_LANES = 128
ROPE_THETA = 10000.0
EPS = 1e-6
NEG_INF = -1e30
LOG2E = 1.4426950408889634
LANES = 128
SUBLANES = 8
VMEM_LIMIT_BYTES = 56 * 1024 * 1024

TOKEN_TILE = 512
PROJ_SEG = 512
FF_SEG = 1024
GDN_REFINE = True
ATTN_HEADS_PER_STEP = 2


def _dot(a, b):
    return jnp.dot(a, b, preferred_element_type=F32)


def _dot_exact(a, b):
    return jnp.dot(a, b, preferred_element_type=F32, precision=lax.Precision.HIGHEST)


def _dot_nt(a, b):
    return lax.dot_general(a, b, (((1,), (1,)), ((), ())), preferred_element_type=F32)


def _dot_tn(a, b):
    return lax.dot_general(a, b, (((0,), (0,)), ((), ())), preferred_element_type=F32)


def _rms(x, g):
    return x * lax.rsqrt(jnp.mean(x * x, axis=-1, keepdims=True) + EPS) * g


def _sigmoid(x):
    return 1.0 / (1.0 + jnp.exp(-x))


def _softplus(x):
    return jnp.maximum(x, 0.0) + jnp.log1p(jnp.exp(-jnp.abs(x)))


def _row_scan(x, period):
    rows = lax.broadcasted_iota(jnp.int32, x.shape, 0) % period
    s = 1
    while s < period:
        x = x + jnp.where(rows >= s, pltpu.roll(x, s, 0), 0.0)
        s *= 2
    return x


def _const_spec(shape):
    return pl.BlockSpec(shape, lambda *_: (0,) * len(shape))


def _params(sem):
    return pltpu.CompilerParams(dimension_semantics=sem, vmem_limit_bytes=VMEM_LIMIT_BYTES)


def _even_in_kernel(x_ref, g_ref, wm_ref, wg_ref, conv_ref, alog_ref, dt_ref, cos_ref, sin_ref,
                    qkv_ref, z_ref, qkb_ref, vb_ref, gates_ref, h_ref, carry_ref, *, tiles_per_seq):
    tm = x_ref.shape[0]
    i = pl.program_id(0)
    h_ref[...] = _rms(x_ref[...], g_ref[...]).astype(BF16)
    seq_start = (i % tiles_per_seq) == 0

    row8 = lax.broadcasted_iota(jnp.int32, (SUBLANES, PROJ_SEG), 0)
    for s in range(3):
        cols = slice(s * PROJ_SEG, (s + 1) * PROJ_SEG)
        y = _dot(h_ref[...], wm_ref[:, cols])
        prev8 = jnp.where(seq_start, 0.0, carry_ref[:, cols])
        carry_ref[:, cols] = y[tm - SUBLANES:, :]
        w = conv_ref[:, cols]
        acc = y * w[CONV_WIDTH - 1:CONV_WIDTH, :]
        top = y[:SUBLANES, :] * w[CONV_WIDTH - 1:CONV_WIDTH, :]
        for k in range(1, CONV_WIDTH):
            wk = w[CONV_WIDTH - 1 - k:CONV_WIDTH - k, :]
            rolled = pltpu.roll(y, k, 0)
            acc = acc + rolled * wk
            top = top + jnp.where(row8 < k, pltpu.roll(prev8, k, 0), rolled[:SUBLANES, :]) * wk
        for part, rows in ((acc, slice(0, tm)), (top, slice(0, SUBLANES))):
            a = part * _sigmoid(part)
            if s < 2:
                outs = []
                for hd in range(GDN_HEADS):
                    seg = a[:, hd * HEAD_LANES:(hd + 1) * HEAD_LANES]
                    n = seg * lax.rsqrt(jnp.sum(seg * seg, axis=-1, keepdims=True) + EPS)
                    outs.append(n * (GDN_HEAD_DIM ** -0.5) if s == 0 else n)
                a = jnp.concatenate(outs, axis=1)
            qkv_ref[rows, cols] = a

    z_ref[...] = _dot(h_ref[...], wm_ref[:, 3 * PROJ_SEG:4 * PROJ_SEG]).astype(BF16)

    cos = jnp.concatenate([cos_ref[...]] * (PROJ_SEG // LANES), axis=1)
    sin = jnp.concatenate([sin_ref[...]] * (PROJ_SEG // LANES), axis=1)
    lane = lax.broadcasted_iota(jnp.int32, (tm, PROJ_SEG), 1)
    first_half = (lane % DIFF_QK_DIM) < (DIFF_QK_DIM // 2)
    for s, scale in ((4, DIFF_QK_DIM ** -0.5 * LOG2E), (5, 1.0)):
        y = _dot(h_ref[...], wm_ref[:, s * PROJ_SEG:(s + 1) * PROJ_SEG])
        swapped = jnp.where(first_half, pltpu.roll(y, PROJ_SEG - DIFF_QK_DIM // 2, 1),
                            pltpu.roll(y, DIFF_QK_DIM // 2, 1))
        r = y * cos + swapped * sin
        qkb_ref[:, (s - 4) * PROJ_SEG:(s - 3) * PROJ_SEG] = (r * scale).astype(BF16)
    vb_ref[...] = _dot(h_ref[...], wm_ref[:, 6 * PROJ_SEG:7 * PROJ_SEG]).astype(BF16)

    graw = _dot(h_ref[...], wg_ref[...])
    beta = _sigmoid(graw)
    g = -jnp.exp(alog_ref[...]) * _softplus(graw + dt_ref[...])
    gc = _row_scan(g, GDN_CHUNK)
    lane_g = lax.broadcasted_iota(jnp.int32, (tm, LANES), 1)
    gates_ref[...] = jnp.where(lane_g < GDN_HEADS, beta, gc)


def _even_in(x2d, g, wm, wg, conv_w, alog_row, dt_row, cos_t, sin_t, seq_len):
    m, d = x2d.shape
    tm = TOKEN_TILE
    n_main = wm.shape[1]
    kern = functools.partial(_even_in_kernel, tiles_per_seq=seq_len // tm)
    row = lambda i: (i, 0)
    return pl.pallas_call(
        kern,
        grid=(m // tm,),
        in_specs=[
            pl.BlockSpec((tm, d), row),
            _const_spec((1, d)),
            _const_spec((d, n_main)),
            _const_spec((d, LANES)),
            _const_spec(conv_w.shape),
            _const_spec((1, LANES)),
            _const_spec((1, LANES)),
            pl.BlockSpec((tm, LANES), row),
            pl.BlockSpec((tm, LANES), row),
        ],
        out_specs=[
            pl.BlockSpec((tm, 3 * PROJ_SEG), row),
            pl.BlockSpec((tm, PROJ_SEG), row),
            pl.BlockSpec((tm, 2 * PROJ_SEG), row),
            pl.BlockSpec((tm, PROJ_SEG), row),
            pl.BlockSpec((tm, LANES), row),
        ],
        out_shape=[
            jax.ShapeDtypeStruct((m, 3 * PROJ_SEG), F32),
            jax.ShapeDtypeStruct((m, PROJ_SEG), BF16),
            jax.ShapeDtypeStruct((m, 2 * PROJ_SEG), BF16),
            jax.ShapeDtypeStruct((m, PROJ_SEG), BF16),
            jax.ShapeDtypeStruct((m, LANES), F32),
        ],
        scratch_shapes=[pltpu.VMEM((tm, d), BF16), pltpu.VMEM((SUBLANES, 3 * PROJ_SEG), F32)],
        compiler_params=_params(("arbitrary",)),
        name="even_in_proj",
    )(x2d, g, wm, wg, conv_w, alog_row, dt_row, cos_t, sin_t)


def _odd_in_kernel(x_ref, g_ref, wm_ref, wf_ref, bf_ref, q_ref, k_ref, v_ref, gate_ref, cum_ref,
                   h_ref, carry_ref, *, tiles_per_seq, d_mix):
    tm = x_ref.shape[0]
    i = pl.program_id(0)
    h_ref[...] = _rms(x_ref[...], g_ref[...]).astype(BF16)
    head_dim = d_mix // FOX_HEADS
    for o_ref, base, scale in ((q_ref, 0, head_dim ** -0.5 * LOG2E), (k_ref, d_mix, 1.0),
                               (v_ref, 2 * d_mix, 1.0), (gate_ref, 3 * d_mix, 1.0)):
        for s in range(d_mix // PROJ_SEG):
            cols = slice(s * PROJ_SEG, (s + 1) * PROJ_SEG)
            y = _dot(h_ref[...], wm_ref[:, base + s * PROJ_SEG:base + (s + 1) * PROJ_SEG])
            o_ref[:, cols] = (y * scale).astype(BF16)
    f = _dot(h_ref[...], wf_ref[...]) + bf_ref[...]
    log_f = jnp.minimum(f, 0.0) - jnp.log1p(jnp.exp(-jnp.abs(f)))
    prev = jnp.where((i % tiles_per_seq) == 0, 0.0, carry_ref[0:1, :])
    cum = _row_scan(log_f, tm) + prev
    cum_ref[...] = cum
    carry_ref[...] = jnp.broadcast_to(cum[tm - 1:tm, :], carry_ref.shape)


def _odd_in(x2d, g, wm, wf, bf_row, seq_len):
    m, d = x2d.shape
    tm = TOKEN_TILE
    d_mix = wm.shape[1] // 4
    kern = functools.partial(_odd_in_kernel, tiles_per_seq=seq_len // tm, d_mix=d_mix)
    row = lambda i: (i, 0)
    return pl.pallas_call(
        kern,
        grid=(m // tm,),
        in_specs=[
            pl.BlockSpec((tm, d), row),
            _const_spec((1, d)),
            _const_spec(wm.shape),
            _const_spec((d, LANES)),
            _const_spec((1, LANES)),
        ],
        out_specs=[pl.BlockSpec((tm, d_mix), row)] * 4 + [pl.BlockSpec((tm, LANES), row)],
        out_shape=[jax.ShapeDtypeStruct((m, d_mix), BF16)] * 4 + [jax.ShapeDtypeStruct((m, LANES), F32)],
        scratch_shapes=[pltpu.VMEM((tm, d), BF16), pltpu.VMEM((SUBLANES, LANES), F32)],
        compiler_params=_params(("arbitrary",)),
        name="odd_in_proj",
    )(x2d, g, wm, wf, bf_row)


def _bias_lanes(c, pieces_first):
    hi, mid, lo = (piece.astype(F32) for piece in _split_bf16(c, 3))
    lane = lax.broadcasted_iota(jnp.int32, (c.shape[0], LANES), 1)
    first = 0 if pieces_first else 3
    vals = jnp.where(lane == first, hi, jnp.where(lane == first + 1, mid, jnp.where(lane == first + 2, lo, 0.0)))
    ones_at = 3 - first
    vals = jnp.where((lane >= ones_at) & (lane < ones_at + 3), 1.0, vals)
    return vals.astype(BF16)


def _attn_kernel(*refs, tq, tk, n_maps, fox, lambda_init, hp):
    if fox:
        q_ref, k_ref, v_ref, gate_ref, cumq_ref, cumk_ref, o_ref, vt_ref, kb_ref = refs
    else:
        q_ref, k_ref, v_ref, lam_ref, nw_ref, o_ref, vt_ref = refs
    assert tq == tk and hp == 2
    hg = pl.program_id(1)
    qi = pl.program_id(2)
    t = k_ref.shape[1]
    head = lambda g: slice(g * HEAD_LANES, (g + 1) * HEAD_LANES)

    def head_col(x, hd):
        lane = lax.broadcasted_iota(jnp.int32, x.shape, 1)
        return jnp.sum(jnp.where(lane == hd, x, 0.0), axis=1, keepdims=True)

    @pl.when(qi == 0)
    def _():
        for g in range(hp):
            for r in range(t // LANES):
                rows = slice(r * LANES, (r + 1) * LANES)
                vt_ref[head(g), rows] = v_ref[0, rows, head(g)].astype(F32).T.astype(BF16)
            if fox:
                kb_ref[:, head(g)] = _bias_lanes(-LOG2E * head_col(cumk_ref[0], hg * hp + g), True)

    qs = []
    for g in range(hp):
        q = q_ref[0, :, head(g)]
        if fox:
            q = jnp.concatenate([q, _bias_lanes(LOG2E * head_col(cumq_ref[0], hg * hp + g), False)], axis=1)
        if n_maps == 2:
            lane = lax.broadcasted_iota(jnp.int32, q.shape, 1)
            zero = jnp.zeros_like(q)
            q = jnp.concatenate([jnp.where(lane < DIFF_QK_DIM, q, zero),
                                 jnp.where(lane >= DIFF_QK_DIM, q, zero)], axis=0)
        qs.append(q)
    cols = n_maps * tq

    def scores(g, j):
        k0 = pl.multiple_of(j * tk, tk)
        kj = k_ref[0, pl.ds(k0, tk), head(g)]
        if fox:
            kj = jnp.concatenate([kj, kb_ref[pl.ds(k0, tk), head(g)]], axis=1)
        return _dot_nt(kj, qs[g])

    def absorb(g, j, st, state, masked):
        m, l, acc = state
        k0 = pl.multiple_of(j * tk, tk)
        if masked:
            kpos = k0 + lax.broadcasted_iota(jnp.int32, (tk, cols), 0)
            qpos = qi * tq + lax.broadcasted_iota(jnp.int32, (tk, cols), 1) % tq
            st = jnp.where(qpos >= kpos, st, NEG_INF)
        m_new = jnp.maximum(m, jnp.max(st, axis=0, keepdims=True))
        alpha = jnp.exp2(m - m_new)
        p = jnp.exp2(st - m_new)
        l = alpha * l + jnp.sum(p, axis=0, keepdims=True)
        acc = alpha * acc + _dot(vt_ref[head(g), pl.ds(k0, tk)], p.astype(BF16))
        return m_new, l, acc

    def step(j, carry):
        st0, s0, s1 = carry
        st1 = scores(1, j)
        s0 = absorb(0, j, st0, s0, False)
        st0 = scores(0, j + 1)
        s1 = absorb(1, j, st1, s1, False)
        return st0, s0, s1

    init = (jnp.full((1, cols), NEG_INF, F32), jnp.zeros((1, cols), F32), jnp.zeros((HEAD_LANES, cols), F32))
    st0, s0, s1 = lax.fori_loop(0, qi, step, (scores(0, 0), init, init))
    st1 = scores(1, qi)
    carry = (absorb(0, qi, st0, s0, True), absorb(1, qi, st1, s1, True))
    if not fox:
        lam_p = lam_ref[...]
        lam = (jnp.exp(jnp.sum(lam_p[0:1] * lam_p[1:2], axis=1, keepdims=True))
               - jnp.exp(jnp.sum(lam_p[2:3] * lam_p[3:4], axis=1, keepdims=True)) + lambda_init)
    for g in range(hp):
        _, l, acc = carry[g]
        ot = acc / l
        if fox:
            o = ot.T * _sigmoid(gate_ref[0, :, head(g)].astype(F32))
        else:
            o = (ot[:, :tq] - lam * ot[:, tq:]).T
            o = _rms(o, nw_ref[...]) * (1.0 - lambda_init)
        o_ref[0, :, head(g)] = o.astype(o_ref.dtype)


def _fox_attention(q, k, v, gate, cum, *, tq=512, tk=512, hp=ATTN_HEADS_PER_STEP):
    b, t, dm = q.shape
    nh = dm // HEAD_LANES
    kern = functools.partial(_attn_kernel, tq=tq, tk=tk, n_maps=1, fox=True, lambda_init=0.0, hp=hp)
    width = hp * HEAD_LANES
    qblk = pl.BlockSpec((1, tq, width), lambda bi, h, i: (bi, i, h))
    kvblk = pl.BlockSpec((1, t, width), lambda bi, h, i: (bi, 0, h))
    return pl.pallas_call(
        kern,
        grid=(b, nh // hp, t // tq),
        in_specs=[qblk, kvblk, kvblk, qblk,
                  pl.BlockSpec((1, tq, LANES), lambda bi, h, i: (bi, i, 0)),
                  pl.BlockSpec((1, t, LANES), lambda bi, h, i: (bi, 0, 0))],
        out_specs=qblk,
        out_shape=jax.ShapeDtypeStruct((b, t, dm), BF16),
        scratch_shapes=[pltpu.VMEM((width, t), BF16), pltpu.VMEM((t, width), BF16)],
        compiler_params=_params(("arbitrary", "arbitrary", "arbitrary")),
        name="fox_attention",
    )(q, k, v, gate, cum, cum)


def _diff_attention(qk, v, lam_params, norm_w, lambda_init, *, tq=256, tk=256, hp=ATTN_HEADS_PER_STEP):
    b, t, _ = qk.shape
    nh = DIFF_HEADS
    groups = nh // hp
    kern = functools.partial(_attn_kernel, tq=tq, tk=tk, n_maps=2, fox=False, lambda_init=lambda_init, hp=hp)
    width = hp * HEAD_LANES
    qblk = pl.BlockSpec((1, tq, width), lambda bi, h, i: (bi, i, h))
    kblk = pl.BlockSpec((1, t, width), lambda bi, h, i: (bi, 0, groups + h))
    vblk = pl.BlockSpec((1, t, width), lambda bi, h, i: (bi, 0, h))
    return pl.pallas_call(
        kern,
        grid=(b, groups, t // tq),
        in_specs=[qblk, kblk, vblk, _const_spec(lam_params.shape), _const_spec((1, HEAD_LANES))],
        out_specs=qblk,
        out_shape=jax.ShapeDtypeStruct((b, t, nh * HEAD_LANES), BF16),
        scratch_shapes=[pltpu.VMEM((width, t), BF16)],
        compiler_params=_params(("arbitrary", "arbitrary", "arbitrary")),
        name="diff_attention",
    )(qk, qk, v, lam_params, norm_w)


def _split_bf16(x, parts):
    out = []
    for _ in range(parts):
        piece = x.astype(BF16)
        out.append(piece)
        x = x - piece.astype(F32)
    return out


def _dot_split(a, b):
    a_hi, a_lo = _split_bf16(a, 2)
    b_hi, b_lo = _split_bf16(b, 2)
    return _dot(a_hi, b_hi) + (_dot(a_hi, b_lo) + _dot(a_lo, b_hi))


def _gdn_prep_kernel(q_ref, k_ref, v_ref, gates_ref, u_ref, w_ref, qd_ref, kd_ref, qk_ref, *, refine):
    c = GDN_CHUNK
    tt = q_ref.shape[1]
    heads = range(GDN_HEADS)
    gt = gates_ref[0]
    lane = lax.broadcasted_iota(jnp.int32, (tt, LANES), 1)
    ri = lax.broadcasted_iota(jnp.int32, (tt, tt), 0)
    ci = lax.broadcasted_iota(jnp.int32, (tt, tt), 1)
    chunk_start = ri - ri % c
    incl = lambda a: jnp.where(ci <= ri, jnp.where(ci >= chunk_start, a, 0.0), 0.0)
    strict = lambda a: jnp.where(ci < ri, jnp.where(ci >= chunk_start, a, 0.0), 0.0)
    ident = jnp.where(ri == ci, 1.0, 0.0)
    ones = jnp.ones((tt, LANES), BF16)
    cols = [slice(hd * HEAD_LANES, (hd + 1) * HEAD_LANES) for hd in heads]
    kt = [k_ref[0, :, cols[hd]] for hd in heads]
    beta = [gt[:, hd:hd + 1] for hd in heads]
    gcc = [gt[:, GDN_HEADS + hd:GDN_HEADS + hd + 1] for hd in heads]
    k16 = [kt[hd].astype(BF16) for hd in heads]
    kb = [kt[hd] * beta[hd] for hd in heads]

    gc_row = []
    for hd in heads:
        g_hi, g_mid, g_lo = (piece.astype(F32) for piece in _split_bf16(gcc[hd], 3))
        pieces = jnp.where(lane == 0, g_hi, jnp.where(lane == 1, g_mid, jnp.where(lane == 2, g_lo, 0.0)))
        gc_row.append(_dot_nt(ones, pieces.astype(BF16)))
    kk = [_dot_nt(kb[hd].astype(BF16), k16[hd]) for hd in heads]
    qk_raw = [_dot_nt(q_ref[0, :, cols[hd]].astype(BF16), k16[hd]) for hd in heads]
    decay = [incl(jnp.exp(incl(gcc[hd] - gc_row[hd]))) for hd in heads]
    lower = [strict(kk[hd] * decay[hd]) for hd in heads]

    inv = [ident - lower[hd] for hd in heads]
    l16 = [lower[hd].astype(BF16) for hd in heads]
    power = [_dot(l16[hd], l16[hd]).astype(BF16) for hd in heads]
    n_sq = int(math.log2(c)) - 1
    for step in range(n_sq - 1):
        r = [_dot(jnp.concatenate([power[hd], inv[hd].astype(BF16)], axis=0), power[hd]) for hd in heads]
        power = [r[hd][:tt].astype(BF16) for hd in heads]
        inv = [inv[hd] + r[hd][tt:] for hd in heads]
    r = [_dot(inv[hd].astype(BF16), power[hd]) for hd in heads]
    inv16 = [(inv[hd] + r[hd]).astype(BF16) for hd in heads]

    eg = [jnp.exp(gcc[hd]) for hd in heads]
    rhs = [jnp.concatenate([v_ref[0, :, cols[hd]] * beta[hd], kb[hd] * eg[hd]], axis=1) for hd in heads]
    sol = [_dot(inv16[hd], rhs[hd].astype(BF16)) for hd in heads]
    if refine:
        a_hi, a_lo, s_hi, s_lo = [], [], [], []
        for hd in heads:
            hi, lo = _split_bf16(ident + lower[hd], 2)
            a_hi.append(hi)
            a_lo.append(lo)
            hi, lo = _split_bf16(sol[hd], 2)
            s_hi.append(hi)
            s_lo.append(lo)
        prod = [_dot(a_hi[hd], s_hi[hd]) + (_dot(a_hi[hd], s_lo[hd]) + _dot(a_lo[hd], s_hi[hd])) for hd in heads]
        corr = [_dot(inv16[hd], (rhs[hd] - prod[hd]).astype(BF16)) for hd in heads]
        sol = [sol[hd] + corr[hd] for hd in heads]
    for hd in heads:
        u_ref[0, :, cols[hd]] = sol[hd][:, :HEAD_LANES]
        w_ref[0, :, cols[hd]] = sol[hd][:, HEAD_LANES:].astype(BF16)
        qk = incl(qk_raw[hd] * decay[hd])
        qd_ref[0, :, cols[hd]] = (q_ref[0, :, cols[hd]] * eg[hd]).astype(BF16)
        for n in range(tt // c):
            rows = slice(n * c, (n + 1) * c)
            qk_ref[0, hd, rows, :] = qk[rows, rows].astype(BF16)
            gl = gcc[hd][(n + 1) * c - 1:(n + 1) * c, :]
            kd_ref[0, rows, cols[hd]] = (kt[hd][rows] * jnp.exp(gl - gcc[hd][rows])).astype(BF16)


def _gdn_prep(qkv, gates, *, tt=256):
    b, t, _ = qkv.shape
    nh = GDN_HEADS
    dm = nh * HEAD_LANES
    blk = lambda part: pl.BlockSpec((1, tt, dm), lambda bi, i: (bi, i, part))
    return pl.pallas_call(
        functools.partial(_gdn_prep_kernel, refine=GDN_REFINE),
        grid=(b, t // tt),
        in_specs=[blk(0), blk(1), blk(2), pl.BlockSpec((1, tt, LANES), lambda bi, i: (bi, i, 0))],
        out_specs=[blk(0)] * 4 + [pl.BlockSpec((1, nh, tt, GDN_CHUNK), lambda bi, i: (bi, 0, i, 0))],
        out_shape=[jax.ShapeDtypeStruct((b, t, dm), F32)]
        + [jax.ShapeDtypeStruct((b, t, dm), BF16)] * 3
        + [jax.ShapeDtypeStruct((b, nh, t, GDN_CHUNK), BF16)],
        compiler_params=_params(("arbitrary", "arbitrary")),
        name="gdn_prep",
    )(qkv, qkv, qkv, gates)


def _gdn_scan_kernel(u_ref, w_ref, qd_ref, kd_ref, qk_ref, gates_ref, z_ref, nw_ref, o_ref, s_ref):
    c = GDN_CHUNK
    tt = u_ref.shape[1]

    @pl.when(pl.program_id(1) == 0)
    def _():
        s_ref[...] = jnp.zeros_like(s_ref)

    def chunk(n, _):
        r0 = pl.multiple_of(n * c, c)
        rows = pl.ds(r0, c)
        g_last = gates_ref[0, pl.ds(r0 + c - SUBLANES, SUBLANES), :]
        heads = range(GDN_HEADS)
        cols = [slice(hd * HEAD_LANES, (hd + 1) * HEAD_LANES) for hd in heads]
        state = [s_ref[hd] for hd in heads]
        r = [_dot(jnp.concatenate([w_ref[0, rows, cols[hd]], qd_ref[0, rows, cols[hd]]], axis=0),
                  state[hd].astype(BF16)) for hd in heads]
        v_new = [(u_ref[0, rows, cols[hd]] - r[hd][:c]).astype(BF16) for hd in heads]
        intra = [_dot(qk_ref[0, hd, rows, :], v_new[hd]) for hd in heads]
        upd = [_dot_tn(kd_ref[0, rows, cols[hd]], v_new[hd]) for hd in heads]
        for hd in heads:
            decay_last = jnp.exp(g_last[SUBLANES - 1:SUBLANES, GDN_HEADS + hd:GDN_HEADS + hd + 1])
            s_ref[hd] = state[hd] * decay_last + upd[hd]
            zt = z_ref[0, rows, cols[hd]].astype(F32)
            o = r[hd][c:] + intra[hd]
            o_ref[0, rows, cols[hd]] = (_rms(o, nw_ref[...]) * (zt * _sigmoid(zt))).astype(o_ref.dtype)
        return 0

    lax.fori_loop(0, tt // c, chunk, 0)


def _gdn_scan(u, w, qd, kd, qk, gates, z, norm_w, *, tt=512):
    b, t, dm = u.shape
    nh = GDN_HEADS
    blk = pl.BlockSpec((1, tt, dm), lambda bi, i: (bi, i, 0))
    return pl.pallas_call(
        _gdn_scan_kernel,
        grid=(b, t // tt),
        in_specs=[blk, blk, blk, blk,
                  pl.BlockSpec((1, nh, tt, GDN_CHUNK), lambda bi, i: (bi, 0, i, 0)),
                  pl.BlockSpec((1, tt, LANES), lambda bi, i: (bi, i, 0)),
                  blk, _const_spec((1, HEAD_LANES))],
        out_specs=blk,
        out_shape=jax.ShapeDtypeStruct((b, t, dm), BF16),
        scratch_shapes=[pltpu.VMEM((nh, GDN_HEAD_DIM, GDN_HEAD_DIM), F32)],
        compiler_params=_params(("arbitrary", "arbitrary")),
        name="gdn_scan",
    )(u, w, qd, kd, qk, gates, z, norm_w)


def _post_kernel(*refs, n_mix, final_norm):
    x_ref = refs[0]
    mix_refs = refs[1:1 + n_mix]
    wout_ref, g_ref, wup_ref, wdn_ref, p_ref, wpp_ref, wpg_ref = refs[1 + n_mix:8 + n_mix]
    rest = refs[8 + n_mix:]
    if final_norm:
        gf_ref, o_ref = rest
    else:
        (o_ref,) = rest
    mix = mix_refs[0][...] if n_mix == 1 else jnp.concatenate([r[...] for r in mix_refs], axis=1)
    x = x_ref[...] + _dot(mix, wout_ref[...])
    h = _rms(x, g_ref[...]).astype(BF16)
    d_ff = wup_ref.shape[1]
    acc = x
    for s in range(d_ff // FF_SEG):
        a = jnp.maximum(_dot(h, wup_ref[:, s * FF_SEG:(s + 1) * FF_SEG]), 0.0)
        acc = acc + _dot((a * a).astype(BF16), wdn_ref[s * FF_SEG:(s + 1) * FF_SEG, :])
    x = acc
    gate = _sigmoid(_dot(x.astype(BF16), wpg_ref[...]))
    x = x + _dot(p_ref[...].astype(BF16), wpp_ref[...]) * gate
    if final_norm:
        x = _rms(x, gf_ref[...])
    o_ref[...] = x


def _post(x2d, mixes, wout, g, wup, wdn, p2d, wpp, wpg, gf=None):
    m, d = x2d.shape
    tm = TOKEN_TILE
    row = lambda i: (i, 0)
    single = pl.Buffered(1)
    const = lambda a: pl.BlockSpec(a.shape, lambda i: (0, 0), pipeline_mode=single)
    args = [x2d, *mixes, wout, g, wup, wdn, p2d, wpp, wpg]
    in_specs = ([pl.BlockSpec((tm, d), row)]
                + [pl.BlockSpec((tm, a.shape[1]), row) for a in mixes]
                + [const(wout), const(g), const(wup), const(wdn), pl.BlockSpec((tm, p2d.shape[1]), row), const(wpp), const(wpg)])
    if gf is not None:
        args.append(gf)
        in_specs.append(const(gf))
    kern = functools.partial(_post_kernel, n_mix=len(mixes), final_norm=gf is not None)
    return pl.pallas_call(
        kern,
        grid=(m // tm,),
        in_specs=in_specs,
        out_specs=pl.BlockSpec((tm, d), row),
        out_shape=jax.ShapeDtypeStruct((m, d), F32),
        compiler_params=_params(("arbitrary",)),
        name="out_proj_mlp_ple",
    )(*args)


def _pad_lanes(a):
    return jnp.pad(a, ((0, 0), (0, LANES - a.shape[1])))


def kernel(x, p, positions, norm_mix, norm_mlp, norm_final, w_in_even, conv_w, a_log, dt_bias, gdn_norm,
           lam_q1, lam_k1, lam_q2, lam_k2, diff_norm, w_out_even, w_in_odd, b_forget, w_out_odd,
           w_mlp_up, w_mlp_down, w_ple_proj, w_ple_gate):
    b, t, d = x.shape
    depth = p.shape[0]
    m = b * t
    assert t % TOKEN_TILE == 0 and d % PROJ_SEG == 0
    nh = GDN_HEADS
    gdn_w = 3 * nh * GDN_HEAD_DIM + nh * GDN_HEAD_DIM
    assert w_in_even.shape[2] == gdn_w + 2 * nh + 3 * DIFF_HEADS * 2 * DIFF_QK_DIM

    inv_freq = ROPE_THETA ** (-jnp.arange(0, DIFF_QK_DIM, 2, dtype=F32) / DIFF_QK_DIM)
    ang = positions.astype(F32)[..., None] * inv_freq
    cos, sin = jnp.cos(ang), jnp.sin(ang)
    cos_t = jnp.concatenate([cos, cos, cos, cos], axis=-1).reshape(m, LANES)
    sin_t = jnp.concatenate([-sin, sin, -sin, sin], axis=-1).reshape(m, LANES)

    x2d = x.reshape(m, d)
    for i in range(depth):
        j = i // 2
        g_mix = norm_mix[i].reshape(1, d)
        if i % 2 == 0:
            lambda_init = 0.8 - 0.6 * math.exp(-0.3 * i)
            w = w_in_even[j]
            wm = jnp.concatenate([w[:, :gdn_w], w[:, gdn_w + 2 * nh:]], axis=1).astype(BF16)
            wg = _pad_lanes(w[:, gdn_w:gdn_w + 2 * nh]).astype(BF16)
            alog_row = _pad_lanes(jnp.concatenate([jnp.zeros((nh,), F32), a_log[j]]).reshape(1, 2 * nh))
            dt_row = _pad_lanes(jnp.concatenate([jnp.zeros((nh,), F32), dt_bias[j]]).reshape(1, 2 * nh))
            qkv, z, qkb, vb, gates = _even_in(x2d, g_mix, wm, wg, conv_w[j], alog_row, dt_row, cos_t, sin_t, t)
            qkv, z, qkb, vb, gates = (a.reshape(b, t, -1) for a in (qkv, z, qkb, vb, gates))
            u, wy, qd, kd, qk = _gdn_prep(qkv, gates)
            o_a = _gdn_scan(u, wy, qd, kd, qk, gates, z, gdn_norm[j].reshape(1, HEAD_LANES))
            lam_params = jnp.stack([lam_q1[j], lam_k1[j], lam_q2[j], lam_k2[j]])
            o_b = _diff_attention(qkb, vb, lam_params, diff_norm[j].reshape(1, HEAD_LANES), lambda_init)
            mixes = [o_a.reshape(m, -1), o_b.reshape(m, -1)]
            wout = w_out_even[j].astype(BF16)
        else:
            w = w_in_odd[j]
            d_mix = (w.shape[1] - FOX_HEADS) // 4
            wm = w[:, :4 * d_mix].astype(BF16)
            wf = _pad_lanes(w[:, 4 * d_mix:]).astype(BF16)
            bf_row = _pad_lanes(b_forget[j].reshape(1, FOX_HEADS))
            q, k, v, gate, cum = _odd_in(x2d, g_mix, wm, wf, bf_row, t)
            q, k, v, gate, cum = (a.reshape(b, t, -1) for a in (q, k, v, gate, cum))
            o = _fox_attention(q, k, v, gate, cum)
            mixes = [o.reshape(m, -1)]
            wout = w_out_odd[j].astype(BF16)
        x2d = _post(x2d, mixes, wout, norm_mlp[i].reshape(1, d), w_mlp_up[i].astype(BF16),
                    w_mlp_down[i].astype(BF16), p[i].reshape(m, -1), w_ple_proj[i].astype(BF16),
                    w_ple_gate[i].astype(BF16), norm_final.reshape(1, d) if i == depth - 1 else None)
    return x2d.reshape(b, t, d)
```

```python
import functools
import math

import jax
import jax.numpy as jnp
from jax import lax
from jax.experimental import pallas as pl
from jax.experimental.pallas import tpu as pltpu

F32 = jnp.float32
BF16 = jnp.bfloat16

GDN_HEADS = 4
GDN_HEAD_DIM = 128
GDN_CHUNK = 64
CONV_WIDTH = 4
DIFF_HEADS = 4
DIFF_QK_DIM = 64
FOX_HEADS = 8
HEAD_LANES = 128
ROPE_THETA = 10000.0
EPS = 1e-6
NEG_INF = -1e30
LOG2E = 1.4426950408889634
LANES = 128
SUBLANES = 8
VMEM_LIMIT_BYTES = 56 * 1024 * 1024

TOKEN_TILE = 512
PROJ_SEG = 512
FF_SEG = 1024
GDN_REFINE = True
ATTN_TILE = 512
ONES_ROWS = 16


def _dot(a, b):
    return jnp.dot(a, b, preferred_element_type=F32)


def _dot_exact(a, b):
    return jnp.dot(a, b, preferred_element_type=F32, precision=lax.Precision.HIGHEST)


def _dot_nt(a, b):
    return lax.dot_general(a, b, (((1,), (1,)), ((), ())), preferred_element_type=F32)


def _dot_tn(a, b):
    return lax.dot_general(a, b, (((0,), (0,)), ((), ())), preferred_element_type=F32)


def _rms(x, g):
    return x * lax.rsqrt(jnp.mean(x * x, axis=-1, keepdims=True) + EPS) * g


def _sigmoid(x):
    return 1.0 / (1.0 + jnp.exp(-x))


def _softplus(x):
    return jnp.maximum(x, 0.0) + jnp.log1p(jnp.exp(-jnp.abs(x)))


def _row_scan(x, period):
    rows = lax.broadcasted_iota(jnp.int32, x.shape, 0) % period
    s = 1
    while s < period:
        x = x + jnp.where(rows >= s, pltpu.roll(x, s, 0), 0.0)
        s *= 2
    return x


def _const_spec(shape):
    return pl.BlockSpec(shape, lambda *_: (0,) * len(shape))


def _params(sem):
    return pltpu.CompilerParams(dimension_semantics=sem, vmem_limit_bytes=VMEM_LIMIT_BYTES)


def _even_in_kernel(x_ref, g_ref, wm_ref, wg_ref, conv_ref, alog_ref, dt_ref, cos_ref, sin_ref,
                    qkv_ref, z_ref, qkb_ref, vbt_ref, gates_ref, h_ref, carry_ref, tr_ref, *, tiles_per_seq):
    tm = x_ref.shape[0]
    i = pl.program_id(0)
    h_ref[...] = _rms(x_ref[...], g_ref[...]).astype(BF16)
    seq_start = (i % tiles_per_seq) == 0

    row8 = lax.broadcasted_iota(jnp.int32, (SUBLANES, PROJ_SEG), 0)
    for s in range(3):
        cols = slice(s * PROJ_SEG, (s + 1) * PROJ_SEG)
        y = _dot(h_ref[...], wm_ref[:, cols])
        prev8 = jnp.where(seq_start, 0.0, carry_ref[:, cols])
        carry_ref[:, cols] = y[tm - SUBLANES:, :]
        w = conv_ref[:, cols]
        acc = y * w[CONV_WIDTH - 1:CONV_WIDTH, :]
        top = y[:SUBLANES, :] * w[CONV_WIDTH - 1:CONV_WIDTH, :]
        for k in range(1, CONV_WIDTH):
            wk = w[CONV_WIDTH - 1 - k:CONV_WIDTH - k, :]
            rolled = pltpu.roll(y, k, 0)
            acc = acc + rolled * wk
            top = top + jnp.where(row8 < k, pltpu.roll(prev8, k, 0), rolled[:SUBLANES, :]) * wk
        for part, rows in ((acc, slice(0, tm)), (top, slice(0, SUBLANES))):
            a = part * _sigmoid(part)
            if s < 2:
                outs = []
                for hd in range(GDN_HEADS):
                    seg = a[:, hd * HEAD_LANES:(hd + 1) * HEAD_LANES]
                    n = seg * lax.rsqrt(jnp.sum(seg * seg, axis=-1, keepdims=True) + EPS)
                    outs.append(n * (GDN_HEAD_DIM ** -0.5) if s == 0 else n)
                a = jnp.concatenate(outs, axis=1)
            qkv_ref[rows, cols] = a

    z_ref[...] = _dot(h_ref[...], wm_ref[:, 3 * PROJ_SEG:4 * PROJ_SEG]).astype(BF16)

    cos = jnp.concatenate([cos_ref[...]] * (PROJ_SEG // LANES), axis=1)
    sin = jnp.concatenate([sin_ref[...]] * (PROJ_SEG // LANES), axis=1)
    lane = lax.broadcasted_iota(jnp.int32, (tm, PROJ_SEG), 1)
    first_half = (lane % DIFF_QK_DIM) < (DIFF_QK_DIM // 2)
    for s, scale in ((4, DIFF_QK_DIM ** -0.5 * LOG2E), (5, 1.0)):
        y = _dot(h_ref[...], wm_ref[:, s * PROJ_SEG:(s + 1) * PROJ_SEG])
        swapped = jnp.where(first_half, pltpu.roll(y, PROJ_SEG - DIFF_QK_DIM // 2, 1),
                            pltpu.roll(y, DIFF_QK_DIM // 2, 1))
        r = y * cos + swapped * sin
        qkb_ref[:, (s - 4) * PROJ_SEG:(s - 3) * PROJ_SEG] = (r * scale).astype(BF16)
    tr_ref[...] = _dot(h_ref[...], wm_ref[:, 6 * PROJ_SEG:7 * PROJ_SEG])
    vbt_ref[0] = tr_ref[...].T.astype(BF16)

    graw = _dot(h_ref[...], wg_ref[...])
    beta = _sigmoid(graw)
    g = -jnp.exp(alog_ref[...]) * _softplus(graw + dt_ref[...])
    gc = _row_scan(g, GDN_CHUNK)
    lane_g = lax.broadcasted_iota(jnp.int32, (tm, LANES), 1)
    gates_ref[...] = jnp.where(lane_g < GDN_HEADS, beta, gc)


def _even_in(x2d, g, wm, wg, conv_w, alog_row, dt_row, cos_t, sin_t, seq_len):
    m, d = x2d.shape
    tm = TOKEN_TILE
    n_main = wm.shape[1]
    tps = seq_len // tm
    kern = functools.partial(_even_in_kernel, tiles_per_seq=tps)
    row = lambda i: (i, 0)
    return pl.pallas_call(
        kern,
        grid=(m // tm,),
        in_specs=[
            pl.BlockSpec((tm, d), row),
            _const_spec((1, d)),
            _const_spec((d, n_main)),
            _const_spec((d, LANES)),
            _const_spec(conv_w.shape),
            _const_spec((1, LANES)),
            _const_spec((1, LANES)),
            pl.BlockSpec((tm, LANES), row),
            pl.BlockSpec((tm, LANES), row),
        ],
        out_specs=[
            pl.BlockSpec((tm, 3 * PROJ_SEG), row),
            pl.BlockSpec((tm, PROJ_SEG), row),
            pl.BlockSpec((tm, 2 * PROJ_SEG), row),
            pl.BlockSpec((1, PROJ_SEG, tm), lambda i: (i // tps, 0, i % tps)),
            pl.BlockSpec((tm, LANES), row),
        ],
        out_shape=[
            jax.ShapeDtypeStruct((m, 3 * PROJ_SEG), F32),
            jax.ShapeDtypeStruct((m, PROJ_SEG), BF16),
            jax.ShapeDtypeStruct((m, 2 * PROJ_SEG), BF16),
            jax.ShapeDtypeStruct((m // seq_len, PROJ_SEG, seq_len), BF16),
            jax.ShapeDtypeStruct((m, LANES), F32),
        ],
        scratch_shapes=[pltpu.VMEM((tm, d), BF16), pltpu.VMEM((SUBLANES, 3 * PROJ_SEG), F32), pltpu.VMEM((tm, PROJ_SEG), F32)],
        compiler_params=_params(("arbitrary",)),
        name="even_in_proj",
    )(x2d, g, wm, wg, conv_w, alog_row, dt_row, cos_t, sin_t)


def _odd_in_kernel(x_ref, g_ref, wm_ref, wf_ref, bf_ref, q_ref, k_ref, vt_ref, gate_ref, cum_ref,
                   h_ref, carry_ref, tr_ref, *, tiles_per_seq, d_mix):
    tm = x_ref.shape[0]
    i = pl.program_id(0)
    h_ref[...] = _rms(x_ref[...], g_ref[...]).astype(BF16)
    head_dim = d_mix // FOX_HEADS
    for o_ref, base, scale in ((q_ref, 0, head_dim ** -0.5 * LOG2E), (k_ref, d_mix, 1.0),
                               (vt_ref, 2 * d_mix, 1.0), (gate_ref, 3 * d_mix, 1.0)):
        for s in range(d_mix // PROJ_SEG):
            cols = slice(s * PROJ_SEG, (s + 1) * PROJ_SEG)
            y = _dot(h_ref[...], wm_ref[:, base + s * PROJ_SEG:base + (s + 1) * PROJ_SEG])
            if o_ref is vt_ref:
                tr_ref[...] = y
                o_ref[0, cols, :] = tr_ref[...].T.astype(BF16)
            else:
                o_ref[:, cols] = (y * scale).astype(BF16)
    f = _dot(h_ref[...], wf_ref[...]) + bf_ref[...]
    log_f = jnp.minimum(f, 0.0) - jnp.log1p(jnp.exp(-jnp.abs(f)))
    prev = jnp.where((i % tiles_per_seq) == 0, 0.0, carry_ref[0:1, :])
    cum = _row_scan(log_f, tm) + prev
    cum_ref[...] = cum
    carry_ref[...] = jnp.broadcast_to(cum[tm - 1:tm, :], carry_ref.shape)


def _odd_in(x2d, g, wm, wf, bf_row, seq_len):
    m, d = x2d.shape
    tm = TOKEN_TILE
    d_mix = wm.shape[1] // 4
    tps = seq_len // tm
    kern = functools.partial(_odd_in_kernel, tiles_per_seq=tps, d_mix=d_mix)
    row = lambda i: (i, 0)
    row_blk = pl.BlockSpec((tm, d_mix), row)
    row_shape = jax.ShapeDtypeStruct((m, d_mix), BF16)
    return pl.pallas_call(
        kern,
        grid=(m // tm,),
        in_specs=[
            pl.BlockSpec((tm, d), row),
            _const_spec((1, d)),
            _const_spec(wm.shape),
            _const_spec((d, LANES)),
            _const_spec((1, LANES)),
        ],
        out_specs=[row_blk, row_blk, pl.BlockSpec((1, d_mix, tm), lambda i: (i // tps, 0, i % tps)), row_blk,
                   pl.BlockSpec((tm, LANES), row)],
        out_shape=[row_shape, row_shape, jax.ShapeDtypeStruct((m // seq_len, d_mix, seq_len), BF16), row_shape,
                   jax.ShapeDtypeStruct((m, LANES), F32)],
        scratch_shapes=[pltpu.VMEM((tm, d), BF16), pltpu.VMEM((SUBLANES, LANES), F32), pltpu.VMEM((tm, PROJ_SEG), F32)],
        compiler_params=_params(("arbitrary",)),
        name="odd_in_proj",
    )(x2d, g, wm, wf, bf_row)


def _bias_lanes(c, pieces_first):
    hi, mid, lo = (piece.astype(F32) for piece in _split_bf16(c, 3))
    lane = lax.broadcasted_iota(jnp.int32, (c.shape[0], LANES), 1)
    first = 0 if pieces_first else 3
    vals = jnp.where(lane == first, hi, jnp.where(lane == first + 1, mid, jnp.where(lane == first + 2, lo, 0.0)))
    ones_at = 3 - first
    vals = jnp.where((lane >= ones_at) & (lane < ones_at + 3), 1.0, vals)
    return vals.astype(BF16)


def _attn_kernel(*refs, tq, fox, lambda_init):
    if fox:
        q_ref, k_ref, vt_ref, gate_ref, cum_ref, o_ref, kb_ref, qb_ref = refs
    else:
        q_ref, k_ref, vt_ref, lam_ref, nw_ref, o_ref = refs
    tk = tq
    hg = pl.program_id(1)
    qi = pl.program_id(2)
    head = lambda g: slice(g * HEAD_LANES, (g + 1) * HEAD_LANES)

    if fox:
        @pl.when(qi == 0)
        def _():
            cum = cum_ref[0]
            lane = lax.broadcasted_iota(jnp.int32, cum.shape, 1)
            for g in range(2):
                c = jnp.sum(jnp.where(lane == hg * 2 + g, cum, 0.0), axis=1, keepdims=True)
                kb_ref[:, head(g)] = _bias_lanes(-LOG2E * c, True)
                qb_ref[:, head(g)] = _bias_lanes(LOG2E * c, False)

        q0 = pl.multiple_of(qi * tq, tq)
        qs = [jnp.concatenate([q_ref[0, :, head(g)], qb_ref[pl.ds(q0, tq), head(g)]], axis=1) for g in range(2)]
        kv = [head(0), head(1)]
    else:
        q = q_ref[0]
        lane = lax.broadcasted_iota(jnp.int32, q.shape, 1)
        zero = jnp.zeros_like(q)
        qs = [jnp.where(lane < DIFF_QK_DIM, q, zero), jnp.where(lane >= DIFF_QK_DIM, q, zero)]
        kv = [head(0), head(0)]
    ones = jnp.ones((ONES_ROWS, tk), BF16)

    def scores(g, j):
        k0 = pl.multiple_of(j * tk, tk)
        kj = k_ref[0, pl.ds(k0, tk), kv[g]]
        if fox:
            kj = jnp.concatenate([kj, kb_ref[pl.ds(k0, tk), kv[g]]], axis=1)
        return _dot_nt(kj, qs[g])

    def absorb(g, j, st, state, masked):
        m, acc = state
        k0 = pl.multiple_of(j * tk, tk)
        if masked:
            kpos = k0 + lax.broadcasted_iota(jnp.int32, (tk, tq), 0)
            qpos = qi * tq + lax.broadcasted_iota(jnp.int32, (tk, tq), 1)
            st = jnp.where(qpos >= kpos, st, NEG_INF)
        m_new = jnp.maximum(m, jnp.max(st, axis=0, keepdims=True))
        p = jnp.exp2(st - m_new).astype(BF16)
        vt = jnp.concatenate([vt_ref[0, kv[g], pl.ds(k0, tk)], ones], axis=0)
        acc = jnp.exp2(m - m_new) * acc + _dot(vt, p)
        return m_new, acc

    def step(j, carry):
        st0, s0, s1 = carry
        st1 = scores(1, j)
        s0 = absorb(0, j, st0, s0, False)
        st0 = scores(0, j + 1)
        s1 = absorb(1, j, st1, s1, False)
        return st0, s0, s1

    init = (jnp.full((1, tq), NEG_INF, F32), jnp.zeros((HEAD_LANES + ONES_ROWS, tq), F32))
    st0, s0, s1 = lax.fori_loop(0, qi, step, (scores(0, 0), init, init))
    st1 = scores(1, qi)
    accs = [absorb(0, qi, st0, s0, True)[1], absorb(1, qi, st1, s1, True)[1]]
    outs = [(a[:HEAD_LANES] / a[HEAD_LANES:HEAD_LANES + 1]) for a in accs]
    if fox:
        for g in range(2):
            o_ref[0, :, head(g)] = (outs[g].T * _sigmoid(gate_ref[0, :, head(g)].astype(F32))).astype(o_ref.dtype)
    else:
        lam_p = lam_ref[...]
        lam = (jnp.exp(jnp.sum(lam_p[0:1] * lam_p[1:2], axis=1, keepdims=True))
               - jnp.exp(jnp.sum(lam_p[2:3] * lam_p[3:4], axis=1, keepdims=True)) + lambda_init)
        o = (outs[0] - lam * outs[1]).T
        o_ref[0] = (_rms(o, nw_ref[...]) * (1.0 - lambda_init)).astype(o_ref.dtype)


def _fox_attention(q, k, vt, gate, cum, *, tq=ATTN_TILE):
    b, t, dm = q.shape
    width = 2 * HEAD_LANES
    kern = functools.partial(_attn_kernel, tq=tq, fox=True, lambda_init=0.0)
    qblk = pl.BlockSpec((1, tq, width), lambda bi, h, i: (bi, i, h))
    return pl.pallas_call(
        kern,
        grid=(b, dm // width, t // tq),
        in_specs=[qblk,
                  pl.BlockSpec((1, t, width), lambda bi, h, i: (bi, 0, h)),
                  pl.BlockSpec((1, width, t), lambda bi, h, i: (bi, h, 0)),
                  qblk,
                  pl.BlockSpec((1, t, LANES), lambda bi, h, i: (bi, 0, 0))],
        out_specs=qblk,
        out_shape=jax.ShapeDtypeStruct((b, t, dm), BF16),
        scratch_shapes=[pltpu.VMEM((t, width), BF16), pltpu.VMEM((t, width), BF16)],
        compiler_params=_params(("arbitrary", "arbitrary", "arbitrary")),
        name="fox_attention",
    )(q, k, vt, gate, cum)


def _diff_attention(qk, vt, lam_params, norm_w, lambda_init, *, tq=ATTN_TILE):
    b, t, _ = qk.shape
    nh = DIFF_HEADS
    kern = functools.partial(_attn_kernel, tq=tq, fox=False, lambda_init=lambda_init)
    qblk = pl.BlockSpec((1, tq, HEAD_LANES), lambda bi, h, i: (bi, i, h))
    return pl.pallas_call(
        kern,
        grid=(b, nh, t // tq),
        in_specs=[qblk,
                  pl.BlockSpec((1, t, HEAD_LANES), lambda bi, h, i: (bi, 0, nh + h)),
                  pl.BlockSpec((1, HEAD_LANES, t), lambda bi, h, i: (bi, h, 0)),
                  _const_spec(lam_params.shape), _const_spec((1, HEAD_LANES))],
        out_specs=qblk,
        out_shape=jax.ShapeDtypeStruct((b, t, nh * HEAD_LANES), BF16),
        compiler_params=_params(("arbitrary", "arbitrary", "arbitrary")),
        name="diff_attention",
    )(qk, qk, vt, lam_params, norm_w)


def _split_bf16(x, parts):
    out = []
    for _ in range(parts):
        piece = x.astype(BF16)
        out.append(piece)
        x = x - piece.astype(F32)
    return out


def _dot_split(a, b):
    a_hi, a_lo = _split_bf16(a, 2)
    b_hi, b_lo = _split_bf16(b, 2)
    return _dot(a_hi, b_hi) + (_dot(a_hi, b_lo) + _dot(a_lo, b_hi))


def _gdn_prep_kernel(q_ref, k_ref, v_ref, gates_ref, u_ref, w_ref, qd_ref, kd_ref, qk_ref, *, refine):
    c = GDN_CHUNK
    tt = q_ref.shape[1]
    heads = range(GDN_HEADS)
    gt = gates_ref[0]
    lane = lax.broadcasted_iota(jnp.int32, (tt, LANES), 1)
    ri = lax.broadcasted_iota(jnp.int32, (tt, tt), 0)
    ci = lax.broadcasted_iota(jnp.int32, (tt, tt), 1)
    chunk_start = ri - ri % c
    incl = lambda a: jnp.where(ci <= ri, jnp.where(ci >= chunk_start, a, 0.0), 0.0)
    strict = lambda a: jnp.where(ci < ri, jnp.where(ci >= chunk_start, a, 0.0), 0.0)
    ident = jnp.where(ri == ci, 1.0, 0.0)
    ones = jnp.ones((tt, LANES), BF16)
    cols = [slice(hd * HEAD_LANES, (hd + 1) * HEAD_LANES) for hd in heads]
    kt = [k_ref[0, :, cols[hd]] for hd in heads]
    beta = [gt[:, hd:hd + 1] for hd in heads]
    gcc = [gt[:, GDN_HEADS + hd:GDN_HEADS + hd + 1] for hd in heads]
    k16 = [kt[hd].astype(BF16) for hd in heads]
    kb = [kt[hd] * beta[hd] for hd in heads]

    gc_row = []
    for hd in heads:
        g_hi, g_mid, g_lo = (piece.astype(F32) for piece in _split_bf16(gcc[hd], 3))
        pieces = jnp.where(lane == 0, g_hi, jnp.where(lane == 1, g_mid, jnp.where(lane == 2, g_lo, 0.0)))
        gc_row.append(_dot_nt(ones, pieces.astype(BF16)))
    kk = [_dot_nt(kb[hd].astype(BF16), k16[hd]) for hd in heads]
    qk_raw = [_dot_nt(q_ref[0, :, cols[hd]].astype(BF16), k16[hd]) for hd in heads]
    decay = [incl(jnp.exp(incl(gcc[hd] - gc_row[hd]))) for hd in heads]
    lower = [strict(kk[hd] * decay[hd]) for hd in heads]

    inv = [ident - lower[hd] for hd in heads]
    l16 = [lower[hd].astype(BF16) for hd in heads]
    power = [_dot(l16[hd], l16[hd]).astype(BF16) for hd in heads]
    n_sq = int(math.log2(c)) - 1
    for step in range(n_sq - 1):
        r = [_dot(jnp.concatenate([power[hd], inv[hd].astype(BF16)], axis=0), power[hd]) for hd in heads]
        power = [r[hd][:tt].astype(BF16) for hd in heads]
        inv = [inv[hd] + r[hd][tt:] for hd in heads]
    r = [_dot(inv[hd].astype(BF16), power[hd]) for hd in heads]
    inv16 = [(inv[hd] + r[hd]).astype(BF16) for hd in heads]

    eg = [jnp.exp(gcc[hd]) for hd in heads]
    rhs = [jnp.concatenate([v_ref[0, :, cols[hd]] * beta[hd], kb[hd] * eg[hd]], axis=1) for hd in heads]
    sol = [_dot(inv16[hd], rhs[hd].astype(BF16)) for hd in heads]
    if refine:
        a_hi, a_lo, s_hi, s_lo = [], [], [], []
        for hd in heads:
            hi, lo = _split_bf16(ident + lower[hd], 2)
            a_hi.append(hi)
            a_lo.append(lo)
            hi, lo = _split_bf16(sol[hd], 2)
            s_hi.append(hi)
            s_lo.append(lo)
        prod = [_dot(a_hi[hd], s_hi[hd]) + (_dot(a_hi[hd], s_lo[hd]) + _dot(a_lo[hd], s_hi[hd])) for hd in heads]
        corr = [_dot(inv16[hd], (rhs[hd] - prod[hd]).astype(BF16)) for hd in heads]
        sol = [sol[hd] + corr[hd] for hd in heads]
    for hd in heads:
        u_ref[0, :, cols[hd]] = sol[hd][:, :HEAD_LANES]
        w_ref[0, :, cols[hd]] = sol[hd][:, HEAD_LANES:].astype(BF16)
        qk = incl(qk_raw[hd] * decay[hd])
        qd_ref[0, :, cols[hd]] = (q_ref[0, :, cols[hd]] * eg[hd]).astype(BF16)
        for n in range(tt // c):
            rows = slice(n * c, (n + 1) * c)
            qk_ref[0, hd, rows, :] = qk[rows, rows].astype(BF16)
            gl = gcc[hd][(n + 1) * c - 1:(n + 1) * c, :]
            kd_ref[0, rows, cols[hd]] = (kt[hd][rows] * jnp.exp(gl - gcc[hd][rows])).astype(BF16)


def _gdn_prep(qkv, gates, *, tt=256):
    b, t, _ = qkv.shape
    nh = GDN_HEADS
    dm = nh * HEAD_LANES
    blk = lambda part: pl.BlockSpec((1, tt, dm), lambda bi, i: (bi, i, part))
    return pl.pallas_call(
        functools.partial(_gdn_prep_kernel, refine=GDN_REFINE),
        grid=(b, t // tt),
        in_specs=[blk(0), blk(1), blk(2), pl.BlockSpec((1, tt, LANES), lambda bi, i: (bi, i, 0))],
        out_specs=[blk(0)] * 4 + [pl.BlockSpec((1, nh, tt, GDN_CHUNK), lambda bi, i: (bi, 0, i, 0))],
        out_shape=[jax.ShapeDtypeStruct((b, t, dm), F32)]
        + [jax.ShapeDtypeStruct((b, t, dm), BF16)] * 3
        + [jax.ShapeDtypeStruct((b, nh, t, GDN_CHUNK), BF16)],
        compiler_params=_params(("arbitrary", "arbitrary")),
        name="gdn_prep",
    )(qkv, qkv, qkv, gates)


def _gdn_scan_kernel(u_ref, w_ref, qd_ref, kd_ref, qk_ref, gates_ref, z_ref, nw_ref, o_ref, s_ref):
    c = GDN_CHUNK
    tt = u_ref.shape[1]

    @pl.when(pl.program_id(1) == 0)
    def _():
        s_ref[...] = jnp.zeros_like(s_ref)

    def chunk(n, _):
        r0 = pl.multiple_of(n * c, c)
        rows = pl.ds(r0, c)
        g_last = gates_ref[0, pl.ds(r0 + c - SUBLANES, SUBLANES), :]
        heads = range(GDN_HEADS)
        cols = [slice(hd * HEAD_LANES, (hd + 1) * HEAD_LANES) for hd in heads]
        state = [s_ref[hd] for hd in heads]
        r = [_dot(jnp.concatenate([w_ref[0, rows, cols[hd]], qd_ref[0, rows, cols[hd]]], axis=0),
                  state[hd].astype(BF16)) for hd in heads]
        v_new = [(u_ref[0, rows, cols[hd]] - r[hd][:c]).astype(BF16) for hd in heads]
        intra = [_dot(qk_ref[0, hd, rows, :], v_new[hd]) for hd in heads]
        upd = [_dot_tn(kd_ref[0, rows, cols[hd]], v_new[hd]) for hd in heads]
        for hd in heads:
            decay_last = jnp.exp(g_last[SUBLANES - 1:SUBLANES, GDN_HEADS + hd:GDN_HEADS + hd + 1])
            s_ref[hd] = state[hd] * decay_last + upd[hd]
            zt = z_ref[0, rows, cols[hd]].astype(F32)
            o = r[hd][c:] + intra[hd]
            o_ref[0, rows, cols[hd]] = (_rms(o, nw_ref[...]) * (zt * _sigmoid(zt))).astype(o_ref.dtype)
        return 0

    lax.fori_loop(0, tt // c, chunk, 0)


def _gdn_scan(u, w, qd, kd, qk, gates, z, norm_w, *, tt=512):
    b, t, dm = u.shape
    nh = GDN_HEADS
    blk = pl.BlockSpec((1, tt, dm), lambda bi, i: (bi, i, 0))
    return pl.pallas_call(
        _gdn_scan_kernel,
        grid=(b, t // tt),
        in_specs=[blk, blk, blk, blk,
                  pl.BlockSpec((1, nh, tt, GDN_CHUNK), lambda bi, i: (bi, 0, i, 0)),
                  pl.BlockSpec((1, tt, LANES), lambda bi, i: (bi, i, 0)),
                  blk, _const_spec((1, HEAD_LANES))],
        out_specs=blk,
        out_shape=jax.ShapeDtypeStruct((b, t, dm), BF16),
        scratch_shapes=[pltpu.VMEM((nh, GDN_HEAD_DIM, GDN_HEAD_DIM), F32)],
        compiler_params=_params(("arbitrary", "arbitrary")),
        name="gdn_scan",
    )(u, w, qd, kd, qk, gates, z, norm_w)


def _post_kernel(*refs, n_mix, final_norm):
    x_ref = refs[0]
    mix_refs = refs[1:1 + n_mix]
    wout_ref, g_ref, wup_ref, wdn_ref, p_ref, wpp_ref, wpg_ref = refs[1 + n_mix:8 + n_mix]
    rest = refs[8 + n_mix:]
    if final_norm:
        gf_ref, o_ref = rest
    else:
        (o_ref,) = rest
    mix = mix_refs[0][...] if n_mix == 1 else jnp.concatenate([r[...] for r in mix_refs], axis=1)
    x = x_ref[...] + _dot(mix, wout_ref[...])
    h = _rms(x, g_ref[...]).astype(BF16)
    d_ff = wup_ref.shape[1]
    acc = x
    for s in range(d_ff // FF_SEG):
        a = jnp.maximum(_dot(h, wup_ref[:, s * FF_SEG:(s + 1) * FF_SEG]), 0.0)
        acc = acc + _dot((a * a).astype(BF16), wdn_ref[s * FF_SEG:(s + 1) * FF_SEG, :])
    x = acc
    gate = _sigmoid(_dot(x.astype(BF16), wpg_ref[...]))
    x = x + _dot(p_ref[...].astype(BF16), wpp_ref[...]) * gate
    if final_norm:
        x = _rms(x, gf_ref[...])
    o_ref[...] = x


def _post(x2d, mixes, wout, g, wup, wdn, p2d, wpp, wpg, gf=None):
    m, d = x2d.shape
    tm = TOKEN_TILE
    row = lambda i: (i, 0)
    single = pl.Buffered(1)
    const = lambda a: pl.BlockSpec(a.shape, lambda i: (0, 0), pipeline_mode=single)
    args = [x2d, *mixes, wout, g, wup, wdn, p2d, wpp, wpg]
    in_specs = ([pl.BlockSpec((tm, d), row)]
                + [pl.BlockSpec((tm, a.shape[1]), row) for a in mixes]
                + [const(wout), const(g), const(wup), const(wdn), pl.BlockSpec((tm, p2d.shape[1]), row), const(wpp), const(wpg)])
    if gf is not None:
        args.append(gf)
        in_specs.append(const(gf))
    kern = functools.partial(_post_kernel, n_mix=len(mixes), final_norm=gf is not None)
    return pl.pallas_call(
        kern,
        grid=(m // tm,),
        in_specs=in_specs,
        out_specs=pl.BlockSpec((tm, d), row),
        out_shape=jax.ShapeDtypeStruct((m, d), F32),
        compiler_params=_params(("arbitrary",)),
        name="out_proj_mlp_ple",
    )(*args)


def _pad_lanes(a):
    return jnp.pad(a, ((0, 0), (0, LANES - a.shape[1])))


def kernel(x, p, positions, norm_mix, norm_mlp, norm_final, w_in_even, conv_w, a_log, dt_bias, gdn_norm,
           lam_q1, lam_k1, lam_q2, lam_k2, diff_norm, w_out_even, w_in_odd, b_forget, w_out_odd,
           w_mlp_up, w_mlp_down, w_ple_proj, w_ple_gate):
    b, t, d = x.shape
    depth = p.shape[0]
    m = b * t
    assert t % TOKEN_TILE == 0 and d % PROJ_SEG == 0
    nh = GDN_HEADS
    gdn_w = 3 * nh * GDN_HEAD_DIM + nh * GDN_HEAD_DIM
    assert w_in_even.shape[2] == gdn_w + 2 * nh + 3 * DIFF_HEADS * 2 * DIFF_QK_DIM

    inv_freq = ROPE_THETA ** (-jnp.arange(0, DIFF_QK_DIM, 2, dtype=F32) / DIFF_QK_DIM)
    ang = positions.astype(F32)[..., None] * inv_freq
    cos, sin = jnp.cos(ang), jnp.sin(ang)
    cos_t = jnp.concatenate([cos, cos, cos, cos], axis=-1).reshape(m, LANES)
    sin_t = jnp.concatenate([-sin, sin, -sin, sin], axis=-1).reshape(m, LANES)

    x2d = x.reshape(m, d)
    for i in range(depth):
        j = i // 2
        g_mix = norm_mix[i].reshape(1, d)
        if i % 2 == 0:
            lambda_init = 0.8 - 0.6 * math.exp(-0.3 * i)
            w = w_in_even[j]
            wm = jnp.concatenate([w[:, :gdn_w], w[:, gdn_w + 2 * nh:]], axis=1).astype(BF16)
            wg = _pad_lanes(w[:, gdn_w:gdn_w + 2 * nh]).astype(BF16)
            alog_row = _pad_lanes(jnp.concatenate([jnp.zeros((nh,), F32), a_log[j]]).reshape(1, 2 * nh))
            dt_row = _pad_lanes(jnp.concatenate([jnp.zeros((nh,), F32), dt_bias[j]]).reshape(1, 2 * nh))
            qkv, z, qkb, vbt, gates = _even_in(x2d, g_mix, wm, wg, conv_w[j], alog_row, dt_row, cos_t, sin_t, t)
            qkv, z, qkb, gates = (a.reshape(b, t, -1) for a in (qkv, z, qkb, gates))
            u, wy, qd, kd, qk = _gdn_prep(qkv, gates)
            o_a = _gdn_scan(u, wy, qd, kd, qk, gates, z, gdn_norm[j].reshape(1, HEAD_LANES))
            lam_params = jnp.stack([lam_q1[j], lam_k1[j], lam_q2[j], lam_k2[j]])
            o_b = _diff_attention(qkb, vbt, lam_params, diff_norm[j].reshape(1, HEAD_LANES), lambda_init)
            mixes = [o_a.reshape(m, -1), o_b.reshape(m, -1)]
            wout = w_out_even[j].astype(BF16)
        else:
            w = w_in_odd[j]
            d_mix = (w.shape[1] - FOX_HEADS) // 4
            wm = w[:, :4 * d_mix].astype(BF16)
            wf = _pad_lanes(w[:, 4 * d_mix:]).astype(BF16)
            bf_row = _pad_lanes(b_forget[j].reshape(1, FOX_HEADS))
            q, k, vt, gate, cum = _odd_in(x2d, g_mix, wm, wf, bf_row, t)
            q, k, gate, cum = (a.reshape(b, t, -1) for a in (q, k, gate, cum))
            o = _fox_attention(q, k, vt, gate, cum)
            mixes = [o.reshape(m, -1)]
            wout = w_out_odd[j].astype(BF16)
        x2d = _post(x2d, mixes, wout, norm_mlp[i].reshape(1, d), w_mlp_up[i].astype(BF16),
                    w_mlp_down[i].astype(BF16), p[i].reshape(m, -1), w_ple_proj[i].astype(BF16),
                    w_ple_gate[i].astype(BF16), norm_final.reshape(1, d) if i == depth - 1 else None)
    return x2d.reshape(b, t, d)
```

```python
import functools
import math

import jax
import jax.numpy as jnp
import numpy as np
from jax import lax
from jax.experimental import pallas as pl
from jax.experimental.pallas import tpu as pltpu

F32 = jnp.float32
BF16 = jnp.bfloat16

GDN_HEADS = 4
GDN_HEAD_DIM = 128
GDN_CHUNK = 64
CONV_WIDTH = 4
DIFF_HEADS = 4
DIFF_QK_DIM = 64
FOX_HEADS = 8
HEAD_LANES = 128
ROPE_THETA = 10000.0
EPS = 1e-6
NEG_INF = -1e30
LOG2E = 1.4426950408889634
LANES = 128
SUBLANES = 8
VMEM_LIMIT_BYTES = 56 * 1024 * 1024

TOKEN_TILE = 512
PROJ_SEG = 512
FF_SEG = 1024
GDN_REFINE = True
ATTN_TILE = 512
BIAS_LANES_PER_HEAD = 16
ONES_ROWS = 16


def _dot(a, b):
    return jnp.dot(a, b, preferred_element_type=F32)


def _dot_exact(a, b):
    return jnp.dot(a, b, preferred_element_type=F32, precision=lax.Precision.HIGHEST)


def _dot_nt(a, b):
    return lax.dot_general(a, b, (((1,), (1,)), ((), ())), preferred_element_type=F32)


def _dot_tn(a, b):
    return lax.dot_general(a, b, (((0,), (0,)), ((), ())), preferred_element_type=F32)


def _rms(x, g):
    return x * lax.rsqrt(jnp.mean(x * x, axis=-1, keepdims=True) + EPS) * g


def _sigmoid(x):
    return 1.0 / (1.0 + jnp.exp(-x))


def _softplus(x):
    return jnp.maximum(x, 0.0) + jnp.log1p(jnp.exp(-jnp.abs(x)))


def _row_scan(x, period):
    rows = lax.broadcasted_iota(jnp.int32, x.shape, 0) % period
    s = 1
    while s < period:
        x = x + jnp.where(rows >= s, pltpu.roll(x, s, 0), 0.0)
        s *= 2
    return x


def _const_spec(shape):
    return pl.BlockSpec(shape, lambda *_: (0,) * len(shape))


def _params(sem):
    return pltpu.CompilerParams(dimension_semantics=sem, vmem_limit_bytes=VMEM_LIMIT_BYTES)


def _even_in_kernel(x_ref, g_ref, wm_ref, wg_ref, conv_ref, alog_ref, dt_ref, cos_ref, sin_ref,
                    qkv_ref, z_ref, qkb_ref, vbt_ref, gates_ref, h_ref, carry_ref, tr_ref, *, tiles_per_seq):
    tm = x_ref.shape[0]
    i = pl.program_id(0)
    h_ref[...] = _rms(x_ref[...], g_ref[...]).astype(BF16)
    seq_start = (i % tiles_per_seq) == 0

    row8 = lax.broadcasted_iota(jnp.int32, (SUBLANES, PROJ_SEG), 0)
    for s in range(3):
        cols = slice(s * PROJ_SEG, (s + 1) * PROJ_SEG)
        y = _dot(h_ref[...], wm_ref[:, cols])
        prev8 = jnp.where(seq_start, 0.0, carry_ref[:, cols])
        carry_ref[:, cols] = y[tm - SUBLANES:, :]
        w = conv_ref[:, cols]
        acc = y * w[CONV_WIDTH - 1:CONV_WIDTH, :]
        top = y[:SUBLANES, :] * w[CONV_WIDTH - 1:CONV_WIDTH, :]
        for k in range(1, CONV_WIDTH):
            wk = w[CONV_WIDTH - 1 - k:CONV_WIDTH - k, :]
            rolled = pltpu.roll(y, k, 0)
            acc = acc + rolled * wk
            top = top + jnp.where(row8 < k, pltpu.roll(prev8, k, 0), rolled[:SUBLANES, :]) * wk
        for part, rows in ((acc, slice(0, tm)), (top, slice(0, SUBLANES))):
            a = part * _sigmoid(part)
            if s < 2:
                outs = []
                for hd in range(GDN_HEADS):
                    seg = a[:, hd * HEAD_LANES:(hd + 1) * HEAD_LANES]
                    n = seg * lax.rsqrt(jnp.sum(seg * seg, axis=-1, keepdims=True) + EPS)
                    outs.append(n * (GDN_HEAD_DIM ** -0.5) if s == 0 else n)
                a = jnp.concatenate(outs, axis=1)
            qkv_ref[rows, cols] = a

    z_ref[...] = _dot(h_ref[...], wm_ref[:, 3 * PROJ_SEG:4 * PROJ_SEG]).astype(BF16)

    cos = jnp.concatenate([cos_ref[...]] * (PROJ_SEG // LANES), axis=1)
    sin = jnp.concatenate([sin_ref[...]] * (PROJ_SEG // LANES), axis=1)
    lane = lax.broadcasted_iota(jnp.int32, (tm, PROJ_SEG), 1)
    first_half = (lane % DIFF_QK_DIM) < (DIFF_QK_DIM // 2)
    for s, scale in ((4, DIFF_QK_DIM ** -0.5 * LOG2E), (5, 1.0)):
        y = _dot(h_ref[...], wm_ref[:, s * PROJ_SEG:(s + 1) * PROJ_SEG])
        swapped = jnp.where(first_half, pltpu.roll(y, PROJ_SEG - DIFF_QK_DIM // 2, 1),
                            pltpu.roll(y, DIFF_QK_DIM // 2, 1))
        r = y * cos + swapped * sin
        qkb_ref[:, (s - 4) * PROJ_SEG:(s - 3) * PROJ_SEG] = (r * scale).astype(BF16)
    tr_ref[...] = _dot(h_ref[...], wm_ref[:, 6 * PROJ_SEG:7 * PROJ_SEG])
    vbt_ref[0] = tr_ref[...].T.astype(BF16)

    graw = _dot(h_ref[...], wg_ref[...])
    beta = _sigmoid(graw)
    g = -jnp.exp(alog_ref[...]) * _softplus(graw + dt_ref[...])
    gc = _row_scan(g, GDN_CHUNK)
    lane_g = lax.broadcasted_iota(jnp.int32, (tm, LANES), 1)
    gates_ref[...] = jnp.where(lane_g < GDN_HEADS, beta, gc)


def _even_in(x2d, g, wm, wg, conv_w, alog_row, dt_row, cos_t, sin_t, seq_len):
    m, d = x2d.shape
    tm = TOKEN_TILE
    n_main = wm.shape[1]
    tps = seq_len // tm
    kern = functools.partial(_even_in_kernel, tiles_per_seq=tps)
    row = lambda i: (i, 0)
    return pl.pallas_call(
        kern,
        grid=(m // tm,),
        in_specs=[
            pl.BlockSpec((tm, d), row),
            _const_spec((1, d)),
            _const_spec((d, n_main)),
            _const_spec((d, LANES)),
            _const_spec(conv_w.shape),
            _const_spec((1, LANES)),
            _const_spec((1, LANES)),
            pl.BlockSpec((tm, LANES), row),
            pl.BlockSpec((tm, LANES), row),
        ],
        out_specs=[
            pl.BlockSpec((tm, 3 * PROJ_SEG), row),
            pl.BlockSpec((tm, PROJ_SEG), row),
            pl.BlockSpec((tm, 2 * PROJ_SEG), row),
            pl.BlockSpec((1, PROJ_SEG, tm), lambda i: (i // tps, 0, i % tps)),
            pl.BlockSpec((tm, LANES), row),
        ],
        out_shape=[
            jax.ShapeDtypeStruct((m, 3 * PROJ_SEG), F32),
            jax.ShapeDtypeStruct((m, PROJ_SEG), BF16),
            jax.ShapeDtypeStruct((m, 2 * PROJ_SEG), BF16),
            jax.ShapeDtypeStruct((m // seq_len, PROJ_SEG, seq_len), BF16),
            jax.ShapeDtypeStruct((m, LANES), F32),
        ],
        scratch_shapes=[pltpu.VMEM((tm, d), BF16), pltpu.VMEM((SUBLANES, 3 * PROJ_SEG), F32), pltpu.VMEM((tm, PROJ_SEG), F32)],
        compiler_params=_params(("arbitrary",)),
        name="even_in_proj",
    )(x2d, g, wm, wg, conv_w, alog_row, dt_row, cos_t, sin_t)


def _odd_in_kernel(x_ref, g_ref, wm_ref, wf_ref, bf_ref, sel_ref, ones_ref, q_ref, k_ref, vt_ref, gate_ref, qb_ref, kb_ref,
                   h_ref, carry_ref, tr_ref, *, tiles_per_seq, d_mix):
    tm = x_ref.shape[0]
    i = pl.program_id(0)
    h_ref[...] = _rms(x_ref[...], g_ref[...]).astype(BF16)
    head_dim = d_mix // FOX_HEADS
    for o_ref, base, scale in ((q_ref, 0, head_dim ** -0.5 * LOG2E), (k_ref, d_mix, 1.0),
                               (vt_ref, 2 * d_mix, 1.0), (gate_ref, 3 * d_mix, 1.0)):
        for s in range(d_mix // PROJ_SEG):
            cols = slice(s * PROJ_SEG, (s + 1) * PROJ_SEG)
            y = _dot(h_ref[...], wm_ref[:, base + s * PROJ_SEG:base + (s + 1) * PROJ_SEG])
            if o_ref is vt_ref:
                tr_ref[...] = y
                o_ref[0, cols, :] = tr_ref[...].T.astype(BF16)
            else:
                o_ref[:, cols] = (y * scale).astype(BF16)
    f = _dot(h_ref[...], wf_ref[...]) + bf_ref[...]
    log_f = jnp.minimum(f, 0.0) - jnp.log1p(jnp.exp(-jnp.abs(f)))
    prev = jnp.where((i % tiles_per_seq) == 0, 0.0, carry_ref[0:1, :])
    cum = _row_scan(log_f, tm) + prev
    carry_ref[...] = jnp.broadcast_to(cum[tm - 1:tm, :], carry_ref.shape)
    pieces = jnp.concatenate(_split_bf16(LOG2E * cum, 3), axis=1)
    lanes = _dot(pieces, sel_ref[...]) + ones_ref[...]
    qb_ref[...] = lanes[:, :LANES].astype(BF16)
    kb_ref[...] = lanes[:, LANES:].astype(BF16)


def _bias_lane_tables():
    sel = np.zeros((3 * LANES, 2 * LANES), np.float32)
    ones = np.zeros((1, 2 * LANES), np.float32)
    for h in range(FOX_HEADS):
        base = BIAS_LANES_PER_HEAD * h
        for piece in range(3):
            sel[LANES * piece + h, base + 3 + piece] = 1.0
            sel[LANES * piece + h, LANES + base + piece] = -1.0
            ones[0, base + piece] = 1.0
            ones[0, LANES + base + 3 + piece] = 1.0
    return jnp.asarray(sel, BF16), jnp.asarray(ones, F32)


def _odd_in(x2d, g, wm, wf, bf_row, seq_len):
    m, d = x2d.shape
    sel, ones_row = _bias_lane_tables()
    tm = TOKEN_TILE
    d_mix = wm.shape[1] // 4
    tps = seq_len // tm
    kern = functools.partial(_odd_in_kernel, tiles_per_seq=tps, d_mix=d_mix)
    row = lambda i: (i, 0)
    row_blk = pl.BlockSpec((tm, d_mix), row)
    row_shape = jax.ShapeDtypeStruct((m, d_mix), BF16)
    return pl.pallas_call(
        kern,
        grid=(m // tm,),
        in_specs=[
            pl.BlockSpec((tm, d), row),
            _const_spec((1, d)),
            _const_spec(wm.shape),
            _const_spec((d, LANES)),
            _const_spec((1, LANES)),
            _const_spec(sel.shape),
            _const_spec(ones_row.shape),
        ],
        out_specs=[row_blk, row_blk, pl.BlockSpec((1, d_mix, tm), lambda i: (i // tps, 0, i % tps)), row_blk,
                   pl.BlockSpec((tm, LANES), row), pl.BlockSpec((tm, LANES), row)],
        out_shape=[row_shape, row_shape, jax.ShapeDtypeStruct((m // seq_len, d_mix, seq_len), BF16), row_shape,
                   jax.ShapeDtypeStruct((m, LANES), BF16), jax.ShapeDtypeStruct((m, LANES), BF16)],
        scratch_shapes=[pltpu.VMEM((tm, d), BF16), pltpu.VMEM((SUBLANES, LANES), F32), pltpu.VMEM((tm, PROJ_SEG), F32)],
        compiler_params=_params(("arbitrary",)),
        name="odd_in_proj",
    )(x2d, g, wm, wf, bf_row, sel, ones_row)


def _attn_kernel(*refs, tq, fox, lambda_init):
    if fox:
        q_ref, k_ref, vt_ref, gate_ref, qb_ref, kball_ref, o_ref, st_ref, m_ref, acc_ref, kb_ref = refs
    else:
        q_ref, k_ref, vt_ref, lam_ref, nw_ref, o_ref, st_ref, m_ref, acc_ref = refs
    tk = tq
    hg = pl.program_id(1)
    qi = pl.program_id(2)
    head = lambda g: slice(g * HEAD_LANES, (g + 1) * HEAD_LANES)

    if fox:
        @pl.when(qi == 0)
        def _():
            kb_all = kball_ref[0]
            owner = lax.broadcasted_iota(jnp.int32, kb_all.shape, 1) // BIAS_LANES_PER_HEAD
            for g in range(2):
                kb_ref[:, head(g)] = jnp.where(owner == hg * 2 + g, kb_all, jnp.zeros_like(kb_all))

        qs = [jnp.concatenate([q_ref[0, :, head(g)], qb_ref[0]], axis=1) for g in range(2)]
        kv = [head(0), head(1)]
    else:
        q = q_ref[0]
        lane = lax.broadcasted_iota(jnp.int32, q.shape, 1)
        zero = jnp.zeros_like(q)
        qs = [jnp.where(lane < DIFF_QK_DIM, q, zero), jnp.where(lane >= DIFF_QK_DIM, q, zero)]
        kv = [head(0), head(0)]
    ones = jnp.ones((ONES_ROWS, tk), BF16)

    def scores(g, j):
        k0 = pl.multiple_of(j * tk, tk)
        kj = k_ref[0, pl.ds(k0, tk), kv[g]]
        if fox:
            kj = jnp.concatenate([kj, kb_ref[pl.ds(k0, tk), kv[g]]], axis=1)
        return _dot_nt(kj, qs[g])

    def absorb(g, j, st, masked):
        m = m_ref[g]
        k0 = pl.multiple_of(j * tk, tk)
        if masked:
            kpos = k0 + lax.broadcasted_iota(jnp.int32, (tk, tq), 0)
            qpos = qi * tq + lax.broadcasted_iota(jnp.int32, (tk, tq), 1)
            st = jnp.where(qpos >= kpos, st, NEG_INF)
        m_new = jnp.maximum(m, jnp.max(st, axis=0, keepdims=True))
        p = jnp.exp2(st - m_new).astype(BF16)
        vt = jnp.concatenate([vt_ref[0, kv[g], pl.ds(k0, tk)], ones], axis=0)
        acc_ref[g] = jnp.exp2(m - m_new) * acc_ref[g] + _dot(vt, p)
        m_ref[g] = m_new

    def step(j, carry):
        st1 = scores(1, j)
        absorb(0, j, st_ref[...], False)
        st_ref[...] = scores(0, j + 1)
        absorb(1, j, st1, False)
        return carry

    m_ref[...] = jnp.full(m_ref.shape, NEG_INF, F32)
    acc_ref[...] = jnp.zeros(acc_ref.shape, F32)
    st_ref[...] = scores(0, 0)
    lax.fori_loop(0, qi, step, 0)
    st1 = scores(1, qi)
    absorb(0, qi, st_ref[...], True)
    absorb(1, qi, st1, True)
    outs = [acc_ref[g, :HEAD_LANES, :] / acc_ref[g, HEAD_LANES:HEAD_LANES + 1, :] for g in range(2)]
    if fox:
        for g in range(2):
            o_ref[0, :, head(g)] = (outs[g].T * _sigmoid(gate_ref[0, :, head(g)].astype(F32))).astype(o_ref.dtype)
    else:
        lam_p = lam_ref[...]
        lam = (jnp.exp(jnp.sum(lam_p[0:1] * lam_p[1:2], axis=1, keepdims=True))
               - jnp.exp(jnp.sum(lam_p[2:3] * lam_p[3:4], axis=1, keepdims=True)) + lambda_init)
        o = (outs[0] - lam * outs[1]).T
        o_ref[0] = (_rms(o, nw_ref[...]) * (1.0 - lambda_init)).astype(o_ref.dtype)


def _attn_state(tq):
    return [pltpu.VMEM((tq, tq), F32), pltpu.VMEM((2, 1, tq), F32), pltpu.VMEM((2, HEAD_LANES + ONES_ROWS, tq), F32)]


def _fox_attention(q, k, vt, gate, qb, kb, *, tq=ATTN_TILE):
    b, t, dm = q.shape
    width = 2 * HEAD_LANES
    kern = functools.partial(_attn_kernel, tq=tq, fox=True, lambda_init=0.0)
    qblk = pl.BlockSpec((1, tq, width), lambda bi, h, i: (bi, i, h))
    return pl.pallas_call(
        kern,
        grid=(b, dm // width, t // tq),
        in_specs=[qblk,
                  pl.BlockSpec((1, t, width), lambda bi, h, i: (bi, 0, h)),
                  pl.BlockSpec((1, width, t), lambda bi, h, i: (bi, h, 0)),
                  qblk,
                  pl.BlockSpec((1, tq, LANES), lambda bi, h, i: (bi, i, 0)),
                  pl.BlockSpec((1, t, LANES), lambda bi, h, i: (bi, 0, 0))],
        out_specs=qblk,
        out_shape=jax.ShapeDtypeStruct((b, t, dm), BF16),
        scratch_shapes=_attn_state(tq) + [pltpu.VMEM((t, width), BF16)],
        compiler_params=_params(("arbitrary", "arbitrary", "arbitrary")),
        name="fox_attention",
    )(q, k, vt, gate, qb, kb)


def _diff_attention(qk, vt, lam_params, norm_w, lambda_init, *, tq=ATTN_TILE):
    b, t, _ = qk.shape
    nh = DIFF_HEADS
    kern = functools.partial(_attn_kernel, tq=tq, fox=False, lambda_init=lambda_init)
    qblk = pl.BlockSpec((1, tq, HEAD_LANES), lambda bi, h, i: (bi, i, h))
    return pl.pallas_call(
        kern,
        grid=(b, nh, t // tq),
        in_specs=[qblk,
                  pl.BlockSpec((1, t, HEAD_LANES), lambda bi, h, i: (bi, 0, nh + h)),
                  pl.BlockSpec((1, HEAD_LANES, t), lambda bi, h, i: (bi, h, 0)),
                  _const_spec(lam_params.shape), _const_spec((1, HEAD_LANES))],
        out_specs=qblk,
        out_shape=jax.ShapeDtypeStruct((b, t, nh * HEAD_LANES), BF16),
        scratch_shapes=_attn_state(tq),
        compiler_params=_params(("arbitrary", "arbitrary", "arbitrary")),
        name="diff_attention",
    )(qk, qk, vt, lam_params, norm_w)


def _split_bf16(x, parts):
    out = []
    for _ in range(parts):
        piece = x.astype(BF16)
        out.append(piece)
        x = x - piece.astype(F32)
    return out


def _dot_split(a, b):
    a_hi, a_lo = _split_bf16(a, 2)
    b_hi, b_lo = _split_bf16(b, 2)
    return _dot(a_hi, b_hi) + (_dot(a_hi, b_lo) + _dot(a_lo, b_hi))


def _gdn_prep_kernel(q_ref, k_ref, v_ref, gates_ref, u_ref, w_ref, qd_ref, kd_ref, qk_ref, *, refine):
    c = GDN_CHUNK
    tt = q_ref.shape[1]
    heads = range(GDN_HEADS)
    gt = gates_ref[0]
    lane = lax.broadcasted_iota(jnp.int32, (tt, LANES), 1)
    ri = lax.broadcasted_iota(jnp.int32, (tt, tt), 0)
    ci = lax.broadcasted_iota(jnp.int32, (tt, tt), 1)
    chunk_start = ri - ri % c
    incl = lambda a: jnp.where(ci <= ri, jnp.where(ci >= chunk_start, a, 0.0), 0.0)
    strict = lambda a: jnp.where(ci < ri, jnp.where(ci >= chunk_start, a, 0.0), 0.0)
    ident = jnp.where(ri == ci, 1.0, 0.0)
    ones = jnp.ones((tt, LANES), BF16)
    cols = [slice(hd * HEAD_LANES, (hd + 1) * HEAD_LANES) for hd in heads]
    kt = [k_ref[0, :, cols[hd]] for hd in heads]
    beta = [gt[:, hd:hd + 1] for hd in heads]
    gcc = [gt[:, GDN_HEADS + hd:GDN_HEADS + hd + 1] for hd in heads]
    k16 = [kt[hd].astype(BF16) for hd in heads]
    kb = [kt[hd] * beta[hd] for hd in heads]

    gc_row = []
    for hd in heads:
        g_hi, g_mid, g_lo = (piece.astype(F32) for piece in _split_bf16(gcc[hd], 3))
        pieces = jnp.where(lane == 0, g_hi, jnp.where(lane == 1, g_mid, jnp.where(lane == 2, g_lo, 0.0)))
        gc_row.append(_dot_nt(ones, pieces.astype(BF16)))
    kk = [_dot_nt(kb[hd].astype(BF16), k16[hd]) for hd in heads]
    qk_raw = [_dot_nt(q_ref[0, :, cols[hd]].astype(BF16), k16[hd]) for hd in heads]
    decay = [incl(jnp.exp(incl(gcc[hd] - gc_row[hd]))) for hd in heads]
    lower = [strict(kk[hd] * decay[hd]) for hd in heads]

    inv = [ident - lower[hd] for hd in heads]
    l16 = [lower[hd].astype(BF16) for hd in heads]
    power = [_dot(l16[hd], l16[hd]).astype(BF16) for hd in heads]
    n_sq = int(math.log2(c)) - 1
    for step in range(n_sq - 1):
        r = [_dot(jnp.concatenate([power[hd], inv[hd].astype(BF16)], axis=0), power[hd]) for hd in heads]
        power = [r[hd][:tt].astype(BF16) for hd in heads]
        inv = [inv[hd] + r[hd][tt:] for hd in heads]
    r = [_dot(inv[hd].astype(BF16), power[hd]) for hd in heads]
    inv16 = [(inv[hd] + r[hd]).astype(BF16) for hd in heads]

    eg = [jnp.exp(gcc[hd]) for hd in heads]
    rhs = [jnp.concatenate([v_ref[0, :, cols[hd]] * beta[hd], kb[hd] * eg[hd]], axis=1) for hd in heads]
    sol = [_dot(inv16[hd], rhs[hd].astype(BF16)) for hd in heads]
    if refine:
        a_hi, a_lo, s_hi, s_lo = [], [], [], []
        for hd in heads:
            hi, lo = _split_bf16(ident + lower[hd], 2)
            a_hi.append(hi)
            a_lo.append(lo)
            hi, lo = _split_bf16(sol[hd], 2)
            s_hi.append(hi)
            s_lo.append(lo)
        prod = [_dot(a_hi[hd], s_hi[hd]) + (_dot(a_hi[hd], s_lo[hd]) + _dot(a_lo[hd], s_hi[hd])) for hd in heads]
        corr = [_dot(inv16[hd], (rhs[hd] - prod[hd]).astype(BF16)) for hd in heads]
        sol = [sol[hd] + corr[hd] for hd in heads]
    for hd in heads:
        u_ref[0, :, cols[hd]] = sol[hd][:, :HEAD_LANES]
        w_ref[0, :, cols[hd]] = sol[hd][:, HEAD_LANES:].astype(BF16)
        qk = incl(qk_raw[hd] * decay[hd])
        qd_ref[0, :, cols[hd]] = (q_ref[0, :, cols[hd]] * eg[hd]).astype(BF16)
        for n in range(tt // c):
            rows = slice(n * c, (n + 1) * c)
            qk_ref[0, hd, rows, :] = qk[rows, rows].astype(BF16)
            gl = gcc[hd][(n + 1) * c - 1:(n + 1) * c, :]
            kd_ref[0, rows, cols[hd]] = (kt[hd][rows] * jnp.exp(gl - gcc[hd][rows])).astype(BF16)


def _gdn_prep(qkv, gates, *, tt=256):
    b, t, _ = qkv.shape
    nh = GDN_HEADS
    dm = nh * HEAD_LANES
    blk = lambda part: pl.BlockSpec((1, tt, dm), lambda bi, i: (bi, i, part))
    return pl.pallas_call(
        functools.partial(_gdn_prep_kernel, refine=GDN_REFINE),
        grid=(b, t // tt),
        in_specs=[blk(0), blk(1), blk(2), pl.BlockSpec((1, tt, LANES), lambda bi, i: (bi, i, 0))],
        out_specs=[blk(0)] * 4 + [pl.BlockSpec((1, nh, tt, GDN_CHUNK), lambda bi, i: (bi, 0, i, 0))],
        out_shape=[jax.ShapeDtypeStruct((b, t, dm), F32)]
        + [jax.ShapeDtypeStruct((b, t, dm), BF16)] * 3
        + [jax.ShapeDtypeStruct((b, nh, t, GDN_CHUNK), BF16)],
        compiler_params=_params(("arbitrary", "arbitrary")),
        name="gdn_prep",
    )(qkv, qkv, qkv, gates)


def _gdn_scan_kernel(u_ref, w_ref, qd_ref, kd_ref, qk_ref, gates_ref, z_ref, nw_ref, o_ref, s_ref):
    c = GDN_CHUNK
    tt = u_ref.shape[1]

    @pl.when(pl.program_id(1) == 0)
    def _():
        s_ref[...] = jnp.zeros_like(s_ref)

    def chunk(n, _):
        r0 = pl.multiple_of(n * c, c)
        rows = pl.ds(r0, c)
        g_last = gates_ref[0, pl.ds(r0 + c - SUBLANES, SUBLANES), :]
        heads = range(GDN_HEADS)
        cols = [slice(hd * HEAD_LANES, (hd + 1) * HEAD_LANES) for hd in heads]
        state = [s_ref[hd] for hd in heads]
        r = [_dot(jnp.concatenate([w_ref[0, rows, cols[hd]], qd_ref[0, rows, cols[hd]]], axis=0),
                  state[hd].astype(BF16)) for hd in heads]
        v_new = [(u_ref[0, rows, cols[hd]] - r[hd][:c]).astype(BF16) for hd in heads]
        intra = [_dot(qk_ref[0, hd, rows, :], v_new[hd]) for hd in heads]
        upd = [_dot_tn(kd_ref[0, rows, cols[hd]], v_new[hd]) for hd in heads]
        for hd in heads:
            decay_last = jnp.exp(g_last[SUBLANES - 1:SUBLANES, GDN_HEADS + hd:GDN_HEADS + hd + 1])
            s_ref[hd] = state[hd] * decay_last + upd[hd]
            zt = z_ref[0, rows, cols[hd]].astype(F32)
            o = r[hd][c:] + intra[hd]
            o_ref[0, rows, cols[hd]] = (_rms(o, nw_ref[...]) * (zt * _sigmoid(zt))).astype(o_ref.dtype)
        return 0

    lax.fori_loop(0, tt // c, chunk, 0)


def _gdn_scan(u, w, qd, kd, qk, gates, z, norm_w, *, tt=512):
    b, t, dm = u.shape
    nh = GDN_HEADS
    blk = pl.BlockSpec((1, tt, dm), lambda bi, i: (bi, i, 0))
    return pl.pallas_call(
        _gdn_scan_kernel,
        grid=(b, t // tt),
        in_specs=[blk, blk, blk, blk,
                  pl.BlockSpec((1, nh, tt, GDN_CHUNK), lambda bi, i: (bi, 0, i, 0)),
                  pl.BlockSpec((1, tt, LANES), lambda bi, i: (bi, i, 0)),
                  blk, _const_spec((1, HEAD_LANES))],
        out_specs=blk,
        out_shape=jax.ShapeDtypeStruct((b, t, dm), BF16),
        scratch_shapes=[pltpu.VMEM((nh, GDN_HEAD_DIM, GDN_HEAD_DIM), F32)],
        compiler_params=_params(("arbitrary", "arbitrary")),
        name="gdn_scan",
    )(u, w, qd, kd, qk, gates, z, norm_w)


def _post_kernel(*refs, n_mix, final_norm):
    x_ref = refs[0]
    mix_refs = refs[1:1 + n_mix]
    wout_ref, g_ref, wup_ref, wdn_ref, p_ref, wpp_ref, wpg_ref = refs[1 + n_mix:8 + n_mix]
    rest = refs[8 + n_mix:]
    if final_norm:
        gf_ref, o_ref = rest
    else:
        (o_ref,) = rest
    mix = mix_refs[0][...] if n_mix == 1 else jnp.concatenate([r[...] for r in mix_refs], axis=1)
    x = x_ref[...] + _dot(mix, wout_ref[...])
    h = _rms(x, g_ref[...]).astype(BF16)
    d_ff = wup_ref.shape[1]
    acc = x
    for s in range(d_ff // FF_SEG):
        a = jnp.maximum(_dot(h, wup_ref[:, s * FF_SEG:(s + 1) * FF_SEG]), 0.0)
        acc = acc + _dot((a * a).astype(BF16), wdn_ref[s * FF_SEG:(s + 1) * FF_SEG, :])
    x = acc
    gate = _sigmoid(_dot(x.astype(BF16), wpg_ref[...]))
    x = x + _dot(p_ref[...].astype(BF16), wpp_ref[...]) * gate
    if final_norm:
        x = _rms(x, gf_ref[...])
    o_ref[...] = x


def _post(x2d, mixes, wout, g, wup, wdn, p2d, wpp, wpg, gf=None):
    m, d = x2d.shape
    tm = TOKEN_TILE
    row = lambda i: (i, 0)
    single = pl.Buffered(1)
    const = lambda a: pl.BlockSpec(a.shape, lambda i: (0, 0), pipeline_mode=single)
    args = [x2d, *mixes, wout, g, wup, wdn, p2d, wpp, wpg]
    in_specs = ([pl.BlockSpec((tm, d), row)]
                + [pl.BlockSpec((tm, a.shape[1]), row) for a in mixes]
                + [const(wout), const(g), const(wup), const(wdn), pl.BlockSpec((tm, p2d.shape[1]), row), const(wpp), const(wpg)])
    if gf is not None:
        args.append(gf)
        in_specs.append(const(gf))
    kern = functools.partial(_post_kernel, n_mix=len(mixes), final_norm=gf is not None)
    return pl.pallas_call(
        kern,
        grid=(m // tm,),
        in_specs=in_specs,
        out_specs=pl.BlockSpec((tm, d), row),
        out_shape=jax.ShapeDtypeStruct((m, d), F32),
        compiler_params=_params(("arbitrary",)),
        name="out_proj_mlp_ple",
    )(*args)


def _pad_lanes(a):
    return jnp.pad(a, ((0, 0), (0, LANES - a.shape[1])))


def kernel(x, p, positions, norm_mix, norm_mlp, norm_final, w_in_even, conv_w, a_log, dt_bias, gdn_norm,
           lam_q1, lam_k1, lam_q2, lam_k2, diff_norm, w_out_even, w_in_odd, b_forget, w_out_odd,
           w_mlp_up, w_mlp_down, w_ple_proj, w_ple_gate):
    b, t, d = x.shape
    depth = p.shape[0]
    m = b * t
    assert t % TOKEN_TILE == 0 and d % PROJ_SEG == 0
    nh = GDN_HEADS
    gdn_w = 3 * nh * GDN_HEAD_DIM + nh * GDN_HEAD_DIM
    assert w_in_even.shape[2] == gdn_w + 2 * nh + 3 * DIFF_HEADS * 2 * DIFF_QK_DIM

    inv_freq = ROPE_THETA ** (-jnp.arange(0, DIFF_QK_DIM, 2, dtype=F32) / DIFF_QK_DIM)
    ang = positions.astype(F32)[..., None] * inv_freq
    cos, sin = jnp.cos(ang), jnp.sin(ang)
    cos_t = jnp.concatenate([cos, cos, cos, cos], axis=-1).reshape(m, LANES)
    sin_t = jnp.concatenate([-sin, sin, -sin, sin], axis=-1).reshape(m, LANES)

    x2d = x.reshape(m, d)
    for i in range(depth):
        j = i // 2
        g_mix = norm_mix[i].reshape(1, d)
        if i % 2 == 0:
            lambda_init = 0.8 - 0.6 * math.exp(-0.3 * i)
            w = w_in_even[j]
            wm = jnp.concatenate([w[:, :gdn_w], w[:, gdn_w + 2 * nh:]], axis=1).astype(BF16)
            wg = _pad_lanes(w[:, gdn_w:gdn_w + 2 * nh]).astype(BF16)
            alog_row = _pad_lanes(jnp.concatenate([jnp.zeros((nh,), F32), a_log[j]]).reshape(1, 2 * nh))
            dt_row = _pad_lanes(jnp.concatenate([jnp.zeros((nh,), F32), dt_bias[j]]).reshape(1, 2 * nh))
            qkv, z, qkb, vbt, gates = _even_in(x2d, g_mix, wm, wg, conv_w[j], alog_row, dt_row, cos_t, sin_t, t)
            qkv, z, qkb, gates = (a.reshape(b, t, -1) for a in (qkv, z, qkb, gates))
            u, wy, qd, kd, qk = _gdn_prep(qkv, gates)
            o_a = _gdn_scan(u, wy, qd, kd, qk, gates, z, gdn_norm[j].reshape(1, HEAD_LANES))
            lam_params = jnp.stack([lam_q1[j], lam_k1[j], lam_q2[j], lam_k2[j]])
            o_b = _diff_attention(qkb, vbt, lam_params, diff_norm[j].reshape(1, HEAD_LANES), lambda_init)
            mixes = [o_a.reshape(m, -1), o_b.reshape(m, -1)]
            wout = w_out_even[j].astype(BF16)
        else:
            w = w_in_odd[j]
            d_mix = (w.shape[1] - FOX_HEADS) // 4
            wm = w[:, :4 * d_mix].astype(BF16)
            wf = _pad_lanes(w[:, 4 * d_mix:]).astype(BF16)
            bf_row = _pad_lanes(b_forget[j].reshape(1, FOX_HEADS))
            q, k, vt, gate, qb, kb = _odd_in(x2d, g_mix, wm, wf, bf_row, t)
            q, k, gate, qb, kb = (a.reshape(b, t, -1) for a in (q, k, gate, qb, kb))
            o = _fox_attention(q, k, vt, gate, qb, kb)
            mixes = [o.reshape(m, -1)]
            wout = w_out_odd[j].astype(BF16)
        x2d = _post(x2d, mixes, wout, norm_mlp[i].reshape(1, d), w_mlp_up[i].astype(BF16),
                    w_mlp_down[i].astype(BF16), p[i].reshape(m, -1), w_ple_proj[i].astype(BF16),
                    w_ple_gate[i].astype(BF16), norm_final.reshape(1, d) if i == depth - 1 else None)
    return x2d.reshape(b, t, d)
```

```python
import functools
import math

import jax
import jax.numpy as jnp
import numpy as np
from jax import lax
from jax.experimental import pallas as pl
from jax.experimental.pallas import tpu as pltpu

F32 = jnp.float32
BF16 = jnp.bfloat16

GDN_HEADS = 4
GDN_HEAD_DIM = 128
GDN_CHUNK = 64
CONV_WIDTH = 4
DIFF_HEADS = 4
DIFF_QK_DIM = 64
FOX_HEADS = 8
HEAD_LANES = 128
ROPE_THETA = 10000.0
EPS = 1e-6
NEG_INF = -1e30
LOG2E = 1.4426950408889634
LANES = 128
SUBLANES = 8
VMEM_LIMIT_BYTES = 56 * 1024 * 1024

TOKEN_TILE = 512
PROJ_SEG = 512
FF_SEG = 1024
GDN_REFINE = True
ATTN_TILE = 512
ATTN_UNROLL = 2
BIAS_LANES_PER_HEAD = 16
ONES_ROWS = 16


def _dot(a, b):
    return jnp.dot(a, b, preferred_element_type=F32)


def _dot_exact(a, b):
    return jnp.dot(a, b, preferred_element_type=F32, precision=lax.Precision.HIGHEST)


def _dot_nt(a, b):
    return lax.dot_general(a, b, (((1,), (1,)), ((), ())), preferred_element_type=F32)


def _dot_tn(a, b):
    return lax.dot_general(a, b, (((0,), (0,)), ((), ())), preferred_element_type=F32)


def _rms(x, g):
    return x * lax.rsqrt(jnp.mean(x * x, axis=-1, keepdims=True) + EPS) * g


def _sigmoid(x):
    return 1.0 / (1.0 + jnp.exp(-x))


def _softplus(x):
    return jnp.maximum(x, 0.0) + jnp.log1p(jnp.exp(-jnp.abs(x)))


def _row_scan(x, period):
    rows = lax.broadcasted_iota(jnp.int32, x.shape, 0) % period
    s = 1
    while s < period:
        x = x + jnp.where(rows >= s, pltpu.roll(x, s, 0), 0.0)
        s *= 2
    return x


def _const_spec(shape):
    return pl.BlockSpec(shape, lambda *_: (0,) * len(shape))


def _params(sem):
    return pltpu.CompilerParams(dimension_semantics=sem, vmem_limit_bytes=VMEM_LIMIT_BYTES)


def _even_in_kernel(x_ref, g_ref, wm_ref, wg_ref, conv_ref, alog_ref, dt_ref, cos_ref, sin_ref,
                    qkv_ref, z_ref, qkb_ref, vbt_ref, gates_ref, h_ref, carry_ref, tr_ref, *, tiles_per_seq):
    tm = x_ref.shape[0]
    i = pl.program_id(0)
    h_ref[...] = _rms(x_ref[...], g_ref[...]).astype(BF16)
    seq_start = (i % tiles_per_seq) == 0

    row8 = lax.broadcasted_iota(jnp.int32, (SUBLANES, PROJ_SEG), 0)
    for s in range(3):
        cols = slice(s * PROJ_SEG, (s + 1) * PROJ_SEG)
        y = _dot(h_ref[...], wm_ref[:, cols])
        prev8 = jnp.where(seq_start, 0.0, carry_ref[:, cols])
        carry_ref[:, cols] = y[tm - SUBLANES:, :]
        w = conv_ref[:, cols]
        acc = y * w[CONV_WIDTH - 1:CONV_WIDTH, :]
        top = y[:SUBLANES, :] * w[CONV_WIDTH - 1:CONV_WIDTH, :]
        for k in range(1, CONV_WIDTH):
            wk = w[CONV_WIDTH - 1 - k:CONV_WIDTH - k, :]
            rolled = pltpu.roll(y, k, 0)
            acc = acc + rolled * wk
            top = top + jnp.where(row8 < k, pltpu.roll(prev8, k, 0), rolled[:SUBLANES, :]) * wk
        for part, rows in ((acc, slice(0, tm)), (top, slice(0, SUBLANES))):
            a = part * _sigmoid(part)
            if s < 2:
                outs = []
                for hd in range(GDN_HEADS):
                    seg = a[:, hd * HEAD_LANES:(hd + 1) * HEAD_LANES]
                    n = seg * lax.rsqrt(jnp.sum(seg * seg, axis=-1, keepdims=True) + EPS)
                    outs.append(n * (GDN_HEAD_DIM ** -0.5) if s == 0 else n)
                a = jnp.concatenate(outs, axis=1)
            qkv_ref[rows, cols] = a

    z_ref[...] = _dot(h_ref[...], wm_ref[:, 3 * PROJ_SEG:4 * PROJ_SEG]).astype(BF16)

    cos = jnp.concatenate([cos_ref[...]] * (PROJ_SEG // LANES), axis=1)
    sin = jnp.concatenate([sin_ref[...]] * (PROJ_SEG // LANES), axis=1)
    lane = lax.broadcasted_iota(jnp.int32, (tm, PROJ_SEG), 1)
    first_half = (lane % DIFF_QK_DIM) < (DIFF_QK_DIM // 2)
    for s, scale in ((4, DIFF_QK_DIM ** -0.5 * LOG2E), (5, 1.0)):
        y = _dot(h_ref[...], wm_ref[:, s * PROJ_SEG:(s + 1) * PROJ_SEG])
        swapped = jnp.where(first_half, pltpu.roll(y, PROJ_SEG - DIFF_QK_DIM // 2, 1),
                            pltpu.roll(y, DIFF_QK_DIM // 2, 1))
        r = y * cos + swapped * sin
        qkb_ref[:, (s - 4) * PROJ_SEG:(s - 3) * PROJ_SEG] = (r * scale).astype(BF16)
    tr_ref[...] = _dot(h_ref[...], wm_ref[:, 6 * PROJ_SEG:7 * PROJ_SEG])
    vbt_ref[0] = tr_ref[...].T.astype(BF16)

    graw = _dot(h_ref[...], wg_ref[...])
    beta = _sigmoid(graw)
    g = -jnp.exp(alog_ref[...]) * _softplus(graw + dt_ref[...])
    gc = _row_scan(g, GDN_CHUNK)
    lane_g = lax.broadcasted_iota(jnp.int32, (tm, LANES), 1)
    gates_ref[...] = jnp.where(lane_g < GDN_HEADS, beta, gc)


def _even_in(x2d, g, wm, wg, conv_w, alog_row, dt_row, cos_t, sin_t, seq_len):
    m, d = x2d.shape
    tm = TOKEN_TILE
    n_main = wm.shape[1]
    tps = seq_len // tm
    kern = functools.partial(_even_in_kernel, tiles_per_seq=tps)
    row = lambda i: (i, 0)
    return pl.pallas_call(
        kern,
        grid=(m // tm,),
        in_specs=[
            pl.BlockSpec((tm, d), row),
            _const_spec((1, d)),
            _const_spec((d, n_main)),
            _const_spec((d, LANES)),
            _const_spec(conv_w.shape),
            _const_spec((1, LANES)),
            _const_spec((1, LANES)),
            pl.BlockSpec((tm, LANES), row),
            pl.BlockSpec((tm, LANES), row),
        ],
        out_specs=[
            pl.BlockSpec((tm, 3 * PROJ_SEG), row),
            pl.BlockSpec((tm, PROJ_SEG), row),
            pl.BlockSpec((tm, 2 * PROJ_SEG), row),
            pl.BlockSpec((1, PROJ_SEG, tm), lambda i: (i // tps, 0, i % tps)),
            pl.BlockSpec((tm, LANES), row),
        ],
        out_shape=[
            jax.ShapeDtypeStruct((m, 3 * PROJ_SEG), F32),
            jax.ShapeDtypeStruct((m, PROJ_SEG), BF16),
            jax.ShapeDtypeStruct((m, 2 * PROJ_SEG), BF16),
            jax.ShapeDtypeStruct((m // seq_len, PROJ_SEG, seq_len), BF16),
            jax.ShapeDtypeStruct((m, LANES), F32),
        ],
        scratch_shapes=[pltpu.VMEM((tm, d), BF16), pltpu.VMEM((SUBLANES, 3 * PROJ_SEG), F32), pltpu.VMEM((tm, PROJ_SEG), F32)],
        compiler_params=_params(("arbitrary",)),
        name="even_in_proj",
    )(x2d, g, wm, wg, conv_w, alog_row, dt_row, cos_t, sin_t)


def _odd_in_kernel(x_ref, g_ref, wm_ref, wf_ref, bf_ref, sel_ref, ones_ref, q_ref, k_ref, vt_ref, gate_ref, qb_ref, kb_ref,
                   h_ref, carry_ref, tr_ref, *, tiles_per_seq, d_mix):
    tm = x_ref.shape[0]
    i = pl.program_id(0)
    h_ref[...] = _rms(x_ref[...], g_ref[...]).astype(BF16)
    head_dim = d_mix // FOX_HEADS
    for o_ref, base, scale in ((q_ref, 0, head_dim ** -0.5 * LOG2E), (k_ref, d_mix, 1.0),
                               (vt_ref, 2 * d_mix, 1.0), (gate_ref, 3 * d_mix, 1.0)):
        for s in range(d_mix // PROJ_SEG):
            cols = slice(s * PROJ_SEG, (s + 1) * PROJ_SEG)
            y = _dot(h_ref[...], wm_ref[:, base + s * PROJ_SEG:base + (s + 1) * PROJ_SEG])
            if o_ref is vt_ref:
                tr_ref[...] = y
                o_ref[0, cols, :] = tr_ref[...].T.astype(BF16)
            else:
                o_ref[:, cols] = (y * scale).astype(BF16)
    f = _dot(h_ref[...], wf_ref[...]) + bf_ref[...]
    log_f = jnp.minimum(f, 0.0) - jnp.log1p(jnp.exp(-jnp.abs(f)))
    prev = jnp.where((i % tiles_per_seq) == 0, 0.0, carry_ref[0:1, :])
    cum = _row_scan(log_f, tm) + prev
    carry_ref[...] = jnp.broadcast_to(cum[tm - 1:tm, :], carry_ref.shape)
    pieces = jnp.concatenate(_split_bf16(LOG2E * cum, 3), axis=1)
    lanes = _dot(pieces, sel_ref[...]) + ones_ref[...]
    qb_ref[...] = lanes[:, :LANES].astype(BF16)
    kb_ref[...] = lanes[:, LANES:].astype(BF16)


def _bias_lane_tables():
    sel = np.zeros((3 * LANES, 2 * LANES), np.float32)
    ones = np.zeros((1, 2 * LANES), np.float32)
    for h in range(FOX_HEADS):
        base = BIAS_LANES_PER_HEAD * h
        for piece in range(3):
            sel[LANES * piece + h, base + 3 + piece] = 1.0
            sel[LANES * piece + h, LANES + base + piece] = -1.0
            ones[0, base + piece] = 1.0
            ones[0, LANES + base + 3 + piece] = 1.0
    return jnp.asarray(sel, BF16), jnp.asarray(ones, F32)


def _odd_in(x2d, g, wm, wf, bf_row, seq_len):
    m, d = x2d.shape
    sel, ones_row = _bias_lane_tables()
    tm = TOKEN_TILE
    d_mix = wm.shape[1] // 4
    tps = seq_len // tm
    kern = functools.partial(_odd_in_kernel, tiles_per_seq=tps, d_mix=d_mix)
    row = lambda i: (i, 0)
    row_blk = pl.BlockSpec((tm, d_mix), row)
    row_shape = jax.ShapeDtypeStruct((m, d_mix), BF16)
    return pl.pallas_call(
        kern,
        grid=(m // tm,),
        in_specs=[
            pl.BlockSpec((tm, d), row),
            _const_spec((1, d)),
            _const_spec(wm.shape),
            _const_spec((d, LANES)),
            _const_spec((1, LANES)),
            _const_spec(sel.shape),
            _const_spec(ones_row.shape),
        ],
        out_specs=[row_blk, row_blk, pl.BlockSpec((1, d_mix, tm), lambda i: (i // tps, 0, i % tps)), row_blk,
                   pl.BlockSpec((tm, LANES), row), pl.BlockSpec((tm, LANES), row)],
        out_shape=[row_shape, row_shape, jax.ShapeDtypeStruct((m // seq_len, d_mix, seq_len), BF16), row_shape,
                   jax.ShapeDtypeStruct((m, LANES), BF16), jax.ShapeDtypeStruct((m, LANES), BF16)],
        scratch_shapes=[pltpu.VMEM((tm, d), BF16), pltpu.VMEM((SUBLANES, LANES), F32), pltpu.VMEM((tm, PROJ_SEG), F32)],
        compiler_params=_params(("arbitrary",)),
        name="odd_in_proj",
    )(x2d, g, wm, wf, bf_row, sel, ones_row)


def _attn_kernel(*refs, tq, fox, lambda_init):
    if fox:
        q_ref, k_ref, vt_ref, gate_ref, qb_ref, kball_ref, o_ref, st_ref, m_ref, acc_ref, kb_ref = refs
    else:
        q_ref, k_ref, vt_ref, lam_ref, nw_ref, o_ref, st_ref, m_ref, acc_ref = refs
    tk = tq
    hg = pl.program_id(1)
    qi = pl.program_id(2)
    head = lambda g: slice(g * HEAD_LANES, (g + 1) * HEAD_LANES)

    if fox:
        @pl.when(qi == 0)
        def _():
            kb_all = kball_ref[0]
            owner = lax.broadcasted_iota(jnp.int32, kb_all.shape, 1) // BIAS_LANES_PER_HEAD
            for g in range(2):
                kb_ref[:, head(g)] = jnp.where(owner == hg * 2 + g, kb_all, jnp.zeros_like(kb_all))

        qs = [jnp.concatenate([q_ref[0, :, head(g)], qb_ref[0]], axis=1) for g in range(2)]
        kv = [head(0), head(1)]
    else:
        q = q_ref[0]
        lane = lax.broadcasted_iota(jnp.int32, q.shape, 1)
        zero = jnp.zeros_like(q)
        qs = [jnp.where(lane < DIFF_QK_DIM, q, zero), jnp.where(lane >= DIFF_QK_DIM, q, zero)]
        kv = [head(0), head(0)]
    ones = jnp.ones((ONES_ROWS, tk), BF16)

    def scores(g, j):
        k0 = pl.multiple_of(j * tk, tk)
        kj = k_ref[0, pl.ds(k0, tk), kv[g]]
        if fox:
            kj = jnp.concatenate([kj, kb_ref[pl.ds(k0, tk), kv[g]]], axis=1)
        return _dot_nt(kj, qs[g])

    def absorb(g, j, st, masked):
        m = m_ref[g]
        k0 = pl.multiple_of(j * tk, tk)
        if masked:
            kpos = k0 + lax.broadcasted_iota(jnp.int32, (tk, tq), 0)
            qpos = qi * tq + lax.broadcasted_iota(jnp.int32, (tk, tq), 1)
            st = jnp.where(qpos >= kpos, st, NEG_INF)
        m_new = jnp.maximum(m, jnp.max(st, axis=0, keepdims=True))
        p = jnp.exp2(st - m_new).astype(BF16)
        vt = jnp.concatenate([vt_ref[0, kv[g], pl.ds(k0, tk)], ones], axis=0)
        acc_ref[g] = jnp.exp2(m - m_new) * acc_ref[g] + _dot(vt, p)
        m_ref[g] = m_new

    def step(j):
        st1 = scores(1, j)
        absorb(0, j, st_ref[...], False)
        st_ref[...] = scores(0, j + 1)
        absorb(1, j, st1, False)

    def steps(j, carry):
        for u in range(ATTN_UNROLL):
            step(j * ATTN_UNROLL + u)
        return carry

    def tail_step(j, carry):
        step(j)
        return carry

    m_ref[...] = jnp.full(m_ref.shape, NEG_INF, F32)
    acc_ref[...] = jnp.zeros(acc_ref.shape, F32)
    st_ref[...] = scores(0, 0)
    n_full = qi // ATTN_UNROLL
    lax.fori_loop(0, n_full, steps, 0)
    lax.fori_loop(n_full * ATTN_UNROLL, qi, tail_step, 0)
    st1 = scores(1, qi)
    absorb(0, qi, st_ref[...], True)
    absorb(1, qi, st1, True)
    outs = [acc_ref[g, :HEAD_LANES, :] / acc_ref[g, HEAD_LANES:HEAD_LANES + 1, :] for g in range(2)]
    if fox:
        for g in range(2):
            o_ref[0, :, head(g)] = (outs[g].T * _sigmoid(gate_ref[0, :, head(g)].astype(F32))).astype(o_ref.dtype)
    else:
        lam_p = lam_ref[...]
        lam = (jnp.exp(jnp.sum(lam_p[0:1] * lam_p[1:2], axis=1, keepdims=True))
               - jnp.exp(jnp.sum(lam_p[2:3] * lam_p[3:4], axis=1, keepdims=True)) + lambda_init)
        o = (outs[0] - lam * outs[1]).T
        o_ref[0] = (_rms(o, nw_ref[...]) * (1.0 - lambda_init)).astype(o_ref.dtype)


def _attn_state(tq):
    return [pltpu.VMEM((tq, tq), F32), pltpu.VMEM((2, 1, tq), F32), pltpu.VMEM((2, HEAD_LANES + ONES_ROWS, tq), F32)]


def _fox_attention(q, k, vt, gate, qb, kb, *, tq=ATTN_TILE):
    b, t, dm = q.shape
    width = 2 * HEAD_LANES
    kern = functools.partial(_attn_kernel, tq=tq, fox=True, lambda_init=0.0)
    qblk = pl.BlockSpec((1, tq, width), lambda bi, h, i: (bi, i, h))
    return pl.pallas_call(
        kern,
        grid=(b, dm // width, t // tq),
        in_specs=[qblk,
                  pl.BlockSpec((1, t, width), lambda bi, h, i: (bi, 0, h)),
                  pl.BlockSpec((1, width, t), lambda bi, h, i: (bi, h, 0)),
                  qblk,
                  pl.BlockSpec((1, tq, LANES), lambda bi, h, i: (bi, i, 0)),
                  pl.BlockSpec((1, t, LANES), lambda bi, h, i: (bi, 0, 0))],
        out_specs=qblk,
        out_shape=jax.ShapeDtypeStruct((b, t, dm), BF16),
        scratch_shapes=_attn_state(tq) + [pltpu.VMEM((t, width), BF16)],
        compiler_params=_params(("arbitrary", "arbitrary", "arbitrary")),
        name="fox_attention",
    )(q, k, vt, gate, qb, kb)


def _diff_attention(qk, vt, lam_params, norm_w, lambda_init, *, tq=ATTN_TILE):
    b, t, _ = qk.shape
    nh = DIFF_HEADS
    kern = functools.partial(_attn_kernel, tq=tq, fox=False, lambda_init=lambda_init)
    qblk = pl.BlockSpec((1, tq, HEAD_LANES), lambda bi, h, i: (bi, i, h))
    return pl.pallas_call(
        kern,
        grid=(b, nh, t // tq),
        in_specs=[qblk,
                  pl.BlockSpec((1, t, HEAD_LANES), lambda bi, h, i: (bi, 0, nh + h)),
                  pl.BlockSpec((1, HEAD_LANES, t), lambda bi, h, i: (bi, h, 0)),
                  _const_spec(lam_params.shape), _const_spec((1, HEAD_LANES))],
        out_specs=qblk,
        out_shape=jax.ShapeDtypeStruct((b, t, nh * HEAD_LANES), BF16),
        scratch_shapes=_attn_state(tq),
        compiler_params=_params(("arbitrary", "arbitrary", "arbitrary")),
        name="diff_attention",
    )(qk, qk, vt, lam_params, norm_w)


def _split_bf16(x, parts):
    out = []
    for _ in range(parts):
        piece = x.astype(BF16)
        out.append(piece)
        x = x - piece.astype(F32)
    return out


def _dot_split(a, b):
    a_hi, a_lo = _split_bf16(a, 2)
    b_hi, b_lo = _split_bf16(b, 2)
    return _dot(a_hi, b_hi) + (_dot(a_hi, b_lo) + _dot(a_lo, b_hi))


def _gdn_prep_kernel(q_ref, k_ref, v_ref, gates_ref, u_ref, w_ref, qd_ref, kd_ref, qk_ref, *, refine):
    c = GDN_CHUNK
    tt = q_ref.shape[1]
    heads = range(GDN_HEADS)
    gt = gates_ref[0]
    lane = lax.broadcasted_iota(jnp.int32, (tt, LANES), 1)
    ri = lax.broadcasted_iota(jnp.int32, (tt, tt), 0)
    ci = lax.broadcasted_iota(jnp.int32, (tt, tt), 1)
    chunk_start = ri - ri % c
    incl = lambda a: jnp.where(ci <= ri, jnp.where(ci >= chunk_start, a, 0.0), 0.0)
    strict = lambda a: jnp.where(ci < ri, jnp.where(ci >= chunk_start, a, 0.0), 0.0)
    ident = jnp.where(ri == ci, 1.0, 0.0)
    ones = jnp.ones((tt, LANES), BF16)
    cols = [slice(hd * HEAD_LANES, (hd + 1) * HEAD_LANES) for hd in heads]
    kt = [k_ref[0, :, cols[hd]] for hd in heads]
    beta = [gt[:, hd:hd + 1] for hd in heads]
    gcc = [gt[:, GDN_HEADS + hd:GDN_HEADS + hd + 1] for hd in heads]
    k16 = [kt[hd].astype(BF16) for hd in heads]
    kb = [kt[hd] * beta[hd] for hd in heads]

    gc_row = []
    for hd in heads:
        g_hi, g_mid, g_lo = (piece.astype(F32) for piece in _split_bf16(gcc[hd], 3))
        pieces = jnp.where(lane == 0, g_hi, jnp.where(lane == 1, g_mid, jnp.where(lane == 2, g_lo, 0.0)))
        gc_row.append(_dot_nt(ones, pieces.astype(BF16)))
    kk = [_dot_nt(kb[hd].astype(BF16), k16[hd]) for hd in heads]
    qk_raw = [_dot_nt(q_ref[0, :, cols[hd]].astype(BF16), k16[hd]) for hd in heads]
    decay = [incl(jnp.exp(incl(gcc[hd] - gc_row[hd]))) for hd in heads]
    lower = [strict(kk[hd] * decay[hd]) for hd in heads]

    inv = [ident - lower[hd] for hd in heads]
    l16 = [lower[hd].astype(BF16) for hd in heads]
    power = [_dot(l16[hd], l16[hd]).astype(BF16) for hd in heads]
    n_sq = int(math.log2(c)) - 1
    for step in range(n_sq - 1):
        r = [_dot(jnp.concatenate([power[hd], inv[hd].astype(BF16)], axis=0), power[hd]) for hd in heads]
        power = [r[hd][:tt].astype(BF16) for hd in heads]
        inv = [inv[hd] + r[hd][tt:] for hd in heads]
    r = [_dot(inv[hd].astype(BF16), power[hd]) for hd in heads]
    inv16 = [(inv[hd] + r[hd]).astype(BF16) for hd in heads]

    eg = [jnp.exp(gcc[hd]) for hd in heads]
    rhs = [jnp.concatenate([v_ref[0, :, cols[hd]] * beta[hd], kb[hd] * eg[hd]], axis=1) for hd in heads]
    sol = [_dot(inv16[hd], rhs[hd].astype(BF16)) for hd in heads]
    if refine:
        a_hi, a_lo, s_hi, s_lo = [], [], [], []
        for hd in heads:
            hi, lo = _split_bf16(ident + lower[hd], 2)
            a_hi.append(hi)
            a_lo.append(lo)
            hi, lo = _split_bf16(sol[hd], 2)
            s_hi.append(hi)
            s_lo.append(lo)
        prod = [_dot(a_hi[hd], s_hi[hd]) + (_dot(a_hi[hd], s_lo[hd]) + _dot(a_lo[hd], s_hi[hd])) for hd in heads]
        corr = [_dot(inv16[hd], (rhs[hd] - prod[hd]).astype(BF16)) for hd in heads]
        sol = [sol[hd] + corr[hd] for hd in heads]
    for hd in heads:
        u_ref[0, :, cols[hd]] = sol[hd][:, :HEAD_LANES]
        w_ref[0, :, cols[hd]] = sol[hd][:, HEAD_LANES:].astype(BF16)
        qk = incl(qk_raw[hd] * decay[hd])
        qd_ref[0, :, cols[hd]] = (q_ref[0, :, cols[hd]] * eg[hd]).astype(BF16)
        for n in range(tt // c):
            rows = slice(n * c, (n + 1) * c)
            qk_ref[0, hd, rows, :] = qk[rows, rows].astype(BF16)
            gl = gcc[hd][(n + 1) * c - 1:(n + 1) * c, :]
            kd_ref[0, rows, cols[hd]] = (kt[hd][rows] * jnp.exp(gl - gcc[hd][rows])).astype(BF16)


def _gdn_prep(qkv, gates, *, tt=256):
    b, t, _ = qkv.shape
    nh = GDN_HEADS
    dm = nh * HEAD_LANES
    blk = lambda part: pl.BlockSpec((1, tt, dm), lambda bi, i: (bi, i, part))
    return pl.pallas_call(
        functools.partial(_gdn_prep_kernel, refine=GDN_REFINE),
        grid=(b, t // tt),
        in_specs=[blk(0), blk(1), blk(2), pl.BlockSpec((1, tt, LANES), lambda bi, i: (bi, i, 0))],
        out_specs=[blk(0)] * 4 + [pl.BlockSpec((1, nh, tt, GDN_CHUNK), lambda bi, i: (bi, 0, i, 0))],
        out_shape=[jax.ShapeDtypeStruct((b, t, dm), F32)]
        + [jax.ShapeDtypeStruct((b, t, dm), BF16)] * 3
        + [jax.ShapeDtypeStruct((b, nh, t, GDN_CHUNK), BF16)],
        compiler_params=_params(("arbitrary", "arbitrary")),
        name="gdn_prep",
    )(qkv, qkv, qkv, gates)


def _gdn_scan_kernel(u_ref, w_ref, qd_ref, kd_ref, qk_ref, gates_ref, z_ref, nw_ref, o_ref, s_ref):
    c = GDN_CHUNK
    tt = u_ref.shape[1]

    @pl.when(pl.program_id(1) == 0)
    def _():
        s_ref[...] = jnp.zeros_like(s_ref)

    def chunk(n, _):
        r0 = pl.multiple_of(n * c, c)
        rows = pl.ds(r0, c)
        g_last = gates_ref[0, pl.ds(r0 + c - SUBLANES, SUBLANES), :]
        heads = range(GDN_HEADS)
        cols = [slice(hd * HEAD_LANES, (hd + 1) * HEAD_LANES) for hd in heads]
        state = [s_ref[hd] for hd in heads]
        r = [_dot(jnp.concatenate([w_ref[0, rows, cols[hd]], qd_ref[0, rows, cols[hd]]], axis=0),
                  state[hd].astype(BF16)) for hd in heads]
        v_new = [(u_ref[0, rows, cols[hd]] - r[hd][:c]).astype(BF16) for hd in heads]
        intra = [_dot(qk_ref[0, hd, rows, :], v_new[hd]) for hd in heads]
        upd = [_dot_tn(kd_ref[0, rows, cols[hd]], v_new[hd]) for hd in heads]
        for hd in heads:
            decay_last = jnp.exp(g_last[SUBLANES - 1:SUBLANES, GDN_HEADS + hd:GDN_HEADS + hd + 1])
            s_ref[hd] = state[hd] * decay_last + upd[hd]
            zt = z_ref[0, rows, cols[hd]].astype(F32)
            o = r[hd][c:] + intra[hd]
            o_ref[0, rows, cols[hd]] = (_rms(o, nw_ref[...]) * (zt * _sigmoid(zt))).astype(o_ref.dtype)
        return 0

    lax.fori_loop(0, tt // c, chunk, 0)


def _gdn_scan(u, w, qd, kd, qk, gates, z, norm_w, *, tt=512):
    b, t, dm = u.shape
    nh = GDN_HEADS
    blk = pl.BlockSpec((1, tt, dm), lambda bi, i: (bi, i, 0))
    return pl.pallas_call(
        _gdn_scan_kernel,
        grid=(b, t // tt),
        in_specs=[blk, blk, blk, blk,
                  pl.BlockSpec((1, nh, tt, GDN_CHUNK), lambda bi, i: (bi, 0, i, 0)),
                  pl.BlockSpec((1, tt, LANES), lambda bi, i: (bi, i, 0)),
                  blk, _const_spec((1, HEAD_LANES))],
        out_specs=blk,
        out_shape=jax.ShapeDtypeStruct((b, t, dm), BF16),
        scratch_shapes=[pltpu.VMEM((nh, GDN_HEAD_DIM, GDN_HEAD_DIM), F32)],
        compiler_params=_params(("arbitrary", "arbitrary")),
        name="gdn_scan",
    )(u, w, qd, kd, qk, gates, z, norm_w)


def _post_kernel(*refs, n_mix, final_norm):
    x_ref = refs[0]
    mix_refs = refs[1:1 + n_mix]
    wout_ref, g_ref, wup_ref, wdn_ref, p_ref, wpp_ref, wpg_ref = refs[1 + n_mix:8 + n_mix]
    rest = refs[8 + n_mix:]
    if final_norm:
        gf_ref, o_ref = rest
    else:
        (o_ref,) = rest
    mix = mix_refs[0][...] if n_mix == 1 else jnp.concatenate([r[...] for r in mix_refs], axis=1)
    x = x_ref[...] + _dot(mix, wout_ref[...])
    h = _rms(x, g_ref[...]).astype(BF16)
    d_ff = wup_ref.shape[1]
    acc = x
    for s in range(d_ff // FF_SEG):
        a = jnp.maximum(_dot(h, wup_ref[:, s * FF_SEG:(s + 1) * FF_SEG]), 0.0)
        acc = acc + _dot((a * a).astype(BF16), wdn_ref[s * FF_SEG:(s + 1) * FF_SEG, :])
    x = acc
    gate = _sigmoid(_dot(x.astype(BF16), wpg_ref[...]))
    x = x + _dot(p_ref[...].astype(BF16), wpp_ref[...]) * gate
    if final_norm:
        x = _rms(x, gf_ref[...])
    o_ref[...] = x


def _post(x2d, mixes, wout, g, wup, wdn, p2d, wpp, wpg, gf=None):
    m, d = x2d.shape
    tm = TOKEN_TILE
    row = lambda i: (i, 0)
    single = pl.Buffered(1)
    const = lambda a: pl.BlockSpec(a.shape, lambda i: (0, 0), pipeline_mode=single)
    args = [x2d, *mixes, wout, g, wup, wdn, p2d, wpp, wpg]
    in_specs = ([pl.BlockSpec((tm, d), row)]
                + [pl.BlockSpec((tm, a.shape[1]), row) for a in mixes]
                + [const(wout), const(g), const(wup), const(wdn), pl.BlockSpec((tm, p2d.shape[1]), row), const(wpp), const(wpg)])
    if gf is not None:
        args.append(gf)
        in_specs.append(const(gf))
    kern = functools.partial(_post_kernel, n_mix=len(mixes), final_norm=gf is not None)
    return pl.pallas_call(
        kern,
        grid=(m // tm,),
        in_specs=in_specs,
        out_specs=pl.BlockSpec((tm, d), row),
        out_shape=jax.ShapeDtypeStruct((m, d), F32),
        compiler_params=_params(("arbitrary",)),
        name="out_proj_mlp_ple",
    )(*args)


def _pad_lanes(a):
    return jnp.pad(a, ((0, 0), (0, LANES - a.shape[1])))


def kernel(x, p, positions, norm_mix, norm_mlp, norm_final, w_in_even, conv_w, a_log, dt_bias, gdn_norm,
           lam_q1, lam_k1, lam_q2, lam_k2, diff_norm, w_out_even, w_in_odd, b_forget, w_out_odd,
           w_mlp_up, w_mlp_down, w_ple_proj, w_ple_gate):
    b, t, d = x.shape
    depth = p.shape[0]
    m = b * t
    assert t % TOKEN_TILE == 0 and d % PROJ_SEG == 0
    nh = GDN_HEADS
    gdn_w = 3 * nh * GDN_HEAD_DIM + nh * GDN_HEAD_DIM
    assert w_in_even.shape[2] == gdn_w + 2 * nh + 3 * DIFF_HEADS * 2 * DIFF_QK_DIM

    inv_freq = ROPE_THETA ** (-jnp.arange(0, DIFF_QK_DIM, 2, dtype=F32) / DIFF_QK_DIM)
    ang = positions.astype(F32)[..., None] * inv_freq
    cos, sin = jnp.cos(ang), jnp.sin(ang)
    cos_t = jnp.concatenate([cos, cos, cos, cos], axis=-1).reshape(m, LANES)
    sin_t = jnp.concatenate([-sin, sin, -sin, sin], axis=-1).reshape(m, LANES)

    x2d = x.reshape(m, d)
    for i in range(depth):
        j = i // 2
        g_mix = norm_mix[i].reshape(1, d)
        if i % 2 == 0:
            lambda_init = 0.8 - 0.6 * math.exp(-0.3 * i)
            w = w_in_even[j]
            wm = jnp.concatenate([w[:, :gdn_w], w[:, gdn_w + 2 * nh:]], axis=1).astype(BF16)
            wg = _pad_lanes(w[:, gdn_w:gdn_w + 2 * nh]).astype(BF16)
            alog_row = _pad_lanes(jnp.concatenate([jnp.zeros((nh,), F32), a_log[j]]).reshape(1, 2 * nh))
            dt_row = _pad_lanes(jnp.concatenate([jnp.zeros((nh,), F32), dt_bias[j]]).reshape(1, 2 * nh))
            qkv, z, qkb, vbt, gates = _even_in(x2d, g_mix, wm, wg, conv_w[j], alog_row, dt_row, cos_t, sin_t, t)
            qkv, z, qkb, gates = (a.reshape(b, t, -1) for a in (qkv, z, qkb, gates))
            u, wy, qd, kd, qk = _gdn_prep(qkv, gates)
            o_a = _gdn_scan(u, wy, qd, kd, qk, gates, z, gdn_norm[j].reshape(1, HEAD_LANES))
            lam_params = jnp.stack([lam_q1[j], lam_k1[j], lam_q2[j], lam_k2[j]])
            o_b = _diff_attention(qkb, vbt, lam_params, diff_norm[j].reshape(1, HEAD_LANES), lambda_init)
            mixes = [o_a.reshape(m, -1), o_b.reshape(m, -1)]
            wout = w_out_even[j].astype(BF16)
        else:
            w = w_in_odd[j]
            d_mix = (w.shape[1] - FOX_HEADS) // 4
            wm = w[:, :4 * d_mix].astype(BF16)
            wf = _pad_lanes(w[:, 4 * d_mix:]).astype(BF16)
            bf_row = _pad_lanes(b_forget[j].reshape(1, FOX_HEADS))
            q, k, vt, gate, qb, kb = _odd_in(x2d, g_mix, wm, wf, bf_row, t)
            q, k, gate, qb, kb = (a.reshape(b, t, -1) for a in (q, k, gate, qb, kb))
            o = _fox_attention(q, k, vt, gate, qb, kb)
            mixes = [o.reshape(m, -1)]
            wout = w_out_odd[j].astype(BF16)
        x2d = _post(x2d, mixes, wout, norm_mlp[i].reshape(1, d), w_mlp_up[i].astype(BF16),
                    w_mlp_down[i].astype(BF16), p[i].reshape(m, -1), w_ple_proj[i].astype(BF16),
                    w_ple_gate[i].astype(BF16), norm_final.reshape(1, d) if i == depth - 1 else None)
    return x2d.reshape(b, t, d)
```

```python
import functools
import math

import jax
import jax.numpy as jnp
import numpy as np
from jax import lax
from jax.experimental import pallas as pl
from jax.experimental.pallas import tpu as pltpu

F32 = jnp.float32
BF16 = jnp.bfloat16

GDN_HEADS = 4
GDN_HEAD_DIM = 128
GDN_CHUNK = 64
CONV_WIDTH = 4
DIFF_HEADS = 4
DIFF_QK_DIM = 64
FOX_HEADS = 8
HEAD_LANES = 128
ROPE_THETA = 10000.0
EPS = 1e-6
NEG_INF = -1e30
LOG2E = 1.4426950408889634
LANES = 128
SUBLANES = 8
VMEM_LIMIT_BYTES = 56 * 1024 * 1024

TOKEN_TILE = 512
PROJ_SEG = 512
FF_SEG = 1024
GDN_REFINE = True
ATTN_TILE = 512
ATTN_UNROLL = 2
BIAS_LANES_PER_HEAD = 16
ONES_ROWS = 16


def _dot(a, b):
    return jnp.dot(a, b, preferred_element_type=F32)


def _dot_exact(a, b):
    return jnp.dot(a, b, preferred_element_type=F32, precision=lax.Precision.HIGHEST)


def _dot_nt(a, b):
    return lax.dot_general(a, b, (((1,), (1,)), ((), ())), preferred_element_type=F32)


def _dot_tn(a, b):
    return lax.dot_general(a, b, (((0,), (0,)), ((), ())), preferred_element_type=F32)


def _rms(x, g):
    return x * lax.rsqrt(jnp.mean(x * x, axis=-1, keepdims=True) + EPS) * g


def _sigmoid(x):
    return 1.0 / (1.0 + jnp.exp(-x))


def _softplus(x):
    return jnp.maximum(x, 0.0) + jnp.log1p(jnp.exp(-jnp.abs(x)))


def _row_scan(x, period):
    rows = lax.broadcasted_iota(jnp.int32, x.shape, 0) % period
    s = 1
    while s < period:
        x = x + jnp.where(rows >= s, pltpu.roll(x, s, 0), 0.0)
        s *= 2
    return x


def _const_spec(shape):
    return pl.BlockSpec(shape, lambda *_: (0,) * len(shape))


def _params(sem):
    return pltpu.CompilerParams(dimension_semantics=sem, vmem_limit_bytes=VMEM_LIMIT_BYTES)


def _even_in_kernel(x_ref, g_ref, wm_ref, wg_ref, conv_ref, alog_ref, dt_ref, cos_ref, sin_ref,
                    qkv_ref, z_ref, qkb_ref, vbt_ref, gates_ref, h_ref, carry_ref, tr_ref, *, tiles_per_seq):
    tm = x_ref.shape[0]
    i = pl.program_id(0)
    h_ref[...] = _rms(x_ref[...], g_ref[...]).astype(BF16)
    seq_start = (i % tiles_per_seq) == 0
    seg = lambda s: slice(s * PROJ_SEG, (s + 1) * PROJ_SEG)
    project = lambda s: _dot(h_ref[...], wm_ref[:, seg(s)])

    def gdn_qkv(s, y):
        cols = seg(s)
        row8 = lax.broadcasted_iota(jnp.int32, (SUBLANES, PROJ_SEG), 0)
        prev8 = jnp.where(seq_start, 0.0, carry_ref[:, cols])
        carry_ref[:, cols] = y[tm - SUBLANES:, :]
        w = conv_ref[:, cols]
        acc = y * w[CONV_WIDTH - 1:CONV_WIDTH, :]
        top = y[:SUBLANES, :] * w[CONV_WIDTH - 1:CONV_WIDTH, :]
        for k in range(1, CONV_WIDTH):
            wk = w[CONV_WIDTH - 1 - k:CONV_WIDTH - k, :]
            rolled = pltpu.roll(y, k, 0)
            acc = acc + rolled * wk
            top = top + jnp.where(row8 < k, pltpu.roll(prev8, k, 0), rolled[:SUBLANES, :]) * wk
        for part, rows in ((acc, slice(0, tm)), (top, slice(0, SUBLANES))):
            a = part * _sigmoid(part)
            if s < 2:
                outs = []
                for hd in range(GDN_HEADS):
                    blk = a[:, hd * HEAD_LANES:(hd + 1) * HEAD_LANES]
                    n = blk * lax.rsqrt(jnp.sum(blk * blk, axis=-1, keepdims=True) + EPS)
                    outs.append(n * (GDN_HEAD_DIM ** -0.5) if s == 0 else n)
                a = jnp.concatenate(outs, axis=1)
            qkv_ref[rows, cols] = a

    def gdn_gate(s, y):
        z_ref[...] = y.astype(BF16)

    def diff_qk(s, y):
        cos = jnp.concatenate([cos_ref[...]] * (PROJ_SEG // LANES), axis=1)
        sin = jnp.concatenate([sin_ref[...]] * (PROJ_SEG // LANES), axis=1)
        lane = lax.broadcasted_iota(jnp.int32, (tm, PROJ_SEG), 1)
        first_half = (lane % DIFF_QK_DIM) < (DIFF_QK_DIM // 2)
        swapped = jnp.where(first_half, pltpu.roll(y, PROJ_SEG - DIFF_QK_DIM // 2, 1),
                            pltpu.roll(y, DIFF_QK_DIM // 2, 1))
        scale = DIFF_QK_DIM ** -0.5 * LOG2E if s == 4 else 1.0
        qkb_ref[:, seg(s - 4)] = ((y * cos + swapped * sin) * scale).astype(BF16)

    def diff_v(s, y):
        tr_ref[...] = y
        vbt_ref[0] = tr_ref[...].T.astype(BF16)

    stages = ((0, gdn_qkv), (3, gdn_gate), (1, gdn_qkv), (6, diff_v), (2, gdn_qkv), (4, diff_qk), (5, diff_qk))
    pending = project(stages[0][0])
    for n, (s, epilogue) in enumerate(stages):
        upcoming = project(stages[n + 1][0]) if n + 1 < len(stages) else _dot(h_ref[...], wg_ref[...])
        epilogue(s, pending)
        pending = upcoming

    graw = pending
    beta = _sigmoid(graw)
    g = -jnp.exp(alog_ref[...]) * _softplus(graw + dt_ref[...])
    gc = _row_scan(g, GDN_CHUNK)
    lane_g = lax.broadcasted_iota(jnp.int32, (tm, LANES), 1)
    gates_ref[...] = jnp.where(lane_g < GDN_HEADS, beta, gc)


def _even_in(x2d, g, wm, wg, conv_w, alog_row, dt_row, cos_t, sin_t, seq_len):
    m, d = x2d.shape
    tm = TOKEN_TILE
    n_main = wm.shape[1]
    tps = seq_len // tm
    kern = functools.partial(_even_in_kernel, tiles_per_seq=tps)
    row = lambda i: (i, 0)
    return pl.pallas_call(
        kern,
        grid=(m // tm,),
        in_specs=[
            pl.BlockSpec((tm, d), row),
            _const_spec((1, d)),
            _const_spec((d, n_main)),
            _const_spec((d, LANES)),
            _const_spec(conv_w.shape),
            _const_spec((1, LANES)),
            _const_spec((1, LANES)),
            pl.BlockSpec((tm, LANES), row),
            pl.BlockSpec((tm, LANES), row),
        ],
        out_specs=[
            pl.BlockSpec((tm, 3 * PROJ_SEG), row),
            pl.BlockSpec((tm, PROJ_SEG), row),
            pl.BlockSpec((tm, 2 * PROJ_SEG), row),
            pl.BlockSpec((1, PROJ_SEG, tm), lambda i: (i // tps, 0, i % tps)),
            pl.BlockSpec((tm, LANES), row),
        ],
        out_shape=[
            jax.ShapeDtypeStruct((m, 3 * PROJ_SEG), F32),
            jax.ShapeDtypeStruct((m, PROJ_SEG), BF16),
            jax.ShapeDtypeStruct((m, 2 * PROJ_SEG), BF16),
            jax.ShapeDtypeStruct((m // seq_len, PROJ_SEG, seq_len), BF16),
            jax.ShapeDtypeStruct((m, LANES), F32),
        ],
        scratch_shapes=[pltpu.VMEM((tm, d), BF16), pltpu.VMEM((SUBLANES, 3 * PROJ_SEG), F32), pltpu.VMEM((tm, PROJ_SEG), F32)],
        compiler_params=_params(("arbitrary",)),
        name="even_in_proj",
    )(x2d, g, wm, wg, conv_w, alog_row, dt_row, cos_t, sin_t)


def _odd_in_kernel(x_ref, g_ref, wm_ref, wf_ref, bf_ref, sel_ref, ones_ref, q_ref, k_ref, vt_ref, gate_ref, qb_ref, kb_ref,
                   h_ref, carry_ref, tr_ref, *, tiles_per_seq, d_mix):
    tm = x_ref.shape[0]
    i = pl.program_id(0)
    h_ref[...] = _rms(x_ref[...], g_ref[...]).astype(BF16)
    head_dim = d_mix // FOX_HEADS
    for o_ref, base, scale in ((q_ref, 0, head_dim ** -0.5 * LOG2E), (k_ref, d_mix, 1.0),
                               (vt_ref, 2 * d_mix, 1.0), (gate_ref, 3 * d_mix, 1.0)):
        for s in range(d_mix // PROJ_SEG):
            cols = slice(s * PROJ_SEG, (s + 1) * PROJ_SEG)
            y = _dot(h_ref[...], wm_ref[:, base + s * PROJ_SEG:base + (s + 1) * PROJ_SEG])
            if o_ref is vt_ref:
                tr_ref[...] = y
                o_ref[0, cols, :] = tr_ref[...].T.astype(BF16)
            else:
                o_ref[:, cols] = (y * scale).astype(BF16)
    f = _dot(h_ref[...], wf_ref[...]) + bf_ref[...]
    log_f = jnp.minimum(f, 0.0) - jnp.log1p(jnp.exp(-jnp.abs(f)))
    prev = jnp.where((i % tiles_per_seq) == 0, 0.0, carry_ref[0:1, :])
    cum = _row_scan(log_f, tm) + prev
    carry_ref[...] = jnp.broadcast_to(cum[tm - 1:tm, :], carry_ref.shape)
    pieces = jnp.concatenate(_split_bf16(LOG2E * cum, 3), axis=1)
    lanes = _dot(pieces, sel_ref[...]) + ones_ref[...]
    qb_ref[...] = lanes[:, :LANES].astype(BF16)
    kb_ref[...] = lanes[:, LANES:].astype(BF16)


def _bias_lane_tables():
    sel = np.zeros((3 * LANES, 2 * LANES), np.float32)
    ones = np.zeros((1, 2 * LANES), np.float32)
    for h in range(FOX_HEADS):
        base = BIAS_LANES_PER_HEAD * h
        for piece in range(3):
            sel[LANES * piece + h, base + 3 + piece] = 1.0
            sel[LANES * piece + h, LANES + base + piece] = -1.0
            ones[0, base + piece] = 1.0
            ones[0, LANES + base + 3 + piece] = 1.0
    return jnp.asarray(sel, BF16), jnp.asarray(ones, F32)


def _odd_in(x2d, g, wm, wf, bf_row, seq_len):
    m, d = x2d.shape
    sel, ones_row = _bias_lane_tables()
    tm = TOKEN_TILE
    d_mix = wm.shape[1] // 4
    tps = seq_len // tm
    kern = functools.partial(_odd_in_kernel, tiles_per_seq=tps, d_mix=d_mix)
    row = lambda i: (i, 0)
    row_blk = pl.BlockSpec((tm, d_mix), row)
    row_shape = jax.ShapeDtypeStruct((m, d_mix), BF16)
    return pl.pallas_call(
        kern,
        grid=(m // tm,),
        in_specs=[
            pl.BlockSpec((tm, d), row),
            _const_spec((1, d)),
            _const_spec(wm.shape),
            _const_spec((d, LANES)),
            _const_spec((1, LANES)),
            _const_spec(sel.shape),
            _const_spec(ones_row.shape),
        ],
        out_specs=[row_blk, row_blk, pl.BlockSpec((1, d_mix, tm), lambda i: (i // tps, 0, i % tps)), row_blk,
                   pl.BlockSpec((tm, LANES), row), pl.BlockSpec((tm, LANES), row)],
        out_shape=[row_shape, row_shape, jax.ShapeDtypeStruct((m // seq_len, d_mix, seq_len), BF16), row_shape,
                   jax.ShapeDtypeStruct((m, LANES), BF16), jax.ShapeDtypeStruct((m, LANES), BF16)],
        scratch_shapes=[pltpu.VMEM((tm, d), BF16), pltpu.VMEM((SUBLANES, LANES), F32), pltpu.VMEM((tm, PROJ_SEG), F32)],
        compiler_params=_params(("arbitrary",)),
        name="odd_in_proj",
    )(x2d, g, wm, wf, bf_row, sel, ones_row)


def _attn_kernel(*refs, tq, fox, lambda_init):
    if fox:
        q_ref, k_ref, vt_ref, gate_ref, qb_ref, kball_ref, o_ref, st_ref, m_ref, acc_ref, kb_ref = refs
    else:
        q_ref, k_ref, vt_ref, lam_ref, nw_ref, o_ref, st_ref, m_ref, acc_ref = refs
    tk = tq
    hg = pl.program_id(1)
    qi = pl.program_id(2)
    n_tiles = pl.num_programs(2)
    head = lambda g: slice(g * HEAD_LANES, (g + 1) * HEAD_LANES)
    kv = [head(0), head(1)] if fox else [head(0), head(0)]

    def queries(g, tile):
        rows = pl.ds(pl.multiple_of(tile * tq, tq), tq)
        if fox:
            return jnp.concatenate([q_ref[0, rows, head(g)], qb_ref[0, rows, :]], axis=1)
        q = q_ref[0, rows, :]
        lane = lax.broadcasted_iota(jnp.int32, q.shape, 1)
        keep = (lane < DIFF_QK_DIM) if g == 0 else (lane >= DIFF_QK_DIM)
        return jnp.where(keep, q, jnp.zeros_like(q))

    def scores_of(qmat, g, j):
        k0 = pl.multiple_of(j * tk, tk)
        kj = k_ref[0, pl.ds(k0, tk), kv[g]]
        if fox:
            kj = jnp.concatenate([kj, kb_ref[pl.ds(k0, tk), kv[g]]], axis=1)
        return _dot_nt(kj, qmat)

    @pl.when(qi == 0)
    def _():
        if fox:
            kb_all = kball_ref[0]
            owner = lax.broadcasted_iota(jnp.int32, kb_all.shape, 1) // BIAS_LANES_PER_HEAD
            for g in range(2):
                kb_ref[:, head(g)] = jnp.where(owner == hg * 2 + g, kb_all, jnp.zeros_like(kb_all))
        st_ref[...] = scores_of(queries(0, 0), 0, 0)

    qs = [queries(g, qi) for g in range(2)]
    scores = lambda g, j: scores_of(qs[g], g, j)
    ones = jnp.ones((ONES_ROWS, tk), BF16)

    def absorb(g, j, st, masked):
        m = m_ref[g]
        k0 = pl.multiple_of(j * tk, tk)
        if masked:
            kpos = k0 + lax.broadcasted_iota(jnp.int32, (tk, tq), 0)
            qpos = qi * tq + lax.broadcasted_iota(jnp.int32, (tk, tq), 1)
            st = jnp.where(qpos >= kpos, st, NEG_INF)
        m_new = jnp.maximum(m, jnp.max(st, axis=0, keepdims=True))
        p = jnp.exp2(st - m_new).astype(BF16)
        vt = jnp.concatenate([vt_ref[0, kv[g], pl.ds(k0, tk)], ones], axis=0)
        acc_ref[g] = jnp.exp2(m - m_new) * acc_ref[g] + _dot(vt, p)
        m_ref[g] = m_new

    def step(j):
        st1 = scores(1, j)
        absorb(0, j, st_ref[...], False)
        st_ref[...] = scores(0, j + 1)
        absorb(1, j, st1, False)

    def steps(j, carry):
        for u in range(ATTN_UNROLL):
            step(j * ATTN_UNROLL + u)
        return carry

    def tail_step(j, carry):
        step(j)
        return carry

    m_ref[...] = jnp.full(m_ref.shape, NEG_INF, F32)
    acc_ref[...] = jnp.zeros(acc_ref.shape, F32)
    n_full = qi // ATTN_UNROLL
    lax.fori_loop(0, n_full, steps, 0)
    lax.fori_loop(n_full * ATTN_UNROLL, qi, tail_step, 0)
    st1 = scores(1, qi)
    absorb(0, qi, st_ref[...], True)
    absorb(1, qi, st1, True)
    st_ref[...] = scores_of(queries(0, jnp.minimum(qi + 1, n_tiles - 1)), 0, 0)
    outs = [acc_ref[g, :HEAD_LANES, :] / acc_ref[g, HEAD_LANES:HEAD_LANES + 1, :] for g in range(2)]
    if fox:
        for g in range(2):
            o_ref[0, :, head(g)] = (outs[g].T * _sigmoid(gate_ref[0, :, head(g)].astype(F32))).astype(o_ref.dtype)
    else:
        lam_p = lam_ref[...]
        lam = (jnp.exp(jnp.sum(lam_p[0:1] * lam_p[1:2], axis=1, keepdims=True))
               - jnp.exp(jnp.sum(lam_p[2:3] * lam_p[3:4], axis=1, keepdims=True)) + lambda_init)
        o = (outs[0] - lam * outs[1]).T
        o_ref[0] = (_rms(o, nw_ref[...]) * (1.0 - lambda_init)).astype(o_ref.dtype)


def _attn_state(tq):
    return [pltpu.VMEM((tq, tq), F32), pltpu.VMEM((2, 1, tq), F32), pltpu.VMEM((2, HEAD_LANES + ONES_ROWS, tq), F32)]


def _fox_attention(q, k, vt, gate, qb, kb, *, tq=ATTN_TILE):
    b, t, dm = q.shape
    width = 2 * HEAD_LANES
    kern = functools.partial(_attn_kernel, tq=tq, fox=True, lambda_init=0.0)
    qblk = pl.BlockSpec((1, tq, width), lambda bi, h, i: (bi, i, h))
    seq = pl.BlockSpec((1, t, width), lambda bi, h, i: (bi, 0, h))
    seq_bias = pl.BlockSpec((1, t, LANES), lambda bi, h, i: (bi, 0, 0))
    return pl.pallas_call(
        kern,
        grid=(b, dm // width, t // tq),
        in_specs=[seq, seq, pl.BlockSpec((1, width, t), lambda bi, h, i: (bi, h, 0)), qblk, seq_bias, seq_bias],
        out_specs=qblk,
        out_shape=jax.ShapeDtypeStruct((b, t, dm), BF16),
        scratch_shapes=_attn_state(tq) + [pltpu.VMEM((t, width), BF16)],
        compiler_params=_params(("arbitrary", "arbitrary", "arbitrary")),
        name="fox_attention",
    )(q, k, vt, gate, qb, kb)


def _diff_attention(qk, vt, lam_params, norm_w, lambda_init, *, tq=ATTN_TILE):
    b, t, _ = qk.shape
    nh = DIFF_HEADS
    kern = functools.partial(_attn_kernel, tq=tq, fox=False, lambda_init=lambda_init)
    qblk = pl.BlockSpec((1, tq, HEAD_LANES), lambda bi, h, i: (bi, i, h))
    return pl.pallas_call(
        kern,
        grid=(b, nh, t // tq),
        in_specs=[pl.BlockSpec((1, t, HEAD_LANES), lambda bi, h, i: (bi, 0, h)),
                  pl.BlockSpec((1, t, HEAD_LANES), lambda bi, h, i: (bi, 0, nh + h)),
                  pl.BlockSpec((1, HEAD_LANES, t), lambda bi, h, i: (bi, h, 0)),
                  _const_spec(lam_params.shape), _const_spec((1, HEAD_LANES))],
        out_specs=qblk,
        out_shape=jax.ShapeDtypeStruct((b, t, nh * HEAD_LANES), BF16),
        scratch_shapes=_attn_state(tq),
        compiler_params=_params(("arbitrary", "arbitrary", "arbitrary")),
        name="diff_attention",
    )(qk, qk, vt, lam_params, norm_w)


def _split_bf16(x, parts):
    out = []
    for _ in range(parts):
        piece = x.astype(BF16)
        out.append(piece)
        x = x - piece.astype(F32)
    return out


def _dot_split(a, b):
    a_hi, a_lo = _split_bf16(a, 2)
    b_hi, b_lo = _split_bf16(b, 2)
    return _dot(a_hi, b_hi) + (_dot(a_hi, b_lo) + _dot(a_lo, b_hi))


def _gdn_prep_kernel(q_ref, k_ref, v_ref, gates_ref, u_ref, w_ref, qd_ref, kd_ref, qk_ref, *, refine):
    c = GDN_CHUNK
    tt = q_ref.shape[1]
    heads = range(GDN_HEADS)
    gt = gates_ref[0]
    lane = lax.broadcasted_iota(jnp.int32, (tt, LANES), 1)
    ri = lax.broadcasted_iota(jnp.int32, (tt, tt), 0)
    ci = lax.broadcasted_iota(jnp.int32, (tt, tt), 1)
    chunk_start = ri - ri % c
    incl = lambda a: jnp.where(ci <= ri, jnp.where(ci >= chunk_start, a, 0.0), 0.0)
    strict = lambda a: jnp.where(ci < ri, jnp.where(ci >= chunk_start, a, 0.0), 0.0)
    ident = jnp.where(ri == ci, 1.0, 0.0)
    ones = jnp.ones((tt, LANES), BF16)
    cols = [slice(hd * HEAD_LANES, (hd + 1) * HEAD_LANES) for hd in heads]
    kt = [k_ref[0, :, cols[hd]] for hd in heads]
    beta = [gt[:, hd:hd + 1] for hd in heads]
    gcc = [gt[:, GDN_HEADS + hd:GDN_HEADS + hd + 1] for hd in heads]
    k16 = [kt[hd].astype(BF16) for hd in heads]
    kb = [kt[hd] * beta[hd] for hd in heads]

    gc_row = []
    for hd in heads:
        g_hi, g_mid, g_lo = (piece.astype(F32) for piece in _split_bf16(gcc[hd], 3))
        pieces = jnp.where(lane == 0, g_hi, jnp.where(lane == 1, g_mid, jnp.where(lane == 2, g_lo, 0.0)))
        gc_row.append(_dot_nt(ones, pieces.astype(BF16)))
    kk = [_dot_nt(kb[hd].astype(BF16), k16[hd]) for hd in heads]
    qk_raw = [_dot_nt(q_ref[0, :, cols[hd]].astype(BF16), k16[hd]) for hd in heads]
    decay = [incl(jnp.exp(incl(gcc[hd] - gc_row[hd]))) for hd in heads]
    lower = [strict(kk[hd] * decay[hd]) for hd in heads]

    inv = [ident - lower[hd] for hd in heads]
    l16 = [lower[hd].astype(BF16) for hd in heads]
    power = [_dot(l16[hd], l16[hd]).astype(BF16) for hd in heads]
    n_sq = int(math.log2(c)) - 1
    for step in range(n_sq - 1):
        r = [_dot(jnp.concatenate([power[hd], inv[hd].astype(BF16)], axis=0), power[hd]) for hd in heads]
        power = [r[hd][:tt].astype(BF16) for hd in heads]
        inv = [inv[hd] + r[hd][tt:] for hd in heads]
    r = [_dot(inv[hd].astype(BF16), power[hd]) for hd in heads]
    inv16 = [(inv[hd] + r[hd]).astype(BF16) for hd in heads]

    eg = [jnp.exp(gcc[hd]) for hd in heads]
    rhs = [jnp.concatenate([v_ref[0, :, cols[hd]] * beta[hd], kb[hd] * eg[hd]], axis=1) for hd in heads]
    sol = [_dot(inv16[hd], rhs[hd].astype(BF16)) for hd in heads]
    if refine:
        a_hi, a_lo, s_hi, s_lo = [], [], [], []
        for hd in heads:
            hi, lo = _split_bf16(ident + lower[hd], 2)
            a_hi.append(hi)
            a_lo.append(lo)
            hi, lo = _split_bf16(sol[hd], 2)
            s_hi.append(hi)
            s_lo.append(lo)
        prod = [_dot(a_hi[hd], s_hi[hd]) + (_dot(a_hi[hd], s_lo[hd]) + _dot(a_lo[hd], s_hi[hd])) for hd in heads]
        corr = [_dot(inv16[hd], (rhs[hd] - prod[hd]).astype(BF16)) for hd in heads]
        sol = [sol[hd] + corr[hd] for hd in heads]
    for hd in heads:
        u_ref[0, :, cols[hd]] = sol[hd][:, :HEAD_LANES]
        w_ref[0, :, cols[hd]] = sol[hd][:, HEAD_LANES:].astype(BF16)
        qk = incl(qk_raw[hd] * decay[hd])
        qd_ref[0, :, cols[hd]] = (q_ref[0, :, cols[hd]] * eg[hd]).astype(BF16)
        for n in range(tt // c):
            rows = slice(n * c, (n + 1) * c)
            qk_ref[0, hd, rows, :] = qk[rows, rows].astype(BF16)
            gl = gcc[hd][(n + 1) * c - 1:(n + 1) * c, :]
            kd_ref[0, rows, cols[hd]] = (kt[hd][rows] * jnp.exp(gl - gcc[hd][rows])).astype(BF16)


def _gdn_prep(qkv, gates, *, tt=256):
    b, t, _ = qkv.shape
    nh = GDN_HEADS
    dm = nh * HEAD_LANES
    blk = lambda part: pl.BlockSpec((1, tt, dm), lambda bi, i: (bi, i, part))
    return pl.pallas_call(
        functools.partial(_gdn_prep_kernel, refine=GDN_REFINE),
        grid=(b, t // tt),
        in_specs=[blk(0), blk(1), blk(2), pl.BlockSpec((1, tt, LANES), lambda bi, i: (bi, i, 0))],
        out_specs=[blk(0)] * 4 + [pl.BlockSpec((1, nh, tt, GDN_CHUNK), lambda bi, i: (bi, 0, i, 0))],
        out_shape=[jax.ShapeDtypeStruct((b, t, dm), F32)]
        + [jax.ShapeDtypeStruct((b, t, dm), BF16)] * 3
        + [jax.ShapeDtypeStruct((b, nh, t, GDN_CHUNK), BF16)],
        compiler_params=_params(("arbitrary", "arbitrary")),
        name="gdn_prep",
    )(qkv, qkv, qkv, gates)


def _gdn_scan_kernel(u_ref, w_ref, qd_ref, kd_ref, qk_ref, gates_ref, z_ref, nw_ref, o_ref, s_ref):
    c = GDN_CHUNK
    tt = u_ref.shape[1]

    @pl.when(pl.program_id(1) == 0)
    def _():
        s_ref[...] = jnp.zeros_like(s_ref)

    def chunk(n, _):
        r0 = pl.multiple_of(n * c, c)
        rows = pl.ds(r0, c)
        g_last = gates_ref[0, pl.ds(r0 + c - SUBLANES, SUBLANES), :]
        heads = range(GDN_HEADS)
        cols = [slice(hd * HEAD_LANES, (hd + 1) * HEAD_LANES) for hd in heads]
        state = [s_ref[hd] for hd in heads]
        r = [_dot(jnp.concatenate([w_ref[0, rows, cols[hd]], qd_ref[0, rows, cols[hd]]], axis=0),
                  state[hd].astype(BF16)) for hd in heads]
        v_new = [(u_ref[0, rows, cols[hd]] - r[hd][:c]).astype(BF16) for hd in heads]
        intra = [_dot(qk_ref[0, hd, rows, :], v_new[hd]) for hd in heads]
        upd = [_dot_tn(kd_ref[0, rows, cols[hd]], v_new[hd]) for hd in heads]
        for hd in heads:
            decay_last = jnp.exp(g_last[SUBLANES - 1:SUBLANES, GDN_HEADS + hd:GDN_HEADS + hd + 1])
            s_ref[hd] = state[hd] * decay_last + upd[hd]
            zt = z_ref[0, rows, cols[hd]].astype(F32)
            o = r[hd][c:] + intra[hd]
            o_ref[0, rows, cols[hd]] = (_rms(o, nw_ref[...]) * (zt * _sigmoid(zt))).astype(o_ref.dtype)
        return 0

    lax.fori_loop(0, tt // c, chunk, 0)


def _gdn_scan(u, w, qd, kd, qk, gates, z, norm_w, *, tt=512):
    b, t, dm = u.shape
    nh = GDN_HEADS
    blk = pl.BlockSpec((1, tt, dm), lambda bi, i: (bi, i, 0))
    return pl.pallas_call(
        _gdn_scan_kernel,
        grid=(b, t // tt),
        in_specs=[blk, blk, blk, blk,
                  pl.BlockSpec((1, nh, tt, GDN_CHUNK), lambda bi, i: (bi, 0, i, 0)),
                  pl.BlockSpec((1, tt, LANES), lambda bi, i: (bi, i, 0)),
                  blk, _const_spec((1, HEAD_LANES))],
        out_specs=blk,
        out_shape=jax.ShapeDtypeStruct((b, t, dm), BF16),
        scratch_shapes=[pltpu.VMEM((nh, GDN_HEAD_DIM, GDN_HEAD_DIM), F32)],
        compiler_params=_params(("arbitrary", "arbitrary")),
        name="gdn_scan",
    )(u, w, qd, kd, qk, gates, z, norm_w)


def _post_kernel(*refs, n_mix, final_norm):
    x_ref = refs[0]
    mix_refs = refs[1:1 + n_mix]
    wout_ref, g_ref, wup_ref, wdn_ref, p_ref, wpp_ref, wpg_ref = refs[1 + n_mix:8 + n_mix]
    rest = refs[8 + n_mix:]
    if final_norm:
        gf_ref, o_ref = rest
    else:
        (o_ref,) = rest
    mix = mix_refs[0][...] if n_mix == 1 else jnp.concatenate([r[...] for r in mix_refs], axis=1)
    x = x_ref[...] + _dot(mix, wout_ref[...])
    h = _rms(x, g_ref[...]).astype(BF16)
    d_ff = wup_ref.shape[1]
    acc = x
    for s in range(d_ff // FF_SEG):
        a = jnp.maximum(_dot(h, wup_ref[:, s * FF_SEG:(s + 1) * FF_SEG]), 0.0)
        acc = acc + _dot((a * a).astype(BF16), wdn_ref[s * FF_SEG:(s + 1) * FF_SEG, :])
    x = acc
    gate = _sigmoid(_dot(x.astype(BF16), wpg_ref[...]))
    x = x + _dot(p_ref[...].astype(BF16), wpp_ref[...]) * gate
    if final_norm:
        x = _rms(x, gf_ref[...])
    o_ref[...] = x


def _post(x2d, mixes, wout, g, wup, wdn, p2d, wpp, wpg, gf=None):
    m, d = x2d.shape
    tm = TOKEN_TILE
    row = lambda i: (i, 0)
    single = pl.Buffered(1)
    const = lambda a: pl.BlockSpec(a.shape, lambda i: (0, 0), pipeline_mode=single)
    args = [x2d, *mixes, wout, g, wup, wdn, p2d, wpp, wpg]
    in_specs = ([pl.BlockSpec((tm, d), row)]
                + [pl.BlockSpec((tm, a.shape[1]), row) for a in mixes]
                + [const(wout), const(g), const(wup), const(wdn), pl.BlockSpec((tm, p2d.shape[1]), row), const(wpp), const(wpg)])
    if gf is not None:
        args.append(gf)
        in_specs.append(const(gf))
    kern = functools.partial(_post_kernel, n_mix=len(mixes), final_norm=gf is not None)
    return pl.pallas_call(
        kern,
        grid=(m // tm,),
        in_specs=in_specs,
        out_specs=pl.BlockSpec((tm, d), row),
        out_shape=jax.ShapeDtypeStruct((m, d), F32),
        compiler_params=_params(("arbitrary",)),
        name="out_proj_mlp_ple",
    )(*args)


def _pad_lanes(a):
    return jnp.pad(a, ((0, 0), (0, LANES - a.shape[1])))


def kernel(x, p, positions, norm_mix, norm_mlp, norm_final, w_in_even, conv_w, a_log, dt_bias, gdn_norm,
           lam_q1, lam_k1, lam_q2, lam_k2, diff_norm, w_out_even, w_in_odd, b_forget, w_out_odd,
           w_mlp_up, w_mlp_down, w_ple_proj, w_ple_gate):
    b, t, d = x.shape
    depth = p.shape[0]
    m = b * t
    assert t % TOKEN_TILE == 0 and d % PROJ_SEG == 0
    nh = GDN_HEADS
    gdn_w = 3 * nh * GDN_HEAD_DIM + nh * GDN_HEAD_DIM
    assert w_in_even.shape[2] == gdn_w + 2 * nh + 3 * DIFF_HEADS * 2 * DIFF_QK_DIM

    inv_freq = ROPE_THETA ** (-jnp.arange(0, DIFF_QK_DIM, 2, dtype=F32) / DIFF_QK_DIM)
    ang = positions.astype(F32)[..., None] * inv_freq
    cos, sin = jnp.cos(ang), jnp.sin(ang)
    cos_t = jnp.concatenate([cos, cos, cos, cos], axis=-1).reshape(m, LANES)
    sin_t = jnp.concatenate([-sin, sin, -sin, sin], axis=-1).reshape(m, LANES)

    x2d = x.reshape(m, d)
    for i in range(depth):
        j = i // 2
        g_mix = norm_mix[i].reshape(1, d)
        if i % 2 == 0:
            lambda_init = 0.8 - 0.6 * math.exp(-0.3 * i)
            w = w_in_even[j]
            wm = jnp.concatenate([w[:, :gdn_w], w[:, gdn_w + 2 * nh:]], axis=1).astype(BF16)
            wg = _pad_lanes(w[:, gdn_w:gdn_w + 2 * nh]).astype(BF16)
            alog_row = _pad_lanes(jnp.concatenate([jnp.zeros((nh,), F32), a_log[j]]).reshape(1, 2 * nh))
            dt_row = _pad_lanes(jnp.concatenate([jnp.zeros((nh,), F32), dt_bias[j]]).reshape(1, 2 * nh))
            qkv, z, qkb, vbt, gates = _even_in(x2d, g_mix, wm, wg, conv_w[j], alog_row, dt_row, cos_t, sin_t, t)
            qkv, z, qkb, gates = (a.reshape(b, t, -1) for a in (qkv, z, qkb, gates))
            u, wy, qd, kd, qk = _gdn_prep(qkv, gates)
            o_a = _gdn_scan(u, wy, qd, kd, qk, gates, z, gdn_norm[j].reshape(1, HEAD_LANES))
            lam_params = jnp.stack([lam_q1[j], lam_k1[j], lam_q2[j], lam_k2[j]])
            o_b = _diff_attention(qkb, vbt, lam_params, diff_norm[j].reshape(1, HEAD_LANES), lambda_init)
            mixes = [o_a.reshape(m, -1), o_b.reshape(m, -1)]
            wout = w_out_even[j].astype(BF16)
        else:
            w = w_in_odd[j]
            d_mix = (w.shape[1] - FOX_HEADS) // 4
            wm = w[:, :4 * d_mix].astype(BF16)
            wf = _pad_lanes(w[:, 4 * d_mix:]).astype(BF16)
            bf_row = _pad_lanes(b_forget[j].reshape(1, FOX_HEADS))
            q, k, vt, gate, qb, kb = _odd_in(x2d, g_mix, wm, wf, bf_row, t)
            q, k, gate, qb, kb = (a.reshape(b, t, -1) for a in (q, k, gate, qb, kb))
            o = _fox_attention(q, k, vt, gate, qb, kb)
            mixes = [o.reshape(m, -1)]
            wout = w_out_odd[j].astype(BF16)
        x2d = _post(x2d, mixes, wout, norm_mlp[i].reshape(1, d), w_mlp_up[i].astype(BF16),
                    w_mlp_down[i].astype(BF16), p[i].reshape(m, -1), w_ple_proj[i].astype(BF16),
                    w_ple_gate[i].astype(BF16), norm_final.reshape(1, d) if i == depth - 1 else None)
    return x2d.reshape(b, t, d)
```

```python
import functools
import math

import jax
import jax.numpy as jnp
import numpy as np
from jax import lax
from jax.experimental import pallas as pl
from jax.experimental.pallas import tpu as pltpu

F32 = jnp.float32
BF16 = jnp.bfloat16

GDN_HEADS = 4
GDN_HEAD_DIM = 128
GDN_CHUNK = 64
CONV_WIDTH = 4
DIFF_HEADS = 4
DIFF_QK_DIM = 64
FOX_HEADS = 8
HEAD_LANES = 128
ROPE_THETA = 10000.0
EPS = 1e-6
NEG_INF = -1e30
LOG2E = 1.4426950408889634
LANES = 128
SUBLANES = 8
VMEM_LIMIT_BYTES = 56 * 1024 * 1024

TOKEN_TILE = 512
PROJ_SEG = 512
FF_SEG = 1024
GDN_REFINE = True
GDN_SCAN_TILE = 256
GDN_SCAN_BATCH = 2
ATTN_TILE = 512
ATTN_UNROLL = 2
BIAS_LANES_PER_HEAD = 16
ONES_ROWS = 16


def _dot(a, b):
    return jnp.dot(a, b, preferred_element_type=F32)


def _dot_exact(a, b):
    return jnp.dot(a, b, preferred_element_type=F32, precision=lax.Precision.HIGHEST)


def _dot_nt(a, b):
    return lax.dot_general(a, b, (((1,), (1,)), ((), ())), preferred_element_type=F32)


def _dot_tn(a, b):
    return lax.dot_general(a, b, (((0,), (0,)), ((), ())), preferred_element_type=F32)


def _rms(x, g):
    return x * lax.rsqrt(jnp.mean(x * x, axis=-1, keepdims=True) + EPS) * g


def _sigmoid(x):
    return 1.0 / (1.0 + jnp.exp(-x))


def _softplus(x):
    return jnp.maximum(x, 0.0) + jnp.log1p(jnp.exp(-jnp.abs(x)))


def _row_scan(x, period):
    rows = lax.broadcasted_iota(jnp.int32, x.shape, 0) % period
    s = 1
    while s < period:
        x = x + jnp.where(rows >= s, pltpu.roll(x, s, 0), 0.0)
        s *= 2
    return x


def _const_spec(shape):
    return pl.BlockSpec(shape, lambda *_: (0,) * len(shape))


def _params(sem):
    return pltpu.CompilerParams(dimension_semantics=sem, vmem_limit_bytes=VMEM_LIMIT_BYTES)


def _even_in_kernel(x_ref, g_ref, wm_ref, wg_ref, conv_ref, alog_ref, dt_ref, cos_ref, sin_ref,
                    qkv_ref, z_ref, qkb_ref, vbt_ref, gates_ref, h_ref, carry_ref, tr_ref, *, tiles_per_seq):
    tm = x_ref.shape[0]
    i = pl.program_id(0)
    h_ref[...] = _rms(x_ref[...], g_ref[...]).astype(BF16)
    seq_start = (i % tiles_per_seq) == 0
    seg = lambda s: slice(s * PROJ_SEG, (s + 1) * PROJ_SEG)
    project = lambda s: _dot(h_ref[...], wm_ref[:, seg(s)])

    def gdn_qkv(s, y):
        cols = seg(s)
        row8 = lax.broadcasted_iota(jnp.int32, (SUBLANES, PROJ_SEG), 0)
        prev8 = jnp.where(seq_start, 0.0, carry_ref[:, cols])
        carry_ref[:, cols] = y[tm - SUBLANES:, :]
        w = conv_ref[:, cols]
        acc = y * w[CONV_WIDTH - 1:CONV_WIDTH, :]
        top = y[:SUBLANES, :] * w[CONV_WIDTH - 1:CONV_WIDTH, :]
        for k in range(1, CONV_WIDTH):
            wk = w[CONV_WIDTH - 1 - k:CONV_WIDTH - k, :]
            rolled = pltpu.roll(y, k, 0)
            acc = acc + rolled * wk
            top = top + jnp.where(row8 < k, pltpu.roll(prev8, k, 0), rolled[:SUBLANES, :]) * wk
        for part, rows in ((acc, slice(0, tm)), (top, slice(0, SUBLANES))):
            a = part * _sigmoid(part)
            if s < 2:
                outs = []
                for hd in range(GDN_HEADS):
                    blk = a[:, hd * HEAD_LANES:(hd + 1) * HEAD_LANES]
                    n = blk * lax.rsqrt(jnp.sum(blk * blk, axis=-1, keepdims=True) + EPS)
                    outs.append(n * (GDN_HEAD_DIM ** -0.5) if s == 0 else n)
                a = jnp.concatenate(outs, axis=1)
            qkv_ref[rows, cols] = a

    def gdn_gate(s, y):
        z_ref[...] = y.astype(BF16)

    def diff_qk(s, y):
        cos = jnp.concatenate([cos_ref[...]] * (PROJ_SEG // LANES), axis=1)
        sin = jnp.concatenate([sin_ref[...]] * (PROJ_SEG // LANES), axis=1)
        lane = lax.broadcasted_iota(jnp.int32, (tm, PROJ_SEG), 1)
        first_half = (lane % DIFF_QK_DIM) < (DIFF_QK_DIM // 2)
        swapped = jnp.where(first_half, pltpu.roll(y, PROJ_SEG - DIFF_QK_DIM // 2, 1),
                            pltpu.roll(y, DIFF_QK_DIM // 2, 1))
        scale = DIFF_QK_DIM ** -0.5 * LOG2E if s == 4 else 1.0
        qkb_ref[:, seg(s - 4)] = ((y * cos + swapped * sin) * scale).astype(BF16)

    def diff_v(s, y):
        tr_ref[...] = y
        vbt_ref[0] = tr_ref[...].T.astype(BF16)

    stages = ((0, gdn_qkv), (3, gdn_gate), (1, gdn_qkv), (6, diff_v), (2, gdn_qkv), (4, diff_qk), (5, diff_qk))
    pending = project(stages[0][0])
    for n, (s, epilogue) in enumerate(stages):
        upcoming = project(stages[n + 1][0]) if n + 1 < len(stages) else _dot(h_ref[...], wg_ref[...])
        epilogue(s, pending)
        pending = upcoming

    graw = pending
    beta = _sigmoid(graw)
    g = -jnp.exp(alog_ref[...]) * _softplus(graw + dt_ref[...])
    gc = _row_scan(g, GDN_CHUNK)
    lane_g = lax.broadcasted_iota(jnp.int32, (tm, LANES), 1)
    gates_ref[...] = jnp.where(lane_g < GDN_HEADS, beta, gc)


def _even_in(x2d, g, wm, wg, conv_w, alog_row, dt_row, cos_t, sin_t, seq_len):
    m, d = x2d.shape
    tm = TOKEN_TILE
    n_main = wm.shape[1]
    tps = seq_len // tm
    kern = functools.partial(_even_in_kernel, tiles_per_seq=tps)
    row = lambda i: (i, 0)
    return pl.pallas_call(
        kern,
        grid=(m // tm,),
        in_specs=[
            pl.BlockSpec((tm, d), row),
            _const_spec((1, d)),
            _const_spec((d, n_main)),
            _const_spec((d, LANES)),
            _const_spec(conv_w.shape),
            _const_spec((1, LANES)),
            _const_spec((1, LANES)),
            pl.BlockSpec((tm, LANES), row),
            pl.BlockSpec((tm, LANES), row),
        ],
        out_specs=[
            pl.BlockSpec((tm, 3 * PROJ_SEG), row),
            pl.BlockSpec((tm, PROJ_SEG), row),
            pl.BlockSpec((tm, 2 * PROJ_SEG), row),
            pl.BlockSpec((1, PROJ_SEG, tm), lambda i: (i // tps, 0, i % tps)),
            pl.BlockSpec((tm, LANES), row),
        ],
        out_shape=[
            jax.ShapeDtypeStruct((m, 3 * PROJ_SEG), F32),
            jax.ShapeDtypeStruct((m, PROJ_SEG), BF16),
            jax.ShapeDtypeStruct((m, 2 * PROJ_SEG), BF16),
            jax.ShapeDtypeStruct((m // seq_len, PROJ_SEG, seq_len), BF16),
            jax.ShapeDtypeStruct((m, LANES), F32),
        ],
        scratch_shapes=[pltpu.VMEM((tm, d), BF16), pltpu.VMEM((SUBLANES, 3 * PROJ_SEG), F32), pltpu.VMEM((tm, PROJ_SEG), F32)],
        compiler_params=_params(("arbitrary",)),
        name="even_in_proj",
    )(x2d, g, wm, wg, conv_w, alog_row, dt_row, cos_t, sin_t)


def _odd_in_kernel(x_ref, g_ref, wm_ref, wf_ref, bf_ref, sel_ref, ones_ref, q_ref, k_ref, vt_ref, gate_ref, qb_ref, kb_ref,
                   h_ref, carry_ref, tr_ref, *, tiles_per_seq, d_mix):
    tm = x_ref.shape[0]
    i = pl.program_id(0)
    h_ref[...] = _rms(x_ref[...], g_ref[...]).astype(BF16)
    head_dim = d_mix // FOX_HEADS
    for o_ref, base, scale in ((q_ref, 0, head_dim ** -0.5 * LOG2E), (k_ref, d_mix, 1.0),
                               (vt_ref, 2 * d_mix, 1.0), (gate_ref, 3 * d_mix, 1.0)):
        for s in range(d_mix // PROJ_SEG):
            cols = slice(s * PROJ_SEG, (s + 1) * PROJ_SEG)
            y = _dot(h_ref[...], wm_ref[:, base + s * PROJ_SEG:base + (s + 1) * PROJ_SEG])
            if o_ref is vt_ref:
                tr_ref[...] = y
                o_ref[0, cols, :] = tr_ref[...].T.astype(BF16)
            else:
                o_ref[:, cols] = (y * scale).astype(BF16)
    f = _dot(h_ref[...], wf_ref[...]) + bf_ref[...]
    log_f = jnp.minimum(f, 0.0) - jnp.log1p(jnp.exp(-jnp.abs(f)))
    prev = jnp.where((i % tiles_per_seq) == 0, 0.0, carry_ref[0:1, :])
    cum = _row_scan(log_f, tm) + prev
    carry_ref[...] = jnp.broadcast_to(cum[tm - 1:tm, :], carry_ref.shape)
    pieces = jnp.concatenate(_split_bf16(LOG2E * cum, 3), axis=1)
    lanes = _dot(pieces, sel_ref[...]) + ones_ref[...]
    qb_ref[...] = lanes[:, :LANES].astype(BF16)
    kb_ref[...] = lanes[:, LANES:].astype(BF16)


def _bias_lane_tables():
    sel = np.zeros((3 * LANES, 2 * LANES), np.float32)
    ones = np.zeros((1, 2 * LANES), np.float32)
    for h in range(FOX_HEADS):
        base = BIAS_LANES_PER_HEAD * h
        for piece in range(3):
            sel[LANES * piece + h, base + 3 + piece] = 1.0
            sel[LANES * piece + h, LANES + base + piece] = -1.0
            ones[0, base + piece] = 1.0
            ones[0, LANES + base + 3 + piece] = 1.0
    return jnp.asarray(sel, BF16), jnp.asarray(ones, F32)


def _odd_in(x2d, g, wm, wf, bf_row, seq_len):
    m, d = x2d.shape
    sel, ones_row = _bias_lane_tables()
    tm = TOKEN_TILE
    d_mix = wm.shape[1] // 4
    tps = seq_len // tm
    kern = functools.partial(_odd_in_kernel, tiles_per_seq=tps, d_mix=d_mix)
    row = lambda i: (i, 0)
    row_blk = pl.BlockSpec((tm, d_mix), row)
    row_shape = jax.ShapeDtypeStruct((m, d_mix), BF16)
    return pl.pallas_call(
        kern,
        grid=(m // tm,),
        in_specs=[
            pl.BlockSpec((tm, d), row),
            _const_spec((1, d)),
            _const_spec(wm.shape),
            _const_spec((d, LANES)),
            _const_spec((1, LANES)),
            _const_spec(sel.shape),
            _const_spec(ones_row.shape),
        ],
        out_specs=[row_blk, row_blk, pl.BlockSpec((1, d_mix, tm), lambda i: (i // tps, 0, i % tps)), row_blk,
                   pl.BlockSpec((tm, LANES), row), pl.BlockSpec((tm, LANES), row)],
        out_shape=[row_shape, row_shape, jax.ShapeDtypeStruct((m // seq_len, d_mix, seq_len), BF16), row_shape,
                   jax.ShapeDtypeStruct((m, LANES), BF16), jax.ShapeDtypeStruct((m, LANES), BF16)],
        scratch_shapes=[pltpu.VMEM((tm, d), BF16), pltpu.VMEM((SUBLANES, LANES), F32), pltpu.VMEM((tm, PROJ_SEG), F32)],
        compiler_params=_params(("arbitrary",)),
        name="odd_in_proj",
    )(x2d, g, wm, wf, bf_row, sel, ones_row)


def _attn_kernel(*refs, tq, fox, lambda_init):
    if fox:
        q_ref, k_ref, vt_ref, gate_ref, qb_ref, kball_ref, o_ref, st_ref, m_ref, acc_ref, kb_ref = refs
    else:
        q_ref, k_ref, vt_ref, lam_ref, nw_ref, o_ref, st_ref, m_ref, acc_ref = refs
    tk = tq
    hg = pl.program_id(1)
    qi = pl.program_id(2)
    n_tiles = pl.num_programs(2)
    head = lambda g: slice(g * HEAD_LANES, (g + 1) * HEAD_LANES)
    kv = [head(0), head(1)] if fox else [head(0), head(0)]

    def queries(g, tile):
        rows = pl.ds(pl.multiple_of(tile * tq, tq), tq)
        if fox:
            return jnp.concatenate([q_ref[0, rows, head(g)], qb_ref[0, rows, :]], axis=1)
        q = q_ref[0, rows, :]
        lane = lax.broadcasted_iota(jnp.int32, q.shape, 1)
        keep = (lane < DIFF_QK_DIM) if g == 0 else (lane >= DIFF_QK_DIM)
        return jnp.where(keep, q, jnp.zeros_like(q))

    def scores_of(qmat, g, j):
        k0 = pl.multiple_of(j * tk, tk)
        kj = k_ref[0, pl.ds(k0, tk), kv[g]]
        if fox:
            kj = jnp.concatenate([kj, kb_ref[pl.ds(k0, tk), kv[g]]], axis=1)
        return _dot_nt(kj, qmat)

    @pl.when(qi == 0)
    def _():
        if fox:
            kb_all = kball_ref[0]
            owner = lax.broadcasted_iota(jnp.int32, kb_all.shape, 1) // BIAS_LANES_PER_HEAD
            for g in range(2):
                kb_ref[:, head(g)] = jnp.where(owner == hg * 2 + g, kb_all, jnp.zeros_like(kb_all))
        st_ref[...] = scores_of(queries(0, 0), 0, 0)

    qs = [queries(g, qi) for g in range(2)]
    scores = lambda g, j: scores_of(qs[g], g, j)
    ones = jnp.ones((ONES_ROWS, tk), BF16)

    def absorb(g, j, st, masked):
        m = m_ref[g]
        k0 = pl.multiple_of(j * tk, tk)
        if masked:
            kpos = k0 + lax.broadcasted_iota(jnp.int32, (tk, tq), 0)
            qpos = qi * tq + lax.broadcasted_iota(jnp.int32, (tk, tq), 1)
            st = jnp.where(qpos >= kpos, st, NEG_INF)
        m_new = jnp.maximum(m, jnp.max(st, axis=0, keepdims=True))
        p = jnp.exp2(st - m_new).astype(BF16)
        vt = jnp.concatenate([vt_ref[0, kv[g], pl.ds(k0, tk)], ones], axis=0)
        acc_ref[g] = jnp.exp2(m - m_new) * acc_ref[g] + _dot(vt, p)
        m_ref[g] = m_new

    def step(j):
        st1 = scores(1, j)
        absorb(0, j, st_ref[...], False)
        st_ref[...] = scores(0, j + 1)
        absorb(1, j, st1, False)

    def steps(j, carry):
        for u in range(ATTN_UNROLL):
            step(j * ATTN_UNROLL + u)
        return carry

    def tail_step(j, carry):
        step(j)
        return carry

    m_ref[...] = jnp.full(m_ref.shape, NEG_INF, F32)
    acc_ref[...] = jnp.zeros(acc_ref.shape, F32)
    n_full = qi // ATTN_UNROLL
    lax.fori_loop(0, n_full, steps, 0)
    lax.fori_loop(n_full * ATTN_UNROLL, qi, tail_step, 0)
    st1 = scores(1, qi)
    absorb(0, qi, st_ref[...], True)
    absorb(1, qi, st1, True)
    st_ref[...] = scores_of(queries(0, jnp.minimum(qi + 1, n_tiles - 1)), 0, 0)
    outs = [acc_ref[g, :HEAD_LANES, :] / acc_ref[g, HEAD_LANES:HEAD_LANES + 1, :] for g in range(2)]
    if fox:
        for g in range(2):
            o_ref[0, :, head(g)] = (outs[g].T * _sigmoid(gate_ref[0, :, head(g)].astype(F32))).astype(o_ref.dtype)
    else:
        lam_p = lam_ref[...]
        lam = (jnp.exp(jnp.sum(lam_p[0:1] * lam_p[1:2], axis=1, keepdims=True))
               - jnp.exp(jnp.sum(lam_p[2:3] * lam_p[3:4], axis=1, keepdims=True)) + lambda_init)
        o = (outs[0] - lam * outs[1]).T
        o_ref[0] = (_rms(o, nw_ref[...]) * (1.0 - lambda_init)).astype(o_ref.dtype)


def _attn_state(tq):
    return [pltpu.VMEM((tq, tq), F32), pltpu.VMEM((2, 1, tq), F32), pltpu.VMEM((2, HEAD_LANES + ONES_ROWS, tq), F32)]


def _fox_attention(q, k, vt, gate, qb, kb, *, tq=ATTN_TILE):
    b, t, dm = q.shape
    width = 2 * HEAD_LANES
    kern = functools.partial(_attn_kernel, tq=tq, fox=True, lambda_init=0.0)
    qblk = pl.BlockSpec((1, tq, width), lambda bi, h, i: (bi, i, h))
    seq = pl.BlockSpec((1, t, width), lambda bi, h, i: (bi, 0, h))
    seq_bias = pl.BlockSpec((1, t, LANES), lambda bi, h, i: (bi, 0, 0))
    return pl.pallas_call(
        kern,
        grid=(b, dm // width, t // tq),
        in_specs=[seq, seq, pl.BlockSpec((1, width, t), lambda bi, h, i: (bi, h, 0)), qblk, seq_bias, seq_bias],
        out_specs=qblk,
        out_shape=jax.ShapeDtypeStruct((b, t, dm), BF16),
        scratch_shapes=_attn_state(tq) + [pltpu.VMEM((t, width), BF16)],
        compiler_params=_params(("arbitrary", "arbitrary", "arbitrary")),
        name="fox_attention",
    )(q, k, vt, gate, qb, kb)


def _diff_attention(qk, vt, lam_params, norm_w, lambda_init, *, tq=ATTN_TILE):
    b, t, _ = qk.shape
    nh = DIFF_HEADS
    kern = functools.partial(_attn_kernel, tq=tq, fox=False, lambda_init=lambda_init)
    qblk = pl.BlockSpec((1, tq, HEAD_LANES), lambda bi, h, i: (bi, i, h))
    return pl.pallas_call(
        kern,
        grid=(b, nh, t // tq),
        in_specs=[pl.BlockSpec((1, t, HEAD_LANES), lambda bi, h, i: (bi, 0, h)),
                  pl.BlockSpec((1, t, HEAD_LANES), lambda bi, h, i: (bi, 0, nh + h)),
                  pl.BlockSpec((1, HEAD_LANES, t), lambda bi, h, i: (bi, h, 0)),
                  _const_spec(lam_params.shape), _const_spec((1, HEAD_LANES))],
        out_specs=qblk,
        out_shape=jax.ShapeDtypeStruct((b, t, nh * HEAD_LANES), BF16),
        scratch_shapes=_attn_state(tq),
        compiler_params=_params(("arbitrary", "arbitrary", "arbitrary")),
        name="diff_attention",
    )(qk, qk, vt, lam_params, norm_w)


def _split_bf16(x, parts):
    out = []
    for _ in range(parts):
        piece = x.astype(BF16)
        out.append(piece)
        x = x - piece.astype(F32)
    return out


def _dot_split(a, b):
    a_hi, a_lo = _split_bf16(a, 2)
    b_hi, b_lo = _split_bf16(b, 2)
    return _dot(a_hi, b_hi) + (_dot(a_hi, b_lo) + _dot(a_lo, b_hi))


def _gdn_prep_kernel(q_ref, k_ref, v_ref, gates_ref, u_ref, w_ref, qd_ref, kd_ref, qk_ref, *, refine):
    c = GDN_CHUNK
    tt = q_ref.shape[1]
    heads = range(GDN_HEADS)
    gt = gates_ref[0]
    lane = lax.broadcasted_iota(jnp.int32, (tt, LANES), 1)
    ri = lax.broadcasted_iota(jnp.int32, (tt, tt), 0)
    ci = lax.broadcasted_iota(jnp.int32, (tt, tt), 1)
    chunk_start = ri - ri % c
    incl = lambda a: jnp.where(ci <= ri, jnp.where(ci >= chunk_start, a, 0.0), 0.0)
    strict = lambda a: jnp.where(ci < ri, jnp.where(ci >= chunk_start, a, 0.0), 0.0)
    ident = jnp.where(ri == ci, 1.0, 0.0)
    ones = jnp.ones((tt, LANES), BF16)
    cols = [slice(hd * HEAD_LANES, (hd + 1) * HEAD_LANES) for hd in heads]
    kt = [k_ref[0, :, cols[hd]] for hd in heads]
    beta = [gt[:, hd:hd + 1] for hd in heads]
    gcc = [gt[:, GDN_HEADS + hd:GDN_HEADS + hd + 1] for hd in heads]
    k16 = [kt[hd].astype(BF16) for hd in heads]
    kb = [kt[hd] * beta[hd] for hd in heads]

    gc_row = []
    for hd in heads:
        g_hi, g_mid, g_lo = (piece.astype(F32) for piece in _split_bf16(gcc[hd], 3))
        pieces = jnp.where(lane == 0, g_hi, jnp.where(lane == 1, g_mid, jnp.where(lane == 2, g_lo, 0.0)))
        gc_row.append(_dot_nt(ones, pieces.astype(BF16)))
    kk = [_dot_nt(kb[hd].astype(BF16), k16[hd]) for hd in heads]
    qk_raw = [_dot_nt(q_ref[0, :, cols[hd]].astype(BF16), k16[hd]) for hd in heads]
    decay = [incl(jnp.exp(incl(gcc[hd] - gc_row[hd]))) for hd in heads]
    lower = [strict(kk[hd] * decay[hd]) for hd in heads]

    inv = [ident - lower[hd] for hd in heads]
    l16 = [lower[hd].astype(BF16) for hd in heads]
    power = [_dot(l16[hd], l16[hd]).astype(BF16) for hd in heads]
    n_sq = int(math.log2(c)) - 1
    for step in range(n_sq - 1):
        r = [_dot(jnp.concatenate([power[hd], inv[hd].astype(BF16)], axis=0), power[hd]) for hd in heads]
        power = [r[hd][:tt].astype(BF16) for hd in heads]
        inv = [inv[hd] + r[hd][tt:] for hd in heads]
    r = [_dot(inv[hd].astype(BF16), power[hd]) for hd in heads]
    inv16 = [(inv[hd] + r[hd]).astype(BF16) for hd in heads]

    eg = [jnp.exp(gcc[hd]) for hd in heads]
    rhs = [jnp.concatenate([v_ref[0, :, cols[hd]] * beta[hd], kb[hd] * eg[hd]], axis=1) for hd in heads]
    sol = [_dot(inv16[hd], rhs[hd].astype(BF16)) for hd in heads]
    if refine:
        a_hi, a_lo, s_hi, s_lo = [], [], [], []
        for hd in heads:
            hi, lo = _split_bf16(ident + lower[hd], 2)
            a_hi.append(hi)
            a_lo.append(lo)
            hi, lo = _split_bf16(sol[hd], 2)
            s_hi.append(hi)
            s_lo.append(lo)
        prod = [_dot(a_hi[hd], s_hi[hd]) + (_dot(a_hi[hd], s_lo[hd]) + _dot(a_lo[hd], s_hi[hd])) for hd in heads]
        corr = [_dot(inv16[hd], (rhs[hd] - prod[hd]).astype(BF16)) for hd in heads]
        sol = [sol[hd] + corr[hd] for hd in heads]
    for hd in heads:
        u_ref[0, :, cols[hd]] = sol[hd][:, :HEAD_LANES]
        w_ref[0, :, cols[hd]] = sol[hd][:, HEAD_LANES:].astype(BF16)
        qk = incl(qk_raw[hd] * decay[hd])
        qd_ref[0, :, cols[hd]] = (q_ref[0, :, cols[hd]] * eg[hd]).astype(BF16)
        for n in range(tt // c):
            rows = slice(n * c, (n + 1) * c)
            qk_ref[0, hd, rows, :] = qk[rows, rows].astype(BF16)
            gl = gcc[hd][(n + 1) * c - 1:(n + 1) * c, :]
            kd_ref[0, rows, cols[hd]] = (kt[hd][rows] * jnp.exp(gl - gcc[hd][rows])).astype(BF16)


def _gdn_prep(qkv, gates, *, tt=256):
    b, t, _ = qkv.shape
    nh = GDN_HEADS
    dm = nh * HEAD_LANES
    blk = lambda part: pl.BlockSpec((1, tt, dm), lambda bi, i: (bi, i, part))
    return pl.pallas_call(
        functools.partial(_gdn_prep_kernel, refine=GDN_REFINE),
        grid=(b, t // tt),
        in_specs=[blk(0), blk(1), blk(2), pl.BlockSpec((1, tt, LANES), lambda bi, i: (bi, i, 0))],
        out_specs=[blk(0)] * 4 + [pl.BlockSpec((1, nh, tt, GDN_CHUNK), lambda bi, i: (bi, 0, i, 0))],
        out_shape=[jax.ShapeDtypeStruct((b, t, dm), F32)]
        + [jax.ShapeDtypeStruct((b, t, dm), BF16)] * 3
        + [jax.ShapeDtypeStruct((b, nh, t, GDN_CHUNK), BF16)],
        compiler_params=_params(("arbitrary", "arbitrary")),
        name="gdn_prep",
    )(qkv, qkv, qkv, gates)


def _gdn_scan_kernel(u_ref, w_ref, qd_ref, kd_ref, qk_ref, gates_ref, z_ref, nw_ref, o_ref, s_ref):
    c = GDN_CHUNK
    nb, tt = u_ref.shape[0], u_ref.shape[1]

    @pl.when(pl.program_id(1) == 0)
    def _():
        s_ref[...] = jnp.zeros_like(s_ref)

    chains = [(bi, hd) for bi in range(nb) for hd in range(GDN_HEADS)]
    cols = [slice(hd * HEAD_LANES, (hd + 1) * HEAD_LANES) for hd in range(GDN_HEADS)]
    state = [s_ref[bi, hd] for bi, hd in chains]
    for n in range(tt // c):
        rows = slice(n * c, (n + 1) * c)
        r = [_dot(jnp.concatenate([w_ref[bi, rows, cols[hd]], qd_ref[bi, rows, cols[hd]]], axis=0),
                  state[i].astype(BF16)) for i, (bi, hd) in enumerate(chains)]
        v_new = [(u_ref[bi, rows, cols[hd]] - r[i][:c]).astype(BF16) for i, (bi, hd) in enumerate(chains)]
        intra = [_dot(qk_ref[bi, hd, rows, :], v_new[i]) for i, (bi, hd) in enumerate(chains)]
        upd = [_dot_tn(kd_ref[bi, rows, cols[hd]], v_new[i]) for i, (bi, hd) in enumerate(chains)]
        for i, (bi, hd) in enumerate(chains):
            last = (n + 1) * c - 1
            decay_last = jnp.exp(gates_ref[bi, last:last + 1, GDN_HEADS + hd:GDN_HEADS + hd + 1])
            state[i] = state[i] * decay_last + upd[i]
            zt = z_ref[bi, rows, cols[hd]].astype(F32)
            o = r[i][c:] + intra[i]
            o_ref[bi, rows, cols[hd]] = (_rms(o, nw_ref[...]) * (zt * _sigmoid(zt))).astype(o_ref.dtype)
    for i, (bi, hd) in enumerate(chains):
        s_ref[bi, hd] = state[i]


def _gdn_scan(u, w, qd, kd, qk, gates, z, norm_w, *, tt=GDN_SCAN_TILE, nb=GDN_SCAN_BATCH):
    b, t, dm = u.shape
    nh = GDN_HEADS
    blk = pl.BlockSpec((nb, tt, dm), lambda bi, i: (bi, i, 0))
    return pl.pallas_call(
        _gdn_scan_kernel,
        grid=(b // nb, t // tt),
        in_specs=[blk, blk, blk, blk,
                  pl.BlockSpec((nb, nh, tt, GDN_CHUNK), lambda bi, i: (bi, 0, i, 0)),
                  pl.BlockSpec((nb, tt, LANES), lambda bi, i: (bi, i, 0)),
                  blk, _const_spec((1, HEAD_LANES))],
        out_specs=blk,
        out_shape=jax.ShapeDtypeStruct((b, t, dm), BF16),
        scratch_shapes=[pltpu.VMEM((nb, nh, GDN_HEAD_DIM, GDN_HEAD_DIM), F32)],
        compiler_params=_params(("arbitrary", "arbitrary")),
        name="gdn_scan",
    )(u, w, qd, kd, qk, gates, z, norm_w)


def _post_kernel(*refs, n_mix, final_norm):
    x_ref = refs[0]
    mix_refs = refs[1:1 + n_mix]
    wout_ref, g_ref, wup_ref, wdn_ref, p_ref, wpp_ref, wpg_ref = refs[1 + n_mix:8 + n_mix]
    rest = refs[8 + n_mix:]
    if final_norm:
        gf_ref, o_ref = rest
    else:
        (o_ref,) = rest
    mix = mix_refs[0][...] if n_mix == 1 else jnp.concatenate([r[...] for r in mix_refs], axis=1)
    x = x_ref[...] + _dot(mix, wout_ref[...])
    h = _rms(x, g_ref[...]).astype(BF16)
    d_ff = wup_ref.shape[1]
    acc = x
    for s in range(d_ff // FF_SEG):
        a = jnp.maximum(_dot(h, wup_ref[:, s * FF_SEG:(s + 1) * FF_SEG]), 0.0)
        acc = acc + _dot((a * a).astype(BF16), wdn_ref[s * FF_SEG:(s + 1) * FF_SEG, :])
    x = acc
    gate = _sigmoid(_dot(x.astype(BF16), wpg_ref[...]))
    x = x + _dot(p_ref[...].astype(BF16), wpp_ref[...]) * gate
    if final_norm:
        x = _rms(x, gf_ref[...])
    o_ref[...] = x


def _post(x2d, mixes, wout, g, wup, wdn, p2d, wpp, wpg, gf=None):
    m, d = x2d.shape
    tm = TOKEN_TILE
    row = lambda i: (i, 0)
    single = pl.Buffered(1)
    const = lambda a: pl.BlockSpec(a.shape, lambda i: (0, 0), pipeline_mode=single)
    args = [x2d, *mixes, wout, g, wup, wdn, p2d, wpp, wpg]
    in_specs = ([pl.BlockSpec((tm, d), row)]
                + [pl.BlockSpec((tm, a.shape[1]), row) for a in mixes]
                + [const(wout), const(g), const(wup), const(wdn), pl.BlockSpec((tm, p2d.shape[1]), row), const(wpp), const(wpg)])
    if gf is not None:
        args.append(gf)
        in_specs.append(const(gf))
    kern = functools.partial(_post_kernel, n_mix=len(mixes), final_norm=gf is not None)
    return pl.pallas_call(
        kern,
        grid=(m // tm,),
        in_specs=in_specs,
        out_specs=pl.BlockSpec((tm, d), row),
        out_shape=jax.ShapeDtypeStruct((m, d), F32),
        compiler_params=_params(("arbitrary",)),
        name="out_proj_mlp_ple",
    )(*args)


def _pad_lanes(a):
    return jnp.pad(a, ((0, 0), (0, LANES - a.shape[1])))


def kernel(x, p, positions, norm_mix, norm_mlp, norm_final, w_in_even, conv_w, a_log, dt_bias, gdn_norm,
           lam_q1, lam_k1, lam_q2, lam_k2, diff_norm, w_out_even, w_in_odd, b_forget, w_out_odd,
           w_mlp_up, w_mlp_down, w_ple_proj, w_ple_gate):
    b, t, d = x.shape
    depth = p.shape[0]
    m = b * t
    assert t % TOKEN_TILE == 0 and d % PROJ_SEG == 0
    nh = GDN_HEADS
    gdn_w = 3 * nh * GDN_HEAD_DIM + nh * GDN_HEAD_DIM
    assert w_in_even.shape[2] == gdn_w + 2 * nh + 3 * DIFF_HEADS * 2 * DIFF_QK_DIM

    inv_freq = ROPE_THETA ** (-jnp.arange(0, DIFF_QK_DIM, 2, dtype=F32) / DIFF_QK_DIM)
    ang = positions.astype(F32)[..., None] * inv_freq
    cos, sin = jnp.cos(ang), jnp.sin(ang)
    cos_t = jnp.concatenate([cos, cos, cos, cos], axis=-1).reshape(m, LANES)
    sin_t = jnp.concatenate([-sin, sin, -sin, sin], axis=-1).reshape(m, LANES)

    x2d = x.reshape(m, d)
    for i in range(depth):
        j = i // 2
        g_mix = norm_mix[i].reshape(1, d)
        if i % 2 == 0:
            lambda_init = 0.8 - 0.6 * math.exp(-0.3 * i)
            w = w_in_even[j]
            wm = jnp.concatenate([w[:, :gdn_w], w[:, gdn_w + 2 * nh:]], axis=1).astype(BF16)
            wg = _pad_lanes(w[:, gdn_w:gdn_w + 2 * nh]).astype(BF16)
            alog_row = _pad_lanes(jnp.concatenate([jnp.zeros((nh,), F32), a_log[j]]).reshape(1, 2 * nh))
            dt_row = _pad_lanes(jnp.concatenate([jnp.zeros((nh,), F32), dt_bias[j]]).reshape(1, 2 * nh))
            qkv, z, qkb, vbt, gates = _even_in(x2d, g_mix, wm, wg, conv_w[j], alog_row, dt_row, cos_t, sin_t, t)
            qkv, z, qkb, gates = (a.reshape(b, t, -1) for a in (qkv, z, qkb, gates))
            u, wy, qd, kd, qk = _gdn_prep(qkv, gates)
            o_a = _gdn_scan(u, wy, qd, kd, qk, gates, z, gdn_norm[j].reshape(1, HEAD_LANES))
            lam_params = jnp.stack([lam_q1[j], lam_k1[j], lam_q2[j], lam_k2[j]])
            o_b = _diff_attention(qkb, vbt, lam_params, diff_norm[j].reshape(1, HEAD_LANES), lambda_init)
            mixes = [o_a.reshape(m, -1), o_b.reshape(m, -1)]
            wout = w_out_even[j].astype(BF16)
        else:
            w = w_in_odd[j]
            d_mix = (w.shape[1] - FOX_HEADS) // 4
            wm = w[:, :4 * d_mix].astype(BF16)
            wf = _pad_lanes(w[:, 4 * d_mix:]).astype(BF16)
            bf_row = _pad_lanes(b_forget[j].reshape(1, FOX_HEADS))
            q, k, vt, gate, qb, kb = _odd_in(x2d, g_mix, wm, wf, bf_row, t)
            q, k, gate, qb, kb = (a.reshape(b, t, -1) for a in (q, k, gate, qb, kb))
            o = _fox_attention(q, k, vt, gate, qb, kb)
            mixes = [o.reshape(m, -1)]
            wout = w_out_odd[j].astype(BF16)
        x2d = _post(x2d, mixes, wout, norm_mlp[i].reshape(1, d), w_mlp_up[i].astype(BF16),
                    w_mlp_down[i].astype(BF16), p[i].reshape(m, -1), w_ple_proj[i].astype(BF16),
                    w_ple_gate[i].astype(BF16), norm_final.reshape(1, d) if i == depth - 1 else None)
    return x2d.reshape(b, t, d)
```

```python
import functools
import math

import jax
import jax.numpy as jnp
import numpy as np
from jax import lax
from jax.experimental import pallas as pl
from jax.experimental.pallas import tpu as pltpu

F32 = jnp.float32
BF16 = jnp.bfloat16

GDN_HEADS = 4
GDN_HEAD_DIM = 128
GDN_CHUNK = 64
CONV_WIDTH = 4
DIFF_HEADS = 4
DIFF_QK_DIM = 64
FOX_HEADS = 8
HEAD_LANES = 128
ROPE_THETA = 10000.0
EPS = 1e-6
NEG_INF = -1e30
LOG2E = 1.4426950408889634
LANES = 128
SUBLANES = 8
VMEM_LIMIT_BYTES = 56 * 1024 * 1024

TOKEN_TILE = 1024
EVEN_TOKEN_TILE = 512
PROJ_SEG = 512
FF_SEG = 1024
GDN_REFINE = True
GDN_SCAN_TILE = 256
GDN_SCAN_BATCH = 2
ATTN_TILE = 512
ATTN_UNROLL = 2
BIAS_LANES_PER_HEAD = 16
ONES_ROWS = 16


def _dot(a, b):
    return jnp.dot(a, b, preferred_element_type=F32)


def _dot_exact(a, b):
    return jnp.dot(a, b, preferred_element_type=F32, precision=lax.Precision.HIGHEST)


def _dot_nt(a, b):
    return lax.dot_general(a, b, (((1,), (1,)), ((), ())), preferred_element_type=F32)


def _dot_tn(a, b):
    return lax.dot_general(a, b, (((0,), (0,)), ((), ())), preferred_element_type=F32)


def _rms(x, g):
    return x * lax.rsqrt(jnp.mean(x * x, axis=-1, keepdims=True) + EPS) * g


def _sigmoid(x):
    return 1.0 / (1.0 + jnp.exp(-x))


def _softplus(x):
    return jnp.maximum(x, 0.0) + jnp.log1p(jnp.exp(-jnp.abs(x)))


def _row_scan(x, period):
    rows = lax.broadcasted_iota(jnp.int32, x.shape, 0) % period
    s = 1
    while s < period:
        x = x + jnp.where(rows >= s, pltpu.roll(x, s, 0), 0.0)
        s *= 2
    return x


def _const_spec(shape):
    return pl.BlockSpec(shape, lambda *_: (0,) * len(shape))


def _params(sem):
    return pltpu.CompilerParams(dimension_semantics=sem, vmem_limit_bytes=VMEM_LIMIT_BYTES)


def _even_in_kernel(x_ref, g_ref, wm_ref, wg_ref, conv_ref, alog_ref, dt_ref, cos_ref, sin_ref,
                    qkv_ref, z_ref, qkb_ref, vbt_ref, gates_ref, h_ref, carry_ref, tr_ref, pad_ref, *, tiles_per_seq):
    tm = x_ref.shape[0]
    i = pl.program_id(0)
    h_ref[...] = _rms(x_ref[...], g_ref[...]).astype(BF16)
    seq_start = (i % tiles_per_seq) == 0
    seg = lambda s: slice(s * PROJ_SEG, (s + 1) * PROJ_SEG)
    project = lambda s: _dot(h_ref[...], wm_ref[:, seg(s)])

    def gdn_qkv(s, y):
        cols = seg(s)
        pad_ref[0:SUBLANES, :] = jnp.where(seq_start, 0.0, carry_ref[:, cols])
        pad_ref[SUBLANES:, :] = y
        carry_ref[:, cols] = y[tm - SUBLANES:, :]
        w = conv_ref[:, cols]
        a = y * w[CONV_WIDTH - 1:CONV_WIDTH, :]
        for k in range(1, CONV_WIDTH):
            a = a + pad_ref[SUBLANES - k:SUBLANES - k + tm, :] * w[CONV_WIDTH - 1 - k:CONV_WIDTH - k, :]
        a = a * _sigmoid(a)
        if s < 2:
            outs = []
            for hd in range(GDN_HEADS):
                blk = a[:, hd * HEAD_LANES:(hd + 1) * HEAD_LANES]
                n = blk * lax.rsqrt(jnp.sum(blk * blk, axis=-1, keepdims=True) + EPS)
                outs.append(n * (GDN_HEAD_DIM ** -0.5) if s == 0 else n)
            a = jnp.concatenate(outs, axis=1)
        qkv_ref[:, cols] = a

    def gdn_gate(s, y):
        z_ref[...] = y.astype(BF16)

    def diff_qk(s, y):
        cos = jnp.concatenate([cos_ref[...]] * (PROJ_SEG // LANES), axis=1)
        sin = jnp.concatenate([sin_ref[...]] * (PROJ_SEG // LANES), axis=1)
        lane = lax.broadcasted_iota(jnp.int32, (tm, PROJ_SEG), 1)
        first_half = (lane % DIFF_QK_DIM) < (DIFF_QK_DIM // 2)
        swapped = jnp.where(first_half, pltpu.roll(y, PROJ_SEG - DIFF_QK_DIM // 2, 1),
                            pltpu.roll(y, DIFF_QK_DIM // 2, 1))
        scale = DIFF_QK_DIM ** -0.5 * LOG2E if s == 4 else 1.0
        qkb_ref[:, seg(s - 4)] = ((y * cos + swapped * sin) * scale).astype(BF16)

    def diff_v(s, y):
        tr_ref[...] = y
        vbt_ref[0] = tr_ref[...].T.astype(BF16)

    stages = ((0, gdn_qkv), (3, gdn_gate), (1, gdn_qkv), (6, diff_v), (2, gdn_qkv), (4, diff_qk), (5, diff_qk))
    pending = project(stages[0][0])
    for n, (s, epilogue) in enumerate(stages):
        upcoming = project(stages[n + 1][0]) if n + 1 < len(stages) else _dot(h_ref[...], wg_ref[...])
        epilogue(s, pending)
        pending = upcoming

    graw = pending
    beta = _sigmoid(graw)
    g = -jnp.exp(alog_ref[...]) * _softplus(graw + dt_ref[...])
    gc = _row_scan(g, GDN_CHUNK)
    lane_g = lax.broadcasted_iota(jnp.int32, (tm, LANES), 1)
    gates_ref[...] = jnp.where(lane_g < GDN_HEADS, beta, gc)


def _even_in(x2d, g, wm, wg, conv_w, alog_row, dt_row, cos_t, sin_t, seq_len):
    m, d = x2d.shape
    tm = EVEN_TOKEN_TILE
    n_main = wm.shape[1]
    tps = seq_len // tm
    kern = functools.partial(_even_in_kernel, tiles_per_seq=tps)
    row = lambda i: (i, 0)
    return pl.pallas_call(
        kern,
        grid=(m // tm,),
        in_specs=[
            pl.BlockSpec((tm, d), row),
            _const_spec((1, d)),
            _const_spec((d, n_main)),
            _const_spec((d, LANES)),
            _const_spec(conv_w.shape),
            _const_spec((1, LANES)),
            _const_spec((1, LANES)),
            pl.BlockSpec((tm, LANES), row),
            pl.BlockSpec((tm, LANES), row),
        ],
        out_specs=[
            pl.BlockSpec((tm, 3 * PROJ_SEG), row),
            pl.BlockSpec((tm, PROJ_SEG), row),
            pl.BlockSpec((tm, 2 * PROJ_SEG), row),
            pl.BlockSpec((1, PROJ_SEG, tm), lambda i: (i // tps, 0, i % tps)),
            pl.BlockSpec((tm, LANES), row),
        ],
        out_shape=[
            jax.ShapeDtypeStruct((m, 3 * PROJ_SEG), F32),
            jax.ShapeDtypeStruct((m, PROJ_SEG), BF16),
            jax.ShapeDtypeStruct((m, 2 * PROJ_SEG), BF16),
            jax.ShapeDtypeStruct((m // seq_len, PROJ_SEG, seq_len), BF16),
            jax.ShapeDtypeStruct((m, LANES), F32),
        ],
        scratch_shapes=[pltpu.VMEM((tm, d), BF16), pltpu.VMEM((SUBLANES, 3 * PROJ_SEG), F32), pltpu.VMEM((tm, PROJ_SEG), F32),
                        pltpu.VMEM((tm + SUBLANES, PROJ_SEG), F32)],
        compiler_params=_params(("arbitrary",)),
        name="even_in_proj",
    )(x2d, g, wm, wg, conv_w, alog_row, dt_row, cos_t, sin_t)


def _odd_in_kernel(x_ref, g_ref, wm_ref, wf_ref, bf_ref, sel_ref, ones_ref, q_ref, k_ref, vt_ref, gate_ref, qb_ref, kb_ref,
                   h_ref, carry_ref, tr_ref, *, tiles_per_seq, d_mix):
    tm = x_ref.shape[0]
    i = pl.program_id(0)
    h_ref[...] = _rms(x_ref[...], g_ref[...]).astype(BF16)
    head_dim = d_mix // FOX_HEADS
    for o_ref, base, scale in ((q_ref, 0, head_dim ** -0.5 * LOG2E), (k_ref, d_mix, 1.0),
                               (vt_ref, 2 * d_mix, 1.0), (gate_ref, 3 * d_mix, 1.0)):
        for s in range(d_mix // PROJ_SEG):
            cols = slice(s * PROJ_SEG, (s + 1) * PROJ_SEG)
            y = _dot(h_ref[...], wm_ref[:, base + s * PROJ_SEG:base + (s + 1) * PROJ_SEG])
            if o_ref is vt_ref:
                tr_ref[...] = y
                o_ref[0, cols, :] = tr_ref[...].T.astype(BF16)
            else:
                o_ref[:, cols] = (y * scale).astype(BF16)
    f = _dot(h_ref[...], wf_ref[...]) + bf_ref[...]
    log_f = jnp.minimum(f, 0.0) - jnp.log1p(jnp.exp(-jnp.abs(f)))
    prev = jnp.where((i % tiles_per_seq) == 0, 0.0, carry_ref[0:1, :])
    cum = _row_scan(log_f, tm) + prev
    carry_ref[...] = jnp.broadcast_to(cum[tm - 1:tm, :], carry_ref.shape)
    pieces = jnp.concatenate(_split_bf16(LOG2E * cum, 3), axis=1)
    lanes = _dot(pieces, sel_ref[...]) + ones_ref[...]
    qb_ref[...] = lanes[:, :LANES].astype(BF16)
    kb_ref[...] = lanes[:, LANES:].astype(BF16)


def _bias_lane_tables():
    sel = np.zeros((3 * LANES, 2 * LANES), np.float32)
    ones = np.zeros((1, 2 * LANES), np.float32)
    for h in range(FOX_HEADS):
        base = BIAS_LANES_PER_HEAD * h
        for piece in range(3):
            sel[LANES * piece + h, base + 3 + piece] = 1.0
            sel[LANES * piece + h, LANES + base + piece] = -1.0
            ones[0, base + piece] = 1.0
            ones[0, LANES + base + 3 + piece] = 1.0
    return jnp.asarray(sel, BF16), jnp.asarray(ones, F32)


def _odd_in(x2d, g, wm, wf, bf_row, seq_len):
    m, d = x2d.shape
    sel, ones_row = _bias_lane_tables()
    tm = TOKEN_TILE
    d_mix = wm.shape[1] // 4
    tps = seq_len // tm
    kern = functools.partial(_odd_in_kernel, tiles_per_seq=tps, d_mix=d_mix)
    row = lambda i: (i, 0)
    row_blk = pl.BlockSpec((tm, d_mix), row)
    row_shape = jax.ShapeDtypeStruct((m, d_mix), BF16)
    return pl.pallas_call(
        kern,
        grid=(m // tm,),
        in_specs=[
            pl.BlockSpec((tm, d), row),
            _const_spec((1, d)),
            _const_spec(wm.shape),
            _const_spec((d, LANES)),
            _const_spec((1, LANES)),
            _const_spec(sel.shape),
            _const_spec(ones_row.shape),
        ],
        out_specs=[row_blk, row_blk, pl.BlockSpec((1, d_mix, tm), lambda i: (i // tps, 0, i % tps)), row_blk,
                   pl.BlockSpec((tm, LANES), row), pl.BlockSpec((tm, LANES), row)],
        out_shape=[row_shape, row_shape, jax.ShapeDtypeStruct((m // seq_len, d_mix, seq_len), BF16), row_shape,
                   jax.ShapeDtypeStruct((m, LANES), BF16), jax.ShapeDtypeStruct((m, LANES), BF16)],
        scratch_shapes=[pltpu.VMEM((tm, d), BF16), pltpu.VMEM((SUBLANES, LANES), F32), pltpu.VMEM((tm, PROJ_SEG), F32)],
        compiler_params=_params(("arbitrary",)),
        name="odd_in_proj",
    )(x2d, g, wm, wf, bf_row, sel, ones_row)


def _attn_kernel(*refs, tq, fox, lambda_init):
    if fox:
        q_ref, k_ref, vt_ref, gate_ref, qb_ref, kball_ref, o_ref, st_ref, m_ref, acc_ref, kb_ref = refs
    else:
        q_ref, k_ref, vt_ref, lam_ref, nw_ref, o_ref, st_ref, m_ref, acc_ref = refs
    tk = tq
    hg = pl.program_id(1)
    qi = pl.program_id(2)
    n_tiles = pl.num_programs(2)
    head = lambda g: slice(g * HEAD_LANES, (g + 1) * HEAD_LANES)
    kv = [head(0), head(1)] if fox else [head(0), head(0)]

    def queries(g, tile):
        rows = pl.ds(pl.multiple_of(tile * tq, tq), tq)
        if fox:
            return jnp.concatenate([q_ref[0, rows, head(g)], qb_ref[0, rows, :]], axis=1)
        q = q_ref[0, rows, :]
        lane = lax.broadcasted_iota(jnp.int32, q.shape, 1)
        keep = (lane < DIFF_QK_DIM) if g == 0 else (lane >= DIFF_QK_DIM)
        return jnp.where(keep, q, jnp.zeros_like(q))

    def scores_of(qmat, g, j):
        k0 = pl.multiple_of(j * tk, tk)
        kj = k_ref[0, pl.ds(k0, tk), kv[g]]
        if fox:
            kj = jnp.concatenate([kj, kb_ref[pl.ds(k0, tk), kv[g]]], axis=1)
        return _dot_nt(kj, qmat)

    @pl.when(qi == 0)
    def _():
        if fox:
            kb_all = kball_ref[0]
            owner = lax.broadcasted_iota(jnp.int32, kb_all.shape, 1) // BIAS_LANES_PER_HEAD
            for g in range(2):
                kb_ref[:, head(g)] = jnp.where(owner == hg * 2 + g, kb_all, jnp.zeros_like(kb_all))
        st_ref[...] = scores_of(queries(0, 0), 0, 0)

    qs = [queries(g, qi) for g in range(2)]
    scores = lambda g, j: scores_of(qs[g], g, j)
    ones = jnp.ones((ONES_ROWS, tk), BF16)

    def absorb(g, j, st, masked):
        m = m_ref[g]
        k0 = pl.multiple_of(j * tk, tk)
        if masked:
            kpos = k0 + lax.broadcasted_iota(jnp.int32, (tk, tq), 0)
            qpos = qi * tq + lax.broadcasted_iota(jnp.int32, (tk, tq), 1)
            st = jnp.where(qpos >= kpos, st, NEG_INF)
        m_new = jnp.maximum(m, jnp.max(st, axis=0, keepdims=True))
        p = jnp.exp2(st - m_new).astype(BF16)
        vt = jnp.concatenate([vt_ref[0, kv[g], pl.ds(k0, tk)], ones], axis=0)
        acc_ref[g] = jnp.exp2(m - m_new) * acc_ref[g] + _dot(vt, p)
        m_ref[g] = m_new

    def step(j):
        st1 = scores(1, j)
        absorb(0, j, st_ref[...], False)
        st_ref[...] = scores(0, j + 1)
        absorb(1, j, st1, False)

    def steps(j, carry):
        for u in range(ATTN_UNROLL):
            step(j * ATTN_UNROLL + u)
        return carry

    def tail_step(j, carry):
        step(j)
        return carry

    m_ref[...] = jnp.full(m_ref.shape, NEG_INF, F32)
    acc_ref[...] = jnp.zeros(acc_ref.shape, F32)
    n_full = qi // ATTN_UNROLL
    lax.fori_loop(0, n_full, steps, 0)
    lax.fori_loop(n_full * ATTN_UNROLL, qi, tail_step, 0)
    st1 = scores(1, qi)
    absorb(0, qi, st_ref[...], True)
    absorb(1, qi, st1, True)
    st_ref[...] = scores_of(queries(0, jnp.minimum(qi + 1, n_tiles - 1)), 0, 0)
    outs = [acc_ref[g, :HEAD_LANES, :] / acc_ref[g, HEAD_LANES:HEAD_LANES + 1, :] for g in range(2)]
    if fox:
        for g in range(2):
            o_ref[0, :, head(g)] = (outs[g].T * _sigmoid(gate_ref[0, :, head(g)].astype(F32))).astype(o_ref.dtype)
    else:
        lam_p = lam_ref[...]
        lam = (jnp.exp(jnp.sum(lam_p[0:1] * lam_p[1:2], axis=1, keepdims=True))
               - jnp.exp(jnp.sum(lam_p[2:3] * lam_p[3:4], axis=1, keepdims=True)) + lambda_init)
        o = (outs[0] - lam * outs[1]).T
        o_ref[0] = (_rms(o, nw_ref[...]) * (1.0 - lambda_init)).astype(o_ref.dtype)


def _attn_state(tq):
    return [pltpu.VMEM((tq, tq), F32), pltpu.VMEM((2, 1, tq), F32), pltpu.VMEM((2, HEAD_LANES + ONES_ROWS, tq), F32)]


def _fox_attention(q, k, vt, gate, qb, kb, *, tq=ATTN_TILE):
    b, t, dm = q.shape
    width = 2 * HEAD_LANES
    kern = functools.partial(_attn_kernel, tq=tq, fox=True, lambda_init=0.0)
    qblk = pl.BlockSpec((1, tq, width), lambda bi, h, i: (bi, i, h))
    seq = pl.BlockSpec((1, t, width), lambda bi, h, i: (bi, 0, h))
    seq_bias = pl.BlockSpec((1, t, LANES), lambda bi, h, i: (bi, 0, 0))
    return pl.pallas_call(
        kern,
        grid=(b, dm // width, t // tq),
        in_specs=[seq, seq, pl.BlockSpec((1, width, t), lambda bi, h, i: (bi, h, 0)), qblk, seq_bias, seq_bias],
        out_specs=qblk,
        out_shape=jax.ShapeDtypeStruct((b, t, dm), BF16),
        scratch_shapes=_attn_state(tq) + [pltpu.VMEM((t, width), BF16)],
        compiler_params=_params(("arbitrary", "arbitrary", "arbitrary")),
        name="fox_attention",
    )(q, k, vt, gate, qb, kb)


def _diff_attention(qk, vt, lam_params, norm_w, lambda_init, *, tq=ATTN_TILE):
    b, t, _ = qk.shape
    nh = DIFF_HEADS
    kern = functools.partial(_attn_kernel, tq=tq, fox=False, lambda_init=lambda_init)
    qblk = pl.BlockSpec((1, tq, HEAD_LANES), lambda bi, h, i: (bi, i, h))
    return pl.pallas_call(
        kern,
        grid=(b, nh, t // tq),
        in_specs=[pl.BlockSpec((1, t, HEAD_LANES), lambda bi, h, i: (bi, 0, h)),
                  pl.BlockSpec((1, t, HEAD_LANES), lambda bi, h, i: (bi, 0, nh + h)),
                  pl.BlockSpec((1, HEAD_LANES, t), lambda bi, h, i: (bi, h, 0)),
                  _const_spec(lam_params.shape), _const_spec((1, HEAD_LANES))],
        out_specs=qblk,
        out_shape=jax.ShapeDtypeStruct((b, t, nh * HEAD_LANES), BF16),
        scratch_shapes=_attn_state(tq),
        compiler_params=_params(("arbitrary", "arbitrary", "arbitrary")),
        name="diff_attention",
    )(qk, qk, vt, lam_params, norm_w)


def _split_bf16(x, parts):
    out = []
    for _ in range(parts):
        piece = x.astype(BF16)
        out.append(piece)
        x = x - piece.astype(F32)
    return out


def _dot_split(a, b):
    a_hi, a_lo = _split_bf16(a, 2)
    b_hi, b_lo = _split_bf16(b, 2)
    return _dot(a_hi, b_hi) + (_dot(a_hi, b_lo) + _dot(a_lo, b_hi))


def _gdn_prep_kernel(q_ref, k_ref, v_ref, gates_ref, u_ref, w_ref, qd_ref, kd_ref, qk_ref, *, refine):
    c = GDN_CHUNK
    tt = q_ref.shape[1]
    heads = range(GDN_HEADS)
    gt = gates_ref[0]
    lane = lax.broadcasted_iota(jnp.int32, (tt, LANES), 1)
    ri = lax.broadcasted_iota(jnp.int32, (tt, tt), 0)
    ci = lax.broadcasted_iota(jnp.int32, (tt, tt), 1)
    chunk_start = ri - ri % c
    incl = lambda a: jnp.where(ci <= ri, jnp.where(ci >= chunk_start, a, 0.0), 0.0)
    strict = lambda a: jnp.where(ci < ri, jnp.where(ci >= chunk_start, a, 0.0), 0.0)
    ident = jnp.where(ri == ci, 1.0, 0.0)
    ones = jnp.ones((tt, LANES), BF16)
    cols = [slice(hd * HEAD_LANES, (hd + 1) * HEAD_LANES) for hd in heads]
    kt = [k_ref[0, :, cols[hd]] for hd in heads]
    beta = [gt[:, hd:hd + 1] for hd in heads]
    gcc = [gt[:, GDN_HEADS + hd:GDN_HEADS + hd + 1] for hd in heads]
    k16 = [kt[hd].astype(BF16) for hd in heads]
    kb = [kt[hd] * beta[hd] for hd in heads]

    gc_row = []
    for hd in heads:
        g_hi, g_mid, g_lo = (piece.astype(F32) for piece in _split_bf16(gcc[hd], 3))
        pieces = jnp.where(lane == 0, g_hi, jnp.where(lane == 1, g_mid, jnp.where(lane == 2, g_lo, 0.0)))
        gc_row.append(_dot_nt(ones, pieces.astype(BF16)))
    kk = [_dot_nt(kb[hd].astype(BF16), k16[hd]) for hd in heads]
    qk_raw = [_dot_nt(q_ref[0, :, cols[hd]].astype(BF16), k16[hd]) for hd in heads]
    decay = [incl(jnp.exp(incl(gcc[hd] - gc_row[hd]))) for hd in heads]
    lower = [strict(kk[hd] * decay[hd]) for hd in heads]

    inv = [ident - lower[hd] for hd in heads]
    l16 = [lower[hd].astype(BF16) for hd in heads]
    power = [_dot(l16[hd], l16[hd]).astype(BF16) for hd in heads]
    n_sq = int(math.log2(c)) - 1
    for step in range(n_sq - 1):
        r = [_dot(jnp.concatenate([power[hd], inv[hd].astype(BF16)], axis=0), power[hd]) for hd in heads]
        power = [r[hd][:tt].astype(BF16) for hd in heads]
        inv = [inv[hd] + r[hd][tt:] for hd in heads]
    r = [_dot(inv[hd].astype(BF16), power[hd]) for hd in heads]
    inv16 = [(inv[hd] + r[hd]).astype(BF16) for hd in heads]

    eg = [jnp.exp(gcc[hd]) for hd in heads]
    rhs = [jnp.concatenate([v_ref[0, :, cols[hd]] * beta[hd], kb[hd] * eg[hd]], axis=1) for hd in heads]
    sol = [_dot(inv16[hd], rhs[hd].astype(BF16)) for hd in heads]
    if refine:
        a_hi, a_lo, s_hi, s_lo = [], [], [], []
        for hd in heads:
            hi, lo = _split_bf16(ident + lower[hd], 2)
            a_hi.append(hi)
            a_lo.append(lo)
            hi, lo = _split_bf16(sol[hd], 2)
            s_hi.append(hi)
            s_lo.append(lo)
        prod = [_dot(a_hi[hd], s_hi[hd]) + (_dot(a_hi[hd], s_lo[hd]) + _dot(a_lo[hd], s_hi[hd])) for hd in heads]
        corr = [_dot(inv16[hd], (rhs[hd] - prod[hd]).astype(BF16)) for hd in heads]
        sol = [sol[hd] + corr[hd] for hd in heads]
    for hd in heads:
        u_ref[0, :, cols[hd]] = sol[hd][:, :HEAD_LANES]
        w_ref[0, :, cols[hd]] = sol[hd][:, HEAD_LANES:].astype(BF16)
        qk = incl(qk_raw[hd] * decay[hd])
        qd_ref[0, :, cols[hd]] = (q_ref[0, :, cols[hd]] * eg[hd]).astype(BF16)
        for n in range(tt // c):
            rows = slice(n * c, (n + 1) * c)
            qk_ref[0, hd, rows, :] = qk[rows, rows].astype(BF16)
            gl = gcc[hd][(n + 1) * c - 1:(n + 1) * c, :]
            kd_ref[0, rows, cols[hd]] = (kt[hd][rows] * jnp.exp(gl - gcc[hd][rows])).astype(BF16)


def _gdn_prep(qkv, gates, *, tt=256):
    b, t, _ = qkv.shape
    nh = GDN_HEADS
    dm = nh * HEAD_LANES
    blk = lambda part: pl.BlockSpec((1, tt, dm), lambda bi, i: (bi, i, part))
    return pl.pallas_call(
        functools.partial(_gdn_prep_kernel, refine=GDN_REFINE),
        grid=(b, t // tt),
        in_specs=[blk(0), blk(1), blk(2), pl.BlockSpec((1, tt, LANES), lambda bi, i: (bi, i, 0))],
        out_specs=[blk(0)] * 4 + [pl.BlockSpec((1, nh, tt, GDN_CHUNK), lambda bi, i: (bi, 0, i, 0))],
        out_shape=[jax.ShapeDtypeStruct((b, t, dm), F32)]
        + [jax.ShapeDtypeStruct((b, t, dm), BF16)] * 3
        + [jax.ShapeDtypeStruct((b, nh, t, GDN_CHUNK), BF16)],
        compiler_params=_params(("arbitrary", "arbitrary")),
        name="gdn_prep",
    )(qkv, qkv, qkv, gates)


def _gdn_scan_kernel(u_ref, w_ref, qd_ref, kd_ref, qk_ref, gates_ref, z_ref, nw_ref, o_ref, s_ref):
    c = GDN_CHUNK
    nb, tt = u_ref.shape[0], u_ref.shape[1]

    @pl.when(pl.program_id(1) == 0)
    def _():
        s_ref[...] = jnp.zeros_like(s_ref)

    chains = [(bi, hd) for bi in range(nb) for hd in range(GDN_HEADS)]
    cols = [slice(hd * HEAD_LANES, (hd + 1) * HEAD_LANES) for hd in range(GDN_HEADS)]
    state = [s_ref[bi, hd] for bi, hd in chains]
    for n in range(tt // c):
        rows = slice(n * c, (n + 1) * c)
        r = [_dot(jnp.concatenate([w_ref[bi, rows, cols[hd]], qd_ref[bi, rows, cols[hd]]], axis=0),
                  state[i].astype(BF16)) for i, (bi, hd) in enumerate(chains)]
        v_new = [(u_ref[bi, rows, cols[hd]] - r[i][:c]).astype(BF16) for i, (bi, hd) in enumerate(chains)]
        intra = [_dot(qk_ref[bi, hd, rows, :], v_new[i]) for i, (bi, hd) in enumerate(chains)]
        upd = [_dot_tn(kd_ref[bi, rows, cols[hd]], v_new[i]) for i, (bi, hd) in enumerate(chains)]
        for i, (bi, hd) in enumerate(chains):
            last = (n + 1) * c - 1
            decay_last = jnp.exp(gates_ref[bi, last:last + 1, GDN_HEADS + hd:GDN_HEADS + hd + 1])
            state[i] = state[i] * decay_last + upd[i]
            zt = z_ref[bi, rows, cols[hd]].astype(F32)
            o = r[i][c:] + intra[i]
            o_ref[bi, rows, cols[hd]] = (_rms(o, nw_ref[...]) * (zt * _sigmoid(zt))).astype(o_ref.dtype)
    for i, (bi, hd) in enumerate(chains):
        s_ref[bi, hd] = state[i]


def _gdn_scan(u, w, qd, kd, qk, gates, z, norm_w, *, tt=GDN_SCAN_TILE, nb=GDN_SCAN_BATCH):
    b, t, dm = u.shape
    nh = GDN_HEADS
    assert b % nb == 0 and t % tt == 0
    blk = pl.BlockSpec((nb, tt, dm), lambda bi, i: (bi, i, 0))
    return pl.pallas_call(
        _gdn_scan_kernel,
        grid=(b // nb, t // tt),
        in_specs=[blk, blk, blk, blk,
                  pl.BlockSpec((nb, nh, tt, GDN_CHUNK), lambda bi, i: (bi, 0, i, 0)),
                  pl.BlockSpec((nb, tt, LANES), lambda bi, i: (bi, i, 0)),
                  blk, _const_spec((1, HEAD_LANES))],
        out_specs=blk,
        out_shape=jax.ShapeDtypeStruct((b, t, dm), BF16),
        scratch_shapes=[pltpu.VMEM((nb, nh, GDN_HEAD_DIM, GDN_HEAD_DIM), F32)],
        compiler_params=_params(("arbitrary", "arbitrary")),
        name="gdn_scan",
    )(u, w, qd, kd, qk, gates, z, norm_w)


def _post_kernel(*refs, n_mix, final_norm):
    x_ref = refs[0]
    mix_refs = refs[1:1 + n_mix]
    wout_ref, g_ref, wup_ref, wdn_ref, p_ref, wpp_ref, wpg_ref = refs[1 + n_mix:8 + n_mix]
    rest = refs[8 + n_mix:]
    if final_norm:
        gf_ref, o_ref = rest
    else:
        (o_ref,) = rest
    mix = mix_refs[0][...] if n_mix == 1 else jnp.concatenate([r[...] for r in mix_refs], axis=1)
    x = x_ref[...] + _dot(mix, wout_ref[...])
    h = _rms(x, g_ref[...]).astype(BF16)
    d_ff = wup_ref.shape[1]
    acc = x
    for s in range(d_ff // FF_SEG):
        a = jnp.maximum(_dot(h, wup_ref[:, s * FF_SEG:(s + 1) * FF_SEG]), 0.0)
        acc = acc + _dot((a * a).astype(BF16), wdn_ref[s * FF_SEG:(s + 1) * FF_SEG, :])
    x = acc
    gate = _sigmoid(_dot(x.astype(BF16), wpg_ref[...]))
    x = x + _dot(p_ref[...].astype(BF16), wpp_ref[...]) * gate
    if final_norm:
        x = _rms(x, gf_ref[...])
    o_ref[...] = x


def _post(x2d, mixes, wout, g, wup, wdn, p2d, wpp, wpg, gf=None):
    m, d = x2d.shape
    tm = TOKEN_TILE
    row = lambda i: (i, 0)
    single = pl.Buffered(1)
    const = lambda a: pl.BlockSpec(a.shape, lambda i: (0, 0), pipeline_mode=single)
    args = [x2d, *mixes, wout, g, wup, wdn, p2d, wpp, wpg]
    in_specs = ([pl.BlockSpec((tm, d), row)]
                + [pl.BlockSpec((tm, a.shape[1]), row) for a in mixes]
                + [const(wout), const(g), const(wup), const(wdn), pl.BlockSpec((tm, p2d.shape[1]), row), const(wpp), const(wpg)])
    if gf is not None:
        args.append(gf)
        in_specs.append(const(gf))
    kern = functools.partial(_post_kernel, n_mix=len(mixes), final_norm=gf is not None)
    return pl.pallas_call(
        kern,
        grid=(m // tm,),
        in_specs=in_specs,
        out_specs=pl.BlockSpec((tm, d), row),
        out_shape=jax.ShapeDtypeStruct((m, d), F32),
        compiler_params=_params(("arbitrary",)),
        name="out_proj_mlp_ple",
    )(*args)


def _pad_lanes(a):
    return jnp.pad(a, ((0, 0), (0, LANES - a.shape[1])))


def kernel(x, p, positions, norm_mix, norm_mlp, norm_final, w_in_even, conv_w, a_log, dt_bias, gdn_norm,
           lam_q1, lam_k1, lam_q2, lam_k2, diff_norm, w_out_even, w_in_odd, b_forget, w_out_odd,
           w_mlp_up, w_mlp_down, w_ple_proj, w_ple_gate):
    b, t, d = x.shape
    depth = p.shape[0]
    m = b * t
    assert t % TOKEN_TILE == 0 and d % PROJ_SEG == 0
    nh = GDN_HEADS
    gdn_w = 3 * nh * GDN_HEAD_DIM + nh * GDN_HEAD_DIM
    assert w_in_even.shape[2] == gdn_w + 2 * nh + 3 * DIFF_HEADS * 2 * DIFF_QK_DIM

    inv_freq = ROPE_THETA ** (-jnp.arange(0, DIFF_QK_DIM, 2, dtype=F32) / DIFF_QK_DIM)
    ang = positions.astype(F32)[..., None] * inv_freq
    cos, sin = jnp.cos(ang), jnp.sin(ang)
    cos_t = jnp.concatenate([cos, cos, cos, cos], axis=-1).reshape(m, LANES)
    sin_t = jnp.concatenate([-sin, sin, -sin, sin], axis=-1).reshape(m, LANES)

    x2d = x.reshape(m, d)
    for i in range(depth):
        j = i // 2
        g_mix = norm_mix[i].reshape(1, d)
        if i % 2 == 0:
            lambda_init = 0.8 - 0.6 * math.exp(-0.3 * i)
            w = w_in_even[j]
            wm = jnp.concatenate([w[:, :gdn_w], w[:, gdn_w + 2 * nh:]], axis=1).astype(BF16)
            wg = _pad_lanes(w[:, gdn_w:gdn_w + 2 * nh]).astype(BF16)
            alog_row = _pad_lanes(jnp.concatenate([jnp.zeros((nh,), F32), a_log[j]]).reshape(1, 2 * nh))
            dt_row = _pad_lanes(jnp.concatenate([jnp.zeros((nh,), F32), dt_bias[j]]).reshape(1, 2 * nh))
            qkv, z, qkb, vbt, gates = _even_in(x2d, g_mix, wm, wg, conv_w[j], alog_row, dt_row, cos_t, sin_t, t)
            qkv, z, qkb, gates = (a.reshape(b, t, -1) for a in (qkv, z, qkb, gates))
            u, wy, qd, kd, qk = _gdn_prep(qkv, gates)
            o_a = _gdn_scan(u, wy, qd, kd, qk, gates, z, gdn_norm[j].reshape(1, HEAD_LANES))
            lam_params = jnp.stack([lam_q1[j], lam_k1[j], lam_q2[j], lam_k2[j]])
            o_b = _diff_attention(qkb, vbt, lam_params, diff_norm[j].reshape(1, HEAD_LANES), lambda_init)
            mixes = [o_a.reshape(m, -1), o_b.reshape(m, -1)]
            wout = w_out_even[j].astype(BF16)
        else:
            w = w_in_odd[j]
            d_mix = (w.shape[1] - FOX_HEADS) // 4
            wm = w[:, :4 * d_mix].astype(BF16)
            wf = _pad_lanes(w[:, 4 * d_mix:]).astype(BF16)
            bf_row = _pad_lanes(b_forget[j].reshape(1, FOX_HEADS))
            q, k, vt, gate, qb, kb = _odd_in(x2d, g_mix, wm, wf, bf_row, t)
            q, k, gate, qb, kb = (a.reshape(b, t, -1) for a in (q, k, gate, qb, kb))
            o = _fox_attention(q, k, vt, gate, qb, kb)
            mixes = [o.reshape(m, -1)]
            wout = w_out_odd[j].astype(BF16)
        x2d = _post(x2d, mixes, wout, norm_mlp[i].reshape(1, d), w_mlp_up[i].astype(BF16),
                    w_mlp_down[i].astype(BF16), p[i].reshape(m, -1), w_ple_proj[i].astype(BF16),
                    w_ple_gate[i].astype(BF16), norm_final.reshape(1, d) if i == depth - 1 else None)
    return x2d.reshape(b, t, d)
```

```python
import functools
import math

import jax
import jax.numpy as jnp
import numpy as np
from jax import lax
from jax.experimental import pallas as pl
from jax.experimental.pallas import tpu as pltpu

F32 = jnp.float32
BF16 = jnp.bfloat16

GDN_HEADS = 4
GDN_HEAD_DIM = 128
GDN_CHUNK = 64
CONV_WIDTH = 4
DIFF_HEADS = 4
DIFF_QK_DIM = 64
FOX_HEADS = 8
HEAD_LANES = 128
ROPE_THETA = 10000.0
EPS = 1e-6
NEG_INF = -1e30
LOG2E = 1.4426950408889634
LANES = 128
SUBLANES = 8
VMEM_LIMIT_BYTES = 56 * 1024 * 1024

TOKEN_TILE = 1024
EVEN_TOKEN_TILE = 512
PROJ_SEG = 512
FF_SEG = 1024
GDN_REFINE = True
GDN_SCAN_TILE = 256
GDN_SCAN_BATCH = 2
ATTN_TILE = 512
ATTN_UNROLL = 2
BIAS_LANES_PER_HEAD = 16
ONES_ROWS = 16


def _dot(a, b):
    return jnp.dot(a, b, preferred_element_type=F32)


def _dot_exact(a, b):
    return jnp.dot(a, b, preferred_element_type=F32, precision=lax.Precision.HIGHEST)


def _dot_nt(a, b):
    return lax.dot_general(a, b, (((1,), (1,)), ((), ())), preferred_element_type=F32)


def _dot_tn(a, b):
    return lax.dot_general(a, b, (((0,), (0,)), ((), ())), preferred_element_type=F32)


def _rms(x, g):
    return x * lax.rsqrt(jnp.mean(x * x, axis=-1, keepdims=True) + EPS) * g


def _sigmoid(x):
    return 1.0 / (1.0 + jnp.exp(-x))


def _softplus(x):
    return jnp.maximum(x, 0.0) + jnp.log1p(jnp.exp(-jnp.abs(x)))


def _row_scan(x, period):
    rows = lax.broadcasted_iota(jnp.int32, x.shape, 0) % period
    s = 1
    while s < period:
        x = x + jnp.where(rows >= s, pltpu.roll(x, s, 0), 0.0)
        s *= 2
    return x


def _const_spec(shape):
    return pl.BlockSpec(shape, lambda *_: (0,) * len(shape))


def _params(sem):
    return pltpu.CompilerParams(dimension_semantics=sem, vmem_limit_bytes=VMEM_LIMIT_BYTES)


def _even_in_kernel(x_ref, g_ref, wm_ref, wg_ref, conv_ref, alog_ref, dt_ref, cos_ref, sin_ref,
                    qkv_ref, z_ref, qkb_ref, vbt_ref, gates_ref, h_ref, carry_ref, tr_ref, pad_ref, *, tiles_per_seq):
    tm = x_ref.shape[0]
    i = pl.program_id(0)
    h_ref[...] = _rms(x_ref[...], g_ref[...]).astype(BF16)
    seq_start = (i % tiles_per_seq) == 0
    seg = lambda s: slice(s * PROJ_SEG, (s + 1) * PROJ_SEG)
    project = lambda s: _dot(h_ref[...], wm_ref[:, seg(s)])

    def gdn_qkv(s, y):
        cols = seg(s)
        pad_ref[0:SUBLANES, :] = jnp.where(seq_start, 0.0, carry_ref[:, cols])
        pad_ref[SUBLANES:, :] = y
        carry_ref[:, cols] = y[tm - SUBLANES:, :]
        w = conv_ref[:, cols]
        a = y * w[CONV_WIDTH - 1:CONV_WIDTH, :]
        for k in range(1, CONV_WIDTH):
            a = a + pad_ref[SUBLANES - k:SUBLANES - k + tm, :] * w[CONV_WIDTH - 1 - k:CONV_WIDTH - k, :]
        a = a * _sigmoid(a)
        if s < 2:
            outs = []
            for hd in range(GDN_HEADS):
                blk = a[:, hd * HEAD_LANES:(hd + 1) * HEAD_LANES]
                n = blk * lax.rsqrt(jnp.sum(blk * blk, axis=-1, keepdims=True) + EPS)
                outs.append(n * (GDN_HEAD_DIM ** -0.5) if s == 0 else n)
            a = jnp.concatenate(outs, axis=1)
        qkv_ref[:, cols] = a

    def gdn_gate(s, y):
        z_ref[...] = y.astype(BF16)

    def diff_qk(s, y):
        cos = jnp.concatenate([cos_ref[...]] * (PROJ_SEG // LANES), axis=1)
        sin = jnp.concatenate([sin_ref[...]] * (PROJ_SEG // LANES), axis=1)
        lane = lax.broadcasted_iota(jnp.int32, (tm, PROJ_SEG), 1)
        first_half = (lane % DIFF_QK_DIM) < (DIFF_QK_DIM // 2)
        swapped = jnp.where(first_half, pltpu.roll(y, PROJ_SEG - DIFF_QK_DIM // 2, 1),
                            pltpu.roll(y, DIFF_QK_DIM // 2, 1))
        scale = DIFF_QK_DIM ** -0.5 * LOG2E if s == 4 else 1.0
        qkb_ref[:, seg(s - 4)] = ((y * cos + swapped * sin) * scale).astype(BF16)

    def diff_v(s, y):
        tr_ref[...] = y
        vbt_ref[0] = tr_ref[...].T.astype(BF16)

    stages = ((0, gdn_qkv), (3, gdn_gate), (1, gdn_qkv), (6, diff_v), (2, gdn_qkv), (4, diff_qk), (5, diff_qk))
    pending = project(stages[0][0])
    for n, (s, epilogue) in enumerate(stages):
        upcoming = project(stages[n + 1][0]) if n + 1 < len(stages) else _dot(h_ref[...], wg_ref[...])
        epilogue(s, pending)
        pending = upcoming

    graw = pending
    beta = _sigmoid(graw)
    g = -jnp.exp(alog_ref[...]) * _softplus(graw + dt_ref[...])
    gc = _row_scan(g, GDN_CHUNK)
    lane_g = lax.broadcasted_iota(jnp.int32, (tm, LANES), 1)
    gates_ref[...] = jnp.where(lane_g < GDN_HEADS, beta, gc)


def _even_in(x2d, g, wm, wg, conv_w, alog_row, dt_row, cos_t, sin_t, seq_len):
    m, d = x2d.shape
    tm = EVEN_TOKEN_TILE
    n_main = wm.shape[1]
    tps = seq_len // tm
    kern = functools.partial(_even_in_kernel, tiles_per_seq=tps)
    row = lambda i: (i, 0)
    return pl.pallas_call(
        kern,
        grid=(m // tm,),
        in_specs=[
            pl.BlockSpec((tm, d), row),
            _const_spec((1, d)),
            _const_spec((d, n_main)),
            _const_spec((d, LANES)),
            _const_spec(conv_w.shape),
            _const_spec((1, LANES)),
            _const_spec((1, LANES)),
            pl.BlockSpec((tm, LANES), row),
            pl.BlockSpec((tm, LANES), row),
        ],
        out_specs=[
            pl.BlockSpec((tm, 3 * PROJ_SEG), row),
            pl.BlockSpec((tm, PROJ_SEG), row),
            pl.BlockSpec((tm, 2 * PROJ_SEG), row),
            pl.BlockSpec((1, PROJ_SEG, tm), lambda i: (i // tps, 0, i % tps)),
            pl.BlockSpec((tm, LANES), row),
        ],
        out_shape=[
            jax.ShapeDtypeStruct((m, 3 * PROJ_SEG), F32),
            jax.ShapeDtypeStruct((m, PROJ_SEG), BF16),
            jax.ShapeDtypeStruct((m, 2 * PROJ_SEG), BF16),
            jax.ShapeDtypeStruct((m // seq_len, PROJ_SEG, seq_len), BF16),
            jax.ShapeDtypeStruct((m, LANES), F32),
        ],
        scratch_shapes=[pltpu.VMEM((tm, d), BF16), pltpu.VMEM((SUBLANES, 3 * PROJ_SEG), F32), pltpu.VMEM((tm, PROJ_SEG), F32),
                        pltpu.VMEM((tm + SUBLANES, PROJ_SEG), F32)],
        compiler_params=_params(("arbitrary",)),
        name="even_in_proj",
    )(x2d, g, wm, wg, conv_w, alog_row, dt_row, cos_t, sin_t)


def _odd_in_kernel(x_ref, g_ref, wm_ref, wf_ref, bf_ref, sel_ref, ones_ref, q_ref, k_ref, vt_ref, gate_ref, qb_ref, kb_ref,
                   h_ref, carry_ref, tr_ref, *, tiles_per_seq, d_mix):
    tm = x_ref.shape[0]
    i = pl.program_id(0)
    h_ref[...] = _rms(x_ref[...], g_ref[...]).astype(BF16)
    head_dim = d_mix // FOX_HEADS
    for o_ref, base, scale in ((q_ref, 0, head_dim ** -0.5 * LOG2E), (k_ref, d_mix, 1.0),
                               (vt_ref, 2 * d_mix, 1.0), (gate_ref, 3 * d_mix, 1.0)):
        for s in range(d_mix // PROJ_SEG):
            cols = slice(s * PROJ_SEG, (s + 1) * PROJ_SEG)
            y = _dot(h_ref[...], wm_ref[:, base + s * PROJ_SEG:base + (s + 1) * PROJ_SEG])
            if o_ref is vt_ref:
                tr_ref[...] = y
                o_ref[0, cols, :] = tr_ref[...].T.astype(BF16)
            else:
                o_ref[:, cols] = (y * scale).astype(BF16)
    f = _dot(h_ref[...], wf_ref[...]) + bf_ref[...]
    log_f = jnp.minimum(f, 0.0) - jnp.log1p(jnp.exp(-jnp.abs(f)))
    prev = jnp.where((i % tiles_per_seq) == 0, 0.0, carry_ref[0:1, :])
    cum = _row_scan(log_f, tm) + prev
    carry_ref[...] = jnp.broadcast_to(cum[tm - 1:tm, :], carry_ref.shape)
    pieces = jnp.concatenate(_split_bf16(LOG2E * cum, 3), axis=1)
    lanes = _dot(pieces, sel_ref[...]) + ones_ref[...]
    qb_ref[...] = lanes[:, :LANES].astype(BF16)
    kb_ref[...] = lanes[:, LANES:].astype(BF16)


def _bias_lane_tables():
    sel = np.zeros((3 * LANES, 2 * LANES), np.float32)
    ones = np.zeros((1, 2 * LANES), np.float32)
    for h in range(FOX_HEADS):
        base = BIAS_LANES_PER_HEAD * h
        for piece in range(3):
            sel[LANES * piece + h, base + 3 + piece] = 1.0
            sel[LANES * piece + h, LANES + base + piece] = -1.0
            ones[0, base + piece] = 1.0
            ones[0, LANES + base + 3 + piece] = 1.0
    return jnp.asarray(sel, BF16), jnp.asarray(ones, F32)


def _odd_in(x2d, g, wm, wf, bf_row, seq_len):
    m, d = x2d.shape
    sel, ones_row = _bias_lane_tables()
    tm = TOKEN_TILE
    d_mix = wm.shape[1] // 4
    tps = seq_len // tm
    kern = functools.partial(_odd_in_kernel, tiles_per_seq=tps, d_mix=d_mix)
    row = lambda i: (i, 0)
    row_blk = pl.BlockSpec((tm, d_mix), row)
    row_shape = jax.ShapeDtypeStruct((m, d_mix), BF16)
    return pl.pallas_call(
        kern,
        grid=(m // tm,),
        in_specs=[
            pl.BlockSpec((tm, d), row),
            _const_spec((1, d)),
            _const_spec(wm.shape),
            _const_spec((d, LANES)),
            _const_spec((1, LANES)),
            _const_spec(sel.shape),
            _const_spec(ones_row.shape),
        ],
        out_specs=[row_blk, row_blk, pl.BlockSpec((1, d_mix, tm), lambda i: (i // tps, 0, i % tps)), row_blk,
                   pl.BlockSpec((tm, LANES), row), pl.BlockSpec((tm, LANES), row)],
        out_shape=[row_shape, row_shape, jax.ShapeDtypeStruct((m // seq_len, d_mix, seq_len), BF16), row_shape,
                   jax.ShapeDtypeStruct((m, LANES), BF16), jax.ShapeDtypeStruct((m, LANES), BF16)],
        scratch_shapes=[pltpu.VMEM((tm, d), BF16), pltpu.VMEM((SUBLANES, LANES), F32), pltpu.VMEM((tm, PROJ_SEG), F32)],
        compiler_params=_params(("arbitrary",)),
        name="odd_in_proj",
    )(x2d, g, wm, wf, bf_row, sel, ones_row)


def _attn_kernel(*refs, tq, fox, lambda_init):
    if fox:
        q_ref, k_ref, vt_ref, gate_ref, qb_ref, kball_ref, o_ref, st_ref, m_ref, acc_ref, kb_ref = refs
    else:
        q_ref, k_ref, vt_ref, lam_ref, nw_ref, o_ref, st_ref, m_ref, acc_ref = refs
    tk = tq
    hg = pl.program_id(1)
    n_tiles = q_ref.shape[1] // tq
    head = lambda g: slice(g * HEAD_LANES, (g + 1) * HEAD_LANES)
    kv = [head(0), head(1)] if fox else [head(0), head(0)]
    tile_rows = lambda tile: pl.ds(pl.multiple_of(tile * tq, tq), tq)

    def queries(g, tile):
        if fox:
            return jnp.concatenate([q_ref[0, tile_rows(tile), head(g)], qb_ref[0, tile_rows(tile), :]], axis=1)
        q = q_ref[0, tile_rows(tile), :]
        lane = lax.broadcasted_iota(jnp.int32, q.shape, 1)
        keep = (lane < DIFF_QK_DIM) if g == 0 else (lane >= DIFF_QK_DIM)
        return jnp.where(keep, q, jnp.zeros_like(q))

    def scores_of(qmat, g, j):
        k0 = pl.multiple_of(j * tk, tk)
        kj = k_ref[0, pl.ds(k0, tk), kv[g]]
        if fox:
            kj = jnp.concatenate([kj, kb_ref[pl.ds(k0, tk), kv[g]]], axis=1)
        return _dot_nt(kj, qmat)

    if fox:
        kb_all = kball_ref[0]
        owner = lax.broadcasted_iota(jnp.int32, kb_all.shape, 1) // BIAS_LANES_PER_HEAD
        for g in range(2):
            kb_ref[:, head(g)] = jnp.where(owner == hg * 2 + g, kb_all, jnp.zeros_like(kb_all))
    else:
        lam_p = lam_ref[...]
        lam = (jnp.exp(jnp.sum(lam_p[0:1] * lam_p[1:2], axis=1, keepdims=True))
               - jnp.exp(jnp.sum(lam_p[2:3] * lam_p[3:4], axis=1, keepdims=True)) + lambda_init)
    st_ref[...] = scores_of(queries(0, 0), 0, 0)
    ones = jnp.ones((ONES_ROWS, tk), BF16)

    def tile(qi, carry):
        qs = [queries(g, qi) for g in range(2)]
        scores = lambda g, j: scores_of(qs[g], g, j)

        def absorb(g, j, st, masked):
            m = m_ref[g]
            k0 = pl.multiple_of(j * tk, tk)
            if masked:
                kpos = k0 + lax.broadcasted_iota(jnp.int32, (tk, tq), 0)
                qpos = qi * tq + lax.broadcasted_iota(jnp.int32, (tk, tq), 1)
                st = jnp.where(qpos >= kpos, st, NEG_INF)
            m_new = jnp.maximum(m, jnp.max(st, axis=0, keepdims=True))
            p = jnp.exp2(st - m_new).astype(BF16)
            vt = jnp.concatenate([vt_ref[0, kv[g], pl.ds(k0, tk)], ones], axis=0)
            acc_ref[g] = jnp.exp2(m - m_new) * acc_ref[g] + _dot(vt, p)
            m_ref[g] = m_new

        def step(j):
            st1 = scores(1, j)
            absorb(0, j, st_ref[...], False)
            st_ref[...] = scores(0, j + 1)
            absorb(1, j, st1, False)

        def steps(j, c):
            for u in range(ATTN_UNROLL):
                step(j * ATTN_UNROLL + u)
            return c

        def tail_step(j, c):
            step(j)
            return c

        m_ref[...] = jnp.full(m_ref.shape, NEG_INF, F32)
        acc_ref[...] = jnp.zeros(acc_ref.shape, F32)
        n_full = qi // ATTN_UNROLL
        lax.fori_loop(0, n_full, steps, 0)
        lax.fori_loop(n_full * ATTN_UNROLL, qi, tail_step, 0)
        st1 = scores(1, qi)
        absorb(0, qi, st_ref[...], True)
        absorb(1, qi, st1, True)
        st_ref[...] = scores_of(queries(0, jnp.minimum(qi + 1, n_tiles - 1)), 0, 0)
        outs = [acc_ref[g, :HEAD_LANES, :] / acc_ref[g, HEAD_LANES:HEAD_LANES + 1, :] for g in range(2)]
        rows = tile_rows(qi)
        if fox:
            for g in range(2):
                gate = _sigmoid(gate_ref[0, rows, head(g)].astype(F32))
                o_ref[0, rows, head(g)] = (outs[g].T * gate).astype(o_ref.dtype)
        else:
            o = (outs[0] - lam * outs[1]).T
            o_ref[0, rows, :] = (_rms(o, nw_ref[...]) * (1.0 - lambda_init)).astype(o_ref.dtype)
        return carry

    lax.fori_loop(0, n_tiles, tile, 0)


def _attn_state(tq):
    return [pltpu.VMEM((tq, tq), F32), pltpu.VMEM((2, 1, tq), F32), pltpu.VMEM((2, HEAD_LANES + ONES_ROWS, tq), F32)]


def _fox_attention(q, k, vt, gate, qb, kb, *, tq=ATTN_TILE):
    b, t, dm = q.shape
    width = 2 * HEAD_LANES
    kern = functools.partial(_attn_kernel, tq=tq, fox=True, lambda_init=0.0)
    seq = pl.BlockSpec((1, t, width), lambda bi, h: (bi, 0, h))
    seq_bias = pl.BlockSpec((1, t, LANES), lambda bi, h: (bi, 0, 0))
    return pl.pallas_call(
        kern,
        grid=(b, dm // width),
        in_specs=[seq, seq, pl.BlockSpec((1, width, t), lambda bi, h: (bi, h, 0)), seq, seq_bias, seq_bias],
        out_specs=seq,
        out_shape=jax.ShapeDtypeStruct((b, t, dm), BF16),
        scratch_shapes=_attn_state(tq) + [pltpu.VMEM((t, width), BF16)],
        compiler_params=_params(("arbitrary", "arbitrary")),
        name="fox_attention",
    )(q, k, vt, gate, qb, kb)


def _diff_attention(qk, vt, lam_params, norm_w, lambda_init, *, tq=ATTN_TILE):
    b, t, _ = qk.shape
    nh = DIFF_HEADS
    kern = functools.partial(_attn_kernel, tq=tq, fox=False, lambda_init=lambda_init)
    seq = pl.BlockSpec((1, t, HEAD_LANES), lambda bi, h: (bi, 0, h))
    return pl.pallas_call(
        kern,
        grid=(b, nh),
        in_specs=[seq,
                  pl.BlockSpec((1, t, HEAD_LANES), lambda bi, h: (bi, 0, nh + h)),
                  pl.BlockSpec((1, HEAD_LANES, t), lambda bi, h: (bi, h, 0)),
                  _const_spec(lam_params.shape), _const_spec((1, HEAD_LANES))],
        out_specs=seq,
        out_shape=jax.ShapeDtypeStruct((b, t, nh * HEAD_LANES), BF16),
        scratch_shapes=_attn_state(tq),
        compiler_params=_params(("arbitrary", "arbitrary")),
        name="diff_attention",
    )(qk, qk, vt, lam_params, norm_w)


def _split_bf16(x, parts):
    out = []
    for _ in range(parts):
        piece = x.astype(BF16)
        out.append(piece)
        x = x - piece.astype(F32)
    return out


def _dot_split(a, b):
    a_hi, a_lo = _split_bf16(a, 2)
    b_hi, b_lo = _split_bf16(b, 2)
    return _dot(a_hi, b_hi) + (_dot(a_hi, b_lo) + _dot(a_lo, b_hi))


def _gdn_prep_kernel(q_ref, k_ref, v_ref, gates_ref, u_ref, w_ref, qd_ref, kd_ref, qk_ref, *, refine):
    c = GDN_CHUNK
    tt = q_ref.shape[1]
    heads = range(GDN_HEADS)
    gt = gates_ref[0]
    lane = lax.broadcasted_iota(jnp.int32, (tt, LANES), 1)
    ri = lax.broadcasted_iota(jnp.int32, (tt, tt), 0)
    ci = lax.broadcasted_iota(jnp.int32, (tt, tt), 1)
    chunk_start = ri - ri % c
    incl = lambda a: jnp.where(ci <= ri, jnp.where(ci >= chunk_start, a, 0.0), 0.0)
    strict = lambda a: jnp.where(ci < ri, jnp.where(ci >= chunk_start, a, 0.0), 0.0)
    ident = jnp.where(ri == ci, 1.0, 0.0)
    ones = jnp.ones((tt, LANES), BF16)
    cols = [slice(hd * HEAD_LANES, (hd + 1) * HEAD_LANES) for hd in heads]
    kt = [k_ref[0, :, cols[hd]] for hd in heads]
    beta = [gt[:, hd:hd + 1] for hd in heads]
    gcc = [gt[:, GDN_HEADS + hd:GDN_HEADS + hd + 1] for hd in heads]
    k16 = [kt[hd].astype(BF16) for hd in heads]
    kb = [kt[hd] * beta[hd] for hd in heads]

    gc_row = []
    for hd in heads:
        g_hi, g_mid, g_lo = (piece.astype(F32) for piece in _split_bf16(gcc[hd], 3))
        pieces = jnp.where(lane == 0, g_hi, jnp.where(lane == 1, g_mid, jnp.where(lane == 2, g_lo, 0.0)))
        gc_row.append(_dot_nt(ones, pieces.astype(BF16)))
    kk = [_dot_nt(kb[hd].astype(BF16), k16[hd]) for hd in heads]
    qk_raw = [_dot_nt(q_ref[0, :, cols[hd]].astype(BF16), k16[hd]) for hd in heads]
    decay = [incl(jnp.exp(incl(gcc[hd] - gc_row[hd]))) for hd in heads]
    lower = [strict(kk[hd] * decay[hd]) for hd in heads]

    inv = [ident - lower[hd] for hd in heads]
    l16 = [lower[hd].astype(BF16) for hd in heads]
    power = [_dot(l16[hd], l16[hd]).astype(BF16) for hd in heads]
    n_sq = int(math.log2(c)) - 1
    for step in range(n_sq - 1):
        r = [_dot(jnp.concatenate([power[hd], inv[hd].astype(BF16)], axis=0), power[hd]) for hd in heads]
        power = [r[hd][:tt].astype(BF16) for hd in heads]
        inv = [inv[hd] + r[hd][tt:] for hd in heads]
    r = [_dot(inv[hd].astype(BF16), power[hd]) for hd in heads]
    inv16 = [(inv[hd] + r[hd]).astype(BF16) for hd in heads]

    eg = [jnp.exp(gcc[hd]) for hd in heads]
    rhs = [jnp.concatenate([v_ref[0, :, cols[hd]] * beta[hd], kb[hd] * eg[hd]], axis=1) for hd in heads]
    sol = [_dot(inv16[hd], rhs[hd].astype(BF16)) for hd in heads]
    if refine:
        a_hi, a_lo, s_hi, s_lo = [], [], [], []
        for hd in heads:
            hi, lo = _split_bf16(ident + lower[hd], 2)
            a_hi.append(hi)
            a_lo.append(lo)
            hi, lo = _split_bf16(sol[hd], 2)
            s_hi.append(hi)
            s_lo.append(lo)
        prod = [_dot(a_hi[hd], s_hi[hd]) + (_dot(a_hi[hd], s_lo[hd]) + _dot(a_lo[hd], s_hi[hd])) for hd in heads]
        corr = [_dot(inv16[hd], (rhs[hd] - prod[hd]).astype(BF16)) for hd in heads]
        sol = [sol[hd] + corr[hd] for hd in heads]
    for hd in heads:
        u_ref[0, :, cols[hd]] = sol[hd][:, :HEAD_LANES]
        w_ref[0, :, cols[hd]] = sol[hd][:, HEAD_LANES:].astype(BF16)
        qk = incl(qk_raw[hd] * decay[hd])
        qd_ref[0, :, cols[hd]] = (q_ref[0, :, cols[hd]] * eg[hd]).astype(BF16)
        for n in range(tt // c):
            rows = slice(n * c, (n + 1) * c)
            qk_ref[0, hd, rows, :] = qk[rows, rows].astype(BF16)
            gl = gcc[hd][(n + 1) * c - 1:(n + 1) * c, :]
            kd_ref[0, rows, cols[hd]] = (kt[hd][rows] * jnp.exp(gl - gcc[hd][rows])).astype(BF16)


def _gdn_prep(qkv, gates, *, tt=256):
    b, t, _ = qkv.shape
    nh = GDN_HEADS
    dm = nh * HEAD_LANES
    blk = lambda part: pl.BlockSpec((1, tt, dm), lambda bi, i: (bi, i, part))
    return pl.pallas_call(
        functools.partial(_gdn_prep_kernel, refine=GDN_REFINE),
        grid=(b, t // tt),
        in_specs=[blk(0), blk(1), blk(2), pl.BlockSpec((1, tt, LANES), lambda bi, i: (bi, i, 0))],
        out_specs=[blk(0)] * 4 + [pl.BlockSpec((1, nh, tt, GDN_CHUNK), lambda bi, i: (bi, 0, i, 0))],
        out_shape=[jax.ShapeDtypeStruct((b, t, dm), F32)]
        + [jax.ShapeDtypeStruct((b, t, dm), BF16)] * 3
        + [jax.ShapeDtypeStruct((b, nh, t, GDN_CHUNK), BF16)],
        compiler_params=_params(("arbitrary", "arbitrary")),
        name="gdn_prep",
    )(qkv, qkv, qkv, gates)


def _gdn_scan_kernel(u_ref, w_ref, qd_ref, kd_ref, qk_ref, gates_ref, z_ref, nw_ref, o_ref, s_ref):
    c = GDN_CHUNK
    nb, tt = u_ref.shape[0], u_ref.shape[1]

    @pl.when(pl.program_id(1) == 0)
    def _():
        s_ref[...] = jnp.zeros_like(s_ref)

    chains = [(bi, hd) for bi in range(nb) for hd in range(GDN_HEADS)]
    cols = [slice(hd * HEAD_LANES, (hd + 1) * HEAD_LANES) for hd in range(GDN_HEADS)]
    state = [s_ref[bi, hd] for bi, hd in chains]
    for n in range(tt // c):
        rows = slice(n * c, (n + 1) * c)
        r = [_dot(jnp.concatenate([w_ref[bi, rows, cols[hd]], qd_ref[bi, rows, cols[hd]]], axis=0),
                  state[i].astype(BF16)) for i, (bi, hd) in enumerate(chains)]
        v_new = [(u_ref[bi, rows, cols[hd]] - r[i][:c]).astype(BF16) for i, (bi, hd) in enumerate(chains)]
        intra = [_dot(qk_ref[bi, hd, rows, :], v_new[i]) for i, (bi, hd) in enumerate(chains)]
        upd = [_dot_tn(kd_ref[bi, rows, cols[hd]], v_new[i]) for i, (bi, hd) in enumerate(chains)]
        for i, (bi, hd) in enumerate(chains):
            last = (n + 1) * c - 1
            decay_last = jnp.exp(gates_ref[bi, last:last + 1, GDN_HEADS + hd:GDN_HEADS + hd + 1])
            state[i] = state[i] * decay_last + upd[i]
            zt = z_ref[bi, rows, cols[hd]].astype(F32)
            o = r[i][c:] + intra[i]
            o_ref[bi, rows, cols[hd]] = (_rms(o, nw_ref[...]) * (zt * _sigmoid(zt))).astype(o_ref.dtype)
    for i, (bi, hd) in enumerate(chains):
        s_ref[bi, hd] = state[i]


def _gdn_scan(u, w, qd, kd, qk, gates, z, norm_w, *, tt=GDN_SCAN_TILE, nb=GDN_SCAN_BATCH):
    b, t, dm = u.shape
    nh = GDN_HEADS
    assert b % nb == 0 and t % tt == 0
    blk = pl.BlockSpec((nb, tt, dm), lambda bi, i: (bi, i, 0))
    return pl.pallas_call(
        _gdn_scan_kernel,
        grid=(b // nb, t // tt),
        in_specs=[blk, blk, blk, blk,
                  pl.BlockSpec((nb, nh, tt, GDN_CHUNK), lambda bi, i: (bi, 0, i, 0)),
                  pl.BlockSpec((nb, tt, LANES), lambda bi, i: (bi, i, 0)),
                  blk, _const_spec((1, HEAD_LANES))],
        out_specs=blk,
        out_shape=jax.ShapeDtypeStruct((b, t, dm), BF16),
        scratch_shapes=[pltpu.VMEM((nb, nh, GDN_HEAD_DIM, GDN_HEAD_DIM), F32)],
        compiler_params=_params(("arbitrary", "arbitrary")),
        name="gdn_scan",
    )(u, w, qd, kd, qk, gates, z, norm_w)


def _post_kernel(*refs, n_mix, final_norm):
    x_ref = refs[0]
    mix_refs = refs[1:1 + n_mix]
    wout_ref, g_ref, wup_ref, wdn_ref, p_ref, wpp_ref, wpg_ref = refs[1 + n_mix:8 + n_mix]
    rest = refs[8 + n_mix:]
    if final_norm:
        gf_ref, o_ref = rest
    else:
        (o_ref,) = rest
    mix = mix_refs[0][...] if n_mix == 1 else jnp.concatenate([r[...] for r in mix_refs], axis=1)
    x = x_ref[...] + _dot(mix, wout_ref[...])
    h = _rms(x, g_ref[...]).astype(BF16)
    d_ff = wup_ref.shape[1]
    acc = x
    for s in range(d_ff // FF_SEG):
        a = jnp.maximum(_dot(h, wup_ref[:, s * FF_SEG:(s + 1) * FF_SEG]), 0.0)
        acc = acc + _dot((a * a).astype(BF16), wdn_ref[s * FF_SEG:(s + 1) * FF_SEG, :])
    x = acc
    gate = _sigmoid(_dot(x.astype(BF16), wpg_ref[...]))
    x = x + _dot(p_ref[...].astype(BF16), wpp_ref[...]) * gate
    if final_norm:
        x = _rms(x, gf_ref[...])
    o_ref[...] = x


def _post(x2d, mixes, wout, g, wup, wdn, p2d, wpp, wpg, gf=None):
    m, d = x2d.shape
    tm = TOKEN_TILE
    row = lambda i: (i, 0)
    single = pl.Buffered(1)
    const = lambda a: pl.BlockSpec(a.shape, lambda i: (0, 0), pipeline_mode=single)
    args = [x2d, *mixes, wout, g, wup, wdn, p2d, wpp, wpg]
    in_specs = ([pl.BlockSpec((tm, d), row)]
                + [pl.BlockSpec((tm, a.shape[1]), row) for a in mixes]
                + [const(wout), const(g), const(wup), const(wdn), pl.BlockSpec((tm, p2d.shape[1]), row), const(wpp), const(wpg)])
    if gf is not None:
        args.append(gf)
        in_specs.append(const(gf))
    kern = functools.partial(_post_kernel, n_mix=len(mixes), final_norm=gf is not None)
    return pl.pallas_call(
        kern,
        grid=(m // tm,),
        in_specs=in_specs,
        out_specs=pl.BlockSpec((tm, d), row),
        out_shape=jax.ShapeDtypeStruct((m, d), F32),
        compiler_params=_params(("arbitrary",)),
        name="out_proj_mlp_ple",
    )(*args)


def _pad_lanes(a):
    return jnp.pad(a, ((0, 0), (0, LANES - a.shape[1])))


def kernel(x, p, positions, norm_mix, norm_mlp, norm_final, w_in_even, conv_w, a_log, dt_bias, gdn_norm,
           lam_q1, lam_k1, lam_q2, lam_k2, diff_norm, w_out_even, w_in_odd, b_forget, w_out_odd,
           w_mlp_up, w_mlp_down, w_ple_proj, w_ple_gate):
    b, t, d = x.shape
    depth = p.shape[0]
    m = b * t
    assert t % TOKEN_TILE == 0 and d % PROJ_SEG == 0
    nh = GDN_HEADS
    gdn_w = 3 * nh * GDN_HEAD_DIM + nh * GDN_HEAD_DIM
    assert w_in_even.shape[2] == gdn_w + 2 * nh + 3 * DIFF_HEADS * 2 * DIFF_QK_DIM

    inv_freq = ROPE_THETA ** (-jnp.arange(0, DIFF_QK_DIM, 2, dtype=F32) / DIFF_QK_DIM)
    ang = positions.astype(F32)[..., None] * inv_freq
    cos, sin = jnp.cos(ang), jnp.sin(ang)
    cos_t = jnp.concatenate([cos, cos, cos, cos], axis=-1).reshape(m, LANES)
    sin_t = jnp.concatenate([-sin, sin, -sin, sin], axis=-1).reshape(m, LANES)

    x2d = x.reshape(m, d)
    for i in range(depth):
        j = i // 2
        g_mix = norm_mix[i].reshape(1, d)
        if i % 2 == 0:
            lambda_init = 0.8 - 0.6 * math.exp(-0.3 * i)
            w = w_in_even[j]
            wm = jnp.concatenate([w[:, :gdn_w], w[:, gdn_w + 2 * nh:]], axis=1).astype(BF16)
            wg = _pad_lanes(w[:, gdn_w:gdn_w + 2 * nh]).astype(BF16)
            alog_row = _pad_lanes(jnp.concatenate([jnp.zeros((nh,), F32), a_log[j]]).reshape(1, 2 * nh))
            dt_row = _pad_lanes(jnp.concatenate([jnp.zeros((nh,), F32), dt_bias[j]]).reshape(1, 2 * nh))
            qkv, z, qkb, vbt, gates = _even_in(x2d, g_mix, wm, wg, conv_w[j], alog_row, dt_row, cos_t, sin_t, t)
            qkv, z, qkb, gates = (a.reshape(b, t, -1) for a in (qkv, z, qkb, gates))
            u, wy, qd, kd, qk = _gdn_prep(qkv, gates)
            o_a = _gdn_scan(u, wy, qd, kd, qk, gates, z, gdn_norm[j].reshape(1, HEAD_LANES))
            lam_params = jnp.stack([lam_q1[j], lam_k1[j], lam_q2[j], lam_k2[j]])
            o_b = _diff_attention(qkb, vbt, lam_params, diff_norm[j].reshape(1, HEAD_LANES), lambda_init)
            mixes = [o_a.reshape(m, -1), o_b.reshape(m, -1)]
            wout = w_out_even[j].astype(BF16)
        else:
            w = w_in_odd[j]
            d_mix = (w.shape[1] - FOX_HEADS) // 4
            wm = w[:, :4 * d_mix].astype(BF16)
            wf = _pad_lanes(w[:, 4 * d_mix:]).astype(BF16)
            bf_row = _pad_lanes(b_forget[j].reshape(1, FOX_HEADS))
            q, k, vt, gate, qb, kb = _odd_in(x2d, g_mix, wm, wf, bf_row, t)
            q, k, gate, qb, kb = (a.reshape(b, t, -1) for a in (q, k, gate, qb, kb))
            o = _fox_attention(q, k, vt, gate, qb, kb)
            mixes = [o.reshape(m, -1)]
            wout = w_out_odd[j].astype(BF16)
        x2d = _post(x2d, mixes, wout, norm_mlp[i].reshape(1, d), w_mlp_up[i].astype(BF16),
                    w_mlp_down[i].astype(BF16), p[i].reshape(m, -1), w_ple_proj[i].astype(BF16),
                    w_ple_gate[i].astype(BF16), norm_final.reshape(1, d) if i == depth - 1 else None)
    return x2d.reshape(b, t, d)
```

```python
import functools
import math

import jax
import jax.numpy as jnp
import numpy as np
from jax import lax
from jax.experimental import pallas as pl
from jax.experimental.pallas import tpu as pltpu

F32 = jnp.float32
BF16 = jnp.bfloat16

GDN_HEADS = 4
GDN_HEAD_DIM = 128
GDN_CHUNK = 64
CONV_WIDTH = 4
DIFF_HEADS = 4
DIFF_QK_DIM = 64
FOX_HEADS = 8
HEAD_LANES = 128
ROPE_THETA = 10000.0
EPS = 1e-6
NEG_INF = -1e30
LOG2E = 1.4426950408889634
LANES = 128
SUBLANES = 8
VMEM_LIMIT_BYTES = 56 * 1024 * 1024

TOKEN_TILE = 1024
EVEN_TOKEN_TILE = 512
PROJ_SEG = 512
FF_SEG = 1024
GDN_SCAN_TILE = 256
GDN_SCAN_BATCH = 4
ATTN_TILE = 512
ATTN_UNROLL = 2
BIAS_LANES_PER_HEAD = 16
ONES_ROWS = 16


def _dot(a, b):
    return jnp.dot(a, b, preferred_element_type=F32)


def _dot_exact(a, b):
    return jnp.dot(a, b, preferred_element_type=F32, precision=lax.Precision.HIGHEST)


def _dot_nt(a, b):
    return lax.dot_general(a, b, (((1,), (1,)), ((), ())), preferred_element_type=F32)


def _dot_tn(a, b):
    return lax.dot_general(a, b, (((0,), (0,)), ((), ())), preferred_element_type=F32)


def _rms(x, g):
    return x * lax.rsqrt(jnp.mean(x * x, axis=-1, keepdims=True) + EPS) * g


def _sigmoid(x):
    return 1.0 / (1.0 + jnp.exp(-x))


def _softplus(x):
    return jnp.maximum(x, 0.0) + jnp.log1p(jnp.exp(-jnp.abs(x)))


def _row_scan(x, period):
    rows = lax.broadcasted_iota(jnp.int32, x.shape, 0) % period
    s = 1
    while s < period:
        x = x + jnp.where(rows >= s, pltpu.roll(x, s, 0), 0.0)
        s *= 2
    return x


def _const_spec(shape):
    return pl.BlockSpec(shape, lambda *_: (0,) * len(shape))


def _params(sem):
    return pltpu.CompilerParams(dimension_semantics=sem, vmem_limit_bytes=VMEM_LIMIT_BYTES)


def _even_in_kernel(x_ref, g_ref, wm_ref, wg_ref, conv_ref, alog_ref, dt_ref, cos_ref, sin_ref,
                    qkv_ref, z_ref, qkb_ref, vbt_ref, gates_ref, h_ref, carry_ref, tr_ref, pad_ref, *, tiles_per_seq):
    tm = x_ref.shape[0]
    i = pl.program_id(0)
    h_ref[...] = _rms(x_ref[...], g_ref[...]).astype(BF16)
    seq_start = (i % tiles_per_seq) == 0
    seg = lambda s: slice(s * PROJ_SEG, (s + 1) * PROJ_SEG)
    project = lambda s: _dot(h_ref[...], wm_ref[:, seg(s)])

    def gdn_qkv(s, y):
        cols = seg(s)
        pad_ref[0:SUBLANES, :] = jnp.where(seq_start, 0.0, carry_ref[:, cols])
        pad_ref[SUBLANES:, :] = y
        carry_ref[:, cols] = y[tm - SUBLANES:, :]
        w = conv_ref[:, cols]
        a = y * w[CONV_WIDTH - 1:CONV_WIDTH, :]
        for k in range(1, CONV_WIDTH):
            a = a + pad_ref[SUBLANES - k:SUBLANES - k + tm, :] * w[CONV_WIDTH - 1 - k:CONV_WIDTH - k, :]
        a = a * _sigmoid(a)
        if s < 2:
            outs = []
            for hd in range(GDN_HEADS):
                blk = a[:, hd * HEAD_LANES:(hd + 1) * HEAD_LANES]
                n = blk * lax.rsqrt(jnp.sum(blk * blk, axis=-1, keepdims=True) + EPS)
                outs.append(n * (GDN_HEAD_DIM ** -0.5) if s == 0 else n)
            a = jnp.concatenate(outs, axis=1)
        qkv_ref[:, cols] = a

    def gdn_gate(s, y):
        z_ref[...] = y.astype(BF16)

    def diff_qk(s, y):
        cos = jnp.concatenate([cos_ref[...]] * (PROJ_SEG // LANES), axis=1)
        sin = jnp.concatenate([sin_ref[...]] * (PROJ_SEG // LANES), axis=1)
        lane = lax.broadcasted_iota(jnp.int32, (tm, PROJ_SEG), 1)
        first_half = (lane % DIFF_QK_DIM) < (DIFF_QK_DIM // 2)
        swapped = jnp.where(first_half, pltpu.roll(y, PROJ_SEG - DIFF_QK_DIM // 2, 1),
                            pltpu.roll(y, DIFF_QK_DIM // 2, 1))
        scale = DIFF_QK_DIM ** -0.5 * LOG2E if s == 4 else 1.0
        qkb_ref[:, seg(s - 4)] = ((y * cos + swapped * sin) * scale).astype(BF16)

    def diff_v(s, y):
        tr_ref[...] = y
        vbt_ref[0] = tr_ref[...].T.astype(BF16)

    stages = ((0, gdn_qkv), (3, gdn_gate), (1, gdn_qkv), (6, diff_v), (2, gdn_qkv), (4, diff_qk), (5, diff_qk))
    pending = project(stages[0][0])
    for n, (s, epilogue) in enumerate(stages):
        upcoming = project(stages[n + 1][0]) if n + 1 < len(stages) else _dot(h_ref[...], wg_ref[...])
        epilogue(s, pending)
        pending = upcoming

    graw = pending
    beta = _sigmoid(graw)
    g = -jnp.exp(alog_ref[...]) * _softplus(graw + dt_ref[...])
    gc = _row_scan(g, GDN_CHUNK)
    lane_g = lax.broadcasted_iota(jnp.int32, (tm, LANES), 1)
    gates_ref[...] = jnp.where(lane_g < GDN_HEADS, beta, gc)


def _even_in(x2d, g, wm, wg, conv_w, alog_row, dt_row, cos_t, sin_t, seq_len):
    m, d = x2d.shape
    tm = EVEN_TOKEN_TILE
    n_main = wm.shape[1]
    tps = seq_len // tm
    kern = functools.partial(_even_in_kernel, tiles_per_seq=tps)
    row = lambda i: (i, 0)
    return pl.pallas_call(
        kern,
        grid=(m // tm,),
        in_specs=[
            pl.BlockSpec((tm, d), row),
            _const_spec((1, d)),
            _const_spec((d, n_main)),
            _const_spec((d, LANES)),
            _const_spec(conv_w.shape),
            _const_spec((1, LANES)),
            _const_spec((1, LANES)),
            pl.BlockSpec((tm, LANES), row),
            pl.BlockSpec((tm, LANES), row),
        ],
        out_specs=[
            pl.BlockSpec((tm, 3 * PROJ_SEG), row),
            pl.BlockSpec((tm, PROJ_SEG), row),
            pl.BlockSpec((tm, 2 * PROJ_SEG), row),
            pl.BlockSpec((1, PROJ_SEG, tm), lambda i: (i // tps, 0, i % tps)),
            pl.BlockSpec((tm, LANES), row),
        ],
        out_shape=[
            jax.ShapeDtypeStruct((m, 3 * PROJ_SEG), F32),
            jax.ShapeDtypeStruct((m, PROJ_SEG), BF16),
            jax.ShapeDtypeStruct((m, 2 * PROJ_SEG), BF16),
            jax.ShapeDtypeStruct((m // seq_len, PROJ_SEG, seq_len), BF16),
            jax.ShapeDtypeStruct((m, LANES), F32),
        ],
        scratch_shapes=[pltpu.VMEM((tm, d), BF16), pltpu.VMEM((SUBLANES, 3 * PROJ_SEG), F32), pltpu.VMEM((tm, PROJ_SEG), F32),
                        pltpu.VMEM((tm + SUBLANES, PROJ_SEG), F32)],
        compiler_params=_params(("arbitrary",)),
        name="even_in_proj",
    )(x2d, g, wm, wg, conv_w, alog_row, dt_row, cos_t, sin_t)


def _odd_in_kernel(x_ref, g_ref, wm_ref, wf_ref, bf_ref, sel_ref, ones_ref, q_ref, k_ref, vt_ref, gate_ref, qb_ref, kb_ref,
                   h_ref, carry_ref, tr_ref, *, tiles_per_seq, d_mix):
    tm = x_ref.shape[0]
    i = pl.program_id(0)
    h_ref[...] = _rms(x_ref[...], g_ref[...]).astype(BF16)
    head_dim = d_mix // FOX_HEADS
    for o_ref, base, scale in ((q_ref, 0, head_dim ** -0.5 * LOG2E), (k_ref, d_mix, 1.0),
                               (vt_ref, 2 * d_mix, 1.0), (gate_ref, 3 * d_mix, 1.0)):
        for s in range(d_mix // PROJ_SEG):
            cols = slice(s * PROJ_SEG, (s + 1) * PROJ_SEG)
            y = _dot(h_ref[...], wm_ref[:, base + s * PROJ_SEG:base + (s + 1) * PROJ_SEG])
            if o_ref is vt_ref:
                tr_ref[...] = y
                o_ref[0, cols, :] = tr_ref[...].T.astype(BF16)
            else:
                o_ref[:, cols] = (y * scale).astype(BF16)
    f = _dot(h_ref[...], wf_ref[...]) + bf_ref[...]
    log_f = jnp.minimum(f, 0.0) - jnp.log1p(jnp.exp(-jnp.abs(f)))
    prev = jnp.where((i % tiles_per_seq) == 0, 0.0, carry_ref[0:1, :])
    cum = _row_scan(log_f, tm) + prev
    carry_ref[...] = jnp.broadcast_to(cum[tm - 1:tm, :], carry_ref.shape)
    pieces = jnp.concatenate(_split_bf16(LOG2E * cum, 3), axis=1)
    lanes = _dot(pieces, sel_ref[...]) + ones_ref[...]
    qb_ref[...] = lanes[:, :LANES].astype(BF16)
    kb_ref[...] = lanes[:, LANES:].astype(BF16)


def _bias_lane_tables():
    sel = np.zeros((3 * LANES, 2 * LANES), np.float32)
    ones = np.zeros((1, 2 * LANES), np.float32)
    for h in range(FOX_HEADS):
        base = BIAS_LANES_PER_HEAD * h
        for piece in range(3):
            sel[LANES * piece + h, base + 3 + piece] = 1.0
            sel[LANES * piece + h, LANES + base + piece] = -1.0
            ones[0, base + piece] = 1.0
            ones[0, LANES + base + 3 + piece] = 1.0
    return jnp.asarray(sel, BF16), jnp.asarray(ones, F32)


def _odd_in(x2d, g, wm, wf, bf_row, seq_len):
    m, d = x2d.shape
    sel, ones_row = _bias_lane_tables()
    tm = TOKEN_TILE
    d_mix = wm.shape[1] // 4
    tps = seq_len // tm
    kern = functools.partial(_odd_in_kernel, tiles_per_seq=tps, d_mix=d_mix)
    row = lambda i: (i, 0)
    row_blk = pl.BlockSpec((tm, d_mix), row)
    row_shape = jax.ShapeDtypeStruct((m, d_mix), BF16)
    return pl.pallas_call(
        kern,
        grid=(m // tm,),
        in_specs=[
            pl.BlockSpec((tm, d), row),
            _const_spec((1, d)),
            _const_spec(wm.shape),
            _const_spec((d, LANES)),
            _const_spec((1, LANES)),
            _const_spec(sel.shape),
            _const_spec(ones_row.shape),
        ],
        out_specs=[row_blk, row_blk, pl.BlockSpec((1, d_mix, tm), lambda i: (i // tps, 0, i % tps)), row_blk,
                   pl.BlockSpec((tm, LANES), row), pl.BlockSpec((tm, LANES), row)],
        out_shape=[row_shape, row_shape, jax.ShapeDtypeStruct((m // seq_len, d_mix, seq_len), BF16), row_shape,
                   jax.ShapeDtypeStruct((m, LANES), BF16), jax.ShapeDtypeStruct((m, LANES), BF16)],
        scratch_shapes=[pltpu.VMEM((tm, d), BF16), pltpu.VMEM((SUBLANES, LANES), F32), pltpu.VMEM((tm, PROJ_SEG), F32)],
        compiler_params=_params(("arbitrary",)),
        name="odd_in_proj",
    )(x2d, g, wm, wf, bf_row, sel, ones_row)


def _attn_kernel(*refs, tq, fox, lambda_init):
    if fox:
        q_ref, k_ref, vt_ref, gate_ref, qb_ref, kball_ref, o_ref, st_ref, m_ref, acc_ref, kb_ref = refs
    else:
        q_ref, k_ref, vt_ref, lam_ref, nw_ref, o_ref, st_ref, m_ref, acc_ref = refs
    tk = tq
    hg = pl.program_id(1)
    n_tiles = q_ref.shape[1] // tq
    head = lambda g: slice(g * HEAD_LANES, (g + 1) * HEAD_LANES)
    kv = [head(0), head(1)] if fox else [head(0), head(0)]
    tile_rows = lambda tile: pl.ds(pl.multiple_of(tile * tq, tq), tq)

    def queries(g, tile):
        if fox:
            return jnp.concatenate([q_ref[0, tile_rows(tile), head(g)], qb_ref[0, tile_rows(tile), :]], axis=1)
        q = q_ref[0, tile_rows(tile), :]
        lane = lax.broadcasted_iota(jnp.int32, q.shape, 1)
        keep = (lane < DIFF_QK_DIM) if g == 0 else (lane >= DIFF_QK_DIM)
        return jnp.where(keep, q, jnp.zeros_like(q))

    def scores_of(qmat, g, j):
        k0 = pl.multiple_of(j * tk, tk)
        kj = k_ref[0, pl.ds(k0, tk), kv[g]]
        if fox:
            kj = jnp.concatenate([kj, kb_ref[pl.ds(k0, tk), kv[g]]], axis=1)
        return _dot_nt(kj, qmat)

    if fox:
        kb_all = kball_ref[0]
        owner = lax.broadcasted_iota(jnp.int32, kb_all.shape, 1) // BIAS_LANES_PER_HEAD
        for g in range(2):
            kb_ref[:, head(g)] = jnp.where(owner == hg * 2 + g, kb_all, jnp.zeros_like(kb_all))
    else:
        lam_p = lam_ref[...]
        lam = (jnp.exp(jnp.sum(lam_p[0:1] * lam_p[1:2], axis=1, keepdims=True))
               - jnp.exp(jnp.sum(lam_p[2:3] * lam_p[3:4], axis=1, keepdims=True)) + lambda_init)
    st_ref[...] = scores_of(queries(0, 0), 0, 0)
    ones = jnp.ones((ONES_ROWS, tk), BF16)

    def tile(qi, carry):
        qs = [queries(g, qi) for g in range(2)]
        scores = lambda g, j: scores_of(qs[g], g, j)

        def absorb(g, j, st, masked):
            m = m_ref[g]
            k0 = pl.multiple_of(j * tk, tk)
            if masked:
                kpos = k0 + lax.broadcasted_iota(jnp.int32, (tk, tq), 0)
                qpos = qi * tq + lax.broadcasted_iota(jnp.int32, (tk, tq), 1)
                st = jnp.where(qpos >= kpos, st, NEG_INF)
            m_new = jnp.maximum(m, jnp.max(st, axis=0, keepdims=True))
            p = jnp.exp2(st - m_new).astype(BF16)
            vt = jnp.concatenate([vt_ref[0, kv[g], pl.ds(k0, tk)], ones], axis=0)
            acc_ref[g] = jnp.exp2(m - m_new) * acc_ref[g] + _dot(vt, p)
            m_ref[g] = m_new

        def step(j):
            st1 = scores(1, j)
            absorb(0, j, st_ref[...], False)
            st_ref[...] = scores(0, j + 1)
            absorb(1, j, st1, False)

        def steps(j, c):
            for u in range(ATTN_UNROLL):
                step(j * ATTN_UNROLL + u)
            return c

        def tail_step(j, c):
            step(j)
            return c

        m_ref[...] = jnp.full(m_ref.shape, NEG_INF, F32)
        acc_ref[...] = jnp.zeros(acc_ref.shape, F32)
        n_full = qi // ATTN_UNROLL
        lax.fori_loop(0, n_full, steps, 0)
        lax.fori_loop(n_full * ATTN_UNROLL, qi, tail_step, 0)
        st1 = scores(1, qi)
        absorb(0, qi, st_ref[...], True)
        absorb(1, qi, st1, True)
        st_ref[...] = scores_of(queries(0, jnp.minimum(qi + 1, n_tiles - 1)), 0, 0)
        outs = [acc_ref[g, :HEAD_LANES, :] / acc_ref[g, HEAD_LANES:HEAD_LANES + 1, :] for g in range(2)]
        rows = tile_rows(qi)
        if fox:
            for g in range(2):
                gate = _sigmoid(gate_ref[0, rows, head(g)].astype(F32))
                o_ref[0, rows, head(g)] = (outs[g].T * gate).astype(o_ref.dtype)
        else:
            o = (outs[0] - lam * outs[1]).T
            o_ref[0, rows, :] = (_rms(o, nw_ref[...]) * (1.0 - lambda_init)).astype(o_ref.dtype)
        return carry

    lax.fori_loop(0, n_tiles, tile, 0)


def _attn_state(tq):
    return [pltpu.VMEM((tq, tq), F32), pltpu.VMEM((2, 1, tq), F32), pltpu.VMEM((2, HEAD_LANES + ONES_ROWS, tq), F32)]


def _fox_attention(q, k, vt, gate, qb, kb, *, tq=ATTN_TILE):
    b, t, dm = q.shape
    width = 2 * HEAD_LANES
    kern = functools.partial(_attn_kernel, tq=tq, fox=True, lambda_init=0.0)
    seq = pl.BlockSpec((1, t, width), lambda bi, h: (bi, 0, h))
    seq_bias = pl.BlockSpec((1, t, LANES), lambda bi, h: (bi, 0, 0))
    return pl.pallas_call(
        kern,
        grid=(b, dm // width),
        in_specs=[seq, seq, pl.BlockSpec((1, width, t), lambda bi, h: (bi, h, 0)), seq, seq_bias, seq_bias],
        out_specs=seq,
        out_shape=jax.ShapeDtypeStruct((b, t, dm), BF16),
        scratch_shapes=_attn_state(tq) + [pltpu.VMEM((t, width), BF16)],
        compiler_params=_params(("arbitrary", "arbitrary")),
        name="fox_attention",
    )(q, k, vt, gate, qb, kb)


def _diff_attention(qk, vt, lam_params, norm_w, lambda_init, *, tq=ATTN_TILE):
    b, t, _ = qk.shape
    nh = DIFF_HEADS
    kern = functools.partial(_attn_kernel, tq=tq, fox=False, lambda_init=lambda_init)
    seq = pl.BlockSpec((1, t, HEAD_LANES), lambda bi, h: (bi, 0, h))
    return pl.pallas_call(
        kern,
        grid=(b, nh),
        in_specs=[seq,
                  pl.BlockSpec((1, t, HEAD_LANES), lambda bi, h: (bi, 0, nh + h)),
                  pl.BlockSpec((1, HEAD_LANES, t), lambda bi, h: (bi, h, 0)),
                  _const_spec(lam_params.shape), _const_spec((1, HEAD_LANES))],
        out_specs=seq,
        out_shape=jax.ShapeDtypeStruct((b, t, nh * HEAD_LANES), BF16),
        scratch_shapes=_attn_state(tq),
        compiler_params=_params(("arbitrary", "arbitrary")),
        name="diff_attention",
    )(qk, qk, vt, lam_params, norm_w)


def _split_bf16(x, parts):
    out = []
    for _ in range(parts):
        piece = x.astype(BF16)
        out.append(piece)
        x = x - piece.astype(F32)
    return out


def _dot_split(a, b):
    a_hi, a_lo = _split_bf16(a, 2)
    b_hi, b_lo = _split_bf16(b, 2)
    return _dot(a_hi, b_hi) + (_dot(a_hi, b_lo) + _dot(a_lo, b_hi))


def _gdn_prep_kernel(q_ref, k_ref, v_ref, gates_ref, u_ref, w_ref, qd_ref, kd_ref, qk_ref):
    c = GDN_CHUNK
    tt = q_ref.shape[1]
    heads = range(GDN_HEADS)
    gt = gates_ref[0]
    lane = lax.broadcasted_iota(jnp.int32, (tt, LANES), 1)
    ri = lax.broadcasted_iota(jnp.int32, (tt, tt), 0)
    ci = lax.broadcasted_iota(jnp.int32, (tt, tt), 1)
    chunk_start = ri - ri % c
    incl = lambda a: jnp.where(ci <= ri, jnp.where(ci >= chunk_start, a, 0.0), 0.0)
    strict = lambda a: jnp.where(ci < ri, jnp.where(ci >= chunk_start, a, 0.0), 0.0)
    ident = jnp.where(ri == ci, 1.0, 0.0)
    ones = jnp.ones((tt, LANES), BF16)
    cols = [slice(hd * HEAD_LANES, (hd + 1) * HEAD_LANES) for hd in heads]
    kt = [k_ref[0, :, cols[hd]] for hd in heads]
    beta = [gt[:, hd:hd + 1] for hd in heads]
    gcc = [gt[:, GDN_HEADS + hd:GDN_HEADS + hd + 1] for hd in heads]
    k16 = [kt[hd].astype(BF16) for hd in heads]
    kb = [kt[hd] * beta[hd] for hd in heads]

    gc_row = []
    for hd in heads:
        g_hi, g_mid, g_lo = (piece.astype(F32) for piece in _split_bf16(gcc[hd], 3))
        pieces = jnp.where(lane == 0, g_hi, jnp.where(lane == 1, g_mid, jnp.where(lane == 2, g_lo, 0.0)))
        gc_row.append(_dot_nt(ones, pieces.astype(BF16)))
    kk = [_dot_nt(kb[hd].astype(BF16), k16[hd]) for hd in heads]
    qk_raw = [_dot_nt(q_ref[0, :, cols[hd]].astype(BF16), k16[hd]) for hd in heads]
    decay = [incl(jnp.exp(incl(gcc[hd] - gc_row[hd]))) for hd in heads]
    lower = [strict(kk[hd] * decay[hd]) for hd in heads]

    span = ri ^ ci
    inv = [ident - jnp.where(span == 1, lower[hd], 0.0) for hd in heads]
    s_blk = 2
    while s_blk < c:
        shift = int(math.log2(s_blk))
        inv16 = [inv[hd].astype(BF16) for hd in heads]
        coupled = [_dot(jnp.where((span >> shift) == 1, lower[hd], 0.0).astype(BF16), inv16[hd]) for hd in heads]
        inv = [inv[hd] - _dot(inv16[hd], coupled[hd].astype(BF16)) for hd in heads]
        s_blk *= 2
    inv16 = [inv[hd].astype(BF16) for hd in heads]

    eg = [jnp.exp(gcc[hd]) for hd in heads]
    rhs = [jnp.concatenate([v_ref[0, :, cols[hd]] * beta[hd], kb[hd] * eg[hd]], axis=1) for hd in heads]
    sol = [_dot(inv16[hd], rhs[hd].astype(BF16)) for hd in heads]
    a_hi, a_lo, s_hi, s_lo = [], [], [], []
    for hd in heads:
        hi, lo = _split_bf16(ident + lower[hd], 2)
        a_hi.append(hi)
        a_lo.append(lo)
        hi, lo = _split_bf16(sol[hd], 2)
        s_hi.append(hi)
        s_lo.append(lo)
    prod = [_dot(a_hi[hd], s_hi[hd]) + (_dot(a_hi[hd], s_lo[hd]) + _dot(a_lo[hd], s_hi[hd])) for hd in heads]
    corr = [_dot(inv16[hd], (rhs[hd] - prod[hd]).astype(BF16)) for hd in heads]
    sol = [sol[hd] + corr[hd] for hd in heads]
    for hd in heads:
        u_ref[0, :, cols[hd]] = sol[hd][:, :HEAD_LANES]
        w_ref[0, :, cols[hd]] = sol[hd][:, HEAD_LANES:].astype(BF16)
        qk = incl(qk_raw[hd] * decay[hd])
        qd_ref[0, :, cols[hd]] = (q_ref[0, :, cols[hd]] * eg[hd]).astype(BF16)
        for n in range(tt // c):
            rows = slice(n * c, (n + 1) * c)
            qk_ref[0, hd, rows, :] = qk[rows, rows].astype(BF16)
            gl = gcc[hd][(n + 1) * c - 1:(n + 1) * c, :]
            kd_ref[0, rows, cols[hd]] = (kt[hd][rows] * jnp.exp(gl - gcc[hd][rows])).astype(BF16)


def _gdn_prep(qkv, gates, *, tt=256):
    b, t, _ = qkv.shape
    nh = GDN_HEADS
    dm = nh * HEAD_LANES
    blk = lambda part: pl.BlockSpec((1, tt, dm), lambda bi, i: (bi, i, part))
    return pl.pallas_call(
        _gdn_prep_kernel,
        grid=(b, t // tt),
        in_specs=[blk(0), blk(1), blk(2), pl.BlockSpec((1, tt, LANES), lambda bi, i: (bi, i, 0))],
        out_specs=[blk(0)] * 4 + [pl.BlockSpec((1, nh, tt, GDN_CHUNK), lambda bi, i: (bi, 0, i, 0))],
        out_shape=[jax.ShapeDtypeStruct((b, t, dm), F32)]
        + [jax.ShapeDtypeStruct((b, t, dm), BF16)] * 3
        + [jax.ShapeDtypeStruct((b, nh, t, GDN_CHUNK), BF16)],
        compiler_params=_params(("arbitrary", "arbitrary")),
        name="gdn_prep",
    )(qkv, qkv, qkv, gates)


def _gdn_scan_kernel(u_ref, w_ref, qd_ref, kd_ref, qk_ref, gates_ref, z_ref, nw_ref, o_ref, s_ref):
    c = GDN_CHUNK
    nb, tt = u_ref.shape[0], u_ref.shape[1]

    @pl.when(pl.program_id(1) == 0)
    def _():
        s_ref[...] = jnp.zeros_like(s_ref)

    chains = [(bi, hd) for bi in range(nb) for hd in range(GDN_HEADS)]
    cols = [slice(hd * HEAD_LANES, (hd + 1) * HEAD_LANES) for hd in range(GDN_HEADS)]
    state = [s_ref[bi, hd] for bi, hd in chains]
    for n in range(tt // c):
        rows = slice(n * c, (n + 1) * c)
        r = [_dot(jnp.concatenate([w_ref[bi, rows, cols[hd]], qd_ref[bi, rows, cols[hd]]], axis=0),
                  state[i].astype(BF16)) for i, (bi, hd) in enumerate(chains)]
        v_new = [(u_ref[bi, rows, cols[hd]] - r[i][:c]).astype(BF16) for i, (bi, hd) in enumerate(chains)]
        intra = [_dot(qk_ref[bi, hd, rows, :], v_new[i]) for i, (bi, hd) in enumerate(chains)]
        upd = [_dot_tn(kd_ref[bi, rows, cols[hd]], v_new[i]) for i, (bi, hd) in enumerate(chains)]
        for i, (bi, hd) in enumerate(chains):
            last = (n + 1) * c - 1
            decay_last = jnp.exp(gates_ref[bi, last:last + 1, GDN_HEADS + hd:GDN_HEADS + hd + 1])
            state[i] = state[i] * decay_last + upd[i]
            zt = z_ref[bi, rows, cols[hd]].astype(F32)
            o = r[i][c:] + intra[i]
            o_ref[bi, rows, cols[hd]] = (_rms(o, nw_ref[...]) * (zt * _sigmoid(zt))).astype(o_ref.dtype)
    for i, (bi, hd) in enumerate(chains):
        s_ref[bi, hd] = state[i]


def _gdn_scan(u, w, qd, kd, qk, gates, z, norm_w, *, tt=GDN_SCAN_TILE, nb=GDN_SCAN_BATCH):
    b, t, dm = u.shape
    nh = GDN_HEADS
    assert b % nb == 0 and t % tt == 0
    blk = pl.BlockSpec((nb, tt, dm), lambda bi, i: (bi, i, 0))
    return pl.pallas_call(
        _gdn_scan_kernel,
        grid=(b // nb, t // tt),
        in_specs=[blk, blk, blk, blk,
                  pl.BlockSpec((nb, nh, tt, GDN_CHUNK), lambda bi, i: (bi, 0, i, 0)),
                  pl.BlockSpec((nb, tt, LANES), lambda bi, i: (bi, i, 0)),
                  blk, _const_spec((1, HEAD_LANES))],
        out_specs=blk,
        out_shape=jax.ShapeDtypeStruct((b, t, dm), BF16),
        scratch_shapes=[pltpu.VMEM((nb, nh, GDN_HEAD_DIM, GDN_HEAD_DIM), F32)],
        compiler_params=_params(("arbitrary", "arbitrary")),
        name="gdn_scan",
    )(u, w, qd, kd, qk, gates, z, norm_w)


def _post_kernel(*refs, n_mix, final_norm):
    x_ref = refs[0]
    mix_refs = refs[1:1 + n_mix]
    wout_ref, g_ref, wup_ref, wdn_ref, p_ref, wpp_ref, wpg_ref = refs[1 + n_mix:8 + n_mix]
    rest = refs[8 + n_mix:]
    if final_norm:
        gf_ref, o_ref = rest
    else:
        (o_ref,) = rest
    mix = mix_refs[0][...] if n_mix == 1 else jnp.concatenate([r[...] for r in mix_refs], axis=1)
    x = x_ref[...] + _dot(mix, wout_ref[...])
    h = _rms(x, g_ref[...]).astype(BF16)
    d_ff = wup_ref.shape[1]
    acc = x
    for s in range(d_ff // FF_SEG):
        a = jnp.maximum(_dot(h, wup_ref[:, s * FF_SEG:(s + 1) * FF_SEG]), 0.0)
        acc = acc + _dot((a * a).astype(BF16), wdn_ref[s * FF_SEG:(s + 1) * FF_SEG, :])
    x = acc
    gate = _sigmoid(_dot(x.astype(BF16), wpg_ref[...]))
    x = x + _dot(p_ref[...].astype(BF16), wpp_ref[...]) * gate
    if final_norm:
        x = _rms(x, gf_ref[...])
    o_ref[...] = x


def _post(x2d, mixes, wout, g, wup, wdn, p2d, wpp, wpg, gf=None):
    m, d = x2d.shape
    tm = TOKEN_TILE
    row = lambda i: (i, 0)
    single = pl.Buffered(1)
    const = lambda a: pl.BlockSpec(a.shape, lambda i: (0, 0), pipeline_mode=single)
    args = [x2d, *mixes, wout, g, wup, wdn, p2d, wpp, wpg]
    in_specs = ([pl.BlockSpec((tm, d), row)]
                + [pl.BlockSpec((tm, a.shape[1]), row) for a in mixes]
                + [const(wout), const(g), const(wup), const(wdn), pl.BlockSpec((tm, p2d.shape[1]), row), const(wpp), const(wpg)])
    if gf is not None:
        args.append(gf)
        in_specs.append(const(gf))
    kern = functools.partial(_post_kernel, n_mix=len(mixes), final_norm=gf is not None)
    return pl.pallas_call(
        kern,
        grid=(m // tm,),
        in_specs=in_specs,
        out_specs=pl.BlockSpec((tm, d), row),
        out_shape=jax.ShapeDtypeStruct((m, d), F32),
        compiler_params=_params(("arbitrary",)),
        name="out_proj_mlp_ple",
    )(*args)


def _pad_lanes(a):
    return jnp.pad(a, ((0, 0), (0, LANES - a.shape[1])))


def kernel(x, p, positions, norm_mix, norm_mlp, norm_final, w_in_even, conv_w, a_log, dt_bias, gdn_norm,
           lam_q1, lam_k1, lam_q2, lam_k2, diff_norm, w_out_even, w_in_odd, b_forget, w_out_odd,
           w_mlp_up, w_mlp_down, w_ple_proj, w_ple_gate):
    b, t, d = x.shape
    depth = p.shape[0]
    m = b * t
    assert t % TOKEN_TILE == 0 and d % PROJ_SEG == 0
    nh = GDN_HEADS
    gdn_w = 3 * nh * GDN_HEAD_DIM + nh * GDN_HEAD_DIM
    assert w_in_even.shape[2] == gdn_w + 2 * nh + 3 * DIFF_HEADS * 2 * DIFF_QK_DIM

    inv_freq = ROPE_THETA ** (-jnp.arange(0, DIFF_QK_DIM, 2, dtype=F32) / DIFF_QK_DIM)
    ang = positions.astype(F32)[..., None] * inv_freq
    cos, sin = jnp.cos(ang), jnp.sin(ang)
    cos_t = jnp.concatenate([cos, cos, cos, cos], axis=-1).reshape(m, LANES)
    sin_t = jnp.concatenate([-sin, sin, -sin, sin], axis=-1).reshape(m, LANES)

    x2d = x.reshape(m, d)
    for i in range(depth):
        j = i // 2
        g_mix = norm_mix[i].reshape(1, d)
        if i % 2 == 0:
            lambda_init = 0.8 - 0.6 * math.exp(-0.3 * i)
            w = w_in_even[j]
            wm = jnp.concatenate([w[:, :gdn_w], w[:, gdn_w + 2 * nh:]], axis=1).astype(BF16)
            wg = _pad_lanes(w[:, gdn_w:gdn_w + 2 * nh]).astype(BF16)
            alog_row = _pad_lanes(jnp.concatenate([jnp.zeros((nh,), F32), a_log[j]]).reshape(1, 2 * nh))
            dt_row = _pad_lanes(jnp.concatenate([jnp.zeros((nh,), F32), dt_bias[j]]).reshape(1, 2 * nh))
            qkv, z, qkb, vbt, gates = _even_in(x2d, g_mix, wm, wg, conv_w[j], alog_row, dt_row, cos_t, sin_t, t)
            qkv, z, qkb, gates = (a.reshape(b, t, -1) for a in (qkv, z, qkb, gates))
            u, wy, qd, kd, qk = _gdn_prep(qkv, gates)
            o_a = _gdn_scan(u, wy, qd, kd, qk, gates, z, gdn_norm[j].reshape(1, HEAD_LANES))
            lam_params = jnp.stack([lam_q1[j], lam_k1[j], lam_q2[j], lam_k2[j]])
            o_b = _diff_attention(qkb, vbt, lam_params, diff_norm[j].reshape(1, HEAD_LANES), lambda_init)
            mixes = [o_a.reshape(m, -1), o_b.reshape(m, -1)]
            wout = w_out_even[j].astype(BF16)
        else:
            w = w_in_odd[j]
            d_mix = (w.shape[1] - FOX_HEADS) // 4
            wm = w[:, :4 * d_mix].astype(BF16)
            wf = _pad_lanes(w[:, 4 * d_mix:]).astype(BF16)
            bf_row = _pad_lanes(b_forget[j].reshape(1, FOX_HEADS))
            q, k, vt, gate, qb, kb = _odd_in(x2d, g_mix, wm, wf, bf_row, t)
            q, k, gate, qb, kb = (a.reshape(b, t, -1) for a in (q, k, gate, qb, kb))
            o = _fox_attention(q, k, vt, gate, qb, kb)
            mixes = [o.reshape(m, -1)]
            wout = w_out_odd[j].astype(BF16)
        x2d = _post(x2d, mixes, wout, norm_mlp[i].reshape(1, d), w_mlp_up[i].astype(BF16),
                    w_mlp_down[i].astype(BF16), p[i].reshape(m, -1), w_ple_proj[i].astype(BF16),
                    w_ple_gate[i].astype(BF16), norm_final.reshape(1, d) if i == depth - 1 else None)
    return x2d.reshape(b, t, d)
```

```python
import functools
import math

import jax
import jax.numpy as jnp
import numpy as np
from jax import lax
from jax.experimental import pallas as pl
from jax.experimental.pallas import tpu as pltpu

F32 = jnp.float32
BF16 = jnp.bfloat16

GDN_HEADS = 4
GDN_HEAD_DIM = 128
GDN_CHUNK = 64
CONV_WIDTH = 4
DIFF_HEADS = 4
DIFF_QK_DIM = 64
FOX_HEADS = 8
HEAD_LANES = 128
ROPE_THETA = 10000.0
EPS = 1e-6
NEG_INF = -1e30
LOG2E = 1.4426950408889634
LANES = 128
SUBLANES = 8
VMEM_LIMIT_BYTES = 56 * 1024 * 1024

TOKEN_TILE = 1024
EVEN_TOKEN_TILE = 512
PROJ_SEG = 512
FF_SEG = 1024
GDN_PREP_SUBTILE = 256
GDN_PREP_TILE = 512
GDN_SCAN_TILE = 256
GDN_SCAN_BATCH = 4
ATTN_TILE = 512
ATTN_UNROLL = 2
BIAS_LANES_PER_HEAD = 16
ONES_ROWS = 16


def _dot(a, b):
    return jnp.dot(a, b, preferred_element_type=F32)


def _dot_exact(a, b):
    return jnp.dot(a, b, preferred_element_type=F32, precision=lax.Precision.HIGHEST)


def _dot_nt(a, b):
    return lax.dot_general(a, b, (((1,), (1,)), ((), ())), preferred_element_type=F32)


def _dot_tn(a, b):
    return lax.dot_general(a, b, (((0,), (0,)), ((), ())), preferred_element_type=F32)


def _rms(x, g):
    return x * lax.rsqrt(jnp.mean(x * x, axis=-1, keepdims=True) + EPS) * g


def _sigmoid(x):
    return 1.0 / (1.0 + jnp.exp(-x))


def _softplus(x):
    return jnp.maximum(x, 0.0) + jnp.log1p(jnp.exp(-jnp.abs(x)))


def _row_scan(x, period):
    rows = lax.broadcasted_iota(jnp.int32, x.shape, 0) % period
    s = 1
    while s < period:
        x = x + jnp.where(rows >= s, pltpu.roll(x, s, 0), 0.0)
        s *= 2
    return x


def _const_spec(shape):
    return pl.BlockSpec(shape, lambda *_: (0,) * len(shape))


def _params(sem):
    return pltpu.CompilerParams(dimension_semantics=sem, vmem_limit_bytes=VMEM_LIMIT_BYTES)


def _even_in_kernel(x_ref, g_ref, wm_ref, wg_ref, conv_ref, alog_ref, dt_ref, cos_ref, sin_ref,
                    qkv_ref, z_ref, qkb_ref, vbt_ref, gates_ref, h_ref, carry_ref, tr_ref, pad_ref, *, tiles_per_seq):
    tm = x_ref.shape[0]
    i = pl.program_id(0)
    h_ref[...] = _rms(x_ref[...], g_ref[...]).astype(BF16)
    seq_start = (i % tiles_per_seq) == 0
    seg = lambda s: slice(s * PROJ_SEG, (s + 1) * PROJ_SEG)
    project = lambda s: _dot(h_ref[...], wm_ref[:, seg(s)])

    def gdn_qkv(s, y):
        cols = seg(s)
        pad_ref[0:SUBLANES, :] = jnp.where(seq_start, 0.0, carry_ref[:, cols])
        pad_ref[SUBLANES:, :] = y
        carry_ref[:, cols] = y[tm - SUBLANES:, :]
        w = conv_ref[:, cols]
        a = y * w[CONV_WIDTH - 1:CONV_WIDTH, :]
        for k in range(1, CONV_WIDTH):
            a = a + pad_ref[SUBLANES - k:SUBLANES - k + tm, :] * w[CONV_WIDTH - 1 - k:CONV_WIDTH - k, :]
        a = a * _sigmoid(a)
        if s < 2:
            outs = []
            for hd in range(GDN_HEADS):
                blk = a[:, hd * HEAD_LANES:(hd + 1) * HEAD_LANES]
                n = blk * lax.rsqrt(jnp.sum(blk * blk, axis=-1, keepdims=True) + EPS)
                outs.append(n * (GDN_HEAD_DIM ** -0.5) if s == 0 else n)
            a = jnp.concatenate(outs, axis=1)
        qkv_ref[:, cols] = a

    def gdn_gate(s, y):
        z_ref[...] = y.astype(BF16)

    def diff_qk(s, y):
        cos = jnp.concatenate([cos_ref[...]] * (PROJ_SEG // LANES), axis=1)
        sin = jnp.concatenate([sin_ref[...]] * (PROJ_SEG // LANES), axis=1)
        lane = lax.broadcasted_iota(jnp.int32, (tm, PROJ_SEG), 1)
        first_half = (lane % DIFF_QK_DIM) < (DIFF_QK_DIM // 2)
        swapped = jnp.where(first_half, pltpu.roll(y, PROJ_SEG - DIFF_QK_DIM // 2, 1),
                            pltpu.roll(y, DIFF_QK_DIM // 2, 1))
        scale = DIFF_QK_DIM ** -0.5 * LOG2E if s == 4 else 1.0
        qkb_ref[:, seg(s - 4)] = ((y * cos + swapped * sin) * scale).astype(BF16)

    def diff_v(s, y):
        tr_ref[...] = y
        vbt_ref[0] = tr_ref[...].T.astype(BF16)

    stages = ((0, gdn_qkv), (3, gdn_gate), (1, gdn_qkv), (6, diff_v), (2, gdn_qkv), (4, diff_qk), (5, diff_qk))
    pending = project(stages[0][0])
    for n, (s, epilogue) in enumerate(stages):
        upcoming = project(stages[n + 1][0]) if n + 1 < len(stages) else _dot(h_ref[...], wg_ref[...])
        epilogue(s, pending)
        pending = upcoming

    graw = pending
    beta = _sigmoid(graw)
    g = -jnp.exp(alog_ref[...]) * _softplus(graw + dt_ref[...])
    gc = _row_scan(g, GDN_CHUNK)
    lane_g = lax.broadcasted_iota(jnp.int32, (tm, LANES), 1)
    gates_ref[...] = jnp.where(lane_g < GDN_HEADS, beta, gc)


def _even_in(x2d, g, wm, wg, conv_w, alog_row, dt_row, cos_t, sin_t, seq_len):
    m, d = x2d.shape
    tm = EVEN_TOKEN_TILE
    n_main = wm.shape[1]
    tps = seq_len // tm
    kern = functools.partial(_even_in_kernel, tiles_per_seq=tps)
    row = lambda i: (i, 0)
    return pl.pallas_call(
        kern,
        grid=(m // tm,),
        in_specs=[
            pl.BlockSpec((tm, d), row),
            _const_spec((1, d)),
            _const_spec((d, n_main)),
            _const_spec((d, LANES)),
            _const_spec(conv_w.shape),
            _const_spec((1, LANES)),
            _const_spec((1, LANES)),
            pl.BlockSpec((tm, LANES), row),
            pl.BlockSpec((tm, LANES), row),
        ],
        out_specs=[
            pl.BlockSpec((tm, 3 * PROJ_SEG), row),
            pl.BlockSpec((tm, PROJ_SEG), row),
            pl.BlockSpec((tm, 2 * PROJ_SEG), row),
            pl.BlockSpec((1, PROJ_SEG, tm), lambda i: (i // tps, 0, i % tps)),
            pl.BlockSpec((tm, LANES), row),
        ],
        out_shape=[
            jax.ShapeDtypeStruct((m, 3 * PROJ_SEG), F32),
            jax.ShapeDtypeStruct((m, PROJ_SEG), BF16),
            jax.ShapeDtypeStruct((m, 2 * PROJ_SEG), BF16),
            jax.ShapeDtypeStruct((m // seq_len, PROJ_SEG, seq_len), BF16),
            jax.ShapeDtypeStruct((m, LANES), F32),
        ],
        scratch_shapes=[pltpu.VMEM((tm, d), BF16), pltpu.VMEM((SUBLANES, 3 * PROJ_SEG), F32), pltpu.VMEM((tm, PROJ_SEG), F32),
                        pltpu.VMEM((tm + SUBLANES, PROJ_SEG), F32)],
        compiler_params=_params(("arbitrary",)),
        name="even_in_proj",
    )(x2d, g, wm, wg, conv_w, alog_row, dt_row, cos_t, sin_t)


def _odd_in_kernel(x_ref, g_ref, wm_ref, wf_ref, bf_ref, sel_ref, ones_ref, q_ref, k_ref, vt_ref, gate_ref, qb_ref, kb_ref,
                   h_ref, carry_ref, tr_ref, *, tiles_per_seq, d_mix):
    tm = x_ref.shape[0]
    i = pl.program_id(0)
    h_ref[...] = _rms(x_ref[...], g_ref[...]).astype(BF16)
    head_dim = d_mix // FOX_HEADS
    for o_ref, base, scale in ((q_ref, 0, head_dim ** -0.5 * LOG2E), (k_ref, d_mix, 1.0),
                               (vt_ref, 2 * d_mix, 1.0), (gate_ref, 3 * d_mix, 1.0)):
        for s in range(d_mix // PROJ_SEG):
            cols = slice(s * PROJ_SEG, (s + 1) * PROJ_SEG)
            y = _dot(h_ref[...], wm_ref[:, base + s * PROJ_SEG:base + (s + 1) * PROJ_SEG])
            if o_ref is vt_ref:
                tr_ref[...] = y
                o_ref[0, cols, :] = tr_ref[...].T.astype(BF16)
            else:
                o_ref[:, cols] = (y * scale).astype(BF16)
    f = _dot(h_ref[...], wf_ref[...]) + bf_ref[...]
    log_f = jnp.minimum(f, 0.0) - jnp.log1p(jnp.exp(-jnp.abs(f)))
    prev = jnp.where((i % tiles_per_seq) == 0, 0.0, carry_ref[0:1, :])
    cum = _row_scan(log_f, tm) + prev
    carry_ref[...] = jnp.broadcast_to(cum[tm - 1:tm, :], carry_ref.shape)
    pieces = jnp.concatenate(_split_bf16(LOG2E * cum, 3), axis=1)
    lanes = _dot(pieces, sel_ref[...]) + ones_ref[...]
    qb_ref[...] = lanes[:, :LANES].astype(BF16)
    kb_ref[...] = lanes[:, LANES:].astype(BF16)


def _bias_lane_tables():
    sel = np.zeros((3 * LANES, 2 * LANES), np.float32)
    ones = np.zeros((1, 2 * LANES), np.float32)
    for h in range(FOX_HEADS):
        base = BIAS_LANES_PER_HEAD * h
        for piece in range(3):
            sel[LANES * piece + h, base + 3 + piece] = 1.0
            sel[LANES * piece + h, LANES + base + piece] = -1.0
            ones[0, base + piece] = 1.0
            ones[0, LANES + base + 3 + piece] = 1.0
    return jnp.asarray(sel, BF16), jnp.asarray(ones, F32)


def _odd_in(x2d, g, wm, wf, bf_row, seq_len):
    m, d = x2d.shape
    sel, ones_row = _bias_lane_tables()
    tm = TOKEN_TILE
    d_mix = wm.shape[1] // 4
    tps = seq_len // tm
    kern = functools.partial(_odd_in_kernel, tiles_per_seq=tps, d_mix=d_mix)
    row = lambda i: (i, 0)
    row_blk = pl.BlockSpec((tm, d_mix), row)
    row_shape = jax.ShapeDtypeStruct((m, d_mix), BF16)
    return pl.pallas_call(
        kern,
        grid=(m // tm,),
        in_specs=[
            pl.BlockSpec((tm, d), row),
            _const_spec((1, d)),
            _const_spec(wm.shape),
            _const_spec((d, LANES)),
            _const_spec((1, LANES)),
            _const_spec(sel.shape),
            _const_spec(ones_row.shape),
        ],
        out_specs=[row_blk, row_blk, pl.BlockSpec((1, d_mix, tm), lambda i: (i // tps, 0, i % tps)), row_blk,
                   pl.BlockSpec((tm, LANES), row), pl.BlockSpec((tm, LANES), row)],
        out_shape=[row_shape, row_shape, jax.ShapeDtypeStruct((m // seq_len, d_mix, seq_len), BF16), row_shape,
                   jax.ShapeDtypeStruct((m, LANES), BF16), jax.ShapeDtypeStruct((m, LANES), BF16)],
        scratch_shapes=[pltpu.VMEM((tm, d), BF16), pltpu.VMEM((SUBLANES, LANES), F32), pltpu.VMEM((tm, PROJ_SEG), F32)],
        compiler_params=_params(("arbitrary",)),
        name="odd_in_proj",
    )(x2d, g, wm, wf, bf_row, sel, ones_row)


def _attn_kernel(*refs, tq, fox, lambda_init):
    if fox:
        q_ref, k_ref, vt_ref, gate_ref, qb_ref, kball_ref, o_ref, st_ref, m_ref, acc_ref, kb_ref = refs
    else:
        q_ref, k_ref, vt_ref, lam_ref, nw_ref, o_ref, st_ref, m_ref, acc_ref = refs
    tk = tq
    hg = pl.program_id(1)
    n_tiles = q_ref.shape[1] // tq
    head = lambda g: slice(g * HEAD_LANES, (g + 1) * HEAD_LANES)
    kv = [head(0), head(1)] if fox else [head(0), head(0)]
    tile_rows = lambda tile: pl.ds(pl.multiple_of(tile * tq, tq), tq)

    def queries(g, tile):
        if fox:
            return jnp.concatenate([q_ref[0, tile_rows(tile), head(g)], qb_ref[0, tile_rows(tile), :]], axis=1)
        q = q_ref[0, tile_rows(tile), :]
        lane = lax.broadcasted_iota(jnp.int32, q.shape, 1)
        keep = (lane < DIFF_QK_DIM) if g == 0 else (lane >= DIFF_QK_DIM)
        return jnp.where(keep, q, jnp.zeros_like(q))

    def scores_of(qmat, g, j):
        k0 = pl.multiple_of(j * tk, tk)
        kj = k_ref[0, pl.ds(k0, tk), kv[g]]
        if fox:
            kj = jnp.concatenate([kj, kb_ref[pl.ds(k0, tk), kv[g]]], axis=1)
        return _dot_nt(kj, qmat)

    if fox:
        kb_all = kball_ref[0]
        owner = lax.broadcasted_iota(jnp.int32, kb_all.shape, 1) // BIAS_LANES_PER_HEAD
        for g in range(2):
            kb_ref[:, head(g)] = jnp.where(owner == hg * 2 + g, kb_all, jnp.zeros_like(kb_all))
    else:
        lam_p = lam_ref[...]
        lam = (jnp.exp(jnp.sum(lam_p[0:1] * lam_p[1:2], axis=1, keepdims=True))
               - jnp.exp(jnp.sum(lam_p[2:3] * lam_p[3:4], axis=1, keepdims=True)) + lambda_init)
    st_ref[...] = scores_of(queries(0, 0), 0, 0)
    ones = jnp.ones((ONES_ROWS, tk), BF16)

    def tile(qi, carry):
        qs = [queries(g, qi) for g in range(2)]
        scores = lambda g, j: scores_of(qs[g], g, j)

        def absorb(g, j, st, masked):
            m = m_ref[g]
            k0 = pl.multiple_of(j * tk, tk)
            if masked:
                kpos = k0 + lax.broadcasted_iota(jnp.int32, (tk, tq), 0)
                qpos = qi * tq + lax.broadcasted_iota(jnp.int32, (tk, tq), 1)
                st = jnp.where(qpos >= kpos, st, NEG_INF)
            m_new = jnp.maximum(m, jnp.max(st, axis=0, keepdims=True))
            p = jnp.exp2(st - m_new).astype(BF16)
            vt = jnp.concatenate([vt_ref[0, kv[g], pl.ds(k0, tk)], ones], axis=0)
            acc_ref[g] = jnp.exp2(m - m_new) * acc_ref[g] + _dot(vt, p)
            m_ref[g] = m_new

        def step(j):
            st1 = scores(1, j)
            absorb(0, j, st_ref[...], False)
            st_ref[...] = scores(0, j + 1)
            absorb(1, j, st1, False)

        def steps(j, c):
            for u in range(ATTN_UNROLL):
                step(j * ATTN_UNROLL + u)
            return c

        def tail_step(j, c):
            step(j)
            return c

        m_ref[...] = jnp.full(m_ref.shape, NEG_INF, F32)
        acc_ref[...] = jnp.zeros(acc_ref.shape, F32)
        n_full = qi // ATTN_UNROLL
        lax.fori_loop(0, n_full, steps, 0)
        lax.fori_loop(n_full * ATTN_UNROLL, qi, tail_step, 0)
        st1 = scores(1, qi)
        absorb(0, qi, st_ref[...], True)
        absorb(1, qi, st1, True)
        st_ref[...] = scores_of(queries(0, jnp.minimum(qi + 1, n_tiles - 1)), 0, 0)
        outs = [acc_ref[g, :HEAD_LANES, :] / acc_ref[g, HEAD_LANES:HEAD_LANES + 1, :] for g in range(2)]
        rows = tile_rows(qi)
        if fox:
            for g in range(2):
                gate = _sigmoid(gate_ref[0, rows, head(g)].astype(F32))
                o_ref[0, rows, head(g)] = (outs[g].T * gate).astype(o_ref.dtype)
        else:
            o = (outs[0] - lam * outs[1]).T
            o_ref[0, rows, :] = (_rms(o, nw_ref[...]) * (1.0 - lambda_init)).astype(o_ref.dtype)
        return carry

    lax.fori_loop(0, n_tiles, tile, 0)


def _attn_state(tq):
    return [pltpu.VMEM((tq, tq), F32), pltpu.VMEM((2, 1, tq), F32), pltpu.VMEM((2, HEAD_LANES + ONES_ROWS, tq), F32)]


def _fox_attention(q, k, vt, gate, qb, kb, *, tq=ATTN_TILE):
    b, t, dm = q.shape
    width = 2 * HEAD_LANES
    kern = functools.partial(_attn_kernel, tq=tq, fox=True, lambda_init=0.0)
    seq = pl.BlockSpec((1, t, width), lambda bi, h: (bi, 0, h))
    seq_bias = pl.BlockSpec((1, t, LANES), lambda bi, h: (bi, 0, 0))
    return pl.pallas_call(
        kern,
        grid=(b, dm // width),
        in_specs=[seq, seq, pl.BlockSpec((1, width, t), lambda bi, h: (bi, h, 0)), seq, seq_bias, seq_bias],
        out_specs=seq,
        out_shape=jax.ShapeDtypeStruct((b, t, dm), BF16),
        scratch_shapes=_attn_state(tq) + [pltpu.VMEM((t, width), BF16)],
        compiler_params=_params(("arbitrary", "arbitrary")),
        name="fox_attention",
    )(q, k, vt, gate, qb, kb)


def _diff_attention(qk, vt, lam_params, norm_w, lambda_init, *, tq=ATTN_TILE):
    b, t, _ = qk.shape
    nh = DIFF_HEADS
    kern = functools.partial(_attn_kernel, tq=tq, fox=False, lambda_init=lambda_init)
    seq = pl.BlockSpec((1, t, HEAD_LANES), lambda bi, h: (bi, 0, h))
    return pl.pallas_call(
        kern,
        grid=(b, nh),
        in_specs=[seq,
                  pl.BlockSpec((1, t, HEAD_LANES), lambda bi, h: (bi, 0, nh + h)),
                  pl.BlockSpec((1, HEAD_LANES, t), lambda bi, h: (bi, h, 0)),
                  _const_spec(lam_params.shape), _const_spec((1, HEAD_LANES))],
        out_specs=seq,
        out_shape=jax.ShapeDtypeStruct((b, t, nh * HEAD_LANES), BF16),
        scratch_shapes=_attn_state(tq),
        compiler_params=_params(("arbitrary", "arbitrary")),
        name="diff_attention",
    )(qk, qk, vt, lam_params, norm_w)


def _split_bf16(x, parts):
    out = []
    for _ in range(parts):
        piece = x.astype(BF16)
        out.append(piece)
        x = x - piece.astype(F32)
    return out


def _dot_split(a, b):
    a_hi, a_lo = _split_bf16(a, 2)
    b_hi, b_lo = _split_bf16(b, 2)
    return _dot(a_hi, b_hi) + (_dot(a_hi, b_lo) + _dot(a_lo, b_hi))


def _gdn_prep_kernel(q_ref, k_ref, v_ref, gates_ref, u_ref, w_ref, qd_ref, kd_ref, qk_ref):
    c = GDN_CHUNK
    sub = GDN_PREP_SUBTILE
    units = [(slice(t0, t0 + sub), hd) for t0 in range(0, q_ref.shape[1], sub) for hd in range(GDN_HEADS)]
    ids = range(len(units))
    col = lambda hd: slice(hd * HEAD_LANES, (hd + 1) * HEAD_LANES)
    lane = lax.broadcasted_iota(jnp.int32, (sub, LANES), 1)
    ri = lax.broadcasted_iota(jnp.int32, (sub, sub), 0)
    ci = lax.broadcasted_iota(jnp.int32, (sub, sub), 1)
    chunk_start = ri - ri % c
    incl = lambda a: jnp.where(ci <= ri, jnp.where(ci >= chunk_start, a, 0.0), 0.0)
    strict = lambda a: jnp.where(ci < ri, jnp.where(ci >= chunk_start, a, 0.0), 0.0)
    ident = jnp.where(ri == ci, 1.0, 0.0)
    ones = jnp.ones((sub, LANES), BF16)
    kt = [k_ref[0, rows, col(hd)] for rows, hd in units]
    beta = [gates_ref[0, rows, hd:hd + 1] for rows, hd in units]
    gcc = [gates_ref[0, rows, GDN_HEADS + hd:GDN_HEADS + hd + 1] for rows, hd in units]
    k16 = [kt[i].astype(BF16) for i in ids]
    kb = [kt[i] * beta[i] for i in ids]

    gc_row = []
    for i in ids:
        g_hi, g_mid, g_lo = (piece.astype(F32) for piece in _split_bf16(gcc[i], 3))
        pieces = jnp.where(lane == 0, g_hi, jnp.where(lane == 1, g_mid, jnp.where(lane == 2, g_lo, 0.0)))
        gc_row.append(_dot_nt(ones, pieces.astype(BF16)))
    kk = [_dot_nt(kb[i].astype(BF16), k16[i]) for i in ids]
    qk_raw = [_dot_nt(q_ref[0, rows, col(hd)].astype(BF16), k16[i]) for i, (rows, hd) in enumerate(units)]
    decay = [incl(jnp.exp(incl(gcc[i] - gc_row[i]))) for i in ids]
    lower = [strict(kk[i] * decay[i]) for i in ids]

    span = ri ^ ci
    inv = [ident - jnp.where(span == 1, lower[i], 0.0) for i in ids]
    s_blk = 2
    while s_blk < c:
        shift = int(math.log2(s_blk))
        inv16 = [inv[i].astype(BF16) for i in ids]
        coupled = [_dot(jnp.where((span >> shift) == 1, lower[i], 0.0).astype(BF16), inv16[i]) for i in ids]
        inv = [inv[i] - _dot(inv16[i], coupled[i].astype(BF16)) for i in ids]
        s_blk *= 2
    inv16 = [inv[i].astype(BF16) for i in ids]

    eg = [jnp.exp(gcc[i]) for i in ids]
    rhs = [jnp.concatenate([v_ref[0, rows, col(hd)] * beta[i], kb[i] * eg[i]], axis=1)
           for i, (rows, hd) in enumerate(units)]
    sol = [_dot(inv16[i], rhs[i].astype(BF16)) for i in ids]
    a_hi, a_lo, s_hi, s_lo = [], [], [], []
    for i in ids:
        hi, lo = _split_bf16(ident + lower[i], 2)
        a_hi.append(hi)
        a_lo.append(lo)
        hi, lo = _split_bf16(sol[i], 2)
        s_hi.append(hi)
        s_lo.append(lo)
    prod = [_dot(a_hi[i], s_hi[i]) + (_dot(a_hi[i], s_lo[i]) + _dot(a_lo[i], s_hi[i])) for i in ids]
    corr = [_dot(inv16[i], (rhs[i] - prod[i]).astype(BF16)) for i in ids]
    sol = [sol[i] + corr[i] for i in ids]
    for i, (rows, hd) in enumerate(units):
        u_ref[0, rows, col(hd)] = sol[i][:, :HEAD_LANES]
        w_ref[0, rows, col(hd)] = sol[i][:, HEAD_LANES:].astype(BF16)
        qk = incl(qk_raw[i] * decay[i])
        qd_ref[0, rows, col(hd)] = (q_ref[0, rows, col(hd)] * eg[i]).astype(BF16)
        for n in range(sub // c):
            blk = slice(n * c, (n + 1) * c)
            out_rows = slice(rows.start + n * c, rows.start + (n + 1) * c)
            qk_ref[0, hd, out_rows, :] = qk[blk, blk].astype(BF16)
            gl = gcc[i][(n + 1) * c - 1:(n + 1) * c, :]
            kd_ref[0, out_rows, col(hd)] = (kt[i][blk] * jnp.exp(gl - gcc[i][blk])).astype(BF16)


def _gdn_prep(qkv, gates, *, tt=GDN_PREP_TILE):
    b, t, _ = qkv.shape
    nh = GDN_HEADS
    dm = nh * HEAD_LANES
    blk = lambda part: pl.BlockSpec((1, tt, dm), lambda bi, i: (bi, i, part))
    return pl.pallas_call(
        _gdn_prep_kernel,
        grid=(b, t // tt),
        in_specs=[blk(0), blk(1), blk(2), pl.BlockSpec((1, tt, LANES), lambda bi, i: (bi, i, 0))],
        out_specs=[blk(0)] * 4 + [pl.BlockSpec((1, nh, tt, GDN_CHUNK), lambda bi, i: (bi, 0, i, 0))],
        out_shape=[jax.ShapeDtypeStruct((b, t, dm), F32)]
        + [jax.ShapeDtypeStruct((b, t, dm), BF16)] * 3
        + [jax.ShapeDtypeStruct((b, nh, t, GDN_CHUNK), BF16)],
        compiler_params=_params(("arbitrary", "arbitrary")),
        name="gdn_prep",
    )(qkv, qkv, qkv, gates)


def _gdn_scan_kernel(u_ref, w_ref, qd_ref, kd_ref, qk_ref, gates_ref, z_ref, nw_ref, o_ref, s_ref):
    c = GDN_CHUNK
    nb, tt = u_ref.shape[0], u_ref.shape[1]

    @pl.when(pl.program_id(1) == 0)
    def _():
        s_ref[...] = jnp.zeros_like(s_ref)

    chains = [(bi, hd) for bi in range(nb) for hd in range(GDN_HEADS)]
    cols = [slice(hd * HEAD_LANES, (hd + 1) * HEAD_LANES) for hd in range(GDN_HEADS)]
    state = [s_ref[bi, hd] for bi, hd in chains]
    for n in range(tt // c):
        rows = slice(n * c, (n + 1) * c)
        r = [_dot(jnp.concatenate([w_ref[bi, rows, cols[hd]], qd_ref[bi, rows, cols[hd]]], axis=0),
                  state[i].astype(BF16)) for i, (bi, hd) in enumerate(chains)]
        v_new = [(u_ref[bi, rows, cols[hd]] - r[i][:c]).astype(BF16) for i, (bi, hd) in enumerate(chains)]
        intra = [_dot(qk_ref[bi, hd, rows, :], v_new[i]) for i, (bi, hd) in enumerate(chains)]
        upd = [_dot_tn(kd_ref[bi, rows, cols[hd]], v_new[i]) for i, (bi, hd) in enumerate(chains)]
        for i, (bi, hd) in enumerate(chains):
            last = (n + 1) * c - 1
            decay_last = jnp.exp(gates_ref[bi, last:last + 1, GDN_HEADS + hd:GDN_HEADS + hd + 1])
            state[i] = state[i] * decay_last + upd[i]
            zt = z_ref[bi, rows, cols[hd]].astype(F32)
            o = r[i][c:] + intra[i]
            o_ref[bi, rows, cols[hd]] = (_rms(o, nw_ref[...]) * (zt * _sigmoid(zt))).astype(o_ref.dtype)
    for i, (bi, hd) in enumerate(chains):
        s_ref[bi, hd] = state[i]


def _gdn_scan(u, w, qd, kd, qk, gates, z, norm_w, *, tt=GDN_SCAN_TILE, nb=GDN_SCAN_BATCH):
    b, t, dm = u.shape
    nh = GDN_HEADS
    assert b % nb == 0 and t % tt == 0
    blk = pl.BlockSpec((nb, tt, dm), lambda bi, i: (bi, i, 0))
    return pl.pallas_call(
        _gdn_scan_kernel,
        grid=(b // nb, t // tt),
        in_specs=[blk, blk, blk, blk,
                  pl.BlockSpec((nb, nh, tt, GDN_CHUNK), lambda bi, i: (bi, 0, i, 0)),
                  pl.BlockSpec((nb, tt, LANES), lambda bi, i: (bi, i, 0)),
                  blk, _const_spec((1, HEAD_LANES))],
        out_specs=blk,
        out_shape=jax.ShapeDtypeStruct((b, t, dm), BF16),
        scratch_shapes=[pltpu.VMEM((nb, nh, GDN_HEAD_DIM, GDN_HEAD_DIM), F32)],
        compiler_params=_params(("arbitrary", "arbitrary")),
        name="gdn_scan",
    )(u, w, qd, kd, qk, gates, z, norm_w)


def _post_kernel(*refs, n_mix, final_norm):
    x_ref = refs[0]
    mix_refs = refs[1:1 + n_mix]
    wout_ref, g_ref, wup_ref, wdn_ref, p_ref, wpp_ref, wpg_ref = refs[1 + n_mix:8 + n_mix]
    rest = refs[8 + n_mix:]
    if final_norm:
        gf_ref, o_ref = rest
    else:
        (o_ref,) = rest
    mix = mix_refs[0][...] if n_mix == 1 else jnp.concatenate([r[...] for r in mix_refs], axis=1)
    x = x_ref[...] + _dot(mix, wout_ref[...])
    h = _rms(x, g_ref[...]).astype(BF16)
    d_ff = wup_ref.shape[1]
    acc = x
    for s in range(d_ff // FF_SEG):
        a = jnp.maximum(_dot(h, wup_ref[:, s * FF_SEG:(s + 1) * FF_SEG]), 0.0)
        acc = acc + _dot((a * a).astype(BF16), wdn_ref[s * FF_SEG:(s + 1) * FF_SEG, :])
    x = acc
    gate = _sigmoid(_dot(x.astype(BF16), wpg_ref[...]))
    x = x + _dot(p_ref[...].astype(BF16), wpp_ref[...]) * gate
    if final_norm:
        x = _rms(x, gf_ref[...])
    o_ref[...] = x


def _post(x2d, mixes, wout, g, wup, wdn, p2d, wpp, wpg, gf=None):
    m, d = x2d.shape
    tm = TOKEN_TILE
    row = lambda i: (i, 0)
    single = pl.Buffered(1)
    const = lambda a: pl.BlockSpec(a.shape, lambda i: (0, 0), pipeline_mode=single)
    args = [x2d, *mixes, wout, g, wup, wdn, p2d, wpp, wpg]
    in_specs = ([pl.BlockSpec((tm, d), row)]
                + [pl.BlockSpec((tm, a.shape[1]), row) for a in mixes]
                + [const(wout), const(g), const(wup), const(wdn), pl.BlockSpec((tm, p2d.shape[1]), row), const(wpp), const(wpg)])
    if gf is not None:
        args.append(gf)
        in_specs.append(const(gf))
    kern = functools.partial(_post_kernel, n_mix=len(mixes), final_norm=gf is not None)
    return pl.pallas_call(
        kern,
        grid=(m // tm,),
        in_specs=in_specs,
        out_specs=pl.BlockSpec((tm, d), row),
        out_shape=jax.ShapeDtypeStruct((m, d), F32),
        compiler_params=_params(("arbitrary",)),
        name="out_proj_mlp_ple",
    )(*args)


def _pad_lanes(a):
    return jnp.pad(a, ((0, 0), (0, LANES - a.shape[1])))


def kernel(x, p, positions, norm_mix, norm_mlp, norm_final, w_in_even, conv_w, a_log, dt_bias, gdn_norm,
           lam_q1, lam_k1, lam_q2, lam_k2, diff_norm, w_out_even, w_in_odd, b_forget, w_out_odd,
           w_mlp_up, w_mlp_down, w_ple_proj, w_ple_gate):
    b, t, d = x.shape
    depth = p.shape[0]
    m = b * t
    assert t % TOKEN_TILE == 0 and d % PROJ_SEG == 0
    nh = GDN_HEADS
    gdn_w = 3 * nh * GDN_HEAD_DIM + nh * GDN_HEAD_DIM
    assert w_in_even.shape[2] == gdn_w + 2 * nh + 3 * DIFF_HEADS * 2 * DIFF_QK_DIM

    inv_freq = ROPE_THETA ** (-jnp.arange(0, DIFF_QK_DIM, 2, dtype=F32) / DIFF_QK_DIM)
    ang = positions.astype(F32)[..., None] * inv_freq
    cos, sin = jnp.cos(ang), jnp.sin(ang)
    cos_t = jnp.concatenate([cos, cos, cos, cos], axis=-1).reshape(m, LANES)
    sin_t = jnp.concatenate([-sin, sin, -sin, sin], axis=-1).reshape(m, LANES)

    x2d = x.reshape(m, d)
    for i in range(depth):
        j = i // 2
        g_mix = norm_mix[i].reshape(1, d)
        if i % 2 == 0:
            lambda_init = 0.8 - 0.6 * math.exp(-0.3 * i)
            w = w_in_even[j]
            wm = jnp.concatenate([w[:, :gdn_w], w[:, gdn_w + 2 * nh:]], axis=1).astype(BF16)
            wg = _pad_lanes(w[:, gdn_w:gdn_w + 2 * nh]).astype(BF16)
            alog_row = _pad_lanes(jnp.concatenate([jnp.zeros((nh,), F32), a_log[j]]).reshape(1, 2 * nh))
            dt_row = _pad_lanes(jnp.concatenate([jnp.zeros((nh,), F32), dt_bias[j]]).reshape(1, 2 * nh))
            qkv, z, qkb, vbt, gates = _even_in(x2d, g_mix, wm, wg, conv_w[j], alog_row, dt_row, cos_t, sin_t, t)
            qkv, z, qkb, gates = (a.reshape(b, t, -1) for a in (qkv, z, qkb, gates))
            u, wy, qd, kd, qk = _gdn_prep(qkv, gates)
            o_a = _gdn_scan(u, wy, qd, kd, qk, gates, z, gdn_norm[j].reshape(1, HEAD_LANES))
            lam_params = jnp.stack([lam_q1[j], lam_k1[j], lam_q2[j], lam_k2[j]])
            o_b = _diff_attention(qkb, vbt, lam_params, diff_norm[j].reshape(1, HEAD_LANES), lambda_init)
            mixes = [o_a.reshape(m, -1), o_b.reshape(m, -1)]
            wout = w_out_even[j].astype(BF16)
        else:
            w = w_in_odd[j]
            d_mix = (w.shape[1] - FOX_HEADS) // 4
            wm = w[:, :4 * d_mix].astype(BF16)
            wf = _pad_lanes(w[:, 4 * d_mix:]).astype(BF16)
            bf_row = _pad_lanes(b_forget[j].reshape(1, FOX_HEADS))
            q, k, vt, gate, qb, kb = _odd_in(x2d, g_mix, wm, wf, bf_row, t)
            q, k, gate, qb, kb = (a.reshape(b, t, -1) for a in (q, k, gate, qb, kb))
            o = _fox_attention(q, k, vt, gate, qb, kb)
            mixes = [o.reshape(m, -1)]
            wout = w_out_odd[j].astype(BF16)
        x2d = _post(x2d, mixes, wout, norm_mlp[i].reshape(1, d), w_mlp_up[i].astype(BF16),
                    w_mlp_down[i].astype(BF16), p[i].reshape(m, -1), w_ple_proj[i].astype(BF16),
                    w_ple_gate[i].astype(BF16), norm_final.reshape(1, d) if i == depth - 1 else None)
    return x2d.reshape(b, t, d)
```

```python
import functools
import math

import jax
import jax.numpy as jnp
import numpy as np
from jax import lax
from jax.experimental import pallas as pl
from jax.experimental.pallas import tpu as pltpu

F32 = jnp.float32
BF16 = jnp.bfloat16

GDN_HEADS = 4
GDN_HEAD_DIM = 128
GDN_CHUNK = 64
CONV_WIDTH = 4
DIFF_HEADS = 4
DIFF_QK_DIM = 64
FOX_HEADS = 8
HEAD_LANES = 128
ROPE_THETA = 10000.0
EPS = 1e-6
NEG_INF = -1e30
LOG2E = 1.4426950408889634
LANES = 128
SUBLANES = 8
VMEM_LIMIT_BYTES = 56 * 1024 * 1024

TOKEN_TILE = 1024
EVEN_TOKEN_TILE = 512
PROJ_SEG = 512
FF_SEG = 1024
GDN_PREP_SUBTILE = 256
GDN_PREP_TILE = 512
GDN_SCAN_TILE = 256
GDN_SCAN_BATCH = 4
ATTN_TILE = 512
ATTN_UNROLL = 2
BIAS_LANES_PER_HEAD = 16
ONES_ROWS = 16


def _dot(a, b):
    return jnp.dot(a, b, preferred_element_type=F32)


def _split_bf16(x, parts):
    out = []
    for _ in range(parts):
        piece = x.astype(BF16)
        out.append(piece)
        x = x - piece.astype(F32)
    return out


def _dot_nt(a, b):
    return lax.dot_general(a, b, (((1,), (1,)), ((), ())), preferred_element_type=F32)


def _dot_tn(a, b):
    return lax.dot_general(a, b, (((0,), (0,)), ((), ())), preferred_element_type=F32)


def _rms(x, g):
    return x * lax.rsqrt(jnp.mean(x * x, axis=-1, keepdims=True) + EPS) * g


def _sigmoid(x):
    return 1.0 / (1.0 + jnp.exp(-x))


def _softplus(x):
    return jnp.maximum(x, 0.0) + jnp.log1p(jnp.exp(-jnp.abs(x)))


def _row_scan(x, period):
    rows = lax.broadcasted_iota(jnp.int32, x.shape, 0) % period
    s = 1
    while s < period:
        x = x + jnp.where(rows >= s, pltpu.roll(x, s, 0), 0.0)
        s *= 2
    return x


def _const_spec(shape):
    return pl.BlockSpec(shape, lambda *_: (0,) * len(shape))


def _params(sem):
    return pltpu.CompilerParams(dimension_semantics=sem, vmem_limit_bytes=VMEM_LIMIT_BYTES)


def _even_in_kernel(x_ref, g_ref, wm_ref, wg_ref, conv_ref, alog_ref, dt_ref, cos_ref, sin_ref,
                    qkv_ref, z_ref, qkb_ref, vbt_ref, gates_ref, h_ref, carry_ref, tr_ref, pad_ref, *, tiles_per_seq):
    tm = x_ref.shape[0]
    i = pl.program_id(0)
    h_ref[...] = _rms(x_ref[...], g_ref[...]).astype(BF16)
    seq_start = (i % tiles_per_seq) == 0
    seg = lambda s: slice(s * PROJ_SEG, (s + 1) * PROJ_SEG)
    project = lambda s: _dot(h_ref[...], wm_ref[:, seg(s)])

    def gdn_qkv(s, y):
        cols = seg(s)
        pad_ref[0:SUBLANES, :] = jnp.where(seq_start, 0.0, carry_ref[:, cols])
        pad_ref[SUBLANES:, :] = y
        carry_ref[:, cols] = y[tm - SUBLANES:, :]
        w = conv_ref[:, cols]
        a = y * w[CONV_WIDTH - 1:CONV_WIDTH, :]
        for k in range(1, CONV_WIDTH):
            a = a + pad_ref[SUBLANES - k:SUBLANES - k + tm, :] * w[CONV_WIDTH - 1 - k:CONV_WIDTH - k, :]
        a = a * _sigmoid(a)
        if s < 2:
            outs = []
            for hd in range(GDN_HEADS):
                blk = a[:, hd * HEAD_LANES:(hd + 1) * HEAD_LANES]
                n = blk * lax.rsqrt(jnp.sum(blk * blk, axis=-1, keepdims=True) + EPS)
                outs.append(n * (GDN_HEAD_DIM ** -0.5) if s == 0 else n)
            a = jnp.concatenate(outs, axis=1)
        qkv_ref[:, cols] = a

    def gdn_gate(s, y):
        z_ref[...] = y.astype(BF16)

    def diff_qk(s, y):
        cos = jnp.concatenate([cos_ref[...]] * (PROJ_SEG // LANES), axis=1)
        sin = jnp.concatenate([sin_ref[...]] * (PROJ_SEG // LANES), axis=1)
        lane = lax.broadcasted_iota(jnp.int32, (tm, PROJ_SEG), 1)
        first_half = (lane % DIFF_QK_DIM) < (DIFF_QK_DIM // 2)
        swapped = jnp.where(first_half, pltpu.roll(y, PROJ_SEG - DIFF_QK_DIM // 2, 1),
                            pltpu.roll(y, DIFF_QK_DIM // 2, 1))
        scale = DIFF_QK_DIM ** -0.5 * LOG2E if s == 4 else 1.0
        qkb_ref[:, seg(s - 4)] = ((y * cos + swapped * sin) * scale).astype(BF16)

    def diff_v(s, y):
        tr_ref[...] = y
        vbt_ref[0] = tr_ref[...].T.astype(BF16)

    stages = ((0, gdn_qkv), (3, gdn_gate), (1, gdn_qkv), (6, diff_v), (2, gdn_qkv), (4, diff_qk), (5, diff_qk))
    pending = project(stages[0][0])
    for n, (s, epilogue) in enumerate(stages):
        upcoming = project(stages[n + 1][0]) if n + 1 < len(stages) else _dot(h_ref[...], wg_ref[...])
        epilogue(s, pending)
        pending = upcoming

    graw = pending
    beta = _sigmoid(graw)
    g = -jnp.exp(alog_ref[...]) * _softplus(graw + dt_ref[...])
    gc = _row_scan(g, GDN_CHUNK)
    lane_g = lax.broadcasted_iota(jnp.int32, (tm, LANES), 1)
    gates_ref[...] = jnp.where(lane_g < GDN_HEADS, beta, gc)


def _even_in(x2d, g, wm, wg, conv_w, alog_row, dt_row, cos_t, sin_t, seq_len):
    m, d = x2d.shape
    tm = EVEN_TOKEN_TILE
    n_main = wm.shape[1]
    tps = seq_len // tm
    kern = functools.partial(_even_in_kernel, tiles_per_seq=tps)
    row = lambda i: (i, 0)
    return pl.pallas_call(
        kern,
        grid=(m // tm,),
        in_specs=[
            pl.BlockSpec((tm, d), row),
            _const_spec((1, d)),
            _const_spec((d, n_main)),
            _const_spec((d, LANES)),
            _const_spec(conv_w.shape),
            _const_spec((1, LANES)),
            _const_spec((1, LANES)),
            pl.BlockSpec((tm, LANES), row),
            pl.BlockSpec((tm, LANES), row),
        ],
        out_specs=[
            pl.BlockSpec((tm, 3 * PROJ_SEG), row),
            pl.BlockSpec((tm, PROJ_SEG), row),
            pl.BlockSpec((tm, 2 * PROJ_SEG), row),
            pl.BlockSpec((1, PROJ_SEG, tm), lambda i: (i // tps, 0, i % tps)),
            pl.BlockSpec((tm, LANES), row),
        ],
        out_shape=[
            jax.ShapeDtypeStruct((m, 3 * PROJ_SEG), F32),
            jax.ShapeDtypeStruct((m, PROJ_SEG), BF16),
            jax.ShapeDtypeStruct((m, 2 * PROJ_SEG), BF16),
            jax.ShapeDtypeStruct((m // seq_len, PROJ_SEG, seq_len), BF16),
            jax.ShapeDtypeStruct((m, LANES), F32),
        ],
        scratch_shapes=[pltpu.VMEM((tm, d), BF16), pltpu.VMEM((SUBLANES, 3 * PROJ_SEG), F32), pltpu.VMEM((tm, PROJ_SEG), F32),
                        pltpu.VMEM((tm + SUBLANES, PROJ_SEG), F32)],
        compiler_params=_params(("arbitrary",)),
        name="even_in_proj",
    )(x2d, g, wm, wg, conv_w, alog_row, dt_row, cos_t, sin_t)


def _odd_in_kernel(x_ref, g_ref, wm_ref, wf_ref, bf_ref, sel_ref, ones_ref, q_ref, k_ref, vt_ref, gate_ref, qb_ref, kb_ref,
                   h_ref, carry_ref, tr_ref, *, tiles_per_seq, d_mix):
    tm = x_ref.shape[0]
    i = pl.program_id(0)
    h_ref[...] = _rms(x_ref[...], g_ref[...]).astype(BF16)
    head_dim = d_mix // FOX_HEADS
    for o_ref, base, scale in ((q_ref, 0, head_dim ** -0.5 * LOG2E), (k_ref, d_mix, 1.0),
                               (vt_ref, 2 * d_mix, 1.0), (gate_ref, 3 * d_mix, 1.0)):
        for s in range(d_mix // PROJ_SEG):
            cols = slice(s * PROJ_SEG, (s + 1) * PROJ_SEG)
            y = _dot(h_ref[...], wm_ref[:, base + s * PROJ_SEG:base + (s + 1) * PROJ_SEG])
            if o_ref is vt_ref:
                tr_ref[...] = y
                o_ref[0, cols, :] = tr_ref[...].T.astype(BF16)
            else:
                o_ref[:, cols] = (y * scale).astype(BF16)
    f = _dot(h_ref[...], wf_ref[...]) + bf_ref[...]
    log_f = jnp.minimum(f, 0.0) - jnp.log1p(jnp.exp(-jnp.abs(f)))
    prev = jnp.where((i % tiles_per_seq) == 0, 0.0, carry_ref[0:1, :])
    cum = _row_scan(log_f, tm) + prev
    carry_ref[...] = jnp.broadcast_to(cum[tm - 1:tm, :], carry_ref.shape)
    pieces = jnp.concatenate(_split_bf16(LOG2E * cum, 3), axis=1)
    lanes = _dot(pieces, sel_ref[...]) + ones_ref[...]
    qb_ref[...] = lanes[:, :LANES].astype(BF16)
    kb_ref[...] = lanes[:, LANES:].astype(BF16)


def _bias_lane_tables():
    sel = np.zeros((3 * LANES, 2 * LANES), np.float32)
    ones = np.zeros((1, 2 * LANES), np.float32)
    for h in range(FOX_HEADS):
        base = BIAS_LANES_PER_HEAD * h
        for piece in range(3):
            sel[LANES * piece + h, base + 3 + piece] = 1.0
            sel[LANES * piece + h, LANES + base + piece] = -1.0
            ones[0, base + piece] = 1.0
            ones[0, LANES + base + 3 + piece] = 1.0
    return jnp.asarray(sel, BF16), jnp.asarray(ones, F32)


def _odd_in(x2d, g, wm, wf, bf_row, seq_len):
    m, d = x2d.shape
    sel, ones_row = _bias_lane_tables()
    tm = TOKEN_TILE
    d_mix = wm.shape[1] // 4
    tps = seq_len // tm
    kern = functools.partial(_odd_in_kernel, tiles_per_seq=tps, d_mix=d_mix)
    row = lambda i: (i, 0)
    row_blk = pl.BlockSpec((tm, d_mix), row)
    row_shape = jax.ShapeDtypeStruct((m, d_mix), BF16)
    return pl.pallas_call(
        kern,
        grid=(m // tm,),
        in_specs=[
            pl.BlockSpec((tm, d), row),
            _const_spec((1, d)),
            _const_spec(wm.shape),
            _const_spec((d, LANES)),
            _const_spec((1, LANES)),
            _const_spec(sel.shape),
            _const_spec(ones_row.shape),
        ],
        out_specs=[row_blk, row_blk, pl.BlockSpec((1, d_mix, tm), lambda i: (i // tps, 0, i % tps)), row_blk,
                   pl.BlockSpec((tm, LANES), row), pl.BlockSpec((tm, LANES), row)],
        out_shape=[row_shape, row_shape, jax.ShapeDtypeStruct((m // seq_len, d_mix, seq_len), BF16), row_shape,
                   jax.ShapeDtypeStruct((m, LANES), BF16), jax.ShapeDtypeStruct((m, LANES), BF16)],
        scratch_shapes=[pltpu.VMEM((tm, d), BF16), pltpu.VMEM((SUBLANES, LANES), F32), pltpu.VMEM((tm, PROJ_SEG), F32)],
        compiler_params=_params(("arbitrary",)),
        name="odd_in_proj",
    )(x2d, g, wm, wf, bf_row, sel, ones_row)


def _attn_kernel(*refs, tq, fox, lambda_init):
    if fox:
        q_ref, k_ref, vt_ref, gate_ref, qb_ref, kball_ref, o_ref, st_ref, m_ref, acc_ref, kb_ref = refs
    else:
        q_ref, k_ref, vt_ref, lam_ref, nw_ref, o_ref, st_ref, m_ref, acc_ref = refs
    tk = tq
    hg = pl.program_id(1)
    n_tiles = q_ref.shape[1] // tq
    head = lambda g: slice(g * HEAD_LANES, (g + 1) * HEAD_LANES)
    kv = [head(0), head(1)] if fox else [head(0), head(0)]
    tile_rows = lambda tile: pl.ds(pl.multiple_of(tile * tq, tq), tq)

    def queries(g, tile):
        if fox:
            return jnp.concatenate([q_ref[0, tile_rows(tile), head(g)], qb_ref[0, tile_rows(tile), :]], axis=1)
        q = q_ref[0, tile_rows(tile), :]
        lane = lax.broadcasted_iota(jnp.int32, q.shape, 1)
        keep = (lane < DIFF_QK_DIM) if g == 0 else (lane >= DIFF_QK_DIM)
        return jnp.where(keep, q, jnp.zeros_like(q))

    def scores_of(qmat, g, j):
        k0 = pl.multiple_of(j * tk, tk)
        kj = k_ref[0, pl.ds(k0, tk), kv[g]]
        if fox:
            kj = jnp.concatenate([kj, kb_ref[pl.ds(k0, tk), kv[g]]], axis=1)
        return _dot_nt(kj, qmat)

    if fox:
        kb_all = kball_ref[0]
        owner = lax.broadcasted_iota(jnp.int32, kb_all.shape, 1) // BIAS_LANES_PER_HEAD
        for g in range(2):
            kb_ref[:, head(g)] = jnp.where(owner == hg * 2 + g, kb_all, jnp.zeros_like(kb_all))
    else:
        lam_p = lam_ref[...]
        lam = (jnp.exp(jnp.sum(lam_p[0:1] * lam_p[1:2], axis=1, keepdims=True))
               - jnp.exp(jnp.sum(lam_p[2:3] * lam_p[3:4], axis=1, keepdims=True)) + lambda_init)
    st_ref[...] = scores_of(queries(0, 0), 0, 0)
    ones = jnp.ones((ONES_ROWS, tk), BF16)

    def tile(qi, carry):
        qs = [queries(g, qi) for g in range(2)]
        scores = lambda g, j: scores_of(qs[g], g, j)

        def absorb(g, j, st, masked):
            m = m_ref[g]
            k0 = pl.multiple_of(j * tk, tk)
            if masked:
                kpos = k0 + lax.broadcasted_iota(jnp.int32, (tk, tq), 0)
                qpos = qi * tq + lax.broadcasted_iota(jnp.int32, (tk, tq), 1)
                st = jnp.where(qpos >= kpos, st, NEG_INF)
            m_new = jnp.maximum(m, jnp.max(st, axis=0, keepdims=True))
            p = jnp.exp2(st - m_new).astype(BF16)
            vt = jnp.concatenate([vt_ref[0, kv[g], pl.ds(k0, tk)], ones], axis=0)
            acc_ref[g] = jnp.exp2(m - m_new) * acc_ref[g] + _dot(vt, p)
            m_ref[g] = m_new

        def step(j):
            st1 = scores(1, j)
            absorb(0, j, st_ref[...], False)
            st_ref[...] = scores(0, j + 1)
            absorb(1, j, st1, False)

        def steps(j, c):
            for u in range(ATTN_UNROLL):
                step(j * ATTN_UNROLL + u)
            return c

        def tail_step(j, c):
            step(j)
            return c

        m_ref[...] = jnp.full(m_ref.shape, NEG_INF, F32)
        acc_ref[...] = jnp.zeros(acc_ref.shape, F32)
        n_full = qi // ATTN_UNROLL
        lax.fori_loop(0, n_full, steps, 0)
        lax.fori_loop(n_full * ATTN_UNROLL, qi, tail_step, 0)
        st1 = scores(1, qi)
        absorb(0, qi, st_ref[...], True)
        absorb(1, qi, st1, True)
        st_ref[...] = scores_of(queries(0, jnp.minimum(qi + 1, n_tiles - 1)), 0, 0)
        outs = [acc_ref[g, :HEAD_LANES, :] / acc_ref[g, HEAD_LANES:HEAD_LANES + 1, :] for g in range(2)]
        rows = tile_rows(qi)
        if fox:
            for g in range(2):
                gate = _sigmoid(gate_ref[0, rows, head(g)].astype(F32))
                o_ref[0, rows, head(g)] = (outs[g].T * gate).astype(o_ref.dtype)
        else:
            o = (outs[0] - lam * outs[1]).T
            o_ref[0, rows, :] = (_rms(o, nw_ref[...]) * (1.0 - lambda_init)).astype(o_ref.dtype)
        return carry

    lax.fori_loop(0, n_tiles, tile, 0)


def _attn_state(tq):
    return [pltpu.VMEM((tq, tq), F32), pltpu.VMEM((2, 1, tq), F32), pltpu.VMEM((2, HEAD_LANES + ONES_ROWS, tq), F32)]


def _fox_attention(q, k, vt, gate, qb, kb, *, tq=ATTN_TILE):
    b, t, dm = q.shape
    width = 2 * HEAD_LANES
    kern = functools.partial(_attn_kernel, tq=tq, fox=True, lambda_init=0.0)
    seq = pl.BlockSpec((1, t, width), lambda bi, h: (bi, 0, h))
    seq_bias = pl.BlockSpec((1, t, LANES), lambda bi, h: (bi, 0, 0))
    return pl.pallas_call(
        kern,
        grid=(b, dm // width),
        in_specs=[seq, seq, pl.BlockSpec((1, width, t), lambda bi, h: (bi, h, 0)), seq, seq_bias, seq_bias],
        out_specs=seq,
        out_shape=jax.ShapeDtypeStruct((b, t, dm), BF16),
        scratch_shapes=_attn_state(tq) + [pltpu.VMEM((t, width), BF16)],
        compiler_params=_params(("arbitrary", "arbitrary")),
        name="fox_attention",
    )(q, k, vt, gate, qb, kb)


def _diff_attention(qk, vt, lam_params, norm_w, lambda_init, *, tq=ATTN_TILE):
    b, t, _ = qk.shape
    nh = DIFF_HEADS
    kern = functools.partial(_attn_kernel, tq=tq, fox=False, lambda_init=lambda_init)
    seq = pl.BlockSpec((1, t, HEAD_LANES), lambda bi, h: (bi, 0, h))
    return pl.pallas_call(
        kern,
        grid=(b, nh),
        in_specs=[seq,
                  pl.BlockSpec((1, t, HEAD_LANES), lambda bi, h: (bi, 0, nh + h)),
                  pl.BlockSpec((1, HEAD_LANES, t), lambda bi, h: (bi, h, 0)),
                  _const_spec(lam_params.shape), _const_spec((1, HEAD_LANES))],
        out_specs=seq,
        out_shape=jax.ShapeDtypeStruct((b, t, nh * HEAD_LANES), BF16),
        scratch_shapes=_attn_state(tq),
        compiler_params=_params(("arbitrary", "arbitrary")),
        name="diff_attention",
    )(qk, qk, vt, lam_params, norm_w)


def _gdn_prep_kernel(q_ref, k_ref, v_ref, gates_ref, u_ref, w_ref, qd_ref, kd_ref, qk_ref):
    c = GDN_CHUNK
    sub = GDN_PREP_SUBTILE
    units = [(slice(t0, t0 + sub), hd) for t0 in range(0, q_ref.shape[1], sub) for hd in range(GDN_HEADS)]
    ids = range(len(units))
    col = lambda hd: slice(hd * HEAD_LANES, (hd + 1) * HEAD_LANES)
    lane = lax.broadcasted_iota(jnp.int32, (sub, LANES), 1)
    ri = lax.broadcasted_iota(jnp.int32, (sub, sub), 0)
    ci = lax.broadcasted_iota(jnp.int32, (sub, sub), 1)
    chunk_start = ri - ri % c
    incl = lambda a: jnp.where(ci <= ri, jnp.where(ci >= chunk_start, a, 0.0), 0.0)
    strict = lambda a: jnp.where(ci < ri, jnp.where(ci >= chunk_start, a, 0.0), 0.0)
    ident = jnp.where(ri == ci, 1.0, 0.0)
    ones = jnp.ones((sub, LANES), BF16)
    kt = [k_ref[0, rows, col(hd)] for rows, hd in units]
    beta = [gates_ref[0, rows, hd:hd + 1] for rows, hd in units]
    gcc = [gates_ref[0, rows, GDN_HEADS + hd:GDN_HEADS + hd + 1] for rows, hd in units]
    k16 = [kt[i].astype(BF16) for i in ids]
    kb = [kt[i] * beta[i] for i in ids]

    gc_row = []
    for i in ids:
        g_hi, g_mid, g_lo = (piece.astype(F32) for piece in _split_bf16(gcc[i], 3))
        pieces = jnp.where(lane == 0, g_hi, jnp.where(lane == 1, g_mid, jnp.where(lane == 2, g_lo, 0.0)))
        gc_row.append(_dot_nt(ones, pieces.astype(BF16)))
    kk = [_dot_nt(kb[i].astype(BF16), k16[i]) for i in ids]
    qk_raw = [_dot_nt(q_ref[0, rows, col(hd)].astype(BF16), k16[i]) for i, (rows, hd) in enumerate(units)]
    decay = [incl(jnp.exp(incl(gcc[i] - gc_row[i]))) for i in ids]
    lower = [strict(kk[i] * decay[i]) for i in ids]

    span = ri ^ ci
    inv = [ident - jnp.where(span == 1, lower[i], 0.0) for i in ids]
    s_blk = 2
    while s_blk < c:
        shift = int(math.log2(s_blk))
        inv16 = [inv[i].astype(BF16) for i in ids]
        coupled = [_dot(jnp.where((span >> shift) == 1, lower[i], 0.0).astype(BF16), inv16[i]) for i in ids]
        inv = [inv[i] - _dot(inv16[i], coupled[i].astype(BF16)) for i in ids]
        s_blk *= 2
    inv16 = [inv[i].astype(BF16) for i in ids]

    eg = [jnp.exp(gcc[i]) for i in ids]
    rhs = [jnp.concatenate([v_ref[0, rows, col(hd)] * beta[i], kb[i] * eg[i]], axis=1)
           for i, (rows, hd) in enumerate(units)]
    sol = [_dot(inv16[i], rhs[i].astype(BF16)) for i in ids]
    a_hi, a_lo, s_hi, s_lo = [], [], [], []
    for i in ids:
        hi, lo = _split_bf16(ident + lower[i], 2)
        a_hi.append(hi)
        a_lo.append(lo)
        hi, lo = _split_bf16(sol[i], 2)
        s_hi.append(hi)
        s_lo.append(lo)
    prod = [_dot(a_hi[i], s_hi[i]) + (_dot(a_hi[i], s_lo[i]) + _dot(a_lo[i], s_hi[i])) for i in ids]
    corr = [_dot(inv16[i], (rhs[i] - prod[i]).astype(BF16)) for i in ids]
    sol = [sol[i] + corr[i] for i in ids]
    for i, (rows, hd) in enumerate(units):
        u_ref[0, rows, col(hd)] = sol[i][:, :HEAD_LANES]
        w_ref[0, rows, col(hd)] = sol[i][:, HEAD_LANES:].astype(BF16)
        qk = incl(qk_raw[i] * decay[i])
        qd_ref[0, rows, col(hd)] = (q_ref[0, rows, col(hd)] * eg[i]).astype(BF16)
        for n in range(sub // c):
            blk = slice(n * c, (n + 1) * c)
            out_rows = slice(rows.start + n * c, rows.start + (n + 1) * c)
            qk_ref[0, hd, out_rows, :] = qk[blk, blk].astype(BF16)
            gl = gcc[i][(n + 1) * c - 1:(n + 1) * c, :]
            kd_ref[0, out_rows, col(hd)] = (kt[i][blk] * jnp.exp(gl - gcc[i][blk])).astype(BF16)


def _gdn_prep(qkv, gates, *, tt=GDN_PREP_TILE):
    b, t, _ = qkv.shape
    nh = GDN_HEADS
    dm = nh * HEAD_LANES
    blk = lambda part: pl.BlockSpec((1, tt, dm), lambda bi, i: (bi, i, part))
    return pl.pallas_call(
        _gdn_prep_kernel,
        grid=(b, t // tt),
        in_specs=[blk(0), blk(1), blk(2), pl.BlockSpec((1, tt, LANES), lambda bi, i: (bi, i, 0))],
        out_specs=[blk(0)] * 4 + [pl.BlockSpec((1, nh, tt, GDN_CHUNK), lambda bi, i: (bi, 0, i, 0))],
        out_shape=[jax.ShapeDtypeStruct((b, t, dm), F32)]
        + [jax.ShapeDtypeStruct((b, t, dm), BF16)] * 3
        + [jax.ShapeDtypeStruct((b, nh, t, GDN_CHUNK), BF16)],
        compiler_params=_params(("arbitrary", "arbitrary")),
        name="gdn_prep",
    )(qkv, qkv, qkv, gates)


def _gdn_scan_kernel(u_ref, w_ref, qd_ref, kd_ref, qk_ref, gates_ref, z_ref, nw_ref, o_ref, s_ref):
    c = GDN_CHUNK
    nb, tt = u_ref.shape[0], u_ref.shape[1]

    @pl.when(pl.program_id(1) == 0)
    def _():
        s_ref[...] = jnp.zeros_like(s_ref)

    chains = [(bi, hd) for bi in range(nb) for hd in range(GDN_HEADS)]
    cols = [slice(hd * HEAD_LANES, (hd + 1) * HEAD_LANES) for hd in range(GDN_HEADS)]
    state = [s_ref[bi, hd] for bi, hd in chains]
    for n in range(tt // c):
        rows = slice(n * c, (n + 1) * c)
        r = [_dot(jnp.concatenate([w_ref[bi, rows, cols[hd]], qd_ref[bi, rows, cols[hd]]], axis=0),
                  state[i].astype(BF16)) for i, (bi, hd) in enumerate(chains)]
        v_new = [(u_ref[bi, rows, cols[hd]] - r[i][:c]).astype(BF16) for i, (bi, hd) in enumerate(chains)]
        intra = [_dot(qk_ref[bi, hd, rows, :], v_new[i]) for i, (bi, hd) in enumerate(chains)]
        upd = [_dot_tn(kd_ref[bi, rows, cols[hd]], v_new[i]) for i, (bi, hd) in enumerate(chains)]
        for i, (bi, hd) in enumerate(chains):
            last = (n + 1) * c - 1
            decay_last = jnp.exp(gates_ref[bi, last:last + 1, GDN_HEADS + hd:GDN_HEADS + hd + 1])
            state[i] = state[i] * decay_last + upd[i]
            zt = z_ref[bi, rows, cols[hd]].astype(F32)
            o = r[i][c:] + intra[i]
            o_ref[bi, rows, cols[hd]] = (_rms(o, nw_ref[...]) * (zt * _sigmoid(zt))).astype(o_ref.dtype)
    for i, (bi, hd) in enumerate(chains):
        s_ref[bi, hd] = state[i]


def _gdn_scan(u, w, qd, kd, qk, gates, z, norm_w, *, tt=GDN_SCAN_TILE, nb=GDN_SCAN_BATCH):
    b, t, dm = u.shape
    nh = GDN_HEADS
    assert b % nb == 0 and t % tt == 0
    blk = pl.BlockSpec((nb, tt, dm), lambda bi, i: (bi, i, 0))
    return pl.pallas_call(
        _gdn_scan_kernel,
        grid=(b // nb, t // tt),
        in_specs=[blk, blk, blk, blk,
                  pl.BlockSpec((nb, nh, tt, GDN_CHUNK), lambda bi, i: (bi, 0, i, 0)),
                  pl.BlockSpec((nb, tt, LANES), lambda bi, i: (bi, i, 0)),
                  blk, _const_spec((1, HEAD_LANES))],
        out_specs=blk,
        out_shape=jax.ShapeDtypeStruct((b, t, dm), BF16),
        scratch_shapes=[pltpu.VMEM((nb, nh, GDN_HEAD_DIM, GDN_HEAD_DIM), F32)],
        compiler_params=_params(("arbitrary", "arbitrary")),
        name="gdn_scan",
    )(u, w, qd, kd, qk, gates, z, norm_w)


def _post_kernel(*refs, n_mix, final_norm):
    x_ref = refs[0]
    mix_refs = refs[1:1 + n_mix]
    wout_ref, g_ref, wup_ref, wdn_ref, p_ref, wpp_ref, wpg_ref = refs[1 + n_mix:8 + n_mix]
    rest = refs[8 + n_mix:]
    if final_norm:
        gf_ref, o_ref = rest
    else:
        (o_ref,) = rest
    mix = mix_refs[0][...] if n_mix == 1 else jnp.concatenate([r[...] for r in mix_refs], axis=1)
    x = x_ref[...] + _dot(mix, wout_ref[...])
    h = _rms(x, g_ref[...]).astype(BF16)
    d_ff = wup_ref.shape[1]
    acc = x
    for s in range(d_ff // FF_SEG):
        a = jnp.maximum(_dot(h, wup_ref[:, s * FF_SEG:(s + 1) * FF_SEG]), 0.0)
        acc = acc + _dot((a * a).astype(BF16), wdn_ref[s * FF_SEG:(s + 1) * FF_SEG, :])
    x = acc
    gate = _sigmoid(_dot(x.astype(BF16), wpg_ref[...]))
    x = x + _dot(p_ref[...].astype(BF16), wpp_ref[...]) * gate
    if final_norm:
        x = _rms(x, gf_ref[...])
    o_ref[...] = x


def _post(x2d, mixes, wout, g, wup, wdn, p2d, wpp, wpg, gf=None):
    m, d = x2d.shape
    tm = TOKEN_TILE
    row = lambda i: (i, 0)
    single = pl.Buffered(1)
    const = lambda a: pl.BlockSpec(a.shape, lambda i: (0, 0), pipeline_mode=single)
    args = [x2d, *mixes, wout, g, wup, wdn, p2d, wpp, wpg]
    in_specs = ([pl.BlockSpec((tm, d), row)]
                + [pl.BlockSpec((tm, a.shape[1]), row) for a in mixes]
                + [const(wout), const(g), const(wup), const(wdn), pl.BlockSpec((tm, p2d.shape[1]), row), const(wpp), const(wpg)])
    if gf is not None:
        args.append(gf)
        in_specs.append(const(gf))
    kern = functools.partial(_post_kernel, n_mix=len(mixes), final_norm=gf is not None)
    return pl.pallas_call(
        kern,
        grid=(m // tm,),
        in_specs=in_specs,
        out_specs=pl.BlockSpec((tm, d), row),
        out_shape=jax.ShapeDtypeStruct((m, d), F32),
        compiler_params=_params(("arbitrary",)),
        name="out_proj_mlp_ple",
    )(*args)


def _pad_lanes(a):
    return jnp.pad(a, ((0, 0), (0, LANES - a.shape[1])))


def kernel(x, p, positions, norm_mix, norm_mlp, norm_final, w_in_even, conv_w, a_log, dt_bias, gdn_norm,
           lam_q1, lam_k1, lam_q2, lam_k2, diff_norm, w_out_even, w_in_odd, b_forget, w_out_odd,
           w_mlp_up, w_mlp_down, w_ple_proj, w_ple_gate):
    b, t, d = x.shape
    depth = p.shape[0]
    m = b * t
    assert t % TOKEN_TILE == 0 and d % PROJ_SEG == 0
    nh = GDN_HEADS
    gdn_w = 3 * nh * GDN_HEAD_DIM + nh * GDN_HEAD_DIM
    assert w_in_even.shape[2] == gdn_w + 2 * nh + 3 * DIFF_HEADS * 2 * DIFF_QK_DIM

    inv_freq = ROPE_THETA ** (-jnp.arange(0, DIFF_QK_DIM, 2, dtype=F32) / DIFF_QK_DIM)
    ang = positions.astype(F32)[..., None] * inv_freq
    cos, sin = jnp.cos(ang), jnp.sin(ang)
    cos_t = jnp.concatenate([cos, cos, cos, cos], axis=-1).reshape(m, LANES)
    sin_t = jnp.concatenate([-sin, sin, -sin, sin], axis=-1).reshape(m, LANES)

    x2d = x.reshape(m, d)
    for i in range(depth):
        j = i // 2
        g_mix = norm_mix[i].reshape(1, d)
        if i % 2 == 0:
            lambda_init = 0.8 - 0.6 * math.exp(-0.3 * i)
            w = w_in_even[j]
            wm = jnp.concatenate([w[:, :gdn_w], w[:, gdn_w + 2 * nh:]], axis=1).astype(BF16)
            wg = _pad_lanes(w[:, gdn_w:gdn_w + 2 * nh]).astype(BF16)
            alog_row = _pad_lanes(jnp.concatenate([jnp.zeros((nh,), F32), a_log[j]]).reshape(1, 2 * nh))
            dt_row = _pad_lanes(jnp.concatenate([jnp.zeros((nh,), F32), dt_bias[j]]).reshape(1, 2 * nh))
            qkv, z, qkb, vbt, gates = _even_in(x2d, g_mix, wm, wg, conv_w[j], alog_row, dt_row, cos_t, sin_t, t)
            qkv, z, qkb, gates = (a.reshape(b, t, -1) for a in (qkv, z, qkb, gates))
            u, wy, qd, kd, qk = _gdn_prep(qkv, gates)
            o_a = _gdn_scan(u, wy, qd, kd, qk, gates, z, gdn_norm[j].reshape(1, HEAD_LANES))
            lam_params = jnp.stack([lam_q1[j], lam_k1[j], lam_q2[j], lam_k2[j]])
            o_b = _diff_attention(qkb, vbt, lam_params, diff_norm[j].reshape(1, HEAD_LANES), lambda_init)
            mixes = [o_a.reshape(m, -1), o_b.reshape(m, -1)]
            wout = w_out_even[j].astype(BF16)
        else:
            w = w_in_odd[j]
            d_mix = (w.shape[1] - FOX_HEADS) // 4
            wm = w[:, :4 * d_mix].astype(BF16)
            wf = _pad_lanes(w[:, 4 * d_mix:]).astype(BF16)
            bf_row = _pad_lanes(b_forget[j].reshape(1, FOX_HEADS))
            q, k, vt, gate, qb, kb = _odd_in(x2d, g_mix, wm, wf, bf_row, t)
            q, k, gate, qb, kb = (a.reshape(b, t, -1) for a in (q, k, gate, qb, kb))
            o = _fox_attention(q, k, vt, gate, qb, kb)
            mixes = [o.reshape(m, -1)]
            wout = w_out_odd[j].astype(BF16)
        x2d = _post(x2d, mixes, wout, norm_mlp[i].reshape(1, d), w_mlp_up[i].astype(BF16),
                    w_mlp_down[i].astype(BF16), p[i].reshape(m, -1), w_ple_proj[i].astype(BF16),
                    w_ple_gate[i].astype(BF16), norm_final.reshape(1, d) if i == depth - 1 else None)
    return x2d.reshape(b, t, d)
```

```python
import functools
import math

import jax
import jax.numpy as jnp
import numpy as np
from jax import lax
from jax.experimental import pallas as pl
from jax.experimental.pallas import tpu as pltpu

F32 = jnp.float32
BF16 = jnp.bfloat16

GDN_HEADS = 4
GDN_HEAD_DIM = 128
GDN_CHUNK = 64
CONV_WIDTH = 4
DIFF_HEADS = 4
DIFF_QK_DIM = 64
FOX_HEADS = 8
HEAD_LANES = 128
ROPE_THETA = 10000.0
EPS = 1e-6
NEG_INF = -1e30
LOG2E = 1.4426950408889634
LANES = 128
SUBLANES = 8
VMEM_LIMIT_BYTES = 56 * 1024 * 1024

TOKEN_TILE = 1024
EVEN_TOKEN_TILE = 512
PROJ_SEG = 512
FF_SEG = 1024
GDN_PREP_SUBTILE = 256
GDN_PREP_TILE = 512
GDN_SCAN_TILE = 256
GDN_SCAN_BATCH = 4
ATTN_TILE = 512
ATTN_COLUMN_SPLIT = 2
ATTN_UNROLL = 2
BIAS_LANES_PER_HEAD = 16
ONES_ROWS = 16


def _dot(a, b):
    return jnp.dot(a, b, preferred_element_type=F32)


def _split_bf16(x, parts):
    out = []
    for _ in range(parts):
        piece = x.astype(BF16)
        out.append(piece)
        x = x - piece.astype(F32)
    return out


def _dot_nt(a, b):
    return lax.dot_general(a, b, (((1,), (1,)), ((), ())), preferred_element_type=F32)


def _dot_tn(a, b):
    return lax.dot_general(a, b, (((0,), (0,)), ((), ())), preferred_element_type=F32)


def _rms(x, g):
    return x * lax.rsqrt(jnp.mean(x * x, axis=-1, keepdims=True) + EPS) * g


def _sigmoid(x):
    return 1.0 / (1.0 + jnp.exp(-x))


def _softplus(x):
    return jnp.maximum(x, 0.0) + jnp.log1p(jnp.exp(-jnp.abs(x)))


def _row_scan(x, period):
    rows = lax.broadcasted_iota(jnp.int32, x.shape, 0) % period
    s = 1
    while s < period:
        x = x + jnp.where(rows >= s, pltpu.roll(x, s, 0), 0.0)
        s *= 2
    return x


def _const_spec(shape):
    return pl.BlockSpec(shape, lambda *_: (0,) * len(shape))


def _params(sem):
    return pltpu.CompilerParams(dimension_semantics=sem, vmem_limit_bytes=VMEM_LIMIT_BYTES)


def _even_in_kernel(x_ref, g_ref, wm_ref, wg_ref, conv_ref, alog_ref, dt_ref, cos_ref, sin_ref,
                    qkv_ref, z_ref, qkb_ref, vbt_ref, gates_ref, h_ref, carry_ref, tr_ref, pad_ref, *, tiles_per_seq):
    tm = x_ref.shape[0]
    i = pl.program_id(0)
    h_ref[...] = _rms(x_ref[...], g_ref[...]).astype(BF16)
    seq_start = (i % tiles_per_seq) == 0
    seg = lambda s: slice(s * PROJ_SEG, (s + 1) * PROJ_SEG)
    project = lambda s: _dot(h_ref[...], wm_ref[:, seg(s)])

    def gdn_qkv(s, y):
        cols = seg(s)
        pad_ref[0:SUBLANES, :] = jnp.where(seq_start, 0.0, carry_ref[:, cols])
        pad_ref[SUBLANES:, :] = y
        carry_ref[:, cols] = y[tm - SUBLANES:, :]
        w = conv_ref[:, cols]
        a = y * w[CONV_WIDTH - 1:CONV_WIDTH, :]
        for k in range(1, CONV_WIDTH):
            a = a + pad_ref[SUBLANES - k:SUBLANES - k + tm, :] * w[CONV_WIDTH - 1 - k:CONV_WIDTH - k, :]
        a = a * _sigmoid(a)
        if s < 2:
            outs = []
            for hd in range(GDN_HEADS):
                blk = a[:, hd * HEAD_LANES:(hd + 1) * HEAD_LANES]
                n = blk * lax.rsqrt(jnp.sum(blk * blk, axis=-1, keepdims=True) + EPS)
                outs.append(n * (GDN_HEAD_DIM ** -0.5) if s == 0 else n)
            a = jnp.concatenate(outs, axis=1)
        qkv_ref[:, cols] = a

    def gdn_gate(s, y):
        z_ref[...] = y.astype(BF16)

    def diff_qk(s, y):
        cos = jnp.concatenate([cos_ref[...]] * (PROJ_SEG // LANES), axis=1)
        sin = jnp.concatenate([sin_ref[...]] * (PROJ_SEG // LANES), axis=1)
        lane = lax.broadcasted_iota(jnp.int32, (tm, PROJ_SEG), 1)
        first_half = (lane % DIFF_QK_DIM) < (DIFF_QK_DIM // 2)
        swapped = jnp.where(first_half, pltpu.roll(y, PROJ_SEG - DIFF_QK_DIM // 2, 1),
                            pltpu.roll(y, DIFF_QK_DIM // 2, 1))
        scale = DIFF_QK_DIM ** -0.5 * LOG2E if s == 4 else 1.0
        qkb_ref[:, seg(s - 4)] = ((y * cos + swapped * sin) * scale).astype(BF16)

    def diff_v(s, y):
        tr_ref[...] = y
        vbt_ref[0] = tr_ref[...].T.astype(BF16)

    stages = ((0, gdn_qkv), (3, gdn_gate), (1, gdn_qkv), (6, diff_v), (2, gdn_qkv), (4, diff_qk), (5, diff_qk))
    pending = project(stages[0][0])
    for n, (s, epilogue) in enumerate(stages):
        upcoming = project(stages[n + 1][0]) if n + 1 < len(stages) else _dot(h_ref[...], wg_ref[...])
        epilogue(s, pending)
        pending = upcoming

    graw = pending
    beta = _sigmoid(graw)
    g = -jnp.exp(alog_ref[...]) * _softplus(graw + dt_ref[...])
    gc = _row_scan(g, GDN_CHUNK)
    lane_g = lax.broadcasted_iota(jnp.int32, (tm, LANES), 1)
    gates_ref[...] = jnp.where(lane_g < GDN_HEADS, beta, gc)


def _even_in(x2d, g, wm, wg, conv_w, alog_row, dt_row, cos_t, sin_t, seq_len):
    m, d = x2d.shape
    tm = EVEN_TOKEN_TILE
    n_main = wm.shape[1]
    tps = seq_len // tm
    kern = functools.partial(_even_in_kernel, tiles_per_seq=tps)
    row = lambda i: (i, 0)
    return pl.pallas_call(
        kern,
        grid=(m // tm,),
        in_specs=[
            pl.BlockSpec((tm, d), row),
            _const_spec((1, d)),
            _const_spec((d, n_main)),
            _const_spec((d, LANES)),
            _const_spec(conv_w.shape),
            _const_spec((1, LANES)),
            _const_spec((1, LANES)),
            pl.BlockSpec((tm, LANES), row),
            pl.BlockSpec((tm, LANES), row),
        ],
        out_specs=[
            pl.BlockSpec((tm, 3 * PROJ_SEG), row),
            pl.BlockSpec((tm, PROJ_SEG), row),
            pl.BlockSpec((tm, 2 * PROJ_SEG), row),
            pl.BlockSpec((1, PROJ_SEG, tm), lambda i: (i // tps, 0, i % tps)),
            pl.BlockSpec((tm, LANES), row),
        ],
        out_shape=[
            jax.ShapeDtypeStruct((m, 3 * PROJ_SEG), F32),
            jax.ShapeDtypeStruct((m, PROJ_SEG), BF16),
            jax.ShapeDtypeStruct((m, 2 * PROJ_SEG), BF16),
            jax.ShapeDtypeStruct((m // seq_len, PROJ_SEG, seq_len), BF16),
            jax.ShapeDtypeStruct((m, LANES), F32),
        ],
        scratch_shapes=[pltpu.VMEM((tm, d), BF16), pltpu.VMEM((SUBLANES, 3 * PROJ_SEG), F32), pltpu.VMEM((tm, PROJ_SEG), F32),
                        pltpu.VMEM((tm + SUBLANES, PROJ_SEG), F32)],
        compiler_params=_params(("arbitrary",)),
        name="even_in_proj",
    )(x2d, g, wm, wg, conv_w, alog_row, dt_row, cos_t, sin_t)


def _odd_in_kernel(x_ref, g_ref, wm_ref, wf_ref, bf_ref, sel_ref, ones_ref, q_ref, k_ref, vt_ref, gate_ref, qb_ref, kb_ref,
                   h_ref, carry_ref, tr_ref, *, tiles_per_seq, d_mix):
    tm = x_ref.shape[0]
    i = pl.program_id(0)
    h_ref[...] = _rms(x_ref[...], g_ref[...]).astype(BF16)
    head_dim = d_mix // FOX_HEADS
    for o_ref, base, scale in ((q_ref, 0, head_dim ** -0.5 * LOG2E), (k_ref, d_mix, 1.0),
                               (vt_ref, 2 * d_mix, 1.0), (gate_ref, 3 * d_mix, 1.0)):
        for s in range(d_mix // PROJ_SEG):
            cols = slice(s * PROJ_SEG, (s + 1) * PROJ_SEG)
            y = _dot(h_ref[...], wm_ref[:, base + s * PROJ_SEG:base + (s + 1) * PROJ_SEG])
            if o_ref is vt_ref:
                tr_ref[...] = y
                o_ref[0, cols, :] = tr_ref[...].T.astype(BF16)
            else:
                o_ref[:, cols] = (y * scale).astype(BF16)
    f = _dot(h_ref[...], wf_ref[...]) + bf_ref[...]
    log_f = jnp.minimum(f, 0.0) - jnp.log1p(jnp.exp(-jnp.abs(f)))
    prev = jnp.where((i % tiles_per_seq) == 0, 0.0, carry_ref[0:1, :])
    cum = _row_scan(log_f, tm) + prev
    carry_ref[...] = jnp.broadcast_to(cum[tm - 1:tm, :], carry_ref.shape)
    pieces = jnp.concatenate(_split_bf16(LOG2E * cum, 3), axis=1)
    lanes = _dot(pieces, sel_ref[...]) + ones_ref[...]
    qb_ref[...] = lanes[:, :LANES].astype(BF16)
    kb_ref[...] = lanes[:, LANES:].astype(BF16)


def _bias_lane_tables():
    sel = np.zeros((3 * LANES, 2 * LANES), np.float32)
    ones = np.zeros((1, 2 * LANES), np.float32)
    for h in range(FOX_HEADS):
        base = BIAS_LANES_PER_HEAD * h
        for piece in range(3):
            sel[LANES * piece + h, base + 3 + piece] = 1.0
            sel[LANES * piece + h, LANES + base + piece] = -1.0
            ones[0, base + piece] = 1.0
            ones[0, LANES + base + 3 + piece] = 1.0
    return jnp.asarray(sel, BF16), jnp.asarray(ones, F32)


def _odd_in(x2d, g, wm, wf, bf_row, seq_len):
    m, d = x2d.shape
    sel, ones_row = _bias_lane_tables()
    tm = TOKEN_TILE
    d_mix = wm.shape[1] // 4
    tps = seq_len // tm
    kern = functools.partial(_odd_in_kernel, tiles_per_seq=tps, d_mix=d_mix)
    row = lambda i: (i, 0)
    row_blk = pl.BlockSpec((tm, d_mix), row)
    row_shape = jax.ShapeDtypeStruct((m, d_mix), BF16)
    return pl.pallas_call(
        kern,
        grid=(m // tm,),
        in_specs=[
            pl.BlockSpec((tm, d), row),
            _const_spec((1, d)),
            _const_spec(wm.shape),
            _const_spec((d, LANES)),
            _const_spec((1, LANES)),
            _const_spec(sel.shape),
            _const_spec(ones_row.shape),
        ],
        out_specs=[row_blk, row_blk, pl.BlockSpec((1, d_mix, tm), lambda i: (i // tps, 0, i % tps)), row_blk,
                   pl.BlockSpec((tm, LANES), row), pl.BlockSpec((tm, LANES), row)],
        out_shape=[row_shape, row_shape, jax.ShapeDtypeStruct((m // seq_len, d_mix, seq_len), BF16), row_shape,
                   jax.ShapeDtypeStruct((m, LANES), BF16), jax.ShapeDtypeStruct((m, LANES), BF16)],
        scratch_shapes=[pltpu.VMEM((tm, d), BF16), pltpu.VMEM((SUBLANES, LANES), F32), pltpu.VMEM((tm, PROJ_SEG), F32)],
        compiler_params=_params(("arbitrary",)),
        name="odd_in_proj",
    )(x2d, g, wm, wf, bf_row, sel, ones_row)


def _attn_kernel(*refs, tq, fox, lambda_init):
    if fox:
        q_ref, k_ref, vt_ref, gate_ref, qb_ref, kball_ref, o_ref, st_ref, m_ref, acc_ref, kb_ref = refs
    else:
        q_ref, k_ref, vt_ref, lam_ref, nw_ref, o_ref, st_ref, m_ref, acc_ref = refs
    tk = tq
    hg = pl.program_id(1)
    n_tiles = q_ref.shape[1] // tq
    head = lambda g: slice(g * HEAD_LANES, (g + 1) * HEAD_LANES)
    kv = [head(0), head(1)] if fox else [head(0), head(0)]
    tile_rows = lambda tile: pl.ds(pl.multiple_of(tile * tq, tq), tq)

    def queries(g, tile):
        if fox:
            return jnp.concatenate([q_ref[0, tile_rows(tile), head(g)], qb_ref[0, tile_rows(tile), :]], axis=1)
        q = q_ref[0, tile_rows(tile), :]
        lane = lax.broadcasted_iota(jnp.int32, q.shape, 1)
        keep = (lane < DIFF_QK_DIM) if g == 0 else (lane >= DIFF_QK_DIM)
        return jnp.where(keep, q, jnp.zeros_like(q))

    def scores_of(qmat, g, j):
        k0 = pl.multiple_of(j * tk, tk)
        kj = k_ref[0, pl.ds(k0, tk), kv[g]]
        if fox:
            kj = jnp.concatenate([kj, kb_ref[pl.ds(k0, tk), kv[g]]], axis=1)
        return _dot_nt(kj, qmat)

    if fox:
        kb_all = kball_ref[0]
        owner = lax.broadcasted_iota(jnp.int32, kb_all.shape, 1) // BIAS_LANES_PER_HEAD
        for g in range(2):
            kb_ref[:, head(g)] = jnp.where(owner == hg * 2 + g, kb_all, jnp.zeros_like(kb_all))
    else:
        lam_p = lam_ref[...]
        lam = (jnp.exp(jnp.sum(lam_p[0:1] * lam_p[1:2], axis=1, keepdims=True))
               - jnp.exp(jnp.sum(lam_p[2:3] * lam_p[3:4], axis=1, keepdims=True)) + lambda_init)
    st_ref[...] = scores_of(queries(0, 0), 0, 0)
    ones = jnp.ones((ONES_ROWS, tk), BF16)

    def tile(qi, carry):
        qs = [queries(g, qi) for g in range(2)]
        scores = lambda g, j: scores_of(qs[g], g, j)

        def absorb(g, j, st, masked):
            k0 = pl.multiple_of(j * tk, tk)
            if masked:
                kpos = k0 + lax.broadcasted_iota(jnp.int32, (tk, tq), 0)
                qpos = qi * tq + lax.broadcasted_iota(jnp.int32, (tk, tq), 1)
                st = jnp.where(qpos >= kpos, st, NEG_INF)
            vt = jnp.concatenate([vt_ref[0, kv[g], pl.ds(k0, tk)], ones], axis=0)
            width = tq // ATTN_COLUMN_SPLIT
            for part in range(ATTN_COLUMN_SPLIT):
                cs = slice(part * width, (part + 1) * width)
                m = m_ref[g, :, cs]
                m_new = jnp.maximum(m, jnp.max(st[:, cs], axis=0, keepdims=True))
                p = jnp.exp2(st[:, cs] - m_new).astype(BF16)
                acc_ref[g, :, cs] = jnp.exp2(m - m_new) * acc_ref[g, :, cs] + _dot(vt, p)
                m_ref[g, :, cs] = m_new

        def step(j):
            st1 = scores(1, j)
            absorb(0, j, st_ref[...], False)
            st_ref[...] = scores(0, j + 1)
            absorb(1, j, st1, False)

        def steps(j, c):
            for u in range(ATTN_UNROLL):
                step(j * ATTN_UNROLL + u)
            return c

        def tail_step(j, c):
            step(j)
            return c

        m_ref[...] = jnp.full(m_ref.shape, NEG_INF, F32)
        acc_ref[...] = jnp.zeros(acc_ref.shape, F32)
        n_full = qi // ATTN_UNROLL
        lax.fori_loop(0, n_full, steps, 0)
        lax.fori_loop(n_full * ATTN_UNROLL, qi, tail_step, 0)
        st1 = scores(1, qi)
        absorb(0, qi, st_ref[...], True)
        absorb(1, qi, st1, True)
        st_ref[...] = scores_of(queries(0, jnp.minimum(qi + 1, n_tiles - 1)), 0, 0)
        outs = [acc_ref[g, :HEAD_LANES, :] / acc_ref[g, HEAD_LANES:HEAD_LANES + 1, :] for g in range(2)]
        rows = tile_rows(qi)
        if fox:
            for g in range(2):
                gate = _sigmoid(gate_ref[0, rows, head(g)].astype(F32))
                o_ref[0, rows, head(g)] = (outs[g].T * gate).astype(o_ref.dtype)
        else:
            o = (outs[0] - lam * outs[1]).T
            o_ref[0, rows, :] = (_rms(o, nw_ref[...]) * (1.0 - lambda_init)).astype(o_ref.dtype)
        return carry

    lax.fori_loop(0, n_tiles, tile, 0)


def _attn_state(tq):
    return [pltpu.VMEM((tq, tq), F32), pltpu.VMEM((2, 1, tq), F32), pltpu.VMEM((2, HEAD_LANES + ONES_ROWS, tq), F32)]


def _fox_attention(q, k, vt, gate, qb, kb, *, tq=ATTN_TILE):
    b, t, dm = q.shape
    width = 2 * HEAD_LANES
    kern = functools.partial(_attn_kernel, tq=tq, fox=True, lambda_init=0.0)
    seq = pl.BlockSpec((1, t, width), lambda bi, h: (bi, 0, h))
    seq_bias = pl.BlockSpec((1, t, LANES), lambda bi, h: (bi, 0, 0))
    return pl.pallas_call(
        kern,
        grid=(b, dm // width),
        in_specs=[seq, seq, pl.BlockSpec((1, width, t), lambda bi, h: (bi, h, 0)), seq, seq_bias, seq_bias],
        out_specs=seq,
        out_shape=jax.ShapeDtypeStruct((b, t, dm), BF16),
        scratch_shapes=_attn_state(tq) + [pltpu.VMEM((t, width), BF16)],
        compiler_params=_params(("arbitrary", "arbitrary")),
        name="fox_attention",
    )(q, k, vt, gate, qb, kb)


def _diff_attention(qk, vt, lam_params, norm_w, lambda_init, *, tq=ATTN_TILE):
    b, t, _ = qk.shape
    nh = DIFF_HEADS
    kern = functools.partial(_attn_kernel, tq=tq, fox=False, lambda_init=lambda_init)
    seq = pl.BlockSpec((1, t, HEAD_LANES), lambda bi, h: (bi, 0, h))
    return pl.pallas_call(
        kern,
        grid=(b, nh),
        in_specs=[seq,
                  pl.BlockSpec((1, t, HEAD_LANES), lambda bi, h: (bi, 0, nh + h)),
                  pl.BlockSpec((1, HEAD_LANES, t), lambda bi, h: (bi, h, 0)),
                  _const_spec(lam_params.shape), _const_spec((1, HEAD_LANES))],
        out_specs=seq,
        out_shape=jax.ShapeDtypeStruct((b, t, nh * HEAD_LANES), BF16),
        scratch_shapes=_attn_state(tq),
        compiler_params=_params(("arbitrary", "arbitrary")),
        name="diff_attention",
    )(qk, qk, vt, lam_params, norm_w)


def _gdn_prep_kernel(q_ref, k_ref, v_ref, gates_ref, u_ref, w_ref, qd_ref, kd_ref, qk_ref):
    c = GDN_CHUNK
    sub = GDN_PREP_SUBTILE
    units = [(slice(t0, t0 + sub), hd) for t0 in range(0, q_ref.shape[1], sub) for hd in range(GDN_HEADS)]
    ids = range(len(units))
    col = lambda hd: slice(hd * HEAD_LANES, (hd + 1) * HEAD_LANES)
    lane = lax.broadcasted_iota(jnp.int32, (sub, LANES), 1)
    ri = lax.broadcasted_iota(jnp.int32, (sub, sub), 0)
    ci = lax.broadcasted_iota(jnp.int32, (sub, sub), 1)
    chunk_start = ri - ri % c
    incl = lambda a: jnp.where(ci <= ri, jnp.where(ci >= chunk_start, a, 0.0), 0.0)
    strict = lambda a: jnp.where(ci < ri, jnp.where(ci >= chunk_start, a, 0.0), 0.0)
    ident = jnp.where(ri == ci, 1.0, 0.0)
    ones = jnp.ones((sub, LANES), BF16)
    kt = [k_ref[0, rows, col(hd)] for rows, hd in units]
    beta = [gates_ref[0, rows, hd:hd + 1] for rows, hd in units]
    gcc = [gates_ref[0, rows, GDN_HEADS + hd:GDN_HEADS + hd + 1] for rows, hd in units]
    k16 = [kt[i].astype(BF16) for i in ids]
    kb = [kt[i] * beta[i] for i in ids]

    gc_row = []
    for i in ids:
        g_hi, g_mid, g_lo = (piece.astype(F32) for piece in _split_bf16(gcc[i], 3))
        pieces = jnp.where(lane == 0, g_hi, jnp.where(lane == 1, g_mid, jnp.where(lane == 2, g_lo, 0.0)))
        gc_row.append(_dot_nt(ones, pieces.astype(BF16)))
    kk = [_dot_nt(kb[i].astype(BF16), k16[i]) for i in ids]
    qk_raw = [_dot_nt(q_ref[0, rows, col(hd)].astype(BF16), k16[i]) for i, (rows, hd) in enumerate(units)]
    decay = [incl(jnp.exp(incl(gcc[i] - gc_row[i]))) for i in ids]
    lower = [strict(kk[i] * decay[i]) for i in ids]

    span = ri ^ ci
    inv = [ident - jnp.where(span == 1, lower[i], 0.0) for i in ids]
    s_blk = 2
    while s_blk < c:
        shift = int(math.log2(s_blk))
        inv16 = [inv[i].astype(BF16) for i in ids]
        coupled = [_dot(jnp.where((span >> shift) == 1, lower[i], 0.0).astype(BF16), inv16[i]) for i in ids]
        inv = [inv[i] - _dot(inv16[i], coupled[i].astype(BF16)) for i in ids]
        s_blk *= 2
    inv16 = [inv[i].astype(BF16) for i in ids]

    eg = [jnp.exp(gcc[i]) for i in ids]
    rhs = [jnp.concatenate([v_ref[0, rows, col(hd)] * beta[i], kb[i] * eg[i]], axis=1)
           for i, (rows, hd) in enumerate(units)]
    sol = [_dot(inv16[i], rhs[i].astype(BF16)) for i in ids]
    a_hi, a_lo, s_hi, s_lo = [], [], [], []
    for i in ids:
        hi, lo = _split_bf16(ident + lower[i], 2)
        a_hi.append(hi)
        a_lo.append(lo)
        hi, lo = _split_bf16(sol[i], 2)
        s_hi.append(hi)
        s_lo.append(lo)
    prod = [_dot(a_hi[i], s_hi[i]) + (_dot(a_hi[i], s_lo[i]) + _dot(a_lo[i], s_hi[i])) for i in ids]
    corr = [_dot(inv16[i], (rhs[i] - prod[i]).astype(BF16)) for i in ids]
    sol = [sol[i] + corr[i] for i in ids]
    for i, (rows, hd) in enumerate(units):
        u_ref[0, rows, col(hd)] = sol[i][:, :HEAD_LANES]
        w_ref[0, rows, col(hd)] = sol[i][:, HEAD_LANES:].astype(BF16)
        qk = incl(qk_raw[i] * decay[i])
        qd_ref[0, rows, col(hd)] = (q_ref[0, rows, col(hd)] * eg[i]).astype(BF16)
        for n in range(sub // c):
            blk = slice(n * c, (n + 1) * c)
            out_rows = slice(rows.start + n * c, rows.start + (n + 1) * c)
            qk_ref[0, hd, out_rows, :] = qk[blk, blk].astype(BF16)
            gl = gcc[i][(n + 1) * c - 1:(n + 1) * c, :]
            kd_ref[0, out_rows, col(hd)] = (kt[i][blk] * jnp.exp(gl - gcc[i][blk])).astype(BF16)


def _gdn_prep(qkv, gates, *, tt=GDN_PREP_TILE):
    b, t, _ = qkv.shape
    nh = GDN_HEADS
    dm = nh * HEAD_LANES
    blk = lambda part: pl.BlockSpec((1, tt, dm), lambda bi, i: (bi, i, part))
    return pl.pallas_call(
        _gdn_prep_kernel,
        grid=(b, t // tt),
        in_specs=[blk(0), blk(1), blk(2), pl.BlockSpec((1, tt, LANES), lambda bi, i: (bi, i, 0))],
        out_specs=[blk(0)] * 4 + [pl.BlockSpec((1, nh, tt, GDN_CHUNK), lambda bi, i: (bi, 0, i, 0))],
        out_shape=[jax.ShapeDtypeStruct((b, t, dm), F32)]
        + [jax.ShapeDtypeStruct((b, t, dm), BF16)] * 3
        + [jax.ShapeDtypeStruct((b, nh, t, GDN_CHUNK), BF16)],
        compiler_params=_params(("arbitrary", "arbitrary")),
        name="gdn_prep",
    )(qkv, qkv, qkv, gates)


def _gdn_scan_kernel(u_ref, w_ref, qd_ref, kd_ref, qk_ref, gates_ref, z_ref, nw_ref, o_ref, s_ref):
    c = GDN_CHUNK
    nb, tt = u_ref.shape[0], u_ref.shape[1]

    @pl.when(pl.program_id(1) == 0)
    def _():
        s_ref[...] = jnp.zeros_like(s_ref)

    chains = [(bi, hd) for bi in range(nb) for hd in range(GDN_HEADS)]
    cols = [slice(hd * HEAD_LANES, (hd + 1) * HEAD_LANES) for hd in range(GDN_HEADS)]
    state = [s_ref[bi, hd] for bi, hd in chains]
    for n in range(tt // c):
        rows = slice(n * c, (n + 1) * c)
        r = [_dot(jnp.concatenate([w_ref[bi, rows, cols[hd]], qd_ref[bi, rows, cols[hd]]], axis=0),
                  state[i].astype(BF16)) for i, (bi, hd) in enumerate(chains)]
        v_new = [(u_ref[bi, rows, cols[hd]] - r[i][:c]).astype(BF16) for i, (bi, hd) in enumerate(chains)]
        intra = [_dot(qk_ref[bi, hd, rows, :], v_new[i]) for i, (bi, hd) in enumerate(chains)]
        upd = [_dot_tn(kd_ref[bi, rows, cols[hd]], v_new[i]) for i, (bi, hd) in enumerate(chains)]
        for i, (bi, hd) in enumerate(chains):
            last = (n + 1) * c - 1
            decay_last = jnp.exp(gates_ref[bi, last:last + 1, GDN_HEADS + hd:GDN_HEADS + hd + 1])
            state[i] = state[i] * decay_last + upd[i]
            zt = z_ref[bi, rows, cols[hd]].astype(F32)
            o = r[i][c:] + intra[i]
            o_ref[bi, rows, cols[hd]] = (_rms(o, nw_ref[...]) * (zt * _sigmoid(zt))).astype(o_ref.dtype)
    for i, (bi, hd) in enumerate(chains):
        s_ref[bi, hd] = state[i]


def _gdn_scan(u, w, qd, kd, qk, gates, z, norm_w, *, tt=GDN_SCAN_TILE, nb=GDN_SCAN_BATCH):
    b, t, dm = u.shape
    nh = GDN_HEADS
    assert b % nb == 0 and t % tt == 0
    blk = pl.BlockSpec((nb, tt, dm), lambda bi, i: (bi, i, 0))
    return pl.pallas_call(
        _gdn_scan_kernel,
        grid=(b // nb, t // tt),
        in_specs=[blk, blk, blk, blk,
                  pl.BlockSpec((nb, nh, tt, GDN_CHUNK), lambda bi, i: (bi, 0, i, 0)),
                  pl.BlockSpec((nb, tt, LANES), lambda bi, i: (bi, i, 0)),
                  blk, _const_spec((1, HEAD_LANES))],
        out_specs=blk,
        out_shape=jax.ShapeDtypeStruct((b, t, dm), BF16),
        scratch_shapes=[pltpu.VMEM((nb, nh, GDN_HEAD_DIM, GDN_HEAD_DIM), F32)],
        compiler_params=_params(("arbitrary", "arbitrary")),
        name="gdn_scan",
    )(u, w, qd, kd, qk, gates, z, norm_w)


def _post_kernel(*refs, n_mix, final_norm):
    x_ref = refs[0]
    mix_refs = refs[1:1 + n_mix]
    wout_ref, g_ref, wup_ref, wdn_ref, p_ref, wpp_ref, wpg_ref = refs[1 + n_mix:8 + n_mix]
    rest = refs[8 + n_mix:]
    if final_norm:
        gf_ref, o_ref = rest
    else:
        (o_ref,) = rest
    mix = mix_refs[0][...] if n_mix == 1 else jnp.concatenate([r[...] for r in mix_refs], axis=1)
    x = x_ref[...] + _dot(mix, wout_ref[...])
    h = _rms(x, g_ref[...]).astype(BF16)
    d_ff = wup_ref.shape[1]
    acc = x
    for s in range(d_ff // FF_SEG):
        a = jnp.maximum(_dot(h, wup_ref[:, s * FF_SEG:(s + 1) * FF_SEG]), 0.0)
        acc = acc + _dot((a * a).astype(BF16), wdn_ref[s * FF_SEG:(s + 1) * FF_SEG, :])
    x = acc
    gate = _sigmoid(_dot(x.astype(BF16), wpg_ref[...]))
    x = x + _dot(p_ref[...].astype(BF16), wpp_ref[...]) * gate
    if final_norm:
        x = _rms(x, gf_ref[...])
    o_ref[...] = x


def _post(x2d, mixes, wout, g, wup, wdn, p2d, wpp, wpg, gf=None):
    m, d = x2d.shape
    tm = TOKEN_TILE
    row = lambda i: (i, 0)
    single = pl.Buffered(1)
    const = lambda a: pl.BlockSpec(a.shape, lambda i: (0, 0), pipeline_mode=single)
    args = [x2d, *mixes, wout, g, wup, wdn, p2d, wpp, wpg]
    in_specs = ([pl.BlockSpec((tm, d), row)]
                + [pl.BlockSpec((tm, a.shape[1]), row) for a in mixes]
                + [const(wout), const(g), const(wup), const(wdn), pl.BlockSpec((tm, p2d.shape[1]), row), const(wpp), const(wpg)])
    if gf is not None:
        args.append(gf)
        in_specs.append(const(gf))
    kern = functools.partial(_post_kernel, n_mix=len(mixes), final_norm=gf is not None)
    return pl.pallas_call(
        kern,
        grid=(m // tm,),
        in_specs=in_specs,
        out_specs=pl.BlockSpec((tm, d), row),
        out_shape=jax.ShapeDtypeStruct((m, d), F32),
        compiler_params=_params(("arbitrary",)),
        name="out_proj_mlp_ple",
    )(*args)


def _pad_lanes(a):
    return jnp.pad(a, ((0, 0), (0, LANES - a.shape[1])))


def kernel(x, p, positions, norm_mix, norm_mlp, norm_final, w_in_even, conv_w, a_log, dt_bias, gdn_norm,
           lam_q1, lam_k1, lam_q2, lam_k2, diff_norm, w_out_even, w_in_odd, b_forget, w_out_odd,
           w_mlp_up, w_mlp_down, w_ple_proj, w_ple_gate):
    b, t, d = x.shape
    depth = p.shape[0]
    m = b * t
    assert t % TOKEN_TILE == 0 and d % PROJ_SEG == 0
    nh = GDN_HEADS
    gdn_w = 3 * nh * GDN_HEAD_DIM + nh * GDN_HEAD_DIM
    assert w_in_even.shape[2] == gdn_w + 2 * nh + 3 * DIFF_HEADS * 2 * DIFF_QK_DIM

    inv_freq = ROPE_THETA ** (-jnp.arange(0, DIFF_QK_DIM, 2, dtype=F32) / DIFF_QK_DIM)
    ang = positions.astype(F32)[..., None] * inv_freq
    cos, sin = jnp.cos(ang), jnp.sin(ang)
    cos_t = jnp.concatenate([cos, cos, cos, cos], axis=-1).reshape(m, LANES)
    sin_t = jnp.concatenate([-sin, sin, -sin, sin], axis=-1).reshape(m, LANES)

    x2d = x.reshape(m, d)
    for i in range(depth):
        j = i // 2
        g_mix = norm_mix[i].reshape(1, d)
        if i % 2 == 0:
            lambda_init = 0.8 - 0.6 * math.exp(-0.3 * i)
            w = w_in_even[j]
            wm = jnp.concatenate([w[:, :gdn_w], w[:, gdn_w + 2 * nh:]], axis=1).astype(BF16)
            wg = _pad_lanes(w[:, gdn_w:gdn_w + 2 * nh]).astype(BF16)
            alog_row = _pad_lanes(jnp.concatenate([jnp.zeros((nh,), F32), a_log[j]]).reshape(1, 2 * nh))
            dt_row = _pad_lanes(jnp.concatenate([jnp.zeros((nh,), F32), dt_bias[j]]).reshape(1, 2 * nh))
            qkv, z, qkb, vbt, gates = _even_in(x2d, g_mix, wm, wg, conv_w[j], alog_row, dt_row, cos_t, sin_t, t)
            qkv, z, qkb, gates = (a.reshape(b, t, -1) for a in (qkv, z, qkb, gates))
            u, wy, qd, kd, qk = _gdn_prep(qkv, gates)
            o_a = _gdn_scan(u, wy, qd, kd, qk, gates, z, gdn_norm[j].reshape(1, HEAD_LANES))
            lam_params = jnp.stack([lam_q1[j], lam_k1[j], lam_q2[j], lam_k2[j]])
            o_b = _diff_attention(qkb, vbt, lam_params, diff_norm[j].reshape(1, HEAD_LANES), lambda_init)
            mixes = [o_a.reshape(m, -1), o_b.reshape(m, -1)]
            wout = w_out_even[j].astype(BF16)
        else:
            w = w_in_odd[j]
            d_mix = (w.shape[1] - FOX_HEADS) // 4
            wm = w[:, :4 * d_mix].astype(BF16)
            wf = _pad_lanes(w[:, 4 * d_mix:]).astype(BF16)
            bf_row = _pad_lanes(b_forget[j].reshape(1, FOX_HEADS))
            q, k, vt, gate, qb, kb = _odd_in(x2d, g_mix, wm, wf, bf_row, t)
            q, k, gate, qb, kb = (a.reshape(b, t, -1) for a in (q, k, gate, qb, kb))
            o = _fox_attention(q, k, vt, gate, qb, kb)
            mixes = [o.reshape(m, -1)]
            wout = w_out_odd[j].astype(BF16)
        x2d = _post(x2d, mixes, wout, norm_mlp[i].reshape(1, d), w_mlp_up[i].astype(BF16),
                    w_mlp_down[i].astype(BF16), p[i].reshape(m, -1), w_ple_proj[i].astype(BF16),
                    w_ple_gate[i].astype(BF16), norm_final.reshape(1, d) if i == depth - 1 else None)
    return x2d.reshape(b, t, d)
```

```python
import functools
import math

import jax
import jax.numpy as jnp
import numpy as np
from jax import lax
from jax.experimental import pallas as pl
from jax.experimental.pallas import tpu as pltpu

F32 = jnp.float32
BF16 = jnp.bfloat16

GDN_HEADS = 4
GDN_HEAD_DIM = 128
GDN_CHUNK = 64
CONV_WIDTH = 4
DIFF_HEADS = 4
DIFF_QK_DIM = 64
FOX_HEADS = 8
HEAD_LANES = 128
ROPE_THETA = 10000.0
EPS = 1e-6
NEG_INF = -1e30
LOG2E = 1.4426950408889634
LANES = 128
SUBLANES = 8
VMEM_LIMIT_BYTES = 56 * 1024 * 1024

TOKEN_TILE = 1024
EVEN_TOKEN_TILE = 512
PROJ_SEG = 512
FF_SEG = 1024
GDN_PREP_SUBTILE = 256
GDN_PREP_TILE = 512
GDN_SCAN_TILE = 256
GDN_SCAN_BATCH = 4
ATTN_TILE = 512
ATTN_COLUMN_SPLIT = 2
ATTN_UNROLL = 2
BIAS_LANES_PER_HEAD = 16
ONES_ROWS = 16


def _dot(a, b):
    return jnp.dot(a, b, preferred_element_type=F32)


def _split_bf16(x, parts):
    out = []
    for _ in range(parts):
        piece = x.astype(BF16)
        out.append(piece)
        x = x - piece.astype(F32)
    return out


def _dot_nt(a, b):
    return lax.dot_general(a, b, (((1,), (1,)), ((), ())), preferred_element_type=F32)


def _dot_tn(a, b):
    return lax.dot_general(a, b, (((0,), (0,)), ((), ())), preferred_element_type=F32)


def _rms(x, g):
    return x * lax.rsqrt(jnp.mean(x * x, axis=-1, keepdims=True) + EPS) * g


def _sigmoid(x):
    return 1.0 / (1.0 + jnp.exp(-x))


def _softplus(x):
    return jnp.maximum(x, 0.0) + jnp.log1p(jnp.exp(-jnp.abs(x)))


def _row_scan(x, period):
    rows = lax.broadcasted_iota(jnp.int32, x.shape, 0) % period
    s = 1
    while s < period:
        x = x + jnp.where(rows >= s, pltpu.roll(x, s, 0), 0.0)
        s *= 2
    return x


def _const_spec(shape):
    return pl.BlockSpec(shape, lambda *_: (0,) * len(shape))


def _params(sem):
    return pltpu.CompilerParams(dimension_semantics=sem, vmem_limit_bytes=VMEM_LIMIT_BYTES)


def _even_in_kernel(x_ref, g_ref, wm_ref, wg_ref, conv_ref, alog_ref, dt_ref, cos_ref, sin_ref,
                    qkv_ref, z_ref, qkb_ref, vbt_ref, gates_ref, h_ref, carry_ref, tr_ref, pad_ref, *, tiles_per_seq):
    tm = x_ref.shape[0]
    i = pl.program_id(0)
    h_ref[...] = _rms(x_ref[...], g_ref[...]).astype(BF16)
    seq_start = (i % tiles_per_seq) == 0
    seg = lambda s: slice(s * PROJ_SEG, (s + 1) * PROJ_SEG)
    project = lambda s: _dot(h_ref[...], wm_ref[:, seg(s)])

    def gdn_qkv(s, y):
        cols = seg(s)
        pad_ref[0:SUBLANES, :] = jnp.where(seq_start, 0.0, carry_ref[:, cols])
        pad_ref[SUBLANES:, :] = y
        carry_ref[:, cols] = y[tm - SUBLANES:, :]
        w = conv_ref[:, cols]
        a = y * w[CONV_WIDTH - 1:CONV_WIDTH, :]
        for k in range(1, CONV_WIDTH):
            a = a + pad_ref[SUBLANES - k:SUBLANES - k + tm, :] * w[CONV_WIDTH - 1 - k:CONV_WIDTH - k, :]
        a = a * _sigmoid(a)
        if s < 2:
            outs = []
            for hd in range(GDN_HEADS):
                blk = a[:, hd * HEAD_LANES:(hd + 1) * HEAD_LANES]
                n = blk * lax.rsqrt(jnp.sum(blk * blk, axis=-1, keepdims=True) + EPS)
                outs.append(n * (GDN_HEAD_DIM ** -0.5) if s == 0 else n)
            a = jnp.concatenate(outs, axis=1)
        qkv_ref[:, cols] = a

    def gdn_gate(s, y):
        z_ref[...] = y.astype(BF16)

    def diff_qk(s, y):
        cos = jnp.concatenate([cos_ref[...]] * (PROJ_SEG // LANES), axis=1)
        sin = jnp.concatenate([sin_ref[...]] * (PROJ_SEG // LANES), axis=1)
        lane = lax.broadcasted_iota(jnp.int32, (tm, PROJ_SEG), 1)
        first_half = (lane % DIFF_QK_DIM) < (DIFF_QK_DIM // 2)
        swapped = jnp.where(first_half, pltpu.roll(y, PROJ_SEG - DIFF_QK_DIM // 2, 1),
                            pltpu.roll(y, DIFF_QK_DIM // 2, 1))
        scale = DIFF_QK_DIM ** -0.5 * LOG2E if s == 4 else 1.0
        qkb_ref[:, seg(s - 4)] = ((y * cos + swapped * sin) * scale).astype(BF16)

    def diff_v(s, y):
        tr_ref[...] = y
        vbt_ref[0] = tr_ref[...].T.astype(BF16)

    stages = ((0, gdn_qkv), (3, gdn_gate), (1, gdn_qkv), (6, diff_v), (2, gdn_qkv), (4, diff_qk), (5, diff_qk))
    pending = project(stages[0][0])
    for n, (s, epilogue) in enumerate(stages):
        upcoming = project(stages[n + 1][0]) if n + 1 < len(stages) else _dot(h_ref[...], wg_ref[...])
        epilogue(s, pending)
        pending = upcoming

    graw = pending
    beta = _sigmoid(graw)
    g = -jnp.exp(alog_ref[...]) * _softplus(graw + dt_ref[...])
    gc = _row_scan(g, GDN_CHUNK)
    lane_g = lax.broadcasted_iota(jnp.int32, (tm, LANES), 1)
    gates_ref[...] = jnp.where(lane_g < GDN_HEADS, beta, gc)


def _even_in(x2d, g, wm, wg, conv_w, alog_row, dt_row, cos_t, sin_t, seq_len):
    m, d = x2d.shape
    tm = EVEN_TOKEN_TILE
    n_main = wm.shape[1]
    tps = seq_len // tm
    kern = functools.partial(_even_in_kernel, tiles_per_seq=tps)
    row = lambda i: (i, 0)
    return pl.pallas_call(
        kern,
        grid=(m // tm,),
        in_specs=[
            pl.BlockSpec((tm, d), row),
            _const_spec((1, d)),
            _const_spec((d, n_main)),
            _const_spec((d, LANES)),
            _const_spec(conv_w.shape),
            _const_spec((1, LANES)),
            _const_spec((1, LANES)),
            pl.BlockSpec((tm, LANES), row),
            pl.BlockSpec((tm, LANES), row),
        ],
        out_specs=[
            pl.BlockSpec((tm, 3 * PROJ_SEG), row),
            pl.BlockSpec((tm, PROJ_SEG), row),
            pl.BlockSpec((tm, 2 * PROJ_SEG), row),
            pl.BlockSpec((1, PROJ_SEG, tm), lambda i: (i // tps, 0, i % tps)),
            pl.BlockSpec((tm, LANES), row),
        ],
        out_shape=[
            jax.ShapeDtypeStruct((m, 3 * PROJ_SEG), F32),
            jax.ShapeDtypeStruct((m, PROJ_SEG), BF16),
            jax.ShapeDtypeStruct((m, 2 * PROJ_SEG), BF16),
            jax.ShapeDtypeStruct((m // seq_len, PROJ_SEG, seq_len), BF16),
            jax.ShapeDtypeStruct((m, LANES), F32),
        ],
        scratch_shapes=[pltpu.VMEM((tm, d), BF16), pltpu.VMEM((SUBLANES, 3 * PROJ_SEG), F32), pltpu.VMEM((tm, PROJ_SEG), F32),
                        pltpu.VMEM((tm + SUBLANES, PROJ_SEG), F32)],
        compiler_params=_params(("arbitrary",)),
        name="even_in_proj",
    )(x2d, g, wm, wg, conv_w, alog_row, dt_row, cos_t, sin_t)


def _odd_in_kernel(x_ref, g_ref, wm_ref, wf_ref, bf_ref, sel_ref, ones_ref, q_ref, k_ref, vt_ref, gate_ref, qb_ref, kb_ref,
                   h_ref, carry_ref, tr_ref, *, tiles_per_seq, d_mix):
    tm = x_ref.shape[0]
    i = pl.program_id(0)
    h_ref[...] = _rms(x_ref[...], g_ref[...]).astype(BF16)
    head_dim = d_mix // FOX_HEADS
    for o_ref, base, scale in ((q_ref, 0, head_dim ** -0.5 * LOG2E), (k_ref, d_mix, 1.0),
                               (vt_ref, 2 * d_mix, 1.0), (gate_ref, 3 * d_mix, 1.0)):
        for s in range(d_mix // PROJ_SEG):
            cols = slice(s * PROJ_SEG, (s + 1) * PROJ_SEG)
            y = _dot(h_ref[...], wm_ref[:, base + s * PROJ_SEG:base + (s + 1) * PROJ_SEG])
            if o_ref is vt_ref:
                tr_ref[...] = y
                o_ref[0, cols, :] = tr_ref[...].T.astype(BF16)
            else:
                o_ref[:, cols] = (y * scale).astype(BF16)
    f = _dot(h_ref[...], wf_ref[...]) + bf_ref[...]
    log_f = jnp.minimum(f, 0.0) - jnp.log1p(jnp.exp(-jnp.abs(f)))
    prev = jnp.where((i % tiles_per_seq) == 0, 0.0, carry_ref[0:1, :])
    cum = _row_scan(log_f, tm) + prev
    carry_ref[...] = jnp.broadcast_to(cum[tm - 1:tm, :], carry_ref.shape)
    pieces = jnp.concatenate(_split_bf16(LOG2E * cum, 3), axis=1)
    lanes = _dot(pieces, sel_ref[...]) + ones_ref[...]
    qb_ref[...] = lanes[:, :LANES].astype(BF16)
    kb_ref[...] = lanes[:, LANES:].astype(BF16)


def _bias_lane_tables():
    sel = np.zeros((3 * LANES, 2 * LANES), np.float32)
    ones = np.zeros((1, 2 * LANES), np.float32)
    for h in range(FOX_HEADS):
        base = BIAS_LANES_PER_HEAD * h
        for piece in range(3):
            sel[LANES * piece + h, base + 3 + piece] = 1.0
            sel[LANES * piece + h, LANES + base + piece] = -1.0
            ones[0, base + piece] = 1.0
            ones[0, LANES + base + 3 + piece] = 1.0
    return jnp.asarray(sel, BF16), jnp.asarray(ones, F32)


def _odd_in(x2d, g, wm, wf, bf_row, seq_len):
    m, d = x2d.shape
    sel, ones_row = _bias_lane_tables()
    tm = TOKEN_TILE
    d_mix = wm.shape[1] // 4
    tps = seq_len // tm
    kern = functools.partial(_odd_in_kernel, tiles_per_seq=tps, d_mix=d_mix)
    row = lambda i: (i, 0)
    row_blk = pl.BlockSpec((tm, d_mix), row)
    row_shape = jax.ShapeDtypeStruct((m, d_mix), BF16)
    return pl.pallas_call(
        kern,
        grid=(m // tm,),
        in_specs=[
            pl.BlockSpec((tm, d), row),
            _const_spec((1, d)),
            _const_spec(wm.shape),
            _const_spec((d, LANES)),
            _const_spec((1, LANES)),
            _const_spec(sel.shape),
            _const_spec(ones_row.shape),
        ],
        out_specs=[row_blk, row_blk, pl.BlockSpec((1, d_mix, tm), lambda i: (i // tps, 0, i % tps)), row_blk,
                   pl.BlockSpec((tm, LANES), row), pl.BlockSpec((tm, LANES), row)],
        out_shape=[row_shape, row_shape, jax.ShapeDtypeStruct((m // seq_len, d_mix, seq_len), BF16), row_shape,
                   jax.ShapeDtypeStruct((m, LANES), BF16), jax.ShapeDtypeStruct((m, LANES), BF16)],
        scratch_shapes=[pltpu.VMEM((tm, d), BF16), pltpu.VMEM((SUBLANES, LANES), F32), pltpu.VMEM((tm, PROJ_SEG), F32)],
        compiler_params=_params(("arbitrary",)),
        name="odd_in_proj",
    )(x2d, g, wm, wf, bf_row, sel, ones_row)


def _attn_kernel(*refs, tq, fox, lambda_init):
    if fox:
        q_ref, k_ref, vt_ref, gate_ref, qb_ref, kball_ref, o_ref, st_ref, mx_ref, m_ref, acc_ref, kb_ref = refs
    else:
        q_ref, k_ref, vt_ref, lam_ref, nw_ref, o_ref, st_ref, mx_ref, m_ref, acc_ref = refs
    tk = tq
    hg = pl.program_id(1)
    n_tiles = q_ref.shape[1] // tq
    head = lambda g: slice(g * HEAD_LANES, (g + 1) * HEAD_LANES)
    kv = [head(0), head(1)] if fox else [head(0), head(0)]
    tile_rows = lambda tile: pl.ds(pl.multiple_of(tile * tq, tq), tq)

    def queries(g, tile):
        if fox:
            return jnp.concatenate([q_ref[0, tile_rows(tile), head(g)], qb_ref[0, tile_rows(tile), :]], axis=1)
        q = q_ref[0, tile_rows(tile), :]
        lane = lax.broadcasted_iota(jnp.int32, q.shape, 1)
        keep = (lane < DIFF_QK_DIM) if g == 0 else (lane >= DIFF_QK_DIM)
        return jnp.where(keep, q, jnp.zeros_like(q))

    def scores_of(qmat, g, j):
        k0 = pl.multiple_of(j * tk, tk)
        kj = k_ref[0, pl.ds(k0, tk), kv[g]]
        if fox:
            kj = jnp.concatenate([kj, kb_ref[pl.ds(k0, tk), kv[g]]], axis=1)
        return _dot_nt(kj, qmat)

    if fox:
        kb_all = kball_ref[0]
        owner = lax.broadcasted_iota(jnp.int32, kb_all.shape, 1) // BIAS_LANES_PER_HEAD
        for g in range(2):
            kb_ref[:, head(g)] = jnp.where(owner == hg * 2 + g, kb_all, jnp.zeros_like(kb_all))
    else:
        lam_p = lam_ref[...]
        lam = (jnp.exp(jnp.sum(lam_p[0:1] * lam_p[1:2], axis=1, keepdims=True))
               - jnp.exp(jnp.sum(lam_p[2:3] * lam_p[3:4], axis=1, keepdims=True)) + lambda_init)
    ones = jnp.ones((ONES_ROWS, tk), BF16)

    def lookahead(g, qmat, tile, j, diagonal):
        st = scores_of(qmat, g, j)
        if diagonal:
            kpos = j * tk + lax.broadcasted_iota(jnp.int32, (tk, tq), 0)
            qpos = tile * tq + lax.broadcasted_iota(jnp.int32, (tk, tq), 1)
            st = jnp.where(qpos >= kpos, st, NEG_INF)
        st_ref[g] = st
        mx_ref[g] = jnp.max(st, axis=0, keepdims=True)

    def absorb(g, j):
        k0 = pl.multiple_of(j * tk, tk)
        vt = jnp.concatenate([vt_ref[0, kv[g], pl.ds(k0, tk)], ones], axis=0)
        width = tq // ATTN_COLUMN_SPLIT
        for part in range(ATTN_COLUMN_SPLIT):
            cs = slice(part * width, (part + 1) * width)
            m = m_ref[g, :, cs]
            m_new = jnp.maximum(m, mx_ref[g, :, cs])
            p = jnp.exp2(st_ref[g, :, cs] - m_new).astype(BF16)
            acc_ref[g, :, cs] = jnp.exp2(m - m_new) * acc_ref[g, :, cs] + _dot(vt, p)
            m_ref[g, :, cs] = m_new

    for g in range(2):
        lookahead(g, queries(g, 0), 0, 0, True)

    def tile(qi, carry):
        qs = [queries(g, qi) for g in range(2)]

        def step(j, next_is_diagonal):
            for g in range(2):
                absorb(g, j)
                lookahead(g, qs[g], qi, j + 1, next_is_diagonal)

        def steps(j, c):
            for u in range(ATTN_UNROLL):
                step(j * ATTN_UNROLL + u, False)
            return c

        def single_step(j, c):
            step(j, False)
            return c

        def last_step(j, c):
            step(j, True)
            return c

        m_ref[...] = jnp.full(m_ref.shape, NEG_INF, F32)
        acc_ref[...] = jnp.zeros(acc_ref.shape, F32)
        n_plain = jnp.maximum(qi - 1, 0)
        n_full = n_plain // ATTN_UNROLL
        lax.fori_loop(0, n_full, steps, 0)
        lax.fori_loop(n_full * ATTN_UNROLL, n_plain, single_step, 0)
        lax.fori_loop(n_plain, qi, last_step, 0)
        nxt = jnp.minimum(qi + 1, n_tiles - 1)
        for g in range(2):
            absorb(g, qi)
            lookahead(g, queries(g, nxt), nxt, 0, False)
        outs = [acc_ref[g, :HEAD_LANES, :] / acc_ref[g, HEAD_LANES:HEAD_LANES + 1, :] for g in range(2)]
        rows = tile_rows(qi)
        if fox:
            for g in range(2):
                gate = _sigmoid(gate_ref[0, rows, head(g)].astype(F32))
                o_ref[0, rows, head(g)] = (outs[g].T * gate).astype(o_ref.dtype)
        else:
            o = (outs[0] - lam * outs[1]).T
            o_ref[0, rows, :] = (_rms(o, nw_ref[...]) * (1.0 - lambda_init)).astype(o_ref.dtype)
        return carry

    lax.fori_loop(0, n_tiles, tile, 0)


def _attn_state(tq):
    return [pltpu.VMEM((2, tq, tq), F32), pltpu.VMEM((2, 1, tq), F32), pltpu.VMEM((2, 1, tq), F32),
            pltpu.VMEM((2, HEAD_LANES + ONES_ROWS, tq), F32)]


def _fox_attention(q, k, vt, gate, qb, kb, *, tq=ATTN_TILE):
    b, t, dm = q.shape
    width = 2 * HEAD_LANES
    kern = functools.partial(_attn_kernel, tq=tq, fox=True, lambda_init=0.0)
    seq = pl.BlockSpec((1, t, width), lambda bi, h: (bi, 0, h))
    seq_bias = pl.BlockSpec((1, t, LANES), lambda bi, h: (bi, 0, 0))
    return pl.pallas_call(
        kern,
        grid=(b, dm // width),
        in_specs=[seq, seq, pl.BlockSpec((1, width, t), lambda bi, h: (bi, h, 0)), seq, seq_bias, seq_bias],
        out_specs=seq,
        out_shape=jax.ShapeDtypeStruct((b, t, dm), BF16),
        scratch_shapes=_attn_state(tq) + [pltpu.VMEM((t, width), BF16)],
        compiler_params=_params(("arbitrary", "arbitrary")),
        name="fox_attention",
    )(q, k, vt, gate, qb, kb)


def _diff_attention(qk, vt, lam_params, norm_w, lambda_init, *, tq=ATTN_TILE):
    b, t, _ = qk.shape
    nh = DIFF_HEADS
    kern = functools.partial(_attn_kernel, tq=tq, fox=False, lambda_init=lambda_init)
    seq = pl.BlockSpec((1, t, HEAD_LANES), lambda bi, h: (bi, 0, h))
    return pl.pallas_call(
        kern,
        grid=(b, nh),
        in_specs=[seq,
                  pl.BlockSpec((1, t, HEAD_LANES), lambda bi, h: (bi, 0, nh + h)),
                  pl.BlockSpec((1, HEAD_LANES, t), lambda bi, h: (bi, h, 0)),
                  _const_spec(lam_params.shape), _const_spec((1, HEAD_LANES))],
        out_specs=seq,
        out_shape=jax.ShapeDtypeStruct((b, t, nh * HEAD_LANES), BF16),
        scratch_shapes=_attn_state(tq),
        compiler_params=_params(("arbitrary", "arbitrary")),
        name="diff_attention",
    )(qk, qk, vt, lam_params, norm_w)


def _gdn_prep_kernel(q_ref, k_ref, v_ref, gates_ref, u_ref, w_ref, qd_ref, kd_ref, qk_ref):
    c = GDN_CHUNK
    sub = GDN_PREP_SUBTILE
    units = [(slice(t0, t0 + sub), hd) for t0 in range(0, q_ref.shape[1], sub) for hd in range(GDN_HEADS)]
    ids = range(len(units))
    col = lambda hd: slice(hd * HEAD_LANES, (hd + 1) * HEAD_LANES)
    lane = lax.broadcasted_iota(jnp.int32, (sub, LANES), 1)
    ri = lax.broadcasted_iota(jnp.int32, (sub, sub), 0)
    ci = lax.broadcasted_iota(jnp.int32, (sub, sub), 1)
    chunk_start = ri - ri % c
    incl = lambda a: jnp.where(ci <= ri, jnp.where(ci >= chunk_start, a, 0.0), 0.0)
    strict = lambda a: jnp.where(ci < ri, jnp.where(ci >= chunk_start, a, 0.0), 0.0)
    ident = jnp.where(ri == ci, 1.0, 0.0)
    ones = jnp.ones((sub, LANES), BF16)
    kt = [k_ref[0, rows, col(hd)] for rows, hd in units]
    beta = [gates_ref[0, rows, hd:hd + 1] for rows, hd in units]
    gcc = [gates_ref[0, rows, GDN_HEADS + hd:GDN_HEADS + hd + 1] for rows, hd in units]
    k16 = [kt[i].astype(BF16) for i in ids]
    kb = [kt[i] * beta[i] for i in ids]

    gc_row = []
    for i in ids:
        g_hi, g_mid, g_lo = (piece.astype(F32) for piece in _split_bf16(gcc[i], 3))
        pieces = jnp.where(lane == 0, g_hi, jnp.where(lane == 1, g_mid, jnp.where(lane == 2, g_lo, 0.0)))
        gc_row.append(_dot_nt(ones, pieces.astype(BF16)))
    kk = [_dot_nt(kb[i].astype(BF16), k16[i]) for i in ids]
    qk_raw = [_dot_nt(q_ref[0, rows, col(hd)].astype(BF16), k16[i]) for i, (rows, hd) in enumerate(units)]
    decay = [incl(jnp.exp(incl(gcc[i] - gc_row[i]))) for i in ids]
    lower = [strict(kk[i] * decay[i]) for i in ids]

    span = ri ^ ci
    inv = [ident - jnp.where(span == 1, lower[i], 0.0) for i in ids]
    s_blk = 2
    while s_blk < c:
        shift = int(math.log2(s_blk))
        inv16 = [inv[i].astype(BF16) for i in ids]
        coupled = [_dot(jnp.where((span >> shift) == 1, lower[i], 0.0).astype(BF16), inv16[i]) for i in ids]
        inv = [inv[i] - _dot(inv16[i], coupled[i].astype(BF16)) for i in ids]
        s_blk *= 2
    inv16 = [inv[i].astype(BF16) for i in ids]

    eg = [jnp.exp(gcc[i]) for i in ids]
    rhs = [jnp.concatenate([v_ref[0, rows, col(hd)] * beta[i], kb[i] * eg[i]], axis=1)
           for i, (rows, hd) in enumerate(units)]
    sol = [_dot(inv16[i], rhs[i].astype(BF16)) for i in ids]
    a_hi, a_lo, s_hi, s_lo = [], [], [], []
    for i in ids:
        hi, lo = _split_bf16(ident + lower[i], 2)
        a_hi.append(hi)
        a_lo.append(lo)
        hi, lo = _split_bf16(sol[i], 2)
        s_hi.append(hi)
        s_lo.append(lo)
    prod = [_dot(a_hi[i], s_hi[i]) + (_dot(a_hi[i], s_lo[i]) + _dot(a_lo[i], s_hi[i])) for i in ids]
    corr = [_dot(inv16[i], (rhs[i] - prod[i]).astype(BF16)) for i in ids]
    sol = [sol[i] + corr[i] for i in ids]
    for i, (rows, hd) in enumerate(units):
        u_ref[0, rows, col(hd)] = sol[i][:, :HEAD_LANES]
        w_ref[0, rows, col(hd)] = sol[i][:, HEAD_LANES:].astype(BF16)
        qk = incl(qk_raw[i] * decay[i])
        qd_ref[0, rows, col(hd)] = (q_ref[0, rows, col(hd)] * eg[i]).astype(BF16)
        for n in range(sub // c):
            blk = slice(n * c, (n + 1) * c)
            out_rows = slice(rows.start + n * c, rows.start + (n + 1) * c)
            qk_ref[0, hd, out_rows, :] = qk[blk, blk].astype(BF16)
            gl = gcc[i][(n + 1) * c - 1:(n + 1) * c, :]
            kd_ref[0, out_rows, col(hd)] = (kt[i][blk] * jnp.exp(gl - gcc[i][blk])).astype(BF16)


def _gdn_prep(qkv, gates, *, tt=GDN_PREP_TILE):
    b, t, _ = qkv.shape
    nh = GDN_HEADS
    dm = nh * HEAD_LANES
    blk = lambda part: pl.BlockSpec((1, tt, dm), lambda bi, i: (bi, i, part))
    return pl.pallas_call(
        _gdn_prep_kernel,
        grid=(b, t // tt),
        in_specs=[blk(0), blk(1), blk(2), pl.BlockSpec((1, tt, LANES), lambda bi, i: (bi, i, 0))],
        out_specs=[blk(0)] * 4 + [pl.BlockSpec((1, nh, tt, GDN_CHUNK), lambda bi, i: (bi, 0, i, 0))],
        out_shape=[jax.ShapeDtypeStruct((b, t, dm), F32)]
        + [jax.ShapeDtypeStruct((b, t, dm), BF16)] * 3
        + [jax.ShapeDtypeStruct((b, nh, t, GDN_CHUNK), BF16)],
        compiler_params=_params(("arbitrary", "arbitrary")),
        name="gdn_prep",
    )(qkv, qkv, qkv, gates)


def _gdn_scan_kernel(u_ref, w_ref, qd_ref, kd_ref, qk_ref, gates_ref, z_ref, nw_ref, o_ref, s_ref):
    c = GDN_CHUNK
    nb, tt = u_ref.shape[0], u_ref.shape[1]

    @pl.when(pl.program_id(1) == 0)
    def _():
        s_ref[...] = jnp.zeros_like(s_ref)

    chains = [(bi, hd) for bi in range(nb) for hd in range(GDN_HEADS)]
    cols = [slice(hd * HEAD_LANES, (hd + 1) * HEAD_LANES) for hd in range(GDN_HEADS)]
    state = [s_ref[bi, hd] for bi, hd in chains]
    for n in range(tt // c):
        rows = slice(n * c, (n + 1) * c)
        r = [_dot(jnp.concatenate([w_ref[bi, rows, cols[hd]], qd_ref[bi, rows, cols[hd]]], axis=0),
                  state[i].astype(BF16)) for i, (bi, hd) in enumerate(chains)]
        v_new = [(u_ref[bi, rows, cols[hd]] - r[i][:c]).astype(BF16) for i, (bi, hd) in enumerate(chains)]
        intra = [_dot(qk_ref[bi, hd, rows, :], v_new[i]) for i, (bi, hd) in enumerate(chains)]
        upd = [_dot_tn(kd_ref[bi, rows, cols[hd]], v_new[i]) for i, (bi, hd) in enumerate(chains)]
        for i, (bi, hd) in enumerate(chains):
            last = (n + 1) * c - 1
            decay_last = jnp.exp(gates_ref[bi, last:last + 1, GDN_HEADS + hd:GDN_HEADS + hd + 1])
            state[i] = state[i] * decay_last + upd[i]
            zt = z_ref[bi, rows, cols[hd]].astype(F32)
            o = r[i][c:] + intra[i]
            o_ref[bi, rows, cols[hd]] = (_rms(o, nw_ref[...]) * (zt * _sigmoid(zt))).astype(o_ref.dtype)
    for i, (bi, hd) in enumerate(chains):
        s_ref[bi, hd] = state[i]


def _gdn_scan(u, w, qd, kd, qk, gates, z, norm_w, *, tt=GDN_SCAN_TILE, nb=GDN_SCAN_BATCH):
    b, t, dm = u.shape
    nh = GDN_HEADS
    assert b % nb == 0 and t % tt == 0
    blk = pl.BlockSpec((nb, tt, dm), lambda bi, i: (bi, i, 0))
    return pl.pallas_call(
        _gdn_scan_kernel,
        grid=(b // nb, t // tt),
        in_specs=[blk, blk, blk, blk,
                  pl.BlockSpec((nb, nh, tt, GDN_CHUNK), lambda bi, i: (bi, 0, i, 0)),
                  pl.BlockSpec((nb, tt, LANES), lambda bi, i: (bi, i, 0)),
                  blk, _const_spec((1, HEAD_LANES))],
        out_specs=blk,
        out_shape=jax.ShapeDtypeStruct((b, t, dm), BF16),
        scratch_shapes=[pltpu.VMEM((nb, nh, GDN_HEAD_DIM, GDN_HEAD_DIM), F32)],
        compiler_params=_params(("arbitrary", "arbitrary")),
        name="gdn_scan",
    )(u, w, qd, kd, qk, gates, z, norm_w)


def _post_kernel(*refs, n_mix, final_norm):
    x_ref = refs[0]
    mix_refs = refs[1:1 + n_mix]
    wout_ref, g_ref, wup_ref, wdn_ref, p_ref, wpp_ref, wpg_ref = refs[1 + n_mix:8 + n_mix]
    rest = refs[8 + n_mix:]
    if final_norm:
        gf_ref, o_ref = rest
    else:
        (o_ref,) = rest
    mix = mix_refs[0][...] if n_mix == 1 else jnp.concatenate([r[...] for r in mix_refs], axis=1)
    x = x_ref[...] + _dot(mix, wout_ref[...])
    h = _rms(x, g_ref[...]).astype(BF16)
    d_ff = wup_ref.shape[1]
    acc = x
    for s in range(d_ff // FF_SEG):
        a = jnp.maximum(_dot(h, wup_ref[:, s * FF_SEG:(s + 1) * FF_SEG]), 0.0)
        acc = acc + _dot((a * a).astype(BF16), wdn_ref[s * FF_SEG:(s + 1) * FF_SEG, :])
    x = acc
    gate = _sigmoid(_dot(x.astype(BF16), wpg_ref[...]))
    x = x + _dot(p_ref[...].astype(BF16), wpp_ref[...]) * gate
    if final_norm:
        x = _rms(x, gf_ref[...])
    o_ref[...] = x


def _post(x2d, mixes, wout, g, wup, wdn, p2d, wpp, wpg, gf=None):
    m, d = x2d.shape
    tm = TOKEN_TILE
    row = lambda i: (i, 0)
    single = pl.Buffered(1)
    const = lambda a: pl.BlockSpec(a.shape, lambda i: (0, 0), pipeline_mode=single)
    args = [x2d, *mixes, wout, g, wup, wdn, p2d, wpp, wpg]
    in_specs = ([pl.BlockSpec((tm, d), row)]
                + [pl.BlockSpec((tm, a.shape[1]), row) for a in mixes]
                + [const(wout), const(g), const(wup), const(wdn), pl.BlockSpec((tm, p2d.shape[1]), row), const(wpp), const(wpg)])
    if gf is not None:
        args.append(gf)
        in_specs.append(const(gf))
    kern = functools.partial(_post_kernel, n_mix=len(mixes), final_norm=gf is not None)
    return pl.pallas_call(
        kern,
        grid=(m // tm,),
        in_specs=in_specs,
        out_specs=pl.BlockSpec((tm, d), row),
        out_shape=jax.ShapeDtypeStruct((m, d), F32),
        compiler_params=_params(("arbitrary",)),
        name="out_proj_mlp_ple",
    )(*args)


def _pad_lanes(a):
    return jnp.pad(a, ((0, 0), (0, LANES - a.shape[1])))


def kernel(x, p, positions, norm_mix, norm_mlp, norm_final, w_in_even, conv_w, a_log, dt_bias, gdn_norm,
           lam_q1, lam_k1, lam_q2, lam_k2, diff_norm, w_out_even, w_in_odd, b_forget, w_out_odd,
           w_mlp_up, w_mlp_down, w_ple_proj, w_ple_gate):
    b, t, d = x.shape
    depth = p.shape[0]
    m = b * t
    assert t % TOKEN_TILE == 0 and d % PROJ_SEG == 0
    nh = GDN_HEADS
    gdn_w = 3 * nh * GDN_HEAD_DIM + nh * GDN_HEAD_DIM
    assert w_in_even.shape[2] == gdn_w + 2 * nh + 3 * DIFF_HEADS * 2 * DIFF_QK_DIM

    inv_freq = ROPE_THETA ** (-jnp.arange(0, DIFF_QK_DIM, 2, dtype=F32) / DIFF_QK_DIM)
    ang = positions.astype(F32)[..., None] * inv_freq
    cos, sin = jnp.cos(ang), jnp.sin(ang)
    cos_t = jnp.concatenate([cos, cos, cos, cos], axis=-1).reshape(m, LANES)
    sin_t = jnp.concatenate([-sin, sin, -sin, sin], axis=-1).reshape(m, LANES)

    x2d = x.reshape(m, d)
    for i in range(depth):
        j = i // 2
        g_mix = norm_mix[i].reshape(1, d)
        if i % 2 == 0:
            lambda_init = 0.8 - 0.6 * math.exp(-0.3 * i)
            w = w_in_even[j]
            wm = jnp.concatenate([w[:, :gdn_w], w[:, gdn_w + 2 * nh:]], axis=1).astype(BF16)
            wg = _pad_lanes(w[:, gdn_w:gdn_w + 2 * nh]).astype(BF16)
            alog_row = _pad_lanes(jnp.concatenate([jnp.zeros((nh,), F32), a_log[j]]).reshape(1, 2 * nh))
            dt_row = _pad_lanes(jnp.concatenate([jnp.zeros((nh,), F32), dt_bias[j]]).reshape(1, 2 * nh))
            qkv, z, qkb, vbt, gates = _even_in(x2d, g_mix, wm, wg, conv_w[j], alog_row, dt_row, cos_t, sin_t, t)
            qkv, z, qkb, gates = (a.reshape(b, t, -1) for a in (qkv, z, qkb, gates))
            u, wy, qd, kd, qk = _gdn_prep(qkv, gates)
            o_a = _gdn_scan(u, wy, qd, kd, qk, gates, z, gdn_norm[j].reshape(1, HEAD_LANES))
            lam_params = jnp.stack([lam_q1[j], lam_k1[j], lam_q2[j], lam_k2[j]])
            o_b = _diff_attention(qkb, vbt, lam_params, diff_norm[j].reshape(1, HEAD_LANES), lambda_init)
            mixes = [o_a.reshape(m, -1), o_b.reshape(m, -1)]
            wout = w_out_even[j].astype(BF16)
        else:
            w = w_in_odd[j]
            d_mix = (w.shape[1] - FOX_HEADS) // 4
            wm = w[:, :4 * d_mix].astype(BF16)
            wf = _pad_lanes(w[:, 4 * d_mix:]).astype(BF16)
            bf_row = _pad_lanes(b_forget[j].reshape(1, FOX_HEADS))
            q, k, vt, gate, qb, kb = _odd_in(x2d, g_mix, wm, wf, bf_row, t)
            q, k, gate, qb, kb = (a.reshape(b, t, -1) for a in (q, k, gate, qb, kb))
            o = _fox_attention(q, k, vt, gate, qb, kb)
            mixes = [o.reshape(m, -1)]
            wout = w_out_odd[j].astype(BF16)
        x2d = _post(x2d, mixes, wout, norm_mlp[i].reshape(1, d), w_mlp_up[i].astype(BF16),
                    w_mlp_down[i].astype(BF16), p[i].reshape(m, -1), w_ple_proj[i].astype(BF16),
                    w_ple_gate[i].astype(BF16), norm_final.reshape(1, d) if i == depth - 1 else None)
    return x2d.reshape(b, t, d)
```

```python
import functools
import math

import jax
import jax.numpy as jnp
import numpy as np
from jax import lax
from jax.experimental import pallas as pl
from jax.experimental.pallas import tpu as pltpu

F32 = jnp.float32
BF16 = jnp.bfloat16

GDN_HEADS = 4
GDN_HEAD_DIM = 128
GDN_CHUNK = 64
CONV_WIDTH = 4
DIFF_HEADS = 4
DIFF_QK_DIM = 64
FOX_HEADS = 8
HEAD_LANES = 128
ROPE_THETA = 10000.0
EPS = 1e-6
NEG_INF = -1e30
LOG2E = 1.4426950408889634
LANES = 128
SUBLANES = 8
VMEM_LIMIT_BYTES = 56 * 1024 * 1024

TOKEN_TILE = 1024
EVEN_TOKEN_TILE = 512
PROJ_SEG = 512
FF_SEG = 1024
GDN_PREP_SUBTILE = 256
GDN_PREP_TILE = 512
GDN_SCAN_TILE = 256
GDN_SCAN_BATCH = 4
ATTN_TILE = 512
ATTN_COLUMN_SPLIT = 2
ATTN_UNROLL = 2
BIAS_LANES_PER_HEAD = 16
ONES_ROWS = 16


def _dot(a, b):
    return jnp.dot(a, b, preferred_element_type=F32)


def _split_bf16(x, parts):
    out = []
    for _ in range(parts):
        piece = x.astype(BF16)
        out.append(piece)
        x = x - piece.astype(F32)
    return out


def _dot_nt(a, b):
    return lax.dot_general(a, b, (((1,), (1,)), ((), ())), preferred_element_type=F32)


def _dot_tn(a, b):
    return lax.dot_general(a, b, (((0,), (0,)), ((), ())), preferred_element_type=F32)


def _rms(x, g):
    return x * lax.rsqrt(jnp.mean(x * x, axis=-1, keepdims=True) + EPS) * g


def _sigmoid(x):
    return 1.0 / (1.0 + jnp.exp(-x))


def _softplus(x):
    return jnp.maximum(x, 0.0) + jnp.log1p(jnp.exp(-jnp.abs(x)))


def _row_scan(x, period):
    rows = lax.broadcasted_iota(jnp.int32, x.shape, 0) % period
    s = 1
    while s < period:
        x = x + jnp.where(rows >= s, pltpu.roll(x, s, 0), 0.0)
        s *= 2
    return x


def _const_spec(shape):
    return pl.BlockSpec(shape, lambda *_: (0,) * len(shape))


def _params(sem):
    return pltpu.CompilerParams(dimension_semantics=sem, vmem_limit_bytes=VMEM_LIMIT_BYTES)


def _even_in_kernel(x_ref, g_ref, wm_ref, wg_ref, conv_ref, alog_ref, dt_ref, cos_ref, sin_ref,
                    qkv_ref, z_ref, qbt_ref, kb_ref, vbt_ref, gates_ref, h_ref, carry_ref, tr_ref, pad_ref, *, tiles_per_seq):
    tm = x_ref.shape[0]
    i = pl.program_id(0)
    h_ref[...] = _rms(x_ref[...], g_ref[...]).astype(BF16)
    seq_start = (i % tiles_per_seq) == 0
    seg = lambda s: slice(s * PROJ_SEG, (s + 1) * PROJ_SEG)
    project = lambda s: _dot(h_ref[...], wm_ref[:, seg(s)])

    def gdn_qkv(s, y):
        cols = seg(s)
        pad_ref[0:SUBLANES, :] = jnp.where(seq_start, 0.0, carry_ref[:, cols])
        pad_ref[SUBLANES:, :] = y
        carry_ref[:, cols] = y[tm - SUBLANES:, :]
        w = conv_ref[:, cols]
        a = y * w[CONV_WIDTH - 1:CONV_WIDTH, :]
        for k in range(1, CONV_WIDTH):
            a = a + pad_ref[SUBLANES - k:SUBLANES - k + tm, :] * w[CONV_WIDTH - 1 - k:CONV_WIDTH - k, :]
        a = a * _sigmoid(a)
        if s < 2:
            outs = []
            for hd in range(GDN_HEADS):
                blk = a[:, hd * HEAD_LANES:(hd + 1) * HEAD_LANES]
                n = blk * lax.rsqrt(jnp.sum(blk * blk, axis=-1, keepdims=True) + EPS)
                outs.append(n * (GDN_HEAD_DIM ** -0.5) if s == 0 else n)
            a = jnp.concatenate(outs, axis=1)
        qkv_ref[:, cols] = a

    def gdn_gate(s, y):
        z_ref[...] = y.astype(BF16)

    def diff_qk(s, y):
        cos = jnp.concatenate([cos_ref[...]] * (PROJ_SEG // LANES), axis=1)
        sin = jnp.concatenate([sin_ref[...]] * (PROJ_SEG // LANES), axis=1)
        lane = lax.broadcasted_iota(jnp.int32, (tm, PROJ_SEG), 1)
        first_half = (lane % DIFF_QK_DIM) < (DIFF_QK_DIM // 2)
        swapped = jnp.where(first_half, pltpu.roll(y, PROJ_SEG - DIFF_QK_DIM // 2, 1),
                            pltpu.roll(y, DIFF_QK_DIM // 2, 1))
        roped = y * cos + swapped * sin
        if s == 4:
            tr_ref[...] = roped * (DIFF_QK_DIM ** -0.5 * LOG2E)
            qbt_ref[0] = tr_ref[...].T.astype(BF16)
        else:
            kb_ref[...] = roped.astype(BF16)

    def diff_v(s, y):
        tr_ref[...] = y
        vbt_ref[0] = tr_ref[...].T.astype(BF16)

    stages = ((0, gdn_qkv), (3, gdn_gate), (1, gdn_qkv), (6, diff_v), (2, gdn_qkv), (4, diff_qk), (5, diff_qk))
    pending = project(stages[0][0])
    for n, (s, epilogue) in enumerate(stages):
        upcoming = project(stages[n + 1][0]) if n + 1 < len(stages) else _dot(h_ref[...], wg_ref[...])
        epilogue(s, pending)
        pending = upcoming

    graw = pending
    beta = _sigmoid(graw)
    g = -jnp.exp(alog_ref[...]) * _softplus(graw + dt_ref[...])
    gc = _row_scan(g, GDN_CHUNK)
    lane_g = lax.broadcasted_iota(jnp.int32, (tm, LANES), 1)
    gates_ref[...] = jnp.where(lane_g < GDN_HEADS, beta, gc)


def _even_in(x2d, g, wm, wg, conv_w, alog_row, dt_row, cos_t, sin_t, seq_len):
    m, d = x2d.shape
    tm = EVEN_TOKEN_TILE
    n_main = wm.shape[1]
    tps = seq_len // tm
    kern = functools.partial(_even_in_kernel, tiles_per_seq=tps)
    row = lambda i: (i, 0)
    return pl.pallas_call(
        kern,
        grid=(m // tm,),
        in_specs=[
            pl.BlockSpec((tm, d), row),
            _const_spec((1, d)),
            _const_spec((d, n_main)),
            _const_spec((d, LANES)),
            _const_spec(conv_w.shape),
            _const_spec((1, LANES)),
            _const_spec((1, LANES)),
            pl.BlockSpec((tm, LANES), row),
            pl.BlockSpec((tm, LANES), row),
        ],
        out_specs=[
            pl.BlockSpec((tm, 3 * PROJ_SEG), row),
            pl.BlockSpec((tm, PROJ_SEG), row),
            pl.BlockSpec((1, PROJ_SEG, tm), lambda i: (i // tps, 0, i % tps)),
            pl.BlockSpec((tm, PROJ_SEG), row),
            pl.BlockSpec((1, PROJ_SEG, tm), lambda i: (i // tps, 0, i % tps)),
            pl.BlockSpec((tm, LANES), row),
        ],
        out_shape=[
            jax.ShapeDtypeStruct((m, 3 * PROJ_SEG), F32),
            jax.ShapeDtypeStruct((m, PROJ_SEG), BF16),
            jax.ShapeDtypeStruct((m // seq_len, PROJ_SEG, seq_len), BF16),
            jax.ShapeDtypeStruct((m, PROJ_SEG), BF16),
            jax.ShapeDtypeStruct((m // seq_len, PROJ_SEG, seq_len), BF16),
            jax.ShapeDtypeStruct((m, LANES), F32),
        ],
        scratch_shapes=[pltpu.VMEM((tm, d), BF16), pltpu.VMEM((SUBLANES, 3 * PROJ_SEG), F32), pltpu.VMEM((tm, PROJ_SEG), F32),
                        pltpu.VMEM((tm + SUBLANES, PROJ_SEG), F32)],
        compiler_params=_params(("arbitrary",)),
        name="even_in_proj",
    )(x2d, g, wm, wg, conv_w, alog_row, dt_row, cos_t, sin_t)


def _odd_in_kernel(x_ref, g_ref, wm_ref, wf_ref, bf_ref, sel_ref, ones_ref, qt_ref, k_ref, vt_ref, gate_ref, qbt_ref, kb_ref,
                   h_ref, carry_ref, tr_ref, *, tiles_per_seq, d_mix):
    tm = x_ref.shape[0]
    i = pl.program_id(0)
    h_ref[...] = _rms(x_ref[...], g_ref[...]).astype(BF16)
    head_dim = d_mix // FOX_HEADS
    for o_ref, base, scale in ((qt_ref, 0, head_dim ** -0.5 * LOG2E), (k_ref, d_mix, 1.0),
                               (vt_ref, 2 * d_mix, 1.0), (gate_ref, 3 * d_mix, 1.0)):
        for s in range(d_mix // PROJ_SEG):
            cols = slice(s * PROJ_SEG, (s + 1) * PROJ_SEG)
            y = _dot(h_ref[...], wm_ref[:, base + s * PROJ_SEG:base + (s + 1) * PROJ_SEG])
            if o_ref is qt_ref or o_ref is vt_ref:
                tr_ref[...] = y * scale
                o_ref[0, cols, :] = tr_ref[...].T.astype(BF16)
            else:
                o_ref[:, cols] = y.astype(BF16)
    f = _dot(h_ref[...], wf_ref[...]) + bf_ref[...]
    log_f = jnp.minimum(f, 0.0) - jnp.log1p(jnp.exp(-jnp.abs(f)))
    prev = jnp.where((i % tiles_per_seq) == 0, 0.0, carry_ref[0:1, :])
    cum = _row_scan(log_f, tm) + prev
    carry_ref[...] = jnp.broadcast_to(cum[tm - 1:tm, :], carry_ref.shape)
    pieces = jnp.concatenate(_split_bf16(LOG2E * cum, 3), axis=1)
    lanes = _dot(pieces, sel_ref[...]) + ones_ref[...]
    tr_ref[:, :LANES] = lanes[:, :LANES]
    qbt_ref[0] = tr_ref[:, :LANES].T.astype(BF16)
    kb_ref[...] = lanes[:, LANES:].astype(BF16)


def _bias_lane_tables():
    sel = np.zeros((3 * LANES, 2 * LANES), np.float32)
    ones = np.zeros((1, 2 * LANES), np.float32)
    for h in range(FOX_HEADS):
        base = BIAS_LANES_PER_HEAD * h
        for piece in range(3):
            sel[LANES * piece + h, base + 3 + piece] = 1.0
            sel[LANES * piece + h, LANES + base + piece] = -1.0
            ones[0, base + piece] = 1.0
            ones[0, LANES + base + 3 + piece] = 1.0
    return jnp.asarray(sel, BF16), jnp.asarray(ones, F32)


def _odd_in(x2d, g, wm, wf, bf_row, seq_len):
    m, d = x2d.shape
    sel, ones_row = _bias_lane_tables()
    tm = TOKEN_TILE
    d_mix = wm.shape[1] // 4
    tps = seq_len // tm
    kern = functools.partial(_odd_in_kernel, tiles_per_seq=tps, d_mix=d_mix)
    row = lambda i: (i, 0)
    row_blk = pl.BlockSpec((tm, d_mix), row)
    row_shape = jax.ShapeDtypeStruct((m, d_mix), BF16)
    col_blk = lambda width: pl.BlockSpec((1, width, tm), lambda i: (i // tps, 0, i % tps))
    col_shape = lambda width: jax.ShapeDtypeStruct((m // seq_len, width, seq_len), BF16)
    return pl.pallas_call(
        kern,
        grid=(m // tm,),
        in_specs=[
            pl.BlockSpec((tm, d), row),
            _const_spec((1, d)),
            _const_spec(wm.shape),
            _const_spec((d, LANES)),
            _const_spec((1, LANES)),
            _const_spec(sel.shape),
            _const_spec(ones_row.shape),
        ],
        out_specs=[col_blk(d_mix), row_blk, col_blk(d_mix), row_blk, col_blk(LANES), pl.BlockSpec((tm, LANES), row)],
        out_shape=[col_shape(d_mix), row_shape, col_shape(d_mix), row_shape, col_shape(LANES),
                   jax.ShapeDtypeStruct((m, LANES), BF16)],
        scratch_shapes=[pltpu.VMEM((tm, d), BF16), pltpu.VMEM((SUBLANES, LANES), F32), pltpu.VMEM((tm, PROJ_SEG), F32)],
        compiler_params=_params(("arbitrary",)),
        name="odd_in_proj",
    )(x2d, g, wm, wf, bf_row, sel, ones_row)


def _attn_kernel(*refs, tq, fox, lambda_init):
    if fox:
        qt_ref, k_ref, vt_ref, gate_ref, qbt_ref, kball_ref, o_ref, st_ref, mx_ref, m_ref, acc_ref, kb_ref = refs
    else:
        qt_ref, k_ref, vt_ref, lam_ref, nw_ref, o_ref, st_ref, mx_ref, m_ref, acc_ref = refs
    tk = tq
    hg = pl.program_id(1)
    n_tiles = k_ref.shape[1] // tq
    head = lambda g: slice(g * HEAD_LANES, (g + 1) * HEAD_LANES)
    kv = [head(0), head(1)] if fox else [head(0), head(0)]
    tile_rows = lambda tile: pl.ds(pl.multiple_of(tile * tq, tq), tq)

    def queries(g, tile):
        if fox:
            return jnp.concatenate([qt_ref[0, head(g), tile_rows(tile)], qbt_ref[0, :, tile_rows(tile)]], axis=0)
        q = qt_ref[0, :, tile_rows(tile)]
        dim = lax.broadcasted_iota(jnp.int32, q.shape, 0)
        keep = (dim < DIFF_QK_DIM) if g == 0 else (dim >= DIFF_QK_DIM)
        return jnp.where(keep, q, jnp.zeros_like(q))

    def scores_of(qmat, g, j):
        k0 = pl.multiple_of(j * tk, tk)
        kj = k_ref[0, pl.ds(k0, tk), kv[g]]
        if fox:
            kj = jnp.concatenate([kj, kb_ref[pl.ds(k0, tk), kv[g]]], axis=1)
        return _dot(kj, qmat)

    if fox:
        kb_all = kball_ref[0]
        owner = lax.broadcasted_iota(jnp.int32, kb_all.shape, 1) // BIAS_LANES_PER_HEAD
        for g in range(2):
            kb_ref[:, head(g)] = jnp.where(owner == hg * 2 + g, kb_all, jnp.zeros_like(kb_all))
    else:
        lam_p = lam_ref[...]
        lam = (jnp.exp(jnp.sum(lam_p[0:1] * lam_p[1:2], axis=1, keepdims=True))
               - jnp.exp(jnp.sum(lam_p[2:3] * lam_p[3:4], axis=1, keepdims=True)) + lambda_init)
    ones = jnp.ones((ONES_ROWS, tk), BF16)

    def lookahead(g, qmat, tile, j, diagonal):
        st = scores_of(qmat, g, j)
        if diagonal:
            kpos = j * tk + lax.broadcasted_iota(jnp.int32, (tk, tq), 0)
            qpos = tile * tq + lax.broadcasted_iota(jnp.int32, (tk, tq), 1)
            st = jnp.where(qpos >= kpos, st, NEG_INF)
        st_ref[g] = st
        mx_ref[g] = jnp.max(st, axis=0, keepdims=True)

    def absorb(g, j):
        k0 = pl.multiple_of(j * tk, tk)
        vt = jnp.concatenate([vt_ref[0, kv[g], pl.ds(k0, tk)], ones], axis=0)
        width = tq // ATTN_COLUMN_SPLIT
        for part in range(ATTN_COLUMN_SPLIT):
            cs = slice(part * width, (part + 1) * width)
            m = m_ref[g, :, cs]
            m_new = jnp.maximum(m, mx_ref[g, :, cs])
            p = jnp.exp2(st_ref[g, :, cs] - m_new).astype(BF16)
            acc_ref[g, :, cs] = jnp.exp2(m - m_new) * acc_ref[g, :, cs] + _dot(vt, p)
            m_ref[g, :, cs] = m_new

    for g in range(2):
        lookahead(g, queries(g, 0), 0, 0, True)

    def tile(qi, carry):
        qs = [queries(g, qi) for g in range(2)]

        def step(j, next_is_diagonal):
            for g in range(2):
                absorb(g, j)
                lookahead(g, qs[g], qi, j + 1, next_is_diagonal)

        def steps(j, c):
            for u in range(ATTN_UNROLL):
                step(j * ATTN_UNROLL + u, False)
            return c

        def single_step(j, c):
            step(j, False)
            return c

        def last_step(j, c):
            step(j, True)
            return c

        m_ref[...] = jnp.full(m_ref.shape, NEG_INF, F32)
        acc_ref[...] = jnp.zeros(acc_ref.shape, F32)
        n_plain = jnp.maximum(qi - 1, 0)
        n_full = n_plain // ATTN_UNROLL
        lax.fori_loop(0, n_full, steps, 0)
        lax.fori_loop(n_full * ATTN_UNROLL, n_plain, single_step, 0)
        lax.fori_loop(n_plain, qi, last_step, 0)
        nxt = jnp.minimum(qi + 1, n_tiles - 1)
        for g in range(2):
            absorb(g, qi)
            lookahead(g, queries(g, nxt), nxt, 0, False)
        outs = [acc_ref[g, :HEAD_LANES, :] / acc_ref[g, HEAD_LANES:HEAD_LANES + 1, :] for g in range(2)]
        rows = tile_rows(qi)
        if fox:
            for g in range(2):
                gate = _sigmoid(gate_ref[0, rows, head(g)].astype(F32))
                o_ref[0, rows, head(g)] = (outs[g].T * gate).astype(o_ref.dtype)
        else:
            o = (outs[0] - lam * outs[1]).T
            o_ref[0, rows, :] = (_rms(o, nw_ref[...]) * (1.0 - lambda_init)).astype(o_ref.dtype)
        return carry

    lax.fori_loop(0, n_tiles, tile, 0)


def _attn_state(tq):
    return [pltpu.VMEM((2, tq, tq), F32), pltpu.VMEM((2, 1, tq), F32), pltpu.VMEM((2, 1, tq), F32),
            pltpu.VMEM((2, HEAD_LANES + ONES_ROWS, tq), F32)]


def _fox_attention(qt, k, vt, gate, qbt, kb, *, tq=ATTN_TILE):
    b, t, dm = k.shape
    width = 2 * HEAD_LANES
    kern = functools.partial(_attn_kernel, tq=tq, fox=True, lambda_init=0.0)
    seq = pl.BlockSpec((1, t, width), lambda bi, h: (bi, 0, h))
    seq_t = pl.BlockSpec((1, width, t), lambda bi, h: (bi, h, 0))
    seq_bias = pl.BlockSpec((1, t, LANES), lambda bi, h: (bi, 0, 0))
    return pl.pallas_call(
        kern,
        grid=(b, dm // width),
        in_specs=[seq_t, seq, seq_t, seq, pl.BlockSpec((1, LANES, t), lambda bi, h: (bi, 0, 0)), seq_bias],
        out_specs=seq,
        out_shape=jax.ShapeDtypeStruct((b, t, dm), BF16),
        scratch_shapes=_attn_state(tq) + [pltpu.VMEM((t, width), BF16)],
        compiler_params=_params(("arbitrary", "arbitrary")),
        name="fox_attention",
    )(qt, k, vt, gate, qbt, kb)


def _diff_attention(qt, k, vt, lam_params, norm_w, lambda_init, *, tq=ATTN_TILE):
    b, t, _ = k.shape
    nh = DIFF_HEADS
    kern = functools.partial(_attn_kernel, tq=tq, fox=False, lambda_init=lambda_init)
    seq = pl.BlockSpec((1, t, HEAD_LANES), lambda bi, h: (bi, 0, h))
    seq_t = pl.BlockSpec((1, HEAD_LANES, t), lambda bi, h: (bi, h, 0))
    return pl.pallas_call(
        kern,
        grid=(b, nh),
        in_specs=[seq_t, seq, seq_t, _const_spec(lam_params.shape), _const_spec((1, HEAD_LANES))],
        out_specs=seq,
        out_shape=jax.ShapeDtypeStruct((b, t, nh * HEAD_LANES), BF16),
        scratch_shapes=_attn_state(tq),
        compiler_params=_params(("arbitrary", "arbitrary")),
        name="diff_attention",
    )(qt, k, vt, lam_params, norm_w)


def _gdn_prep_kernel(q_ref, k_ref, v_ref, gates_ref, u_ref, w_ref, qd_ref, kd_ref, qk_ref):
    c = GDN_CHUNK
    sub = GDN_PREP_SUBTILE
    units = [(slice(t0, t0 + sub), hd) for t0 in range(0, q_ref.shape[1], sub) for hd in range(GDN_HEADS)]
    ids = range(len(units))
    col = lambda hd: slice(hd * HEAD_LANES, (hd + 1) * HEAD_LANES)
    lane = lax.broadcasted_iota(jnp.int32, (sub, LANES), 1)
    ri = lax.broadcasted_iota(jnp.int32, (sub, sub), 0)
    ci = lax.broadcasted_iota(jnp.int32, (sub, sub), 1)
    chunk_start = ri - ri % c
    incl = lambda a: jnp.where(ci <= ri, jnp.where(ci >= chunk_start, a, 0.0), 0.0)
    strict = lambda a: jnp.where(ci < ri, jnp.where(ci >= chunk_start, a, 0.0), 0.0)
    ident = jnp.where(ri == ci, 1.0, 0.0)
    ones = jnp.ones((sub, LANES), BF16)
    kt = [k_ref[0, rows, col(hd)] for rows, hd in units]
    beta = [gates_ref[0, rows, hd:hd + 1] for rows, hd in units]
    gcc = [gates_ref[0, rows, GDN_HEADS + hd:GDN_HEADS + hd + 1] for rows, hd in units]
    k16 = [kt[i].astype(BF16) for i in ids]
    kb = [kt[i] * beta[i] for i in ids]

    gc_row = []
    for i in ids:
        g_hi, g_mid, g_lo = (piece.astype(F32) for piece in _split_bf16(gcc[i], 3))
        pieces = jnp.where(lane == 0, g_hi, jnp.where(lane == 1, g_mid, jnp.where(lane == 2, g_lo, 0.0)))
        gc_row.append(_dot_nt(ones, pieces.astype(BF16)))
    kk = [_dot_nt(kb[i].astype(BF16), k16[i]) for i in ids]
    qk_raw = [_dot_nt(q_ref[0, rows, col(hd)].astype(BF16), k16[i]) for i, (rows, hd) in enumerate(units)]
    decay = [incl(jnp.exp(incl(gcc[i] - gc_row[i]))) for i in ids]
    lower = [strict(kk[i] * decay[i]) for i in ids]

    span = ri ^ ci
    inv = [ident - jnp.where(span == 1, lower[i], 0.0) for i in ids]
    s_blk = 2
    while s_blk < c:
        shift = int(math.log2(s_blk))
        inv16 = [inv[i].astype(BF16) for i in ids]
        coupled = [_dot(jnp.where((span >> shift) == 1, lower[i], 0.0).astype(BF16), inv16[i]) for i in ids]
        inv = [inv[i] - _dot(inv16[i], coupled[i].astype(BF16)) for i in ids]
        s_blk *= 2
    inv16 = [inv[i].astype(BF16) for i in ids]

    eg = [jnp.exp(gcc[i]) for i in ids]
    rhs = [jnp.concatenate([v_ref[0, rows, col(hd)] * beta[i], kb[i] * eg[i]], axis=1)
           for i, (rows, hd) in enumerate(units)]
    sol = [_dot(inv16[i], rhs[i].astype(BF16)) for i in ids]
    a_hi, a_lo, s_hi, s_lo = [], [], [], []
    for i in ids:
        hi, lo = _split_bf16(ident + lower[i], 2)
        a_hi.append(hi)
        a_lo.append(lo)
        hi, lo = _split_bf16(sol[i], 2)
        s_hi.append(hi)
        s_lo.append(lo)
    prod = [_dot(a_hi[i], s_hi[i]) + (_dot(a_hi[i], s_lo[i]) + _dot(a_lo[i], s_hi[i])) for i in ids]
    corr = [_dot(inv16[i], (rhs[i] - prod[i]).astype(BF16)) for i in ids]
    sol = [sol[i] + corr[i] for i in ids]
    for i, (rows, hd) in enumerate(units):
        u_ref[0, rows, col(hd)] = sol[i][:, :HEAD_LANES]
        w_ref[0, rows, col(hd)] = sol[i][:, HEAD_LANES:].astype(BF16)
        qk = incl(qk_raw[i] * decay[i])
        qd_ref[0, rows, col(hd)] = (q_ref[0, rows, col(hd)] * eg[i]).astype(BF16)
        for n in range(sub // c):
            blk = slice(n * c, (n + 1) * c)
            out_rows = slice(rows.start + n * c, rows.start + (n + 1) * c)
            qk_ref[0, hd, out_rows, :] = qk[blk, blk].astype(BF16)
            gl = gcc[i][(n + 1) * c - 1:(n + 1) * c, :]
            kd_ref[0, out_rows, col(hd)] = (kt[i][blk] * jnp.exp(gl - gcc[i][blk])).astype(BF16)


def _gdn_prep(qkv, gates, *, tt=GDN_PREP_TILE):
    b, t, _ = qkv.shape
    nh = GDN_HEADS
    dm = nh * HEAD_LANES
    blk = lambda part: pl.BlockSpec((1, tt, dm), lambda bi, i: (bi, i, part))
    return pl.pallas_call(
        _gdn_prep_kernel,
        grid=(b, t // tt),
        in_specs=[blk(0), blk(1), blk(2), pl.BlockSpec((1, tt, LANES), lambda bi, i: (bi, i, 0))],
        out_specs=[blk(0)] * 4 + [pl.BlockSpec((1, nh, tt, GDN_CHUNK), lambda bi, i: (bi, 0, i, 0))],
        out_shape=[jax.ShapeDtypeStruct((b, t, dm), F32)]
        + [jax.ShapeDtypeStruct((b, t, dm), BF16)] * 3
        + [jax.ShapeDtypeStruct((b, nh, t, GDN_CHUNK), BF16)],
        compiler_params=_params(("arbitrary", "arbitrary")),
        name="gdn_prep",
    )(qkv, qkv, qkv, gates)


def _gdn_scan_kernel(u_ref, w_ref, qd_ref, kd_ref, qk_ref, gates_ref, z_ref, nw_ref, o_ref, s_ref):
    c = GDN_CHUNK
    nb, tt = u_ref.shape[0], u_ref.shape[1]

    @pl.when(pl.program_id(1) == 0)
    def _():
        s_ref[...] = jnp.zeros_like(s_ref)

    chains = [(bi, hd) for bi in range(nb) for hd in range(GDN_HEADS)]
    cols = [slice(hd * HEAD_LANES, (hd + 1) * HEAD_LANES) for hd in range(GDN_HEADS)]
    state = [s_ref[bi, hd] for bi, hd in chains]
    for n in range(tt // c):
        rows = slice(n * c, (n + 1) * c)
        r = [_dot(jnp.concatenate([w_ref[bi, rows, cols[hd]], qd_ref[bi, rows, cols[hd]]], axis=0),
                  state[i].astype(BF16)) for i, (bi, hd) in enumerate(chains)]
        v_new = [(u_ref[bi, rows, cols[hd]] - r[i][:c]).astype(BF16) for i, (bi, hd) in enumerate(chains)]
        intra = [_dot(qk_ref[bi, hd, rows, :], v_new[i]) for i, (bi, hd) in enumerate(chains)]
        upd = [_dot_tn(kd_ref[bi, rows, cols[hd]], v_new[i]) for i, (bi, hd) in enumerate(chains)]
        for i, (bi, hd) in enumerate(chains):
            last = (n + 1) * c - 1
            decay_last = jnp.exp(gates_ref[bi, last:last + 1, GDN_HEADS + hd:GDN_HEADS + hd + 1])
            state[i] = state[i] * decay_last + upd[i]
            zt = z_ref[bi, rows, cols[hd]].astype(F32)
            o = r[i][c:] + intra[i]
            o_ref[bi, rows, cols[hd]] = (_rms(o, nw_ref[...]) * (zt * _sigmoid(zt))).astype(o_ref.dtype)
    for i, (bi, hd) in enumerate(chains):
        s_ref[bi, hd] = state[i]


def _gdn_scan(u, w, qd, kd, qk, gates, z, norm_w, *, tt=GDN_SCAN_TILE, nb=GDN_SCAN_BATCH):
    b, t, dm = u.shape
    nh = GDN_HEADS
    assert b % nb == 0 and t % tt == 0
    blk = pl.BlockSpec((nb, tt, dm), lambda bi, i: (bi, i, 0))
    return pl.pallas_call(
        _gdn_scan_kernel,
        grid=(b // nb, t // tt),
        in_specs=[blk, blk, blk, blk,
                  pl.BlockSpec((nb, nh, tt, GDN_CHUNK), lambda bi, i: (bi, 0, i, 0)),
                  pl.BlockSpec((nb, tt, LANES), lambda bi, i: (bi, i, 0)),
                  blk, _const_spec((1, HEAD_LANES))],
        out_specs=blk,
        out_shape=jax.ShapeDtypeStruct((b, t, dm), BF16),
        scratch_shapes=[pltpu.VMEM((nb, nh, GDN_HEAD_DIM, GDN_HEAD_DIM), F32)],
        compiler_params=_params(("arbitrary", "arbitrary")),
        name="gdn_scan",
    )(u, w, qd, kd, qk, gates, z, norm_w)


def _post_kernel(*refs, n_mix, final_norm):
    x_ref = refs[0]
    mix_refs = refs[1:1 + n_mix]
    wout_ref, g_ref, wup_ref, wdn_ref, p_ref, wpp_ref, wpg_ref = refs[1 + n_mix:8 + n_mix]
    rest = refs[8 + n_mix:]
    if final_norm:
        gf_ref, o_ref = rest
    else:
        (o_ref,) = rest
    mix = mix_refs[0][...] if n_mix == 1 else jnp.concatenate([r[...] for r in mix_refs], axis=1)
    x = x_ref[...] + _dot(mix, wout_ref[...])
    h = _rms(x, g_ref[...]).astype(BF16)
    d_ff = wup_ref.shape[1]
    acc = x
    for s in range(d_ff // FF_SEG):
        a = jnp.maximum(_dot(h, wup_ref[:, s * FF_SEG:(s + 1) * FF_SEG]), 0.0)
        acc = acc + _dot((a * a).astype(BF16), wdn_ref[s * FF_SEG:(s + 1) * FF_SEG, :])
    x = acc
    gate = _sigmoid(_dot(x.astype(BF16), wpg_ref[...]))
    x = x + _dot(p_ref[...].astype(BF16), wpp_ref[...]) * gate
    if final_norm:
        x = _rms(x, gf_ref[...])
    o_ref[...] = x


def _post(x2d, mixes, wout, g, wup, wdn, p2d, wpp, wpg, gf=None):
    m, d = x2d.shape
    tm = TOKEN_TILE
    row = lambda i: (i, 0)
    single = pl.Buffered(1)
    const = lambda a: pl.BlockSpec(a.shape, lambda i: (0, 0), pipeline_mode=single)
    args = [x2d, *mixes, wout, g, wup, wdn, p2d, wpp, wpg]
    in_specs = ([pl.BlockSpec((tm, d), row)]
                + [pl.BlockSpec((tm, a.shape[1]), row) for a in mixes]
                + [const(wout), const(g), const(wup), const(wdn), pl.BlockSpec((tm, p2d.shape[1]), row), const(wpp), const(wpg)])
    if gf is not None:
        args.append(gf)
        in_specs.append(const(gf))
    kern = functools.partial(_post_kernel, n_mix=len(mixes), final_norm=gf is not None)
    return pl.pallas_call(
        kern,
        grid=(m // tm,),
        in_specs=in_specs,
        out_specs=pl.BlockSpec((tm, d), row),
        out_shape=jax.ShapeDtypeStruct((m, d), F32),
        compiler_params=_params(("arbitrary",)),
        name="out_proj_mlp_ple",
    )(*args)


def _pad_lanes(a):
    return jnp.pad(a, ((0, 0), (0, LANES - a.shape[1])))


def kernel(x, p, positions, norm_mix, norm_mlp, norm_final, w_in_even, conv_w, a_log, dt_bias, gdn_norm,
           lam_q1, lam_k1, lam_q2, lam_k2, diff_norm, w_out_even, w_in_odd, b_forget, w_out_odd,
           w_mlp_up, w_mlp_down, w_ple_proj, w_ple_gate):
    b, t, d = x.shape
    depth = p.shape[0]
    m = b * t
    assert t % TOKEN_TILE == 0 and d % PROJ_SEG == 0
    nh = GDN_HEADS
    gdn_w = 3 * nh * GDN_HEAD_DIM + nh * GDN_HEAD_DIM
    assert w_in_even.shape[2] == gdn_w + 2 * nh + 3 * DIFF_HEADS * 2 * DIFF_QK_DIM

    inv_freq = ROPE_THETA ** (-jnp.arange(0, DIFF_QK_DIM, 2, dtype=F32) / DIFF_QK_DIM)
    ang = positions.astype(F32)[..., None] * inv_freq
    cos, sin = jnp.cos(ang), jnp.sin(ang)
    cos_t = jnp.concatenate([cos, cos, cos, cos], axis=-1).reshape(m, LANES)
    sin_t = jnp.concatenate([-sin, sin, -sin, sin], axis=-1).reshape(m, LANES)

    x2d = x.reshape(m, d)
    for i in range(depth):
        j = i // 2
        g_mix = norm_mix[i].reshape(1, d)
        if i % 2 == 0:
            lambda_init = 0.8 - 0.6 * math.exp(-0.3 * i)
            w = w_in_even[j]
            wm = jnp.concatenate([w[:, :gdn_w], w[:, gdn_w + 2 * nh:]], axis=1).astype(BF16)
            wg = _pad_lanes(w[:, gdn_w:gdn_w + 2 * nh]).astype(BF16)
            alog_row = _pad_lanes(jnp.concatenate([jnp.zeros((nh,), F32), a_log[j]]).reshape(1, 2 * nh))
            dt_row = _pad_lanes(jnp.concatenate([jnp.zeros((nh,), F32), dt_bias[j]]).reshape(1, 2 * nh))
            qkv, z, qbt, kb, vbt, gates = _even_in(x2d, g_mix, wm, wg, conv_w[j], alog_row, dt_row, cos_t, sin_t, t)
            qkv, z, kb, gates = (a.reshape(b, t, -1) for a in (qkv, z, kb, gates))
            u, wy, qd, kd, qk = _gdn_prep(qkv, gates)
            o_a = _gdn_scan(u, wy, qd, kd, qk, gates, z, gdn_norm[j].reshape(1, HEAD_LANES))
            lam_params = jnp.stack([lam_q1[j], lam_k1[j], lam_q2[j], lam_k2[j]])
            o_b = _diff_attention(qbt, kb, vbt, lam_params, diff_norm[j].reshape(1, HEAD_LANES), lambda_init)
            mixes = [o_a.reshape(m, -1), o_b.reshape(m, -1)]
            wout = w_out_even[j].astype(BF16)
        else:
            w = w_in_odd[j]
            d_mix = (w.shape[1] - FOX_HEADS) // 4
            wm = w[:, :4 * d_mix].astype(BF16)
            wf = _pad_lanes(w[:, 4 * d_mix:]).astype(BF16)
            bf_row = _pad_lanes(b_forget[j].reshape(1, FOX_HEADS))
            qt, k, vt, gate, qbt, kb = _odd_in(x2d, g_mix, wm, wf, bf_row, t)
            k, gate, kb = (a.reshape(b, t, -1) for a in (k, gate, kb))
            o = _fox_attention(qt, k, vt, gate, qbt, kb)
            mixes = [o.reshape(m, -1)]
            wout = w_out_odd[j].astype(BF16)
        x2d = _post(x2d, mixes, wout, norm_mlp[i].reshape(1, d), w_mlp_up[i].astype(BF16),
                    w_mlp_down[i].astype(BF16), p[i].reshape(m, -1), w_ple_proj[i].astype(BF16),
                    w_ple_gate[i].astype(BF16), norm_final.reshape(1, d) if i == depth - 1 else None)
    return x2d.reshape(b, t, d)
```

```python
import functools
import math

import jax
import jax.numpy as jnp
import numpy as np
from jax import lax
from jax.experimental import pallas as pl
from jax.experimental.pallas import tpu as pltpu

F32 = jnp.float32
BF16 = jnp.bfloat16

GDN_HEADS = 4
GDN_HEAD_DIM = 128
GDN_CHUNK = 64
CONV_WIDTH = 4
DIFF_HEADS = 4
DIFF_QK_DIM = 64
FOX_HEADS = 8
HEAD_LANES = 128
ROPE_THETA = 10000.0
EPS = 1e-6
NEG_INF = -1e30
LOG2E = 1.4426950408889634
LANES = 128
SUBLANES = 8
VMEM_LIMIT_BYTES = 56 * 1024 * 1024

TOKEN_TILE = 1024
EVEN_TOKEN_TILE = 512
PROJ_SEG = 512
FF_SEG = 1024
GDN_PREP_SUBTILE = 256
GDN_PREP_TILE = 512
GDN_SCAN_TILE = 256
GDN_SCAN_BATCH = 4
ATTN_TILE = 512
ATTN_COLUMN_SPLIT = 2
ATTN_UNROLL = 2
BIAS_LANES_PER_HEAD = 16
ONES_ROWS = 16


def _dot(a, b):
    return jnp.dot(a, b, preferred_element_type=F32)


def _split_bf16(x, parts):
    out = []
    for _ in range(parts):
        piece = x.astype(BF16)
        out.append(piece)
        x = x - piece.astype(F32)
    return out


def _dot_nt(a, b):
    return lax.dot_general(a, b, (((1,), (1,)), ((), ())), preferred_element_type=F32)


def _dot_tn(a, b):
    return lax.dot_general(a, b, (((0,), (0,)), ((), ())), preferred_element_type=F32)


def _rms(x, g):
    return x * lax.rsqrt(jnp.mean(x * x, axis=-1, keepdims=True) + EPS) * g


def _sigmoid(x):
    return 1.0 / (1.0 + jnp.exp(-x))


def _softplus(x):
    return jnp.maximum(x, 0.0) + jnp.log1p(jnp.exp(-jnp.abs(x)))


def _row_scan(x, period):
    rows = lax.broadcasted_iota(jnp.int32, x.shape, 0) % period
    s = 1
    while s < period:
        x = x + jnp.where(rows >= s, pltpu.roll(x, s, 0), 0.0)
        s *= 2
    return x


def _const_spec(shape):
    return pl.BlockSpec(shape, lambda *_: (0,) * len(shape))


def _params(sem):
    return pltpu.CompilerParams(dimension_semantics=sem, vmem_limit_bytes=VMEM_LIMIT_BYTES)


def _even_in_kernel(x_ref, g_ref, wm_ref, wg_ref, conv_ref, alog_ref, dt_ref, cos_ref, sin_ref,
                    qkv_ref, z_ref, qkb_ref, vbt_ref, gates_ref, h_ref, carry_ref, tr_ref, pad_ref, *, tiles_per_seq):
    tm = x_ref.shape[0]
    i = pl.program_id(0)
    h_ref[...] = _rms(x_ref[...], g_ref[...]).astype(BF16)
    seq_start = (i % tiles_per_seq) == 0
    seg = lambda s: slice(s * PROJ_SEG, (s + 1) * PROJ_SEG)
    project = lambda s: _dot(h_ref[...], wm_ref[:, seg(s)])

    def gdn_qkv(s, y):
        cols = seg(s)
        pad_ref[0:SUBLANES, :] = jnp.where(seq_start, 0.0, carry_ref[:, cols])
        pad_ref[SUBLANES:, :] = y
        carry_ref[:, cols] = y[tm - SUBLANES:, :]
        w = conv_ref[:, cols]
        a = y * w[CONV_WIDTH - 1:CONV_WIDTH, :]
        for k in range(1, CONV_WIDTH):
            a = a + pad_ref[SUBLANES - k:SUBLANES - k + tm, :] * w[CONV_WIDTH - 1 - k:CONV_WIDTH - k, :]
        a = a * _sigmoid(a)
        if s < 2:
            outs = []
            for hd in range(GDN_HEADS):
                blk = a[:, hd * HEAD_LANES:(hd + 1) * HEAD_LANES]
                n = blk * lax.rsqrt(jnp.sum(blk * blk, axis=-1, keepdims=True) + EPS)
                outs.append(n * (GDN_HEAD_DIM ** -0.5) if s == 0 else n)
            a = jnp.concatenate(outs, axis=1)
        qkv_ref[:, cols] = a

    def gdn_gate(s, y):
        z_ref[...] = y.astype(BF16)

    def diff_qk(s, y):
        cos = jnp.concatenate([cos_ref[...]] * (PROJ_SEG // LANES), axis=1)
        sin = jnp.concatenate([sin_ref[...]] * (PROJ_SEG // LANES), axis=1)
        lane = lax.broadcasted_iota(jnp.int32, (tm, PROJ_SEG), 1)
        first_half = (lane % DIFF_QK_DIM) < (DIFF_QK_DIM // 2)
        swapped = jnp.where(first_half, pltpu.roll(y, PROJ_SEG - DIFF_QK_DIM // 2, 1),
                            pltpu.roll(y, DIFF_QK_DIM // 2, 1))
        scale = DIFF_QK_DIM ** -0.5 * LOG2E if s == 4 else 1.0
        qkb_ref[:, seg(s - 4)] = ((y * cos + swapped * sin) * scale).astype(BF16)

    def diff_v(s, y):
        tr_ref[...] = y
        vbt_ref[0] = tr_ref[...].T.astype(BF16)

    stages = ((0, gdn_qkv), (3, gdn_gate), (1, gdn_qkv), (6, diff_v), (2, gdn_qkv), (4, diff_qk), (5, diff_qk))
    pending = project(stages[0][0])
    for n, (s, epilogue) in enumerate(stages):
        upcoming = project(stages[n + 1][0]) if n + 1 < len(stages) else _dot(h_ref[...], wg_ref[...])
        epilogue(s, pending)
        pending = upcoming

    graw = pending
    beta = _sigmoid(graw)
    g = -jnp.exp(alog_ref[...]) * _softplus(graw + dt_ref[...])
    gc = _row_scan(g, GDN_CHUNK)
    lane_g = lax.broadcasted_iota(jnp.int32, (tm, LANES), 1)
    gates_ref[...] = jnp.where(lane_g < GDN_HEADS, beta, gc)


def _even_in(x2d, g, wm, wg, conv_w, alog_row, dt_row, cos_t, sin_t, seq_len):
    m, d = x2d.shape
    tm = EVEN_TOKEN_TILE
    n_main = wm.shape[1]
    tps = seq_len // tm
    kern = functools.partial(_even_in_kernel, tiles_per_seq=tps)
    row = lambda i: (i, 0)
    return pl.pallas_call(
        kern,
        grid=(m // tm,),
        in_specs=[
            pl.BlockSpec((tm, d), row),
            _const_spec((1, d)),
            _const_spec((d, n_main)),
            _const_spec((d, LANES)),
            _const_spec(conv_w.shape),
            _const_spec((1, LANES)),
            _const_spec((1, LANES)),
            pl.BlockSpec((tm, LANES), row),
            pl.BlockSpec((tm, LANES), row),
        ],
        out_specs=[
            pl.BlockSpec((tm, 3 * PROJ_SEG), row),
            pl.BlockSpec((tm, PROJ_SEG), row),
            pl.BlockSpec((tm, 2 * PROJ_SEG), row),
            pl.BlockSpec((1, PROJ_SEG, tm), lambda i: (i // tps, 0, i % tps)),
            pl.BlockSpec((tm, LANES), row),
        ],
        out_shape=[
            jax.ShapeDtypeStruct((m, 3 * PROJ_SEG), F32),
            jax.ShapeDtypeStruct((m, PROJ_SEG), BF16),
            jax.ShapeDtypeStruct((m, 2 * PROJ_SEG), BF16),
            jax.ShapeDtypeStruct((m // seq_len, PROJ_SEG, seq_len), BF16),
            jax.ShapeDtypeStruct((m, LANES), F32),
        ],
        scratch_shapes=[pltpu.VMEM((tm, d), BF16), pltpu.VMEM((SUBLANES, 3 * PROJ_SEG), F32), pltpu.VMEM((tm, PROJ_SEG), F32),
                        pltpu.VMEM((tm + SUBLANES, PROJ_SEG), F32)],
        compiler_params=_params(("arbitrary",)),
        name="even_in_proj",
    )(x2d, g, wm, wg, conv_w, alog_row, dt_row, cos_t, sin_t)


def _odd_in_kernel(x_ref, g_ref, wm_ref, wf_ref, bf_ref, sel_ref, ones_ref, qt_ref, k_ref, vt_ref, gate_ref, qbt_ref, kb_ref,
                   h_ref, carry_ref, tr_ref, *, tiles_per_seq, d_mix):
    tm = x_ref.shape[0]
    i = pl.program_id(0)
    h_ref[...] = _rms(x_ref[...], g_ref[...]).astype(BF16)
    head_dim = d_mix // FOX_HEADS
    for o_ref, base, scale in ((qt_ref, 0, head_dim ** -0.5 * LOG2E), (k_ref, d_mix, 1.0),
                               (vt_ref, 2 * d_mix, 1.0), (gate_ref, 3 * d_mix, 1.0)):
        for s in range(d_mix // PROJ_SEG):
            cols = slice(s * PROJ_SEG, (s + 1) * PROJ_SEG)
            y = _dot(h_ref[...], wm_ref[:, base + s * PROJ_SEG:base + (s + 1) * PROJ_SEG])
            if o_ref is qt_ref or o_ref is vt_ref:
                tr_ref[...] = y * scale
                o_ref[0, cols, :] = tr_ref[...].T.astype(BF16)
            else:
                o_ref[:, cols] = y.astype(BF16)
    f = _dot(h_ref[...], wf_ref[...]) + bf_ref[...]
    log_f = jnp.minimum(f, 0.0) - jnp.log1p(jnp.exp(-jnp.abs(f)))
    prev = jnp.where((i % tiles_per_seq) == 0, 0.0, carry_ref[0:1, :])
    cum = _row_scan(log_f, tm) + prev
    carry_ref[...] = jnp.broadcast_to(cum[tm - 1:tm, :], carry_ref.shape)
    pieces = jnp.concatenate(_split_bf16(LOG2E * cum, 3), axis=1)
    lanes = _dot(pieces, sel_ref[...]) + ones_ref[...]
    tr_ref[:, :LANES] = lanes[:, :LANES]
    qbt_ref[0] = tr_ref[:, :LANES].T.astype(BF16)
    kb_ref[...] = lanes[:, LANES:].astype(BF16)


def _bias_lane_tables():
    sel = np.zeros((3 * LANES, 2 * LANES), np.float32)
    ones = np.zeros((1, 2 * LANES), np.float32)
    for h in range(FOX_HEADS):
        base = BIAS_LANES_PER_HEAD * h
        for piece in range(3):
            sel[LANES * piece + h, base + 3 + piece] = 1.0
            sel[LANES * piece + h, LANES + base + piece] = -1.0
            ones[0, base + piece] = 1.0
            ones[0, LANES + base + 3 + piece] = 1.0
    return jnp.asarray(sel, BF16), jnp.asarray(ones, F32)


def _odd_in(x2d, g, wm, wf, bf_row, seq_len):
    m, d = x2d.shape
    sel, ones_row = _bias_lane_tables()
    tm = TOKEN_TILE
    d_mix = wm.shape[1] // 4
    tps = seq_len // tm
    kern = functools.partial(_odd_in_kernel, tiles_per_seq=tps, d_mix=d_mix)
    row = lambda i: (i, 0)
    row_blk = pl.BlockSpec((tm, d_mix), row)
    row_shape = jax.ShapeDtypeStruct((m, d_mix), BF16)
    col_blk = lambda width: pl.BlockSpec((1, width, tm), lambda i: (i // tps, 0, i % tps))
    col_shape = lambda width: jax.ShapeDtypeStruct((m // seq_len, width, seq_len), BF16)
    return pl.pallas_call(
        kern,
        grid=(m // tm,),
        in_specs=[
            pl.BlockSpec((tm, d), row),
            _const_spec((1, d)),
            _const_spec(wm.shape),
            _const_spec((d, LANES)),
            _const_spec((1, LANES)),
            _const_spec(sel.shape),
            _const_spec(ones_row.shape),
        ],
        out_specs=[col_blk(d_mix), row_blk, col_blk(d_mix), row_blk, col_blk(LANES), pl.BlockSpec((tm, LANES), row)],
        out_shape=[col_shape(d_mix), row_shape, col_shape(d_mix), row_shape, col_shape(LANES),
                   jax.ShapeDtypeStruct((m, LANES), BF16)],
        scratch_shapes=[pltpu.VMEM((tm, d), BF16), pltpu.VMEM((SUBLANES, LANES), F32), pltpu.VMEM((tm, PROJ_SEG), F32)],
        compiler_params=_params(("arbitrary",)),
        name="odd_in_proj",
    )(x2d, g, wm, wf, bf_row, sel, ones_row)


def _attn_kernel(*refs, tq, fox, lambda_init):
    if fox:
        q_ref, k_ref, vt_ref, gate_ref, qbt_ref, kball_ref, o_ref, st_ref, mx_ref, m_ref, acc_ref, kb_ref = refs
    else:
        q_ref, k_ref, vt_ref, lam_ref, nw_ref, o_ref, st_ref, mx_ref, m_ref, acc_ref = refs
    tk = tq
    hg = pl.program_id(1)
    n_tiles = k_ref.shape[1] // tq
    head = lambda g: slice(g * HEAD_LANES, (g + 1) * HEAD_LANES)
    kv = [head(0), head(1)] if fox else [head(0), head(0)]
    tile_rows = lambda tile: pl.ds(pl.multiple_of(tile * tq, tq), tq)

    def queries(g, tile):
        if fox:
            return jnp.concatenate([q_ref[0, head(g), tile_rows(tile)], qbt_ref[0, :, tile_rows(tile)]], axis=0)
        q = q_ref[0, tile_rows(tile), :]
        lane = lax.broadcasted_iota(jnp.int32, q.shape, 1)
        keep = (lane < DIFF_QK_DIM) if g == 0 else (lane >= DIFF_QK_DIM)
        return jnp.where(keep, q, jnp.zeros_like(q))

    def scores_of(qmat, g, j):
        k0 = pl.multiple_of(j * tk, tk)
        kj = k_ref[0, pl.ds(k0, tk), kv[g]]
        if fox:
            return _dot(jnp.concatenate([kj, kb_ref[pl.ds(k0, tk), kv[g]]], axis=1), qmat)
        return _dot_nt(kj, qmat)

    if fox:
        kb_all = kball_ref[0]
        owner = lax.broadcasted_iota(jnp.int32, kb_all.shape, 1) // BIAS_LANES_PER_HEAD
        for g in range(2):
            kb_ref[:, head(g)] = jnp.where(owner == hg * 2 + g, kb_all, jnp.zeros_like(kb_all))
    else:
        lam_p = lam_ref[...]
        lam = (jnp.exp(jnp.sum(lam_p[0:1] * lam_p[1:2], axis=1, keepdims=True))
               - jnp.exp(jnp.sum(lam_p[2:3] * lam_p[3:4], axis=1, keepdims=True)) + lambda_init)
    ones = jnp.ones((ONES_ROWS, tk), BF16)

    def lookahead(g, qmat, tile, j, diagonal):
        st = scores_of(qmat, g, j)
        if diagonal:
            kpos = j * tk + lax.broadcasted_iota(jnp.int32, (tk, tq), 0)
            qpos = tile * tq + lax.broadcasted_iota(jnp.int32, (tk, tq), 1)
            st = jnp.where(qpos >= kpos, st, NEG_INF)
        st_ref[g] = st
        mx_ref[g] = jnp.max(st, axis=0, keepdims=True)

    def absorb(g, j):
        k0 = pl.multiple_of(j * tk, tk)
        vt = jnp.concatenate([vt_ref[0, kv[g], pl.ds(k0, tk)], ones], axis=0)
        width = tq // ATTN_COLUMN_SPLIT
        for part in range(ATTN_COLUMN_SPLIT):
            cs = slice(part * width, (part + 1) * width)
            m = m_ref[g, :, cs]
            m_new = jnp.maximum(m, mx_ref[g, :, cs])
            p = jnp.exp2(st_ref[g, :, cs] - m_new).astype(BF16)
            acc_ref[g, :, cs] = jnp.exp2(m - m_new) * acc_ref[g, :, cs] + _dot(vt, p)
            m_ref[g, :, cs] = m_new

    for g in range(2):
        lookahead(g, queries(g, 0), 0, 0, True)

    def tile(qi, carry):
        qs = [queries(g, qi) for g in range(2)]

        def step(j, next_is_diagonal):
            for g in range(2):
                absorb(g, j)
                lookahead(g, qs[g], qi, j + 1, next_is_diagonal)

        def steps(j, c):
            for u in range(ATTN_UNROLL):
                step(j * ATTN_UNROLL + u, False)
            return c

        def single_step(j, c):
            step(j, False)
            return c

        def last_step(j, c):
            step(j, True)
            return c

        m_ref[...] = jnp.full(m_ref.shape, NEG_INF, F32)
        acc_ref[...] = jnp.zeros(acc_ref.shape, F32)
        n_plain = jnp.maximum(qi - 1, 0)
        n_full = n_plain // ATTN_UNROLL
        lax.fori_loop(0, n_full, steps, 0)
        lax.fori_loop(n_full * ATTN_UNROLL, n_plain, single_step, 0)
        lax.fori_loop(n_plain, qi, last_step, 0)
        nxt = jnp.minimum(qi + 1, n_tiles - 1)
        for g in range(2):
            absorb(g, qi)
            lookahead(g, queries(g, nxt), nxt, 0, False)
        outs = [acc_ref[g, :HEAD_LANES, :] / acc_ref[g, HEAD_LANES:HEAD_LANES + 1, :] for g in range(2)]
        rows = tile_rows(qi)
        if fox:
            for g in range(2):
                gate = _sigmoid(gate_ref[0, rows, head(g)].astype(F32))
                o_ref[0, rows, head(g)] = (outs[g].T * gate).astype(o_ref.dtype)
        else:
            o = (outs[0] - lam * outs[1]).T
            o_ref[0, rows, :] = (_rms(o, nw_ref[...]) * (1.0 - lambda_init)).astype(o_ref.dtype)
        return carry

    lax.fori_loop(0, n_tiles, tile, 0)


def _attn_state(tq):
    return [pltpu.VMEM((2, tq, tq), F32), pltpu.VMEM((2, 1, tq), F32), pltpu.VMEM((2, 1, tq), F32),
            pltpu.VMEM((2, HEAD_LANES + ONES_ROWS, tq), F32)]


def _fox_attention(qt, k, vt, gate, qbt, kb, *, tq=ATTN_TILE):
    b, t, dm = k.shape
    width = 2 * HEAD_LANES
    kern = functools.partial(_attn_kernel, tq=tq, fox=True, lambda_init=0.0)
    seq = pl.BlockSpec((1, t, width), lambda bi, h: (bi, 0, h))
    seq_t = pl.BlockSpec((1, width, t), lambda bi, h: (bi, h, 0))
    seq_bias = pl.BlockSpec((1, t, LANES), lambda bi, h: (bi, 0, 0))
    return pl.pallas_call(
        kern,
        grid=(b, dm // width),
        in_specs=[seq_t, seq, seq_t, seq, pl.BlockSpec((1, LANES, t), lambda bi, h: (bi, 0, 0)), seq_bias],
        out_specs=seq,
        out_shape=jax.ShapeDtypeStruct((b, t, dm), BF16),
        scratch_shapes=_attn_state(tq) + [pltpu.VMEM((t, width), BF16)],
        compiler_params=_params(("arbitrary", "arbitrary")),
        name="fox_attention",
    )(qt, k, vt, gate, qbt, kb)


def _diff_attention(qk, vt, lam_params, norm_w, lambda_init, *, tq=ATTN_TILE):
    b, t, _ = qk.shape
    nh = DIFF_HEADS
    kern = functools.partial(_attn_kernel, tq=tq, fox=False, lambda_init=lambda_init)
    seq = pl.BlockSpec((1, t, HEAD_LANES), lambda bi, h: (bi, 0, h))
    return pl.pallas_call(
        kern,
        grid=(b, nh),
        in_specs=[seq,
                  pl.BlockSpec((1, t, HEAD_LANES), lambda bi, h: (bi, 0, nh + h)),
                  pl.BlockSpec((1, HEAD_LANES, t), lambda bi, h: (bi, h, 0)),
                  _const_spec(lam_params.shape), _const_spec((1, HEAD_LANES))],
        out_specs=seq,
        out_shape=jax.ShapeDtypeStruct((b, t, nh * HEAD_LANES), BF16),
        scratch_shapes=_attn_state(tq),
        compiler_params=_params(("arbitrary", "arbitrary")),
        name="diff_attention",
    )(qk, qk, vt, lam_params, norm_w)


def _gdn_prep_kernel(q_ref, k_ref, v_ref, gates_ref, u_ref, w_ref, qd_ref, kd_ref, qk_ref):
    c = GDN_CHUNK
    sub = GDN_PREP_SUBTILE
    units = [(slice(t0, t0 + sub), hd) for t0 in range(0, q_ref.shape[1], sub) for hd in range(GDN_HEADS)]
    ids = range(len(units))
    col = lambda hd: slice(hd * HEAD_LANES, (hd + 1) * HEAD_LANES)
    lane = lax.broadcasted_iota(jnp.int32, (sub, LANES), 1)
    ri = lax.broadcasted_iota(jnp.int32, (sub, sub), 0)
    ci = lax.broadcasted_iota(jnp.int32, (sub, sub), 1)
    chunk_start = ri - ri % c
    incl = lambda a: jnp.where(ci <= ri, jnp.where(ci >= chunk_start, a, 0.0), 0.0)
    strict = lambda a: jnp.where(ci < ri, jnp.where(ci >= chunk_start, a, 0.0), 0.0)
    ident = jnp.where(ri == ci, 1.0, 0.0)
    ones = jnp.ones((sub, LANES), BF16)
    kt = [k_ref[0, rows, col(hd)] for rows, hd in units]
    beta = [gates_ref[0, rows, hd:hd + 1] for rows, hd in units]
    gcc = [gates_ref[0, rows, GDN_HEADS + hd:GDN_HEADS + hd + 1] for rows, hd in units]
    k16 = [kt[i].astype(BF16) for i in ids]
    kb = [kt[i] * beta[i] for i in ids]

    gc_row = []
    for i in ids:
        g_hi, g_mid, g_lo = (piece.astype(F32) for piece in _split_bf16(gcc[i], 3))
        pieces = jnp.where(lane == 0, g_hi, jnp.where(lane == 1, g_mid, jnp.where(lane == 2, g_lo, 0.0)))
        gc_row.append(_dot_nt(ones, pieces.astype(BF16)))
    kk = [_dot_nt(kb[i].astype(BF16), k16[i]) for i in ids]
    qk_raw = [_dot_nt(q_ref[0, rows, col(hd)].astype(BF16), k16[i]) for i, (rows, hd) in enumerate(units)]
    decay = [incl(jnp.exp(incl(gcc[i] - gc_row[i]))) for i in ids]
    lower = [strict(kk[i] * decay[i]) for i in ids]

    span = ri ^ ci
    inv = [ident - jnp.where(span == 1, lower[i], 0.0) for i in ids]
    s_blk = 2
    while s_blk < c:
        shift = int(math.log2(s_blk))
        inv16 = [inv[i].astype(BF16) for i in ids]
        coupled = [_dot(jnp.where((span >> shift) == 1, lower[i], 0.0).astype(BF16), inv16[i]) for i in ids]
        inv = [inv[i] - _dot(inv16[i], coupled[i].astype(BF16)) for i in ids]
        s_blk *= 2
    inv16 = [inv[i].astype(BF16) for i in ids]

    eg = [jnp.exp(gcc[i]) for i in ids]
    rhs = [jnp.concatenate([v_ref[0, rows, col(hd)] * beta[i], kb[i] * eg[i]], axis=1)
           for i, (rows, hd) in enumerate(units)]
    sol = [_dot(inv16[i], rhs[i].astype(BF16)) for i in ids]
    a_hi, a_lo, s_hi, s_lo = [], [], [], []
    for i in ids:
        hi, lo = _split_bf16(ident + lower[i], 2)
        a_hi.append(hi)
        a_lo.append(lo)
        hi, lo = _split_bf16(sol[i], 2)
        s_hi.append(hi)
        s_lo.append(lo)
    prod = [_dot(a_hi[i], s_hi[i]) + (_dot(a_hi[i], s_lo[i]) + _dot(a_lo[i], s_hi[i])) for i in ids]
    corr = [_dot(inv16[i], (rhs[i] - prod[i]).astype(BF16)) for i in ids]
    sol = [sol[i] + corr[i] for i in ids]
    for i, (rows, hd) in enumerate(units):
        u_ref[0, rows, col(hd)] = sol[i][:, :HEAD_LANES]
        w_ref[0, rows, col(hd)] = sol[i][:, HEAD_LANES:].astype(BF16)
        qk = incl(qk_raw[i] * decay[i])
        qd_ref[0, rows, col(hd)] = (q_ref[0, rows, col(hd)] * eg[i]).astype(BF16)
        for n in range(sub // c):
            blk = slice(n * c, (n + 1) * c)
            out_rows = slice(rows.start + n * c, rows.start + (n + 1) * c)
            qk_ref[0, hd, out_rows, :] = qk[blk, blk].astype(BF16)
            gl = gcc[i][(n + 1) * c - 1:(n + 1) * c, :]
            kd_ref[0, out_rows, col(hd)] = (kt[i][blk] * jnp.exp(gl - gcc[i][blk])).astype(BF16)


def _gdn_prep(qkv, gates, *, tt=GDN_PREP_TILE):
    b, t, _ = qkv.shape
    nh = GDN_HEADS
    dm = nh * HEAD_LANES
    blk = lambda part: pl.BlockSpec((1, tt, dm), lambda bi, i: (bi, i, part))
    return pl.pallas_call(
        _gdn_prep_kernel,
        grid=(b, t // tt),
        in_specs=[blk(0), blk(1), blk(2), pl.BlockSpec((1, tt, LANES), lambda bi, i: (bi, i, 0))],
        out_specs=[blk(0)] * 4 + [pl.BlockSpec((1, nh, tt, GDN_CHUNK), lambda bi, i: (bi, 0, i, 0))],
        out_shape=[jax.ShapeDtypeStruct((b, t, dm), F32)]
        + [jax.ShapeDtypeStruct((b, t, dm), BF16)] * 3
        + [jax.ShapeDtypeStruct((b, nh, t, GDN_CHUNK), BF16)],
        compiler_params=_params(("arbitrary", "arbitrary")),
        name="gdn_prep",
    )(qkv, qkv, qkv, gates)


def _gdn_scan_kernel(u_ref, w_ref, qd_ref, kd_ref, qk_ref, gates_ref, z_ref, nw_ref, o_ref, s_ref):
    c = GDN_CHUNK
    nb, tt = u_ref.shape[0], u_ref.shape[1]

    @pl.when(pl.program_id(1) == 0)
    def _():
        s_ref[...] = jnp.zeros_like(s_ref)

    chains = [(bi, hd) for bi in range(nb) for hd in range(GDN_HEADS)]
    cols = [slice(hd * HEAD_LANES, (hd + 1) * HEAD_LANES) for hd in range(GDN_HEADS)]
    state = [s_ref[bi, hd] for bi, hd in chains]
    for n in range(tt // c):
        rows = slice(n * c, (n + 1) * c)
        r = [_dot(jnp.concatenate([w_ref[bi, rows, cols[hd]], qd_ref[bi, rows, cols[hd]]], axis=0),
                  state[i].astype(BF16)) for i, (bi, hd) in enumerate(chains)]
        v_new = [(u_ref[bi, rows, cols[hd]] - r[i][:c]).astype(BF16) for i, (bi, hd) in enumerate(chains)]
        intra = [_dot(qk_ref[bi, hd, rows, :], v_new[i]) for i, (bi, hd) in enumerate(chains)]
        upd = [_dot_tn(kd_ref[bi, rows, cols[hd]], v_new[i]) for i, (bi, hd) in enumerate(chains)]
        for i, (bi, hd) in enumerate(chains):
            last = (n + 1) * c - 1
            decay_last = jnp.exp(gates_ref[bi, last:last + 1, GDN_HEADS + hd:GDN_HEADS + hd + 1])
            state[i] = state[i] * decay_last + upd[i]
            zt = z_ref[bi, rows, cols[hd]].astype(F32)
            o = r[i][c:] + intra[i]
            o_ref[bi, rows, cols[hd]] = (_rms(o, nw_ref[...]) * (zt * _sigmoid(zt))).astype(o_ref.dtype)
    for i, (bi, hd) in enumerate(chains):
        s_ref[bi, hd] = state[i]


def _gdn_scan(u, w, qd, kd, qk, gates, z, norm_w, *, tt=GDN_SCAN_TILE, nb=GDN_SCAN_BATCH):
    b, t, dm = u.shape
    nh = GDN_HEADS
    assert b % nb == 0 and t % tt == 0
    blk = pl.BlockSpec((nb, tt, dm), lambda bi, i: (bi, i, 0))
    return pl.pallas_call(
        _gdn_scan_kernel,
        grid=(b // nb, t // tt),
        in_specs=[blk, blk, blk, blk,
                  pl.BlockSpec((nb, nh, tt, GDN_CHUNK), lambda bi, i: (bi, 0, i, 0)),
                  pl.BlockSpec((nb, tt, LANES), lambda bi, i: (bi, i, 0)),
                  blk, _const_spec((1, HEAD_LANES))],
        out_specs=blk,
        out_shape=jax.ShapeDtypeStruct((b, t, dm), BF16),
        scratch_shapes=[pltpu.VMEM((nb, nh, GDN_HEAD_DIM, GDN_HEAD_DIM), F32)],
        compiler_params=_params(("arbitrary", "arbitrary")),
        name="gdn_scan",
    )(u, w, qd, kd, qk, gates, z, norm_w)


def _post_kernel(*refs, n_mix, final_norm):
    x_ref = refs[0]
    mix_refs = refs[1:1 + n_mix]
    wout_ref, g_ref, wup_ref, wdn_ref, p_ref, wpp_ref, wpg_ref = refs[1 + n_mix:8 + n_mix]
    rest = refs[8 + n_mix:]
    if final_norm:
        gf_ref, o_ref = rest
    else:
        (o_ref,) = rest
    mix = mix_refs[0][...] if n_mix == 1 else jnp.concatenate([r[...] for r in mix_refs], axis=1)
    x = x_ref[...] + _dot(mix, wout_ref[...])
    h = _rms(x, g_ref[...]).astype(BF16)
    d_ff = wup_ref.shape[1]
    acc = x
    for s in range(d_ff // FF_SEG):
        a = jnp.maximum(_dot(h, wup_ref[:, s * FF_SEG:(s + 1) * FF_SEG]), 0.0)
        acc = acc + _dot((a * a).astype(BF16), wdn_ref[s * FF_SEG:(s + 1) * FF_SEG, :])
    x = acc
    gate = _sigmoid(_dot(x.astype(BF16), wpg_ref[...]))
    x = x + _dot(p_ref[...].astype(BF16), wpp_ref[...]) * gate
    if final_norm:
        x = _rms(x, gf_ref[...])
    o_ref[...] = x


def _post(x2d, mixes, wout, g, wup, wdn, p2d, wpp, wpg, gf=None):
    m, d = x2d.shape
    tm = TOKEN_TILE
    row = lambda i: (i, 0)
    single = pl.Buffered(1)
    const = lambda a: pl.BlockSpec(a.shape, lambda i: (0, 0), pipeline_mode=single)
    args = [x2d, *mixes, wout, g, wup, wdn, p2d, wpp, wpg]
    in_specs = ([pl.BlockSpec((tm, d), row)]
                + [pl.BlockSpec((tm, a.shape[1]), row) for a in mixes]
                + [const(wout), const(g), const(wup), const(wdn), pl.BlockSpec((tm, p2d.shape[1]), row), const(wpp), const(wpg)])
    if gf is not None:
        args.append(gf)
        in_specs.append(const(gf))
    kern = functools.partial(_post_kernel, n_mix=len(mixes), final_norm=gf is not None)
    return pl.pallas_call(
        kern,
        grid=(m // tm,),
        in_specs=in_specs,
        out_specs=pl.BlockSpec((tm, d), row),
        out_shape=jax.ShapeDtypeStruct((m, d), F32),
        compiler_params=_params(("arbitrary",)),
        name="out_proj_mlp_ple",
    )(*args)


def _pad_lanes(a):
    return jnp.pad(a, ((0, 0), (0, LANES - a.shape[1])))


def kernel(x, p, positions, norm_mix, norm_mlp, norm_final, w_in_even, conv_w, a_log, dt_bias, gdn_norm,
           lam_q1, lam_k1, lam_q2, lam_k2, diff_norm, w_out_even, w_in_odd, b_forget, w_out_odd,
           w_mlp_up, w_mlp_down, w_ple_proj, w_ple_gate):
    b, t, d = x.shape
    depth = p.shape[0]
    m = b * t
    assert t % TOKEN_TILE == 0 and d % PROJ_SEG == 0
    nh = GDN_HEADS
    gdn_w = 3 * nh * GDN_HEAD_DIM + nh * GDN_HEAD_DIM
    assert w_in_even.shape[2] == gdn_w + 2 * nh + 3 * DIFF_HEADS * 2 * DIFF_QK_DIM

    inv_freq = ROPE_THETA ** (-jnp.arange(0, DIFF_QK_DIM, 2, dtype=F32) / DIFF_QK_DIM)
    ang = positions.astype(F32)[..., None] * inv_freq
    cos, sin = jnp.cos(ang), jnp.sin(ang)
    cos_t = jnp.concatenate([cos, cos, cos, cos], axis=-1).reshape(m, LANES)
    sin_t = jnp.concatenate([-sin, sin, -sin, sin], axis=-1).reshape(m, LANES)

    x2d = x.reshape(m, d)
    for i in range(depth):
        j = i // 2
        g_mix = norm_mix[i].reshape(1, d)
        if i % 2 == 0:
            lambda_init = 0.8 - 0.6 * math.exp(-0.3 * i)
            w = w_in_even[j]
            wm = jnp.concatenate([w[:, :gdn_w], w[:, gdn_w + 2 * nh:]], axis=1).astype(BF16)
            wg = _pad_lanes(w[:, gdn_w:gdn_w + 2 * nh]).astype(BF16)
            alog_row = _pad_lanes(jnp.concatenate([jnp.zeros((nh,), F32), a_log[j]]).reshape(1, 2 * nh))
            dt_row = _pad_lanes(jnp.concatenate([jnp.zeros((nh,), F32), dt_bias[j]]).reshape(1, 2 * nh))
            qkv, z, qkb, vbt, gates = _even_in(x2d, g_mix, wm, wg, conv_w[j], alog_row, dt_row, cos_t, sin_t, t)
            qkv, z, qkb, gates = (a.reshape(b, t, -1) for a in (qkv, z, qkb, gates))
            u, wy, qd, kd, qk = _gdn_prep(qkv, gates)
            o_a = _gdn_scan(u, wy, qd, kd, qk, gates, z, gdn_norm[j].reshape(1, HEAD_LANES))
            lam_params = jnp.stack([lam_q1[j], lam_k1[j], lam_q2[j], lam_k2[j]])
            o_b = _diff_attention(qkb, vbt, lam_params, diff_norm[j].reshape(1, HEAD_LANES), lambda_init)
            mixes = [o_a.reshape(m, -1), o_b.reshape(m, -1)]
            wout = w_out_even[j].astype(BF16)
        else:
            w = w_in_odd[j]
            d_mix = (w.shape[1] - FOX_HEADS) // 4
            wm = w[:, :4 * d_mix].astype(BF16)
            wf = _pad_lanes(w[:, 4 * d_mix:]).astype(BF16)
            bf_row = _pad_lanes(b_forget[j].reshape(1, FOX_HEADS))
            qt, k, vt, gate, qbt, kb = _odd_in(x2d, g_mix, wm, wf, bf_row, t)
            k, gate, kb = (a.reshape(b, t, -1) for a in (k, gate, kb))
            o = _fox_attention(qt, k, vt, gate, qbt, kb)
            mixes = [o.reshape(m, -1)]
            wout = w_out_odd[j].astype(BF16)
        x2d = _post(x2d, mixes, wout, norm_mlp[i].reshape(1, d), w_mlp_up[i].astype(BF16),
                    w_mlp_down[i].astype(BF16), p[i].reshape(m, -1), w_ple_proj[i].astype(BF16),
                    w_ple_gate[i].astype(BF16), norm_final.reshape(1, d) if i == depth - 1 else None)
    return x2d.reshape(b, t, d)
```

```python
import functools
import math

import jax
import jax.numpy as jnp
import numpy as np
from jax import lax
from jax.experimental import pallas as pl
from jax.experimental.pallas import tpu as pltpu

F32 = jnp.float32
BF16 = jnp.bfloat16

GDN_HEADS = 4
GDN_HEAD_DIM = 128
GDN_CHUNK = 64
CONV_WIDTH = 4
DIFF_HEADS = 4
DIFF_QK_DIM = 64
FOX_HEADS = 8
HEAD_LANES = 128
ROPE_THETA = 10000.0
EPS = 1e-6
NEG_INF = -1e30
LOG2E = 1.4426950408889634
LANES = 128
SUBLANES = 8
VMEM_LIMIT_BYTES = 56 * 1024 * 1024

TOKEN_TILE = 1024
EVEN_TOKEN_TILE = 512
PROJ_SEG = 512
FF_SEG = 1024
GDN_PREP_SUBTILE = 256
GDN_PREP_TILE = 512
GDN_SCAN_TILE = 256
GDN_SCAN_BATCH = 4
ATTN_TILE = 512
ATTN_COLUMN_SPLIT = 2
BIAS_LANES_PER_HEAD = 16
ONES_ROWS = 16


def _dot(a, b):
    return jnp.dot(a, b, preferred_element_type=F32)


def _split_bf16(x, parts):
    out = []
    for _ in range(parts):
        piece = x.astype(BF16)
        out.append(piece)
        x = x - piece.astype(F32)
    return out


def _dot_nt(a, b):
    return lax.dot_general(a, b, (((1,), (1,)), ((), ())), preferred_element_type=F32)


def _dot_tn(a, b):
    return lax.dot_general(a, b, (((0,), (0,)), ((), ())), preferred_element_type=F32)


def _rms(x, g):
    return x * lax.rsqrt(jnp.mean(x * x, axis=-1, keepdims=True) + EPS) * g


def _sigmoid(x):
    return 1.0 / (1.0 + jnp.exp(-x))


def _softplus(x):
    return jnp.maximum(x, 0.0) + jnp.log1p(jnp.exp(-jnp.abs(x)))


def _row_scan(x, period):
    rows = lax.broadcasted_iota(jnp.int32, x.shape, 0) % period
    s = 1
    while s < period:
        x = x + jnp.where(rows >= s, pltpu.roll(x, s, 0), 0.0)
        s *= 2
    return x


def _const_spec(shape):
    return pl.BlockSpec(shape, lambda *_: (0,) * len(shape))


def _params(sem):
    return pltpu.CompilerParams(dimension_semantics=sem, vmem_limit_bytes=VMEM_LIMIT_BYTES)


def _even_in_kernel(x_ref, g_ref, wm_ref, wg_ref, conv_ref, alog_ref, dt_ref, cos_ref, sin_ref,
                    qkv_ref, z_ref, qkb_ref, vbt_ref, gates_ref, h_ref, carry_ref, tr_ref, pad_ref, *, tiles_per_seq):
    tm = x_ref.shape[0]
    i = pl.program_id(0)
    h_ref[...] = _rms(x_ref[...], g_ref[...]).astype(BF16)
    seq_start = (i % tiles_per_seq) == 0
    seg = lambda s: slice(s * PROJ_SEG, (s + 1) * PROJ_SEG)
    project = lambda s: _dot(h_ref[...], wm_ref[:, seg(s)])

    def gdn_qkv(s, y):
        cols = seg(s)
        pad_ref[0:SUBLANES, :] = jnp.where(seq_start, 0.0, carry_ref[:, cols])
        pad_ref[SUBLANES:, :] = y
        carry_ref[:, cols] = y[tm - SUBLANES:, :]
        w = conv_ref[:, cols]
        a = y * w[CONV_WIDTH - 1:CONV_WIDTH, :]
        for k in range(1, CONV_WIDTH):
            a = a + pad_ref[SUBLANES - k:SUBLANES - k + tm, :] * w[CONV_WIDTH - 1 - k:CONV_WIDTH - k, :]
        a = a * _sigmoid(a)
        if s < 2:
            outs = []
            for hd in range(GDN_HEADS):
                blk = a[:, hd * HEAD_LANES:(hd + 1) * HEAD_LANES]
                n = blk * lax.rsqrt(jnp.sum(blk * blk, axis=-1, keepdims=True) + EPS)
                outs.append(n * (GDN_HEAD_DIM ** -0.5) if s == 0 else n)
            a = jnp.concatenate(outs, axis=1)
        qkv_ref[:, cols] = a

    def gdn_gate(s, y):
        z_ref[...] = y.astype(BF16)

    def diff_qk(s, y):
        cos = jnp.concatenate([cos_ref[...]] * (PROJ_SEG // LANES), axis=1)
        sin = jnp.concatenate([sin_ref[...]] * (PROJ_SEG // LANES), axis=1)
        lane = lax.broadcasted_iota(jnp.int32, (tm, PROJ_SEG), 1)
        first_half = (lane % DIFF_QK_DIM) < (DIFF_QK_DIM // 2)
        swapped = jnp.where(first_half, pltpu.roll(y, PROJ_SEG - DIFF_QK_DIM // 2, 1),
                            pltpu.roll(y, DIFF_QK_DIM // 2, 1))
        scale = DIFF_QK_DIM ** -0.5 * LOG2E if s == 4 else 1.0
        qkb_ref[:, seg(s - 4)] = ((y * cos + swapped * sin) * scale).astype(BF16)

    def diff_v(s, y):
        tr_ref[...] = y
        vbt_ref[0] = tr_ref[...].T.astype(BF16)

    stages = ((0, gdn_qkv), (3, gdn_gate), (1, gdn_qkv), (6, diff_v), (2, gdn_qkv), (4, diff_qk), (5, diff_qk))
    pending = project(stages[0][0])
    for n, (s, epilogue) in enumerate(stages):
        upcoming = project(stages[n + 1][0]) if n + 1 < len(stages) else _dot(h_ref[...], wg_ref[...])
        epilogue(s, pending)
        pending = upcoming

    graw = pending
    beta = _sigmoid(graw)
    g = -jnp.exp(alog_ref[...]) * _softplus(graw + dt_ref[...])
    gc = _row_scan(g, GDN_CHUNK)
    lane_g = lax.broadcasted_iota(jnp.int32, (tm, LANES), 1)
    gates_ref[...] = jnp.where(lane_g < GDN_HEADS, beta, gc)


def _even_in(x2d, g, wm, wg, conv_w, alog_row, dt_row, cos_t, sin_t, seq_len):
    m, d = x2d.shape
    tm = EVEN_TOKEN_TILE
    n_main = wm.shape[1]
    tps = seq_len // tm
    kern = functools.partial(_even_in_kernel, tiles_per_seq=tps)
    row = lambda i: (i, 0)
    return pl.pallas_call(
        kern,
        grid=(m // tm,),
        in_specs=[
            pl.BlockSpec((tm, d), row),
            _const_spec((1, d)),
            _const_spec((d, n_main)),
            _const_spec((d, LANES)),
            _const_spec(conv_w.shape),
            _const_spec((1, LANES)),
            _const_spec((1, LANES)),
            pl.BlockSpec((tm, LANES), row),
            pl.BlockSpec((tm, LANES), row),
        ],
        out_specs=[
            pl.BlockSpec((tm, 3 * PROJ_SEG), row),
            pl.BlockSpec((tm, PROJ_SEG), row),
            pl.BlockSpec((tm, 2 * PROJ_SEG), row),
            pl.BlockSpec((1, PROJ_SEG, tm), lambda i: (i // tps, 0, i % tps)),
            pl.BlockSpec((tm, LANES), row),
        ],
        out_shape=[
            jax.ShapeDtypeStruct((m, 3 * PROJ_SEG), F32),
            jax.ShapeDtypeStruct((m, PROJ_SEG), BF16),
            jax.ShapeDtypeStruct((m, 2 * PROJ_SEG), BF16),
            jax.ShapeDtypeStruct((m // seq_len, PROJ_SEG, seq_len), BF16),
            jax.ShapeDtypeStruct((m, LANES), F32),
        ],
        scratch_shapes=[pltpu.VMEM((tm, d), BF16), pltpu.VMEM((SUBLANES, 3 * PROJ_SEG), F32), pltpu.VMEM((tm, PROJ_SEG), F32),
                        pltpu.VMEM((tm + SUBLANES, PROJ_SEG), F32)],
        compiler_params=_params(("arbitrary",)),
        name="even_in_proj",
    )(x2d, g, wm, wg, conv_w, alog_row, dt_row, cos_t, sin_t)


def _odd_in_kernel(x_ref, g_ref, wm_ref, wf_ref, bf_ref, sel_ref, ones_ref, qt_ref, k_ref, vt_ref, gate_ref, qbt_ref, kb_ref,
                   h_ref, carry_ref, tr_ref, *, tiles_per_seq, d_mix):
    tm = x_ref.shape[0]
    i = pl.program_id(0)
    h_ref[...] = _rms(x_ref[...], g_ref[...]).astype(BF16)
    head_dim = d_mix // FOX_HEADS
    for o_ref, base, scale in ((qt_ref, 0, head_dim ** -0.5 * LOG2E), (k_ref, d_mix, 1.0),
                               (vt_ref, 2 * d_mix, 1.0), (gate_ref, 3 * d_mix, 1.0)):
        for s in range(d_mix // PROJ_SEG):
            cols = slice(s * PROJ_SEG, (s + 1) * PROJ_SEG)
            y = _dot(h_ref[...], wm_ref[:, base + s * PROJ_SEG:base + (s + 1) * PROJ_SEG])
            if o_ref is qt_ref or o_ref is vt_ref:
                tr_ref[...] = y * scale
                o_ref[0, cols, :] = tr_ref[...].T.astype(BF16)
            else:
                o_ref[:, cols] = y.astype(BF16)
    f = _dot(h_ref[...], wf_ref[...]) + bf_ref[...]
    log_f = jnp.minimum(f, 0.0) - jnp.log1p(jnp.exp(-jnp.abs(f)))
    prev = jnp.where((i % tiles_per_seq) == 0, 0.0, carry_ref[0:1, :])
    cum = _row_scan(log_f, tm) + prev
    carry_ref[...] = jnp.broadcast_to(cum[tm - 1:tm, :], carry_ref.shape)
    pieces = jnp.concatenate(_split_bf16(LOG2E * cum, 3), axis=1)
    lanes = _dot(pieces, sel_ref[...]) + ones_ref[...]
    tr_ref[:, :LANES] = lanes[:, :LANES]
    qbt_ref[0] = tr_ref[:, :LANES].T.astype(BF16)
    kb_ref[...] = lanes[:, LANES:].astype(BF16)


def _bias_lane_tables():
    sel = np.zeros((3 * LANES, 2 * LANES), np.float32)
    ones = np.zeros((1, 2 * LANES), np.float32)
    for h in range(FOX_HEADS):
        base = BIAS_LANES_PER_HEAD * h
        for piece in range(3):
            sel[LANES * piece + h, base + 3 + piece] = 1.0
            sel[LANES * piece + h, LANES + base + piece] = -1.0
            ones[0, base + piece] = 1.0
            ones[0, LANES + base + 3 + piece] = 1.0
    return jnp.asarray(sel, BF16), jnp.asarray(ones, F32)


def _odd_in(x2d, g, wm, wf, bf_row, seq_len):
    m, d = x2d.shape
    sel, ones_row = _bias_lane_tables()
    tm = TOKEN_TILE
    d_mix = wm.shape[1] // 4
    tps = seq_len // tm
    kern = functools.partial(_odd_in_kernel, tiles_per_seq=tps, d_mix=d_mix)
    row = lambda i: (i, 0)
    row_blk = pl.BlockSpec((tm, d_mix), row)
    row_shape = jax.ShapeDtypeStruct((m, d_mix), BF16)
    col_blk = lambda width: pl.BlockSpec((1, width, tm), lambda i: (i // tps, 0, i % tps))
    col_shape = lambda width: jax.ShapeDtypeStruct((m // seq_len, width, seq_len), BF16)
    return pl.pallas_call(
        kern,
        grid=(m // tm,),
        in_specs=[
            pl.BlockSpec((tm, d), row),
            _const_spec((1, d)),
            _const_spec(wm.shape),
            _const_spec((d, LANES)),
            _const_spec((1, LANES)),
            _const_spec(sel.shape),
            _const_spec(ones_row.shape),
        ],
        out_specs=[col_blk(d_mix), row_blk, col_blk(d_mix), row_blk, col_blk(LANES), pl.BlockSpec((tm, LANES), row)],
        out_shape=[col_shape(d_mix), row_shape, col_shape(d_mix), row_shape, col_shape(LANES),
                   jax.ShapeDtypeStruct((m, LANES), BF16)],
        scratch_shapes=[pltpu.VMEM((tm, d), BF16), pltpu.VMEM((SUBLANES, LANES), F32), pltpu.VMEM((tm, PROJ_SEG), F32)],
        compiler_params=_params(("arbitrary",)),
        name="odd_in_proj",
    )(x2d, g, wm, wf, bf_row, sel, ones_row)


def _attn_kernel(*refs, tq, fox, lambda_init):
    if fox:
        q_ref, k_ref, vt_ref, gate_ref, qbt_ref, kball_ref, o_ref, st_ref, mx_ref, m_ref, acc_ref, kb_ref = refs
    else:
        q_ref, k_ref, vt_ref, lam_ref, nw_ref, o_ref, st_ref, mx_ref, m_ref, acc_ref = refs
    tk = tq
    hg = pl.program_id(1)
    n_tiles = k_ref.shape[1] // tq
    head = lambda g: slice(g * HEAD_LANES, (g + 1) * HEAD_LANES)
    kv = [head(0), head(1)] if fox else [head(0), head(0)]
    tile_rows = lambda tile: pl.ds(pl.multiple_of(tile * tq, tq), tq)

    def queries(g, tile):
        if fox:
            return jnp.concatenate([q_ref[0, head(g), tile_rows(tile)], qbt_ref[0, :, tile_rows(tile)]], axis=0)
        q = q_ref[0, tile_rows(tile), :]
        lane = lax.broadcasted_iota(jnp.int32, q.shape, 1)
        keep = (lane < DIFF_QK_DIM) if g == 0 else (lane >= DIFF_QK_DIM)
        return jnp.where(keep, q, jnp.zeros_like(q))

    def scores_of(qmat, g, j):
        k0 = pl.multiple_of(j * tk, tk)
        kj = k_ref[0, pl.ds(k0, tk), kv[g]]
        if fox:
            return _dot(jnp.concatenate([kj, kb_ref[pl.ds(k0, tk), kv[g]]], axis=1), qmat)
        return _dot_nt(kj, qmat)

    if fox:
        kb_all = kball_ref[0]
        owner = lax.broadcasted_iota(jnp.int32, kb_all.shape, 1) // BIAS_LANES_PER_HEAD
        for g in range(2):
            kb_ref[:, head(g)] = jnp.where(owner == hg * 2 + g, kb_all, jnp.zeros_like(kb_all))
    else:
        lam_p = lam_ref[...]
        lam = (jnp.exp(jnp.sum(lam_p[0:1] * lam_p[1:2], axis=1, keepdims=True))
               - jnp.exp(jnp.sum(lam_p[2:3] * lam_p[3:4], axis=1, keepdims=True)) + lambda_init)
    ones = jnp.ones((ONES_ROWS, tk), BF16)

    def lookahead(g, qmat, tile, j, diagonal, slot):
        st = scores_of(qmat, g, j)
        if diagonal:
            kpos = j * tk + lax.broadcasted_iota(jnp.int32, (tk, tq), 0)
            qpos = tile * tq + lax.broadcasted_iota(jnp.int32, (tk, tq), 1)
            st = jnp.where(qpos >= kpos, st, NEG_INF)
        st_ref[g, slot] = st
        mx_ref[g, slot] = jnp.max(st, axis=0, keepdims=True)

    def absorb(g, j, slot):
        k0 = pl.multiple_of(j * tk, tk)
        vt = jnp.concatenate([vt_ref[0, kv[g], pl.ds(k0, tk)], ones], axis=0)
        width = tq // ATTN_COLUMN_SPLIT
        for part in range(ATTN_COLUMN_SPLIT):
            cs = slice(part * width, (part + 1) * width)
            m = m_ref[g, :, cs]
            m_new = jnp.maximum(m, mx_ref[g, slot, :, cs])
            p = jnp.exp2(st_ref[g, slot, :, cs] - m_new).astype(BF16)
            acc_ref[g, :, cs] = jnp.exp2(m - m_new) * acc_ref[g, :, cs] + _dot(vt, p)
            m_ref[g, :, cs] = m_new

    for g in range(2):
        lookahead(g, queries(g, 0), 0, 0, True, 0)

    def tile(qi, carry):
        qs = [queries(g, qi) for g in range(2)]

        def pair(jj, c):
            for u in range(2):
                for g in range(2):
                    lookahead(g, qs[g], qi, 2 * jj + u + 1, False, 1 - u)
                    absorb(g, 2 * jj + u, u)
            return c

        def even_step(j, c):
            for g in range(2):
                lookahead(g, qs[g], qi, j + 1, False, 1)
                absorb(g, j, 0)
            return c

        m_ref[...] = jnp.full(m_ref.shape, NEG_INF, F32)
        acc_ref[...] = jnp.zeros(acc_ref.shape, F32)
        n_plain = jnp.maximum(qi - 1, 0)
        n_pairs = n_plain // 2
        lax.fori_loop(0, n_pairs, pair, 0)
        lax.fori_loop(2 * n_pairs, n_plain, even_step, 0)
        nxt = jnp.minimum(qi + 1, n_tiles - 1)

        def finish(diag_slot):
            def last_step(j, c):
                for g in range(2):
                    lookahead(g, qs[g], qi, j + 1, True, diag_slot)
                    absorb(g, j, 1 - diag_slot)
                return c

            lax.fori_loop(n_plain, qi, last_step, 0)
            for g in range(2):
                if diag_slot == 0:
                    absorb(g, qi, 0)
                    lookahead(g, queries(g, nxt), nxt, 0, False, 0)
                else:
                    lookahead(g, queries(g, nxt), nxt, 0, False, 0)
                    absorb(g, qi, 1)
            outs = [acc_ref[g, :HEAD_LANES, :] / acc_ref[g, HEAD_LANES:HEAD_LANES + 1, :] for g in range(2)]
            rows = tile_rows(qi)
            if fox:
                for g in range(2):
                    gate = _sigmoid(gate_ref[0, rows, head(g)].astype(F32))
                    o_ref[0, rows, head(g)] = (outs[g].T * gate).astype(o_ref.dtype)
            else:
                o = (outs[0] - lam * outs[1]).T
                o_ref[0, rows, :] = (_rms(o, nw_ref[...]) * (1.0 - lambda_init)).astype(o_ref.dtype)

        for parity in range(2):
            pl.when(qi % 2 == parity)(functools.partial(finish, parity))
        return carry

    lax.fori_loop(0, n_tiles, tile, 0)


def _attn_state(tq):
    return [pltpu.VMEM((2, 2, tq, tq), F32), pltpu.VMEM((2, 2, 1, tq), F32), pltpu.VMEM((2, 1, tq), F32),
            pltpu.VMEM((2, HEAD_LANES + ONES_ROWS, tq), F32)]


def _fox_attention(qt, k, vt, gate, qbt, kb, *, tq=ATTN_TILE):
    b, t, dm = k.shape
    width = 2 * HEAD_LANES
    kern = functools.partial(_attn_kernel, tq=tq, fox=True, lambda_init=0.0)
    seq = pl.BlockSpec((1, t, width), lambda bi, h: (bi, 0, h))
    seq_t = pl.BlockSpec((1, width, t), lambda bi, h: (bi, h, 0))
    seq_bias = pl.BlockSpec((1, t, LANES), lambda bi, h: (bi, 0, 0))
    return pl.pallas_call(
        kern,
        grid=(b, dm // width),
        in_specs=[seq_t, seq, seq_t, seq, pl.BlockSpec((1, LANES, t), lambda bi, h: (bi, 0, 0)), seq_bias],
        out_specs=seq,
        out_shape=jax.ShapeDtypeStruct((b, t, dm), BF16),
        scratch_shapes=_attn_state(tq) + [pltpu.VMEM((t, width), BF16)],
        compiler_params=_params(("arbitrary", "arbitrary")),
        name="fox_attention",
    )(qt, k, vt, gate, qbt, kb)


def _diff_attention(qk, vt, lam_params, norm_w, lambda_init, *, tq=ATTN_TILE):
    b, t, _ = qk.shape
    nh = DIFF_HEADS
    kern = functools.partial(_attn_kernel, tq=tq, fox=False, lambda_init=lambda_init)
    seq = pl.BlockSpec((1, t, HEAD_LANES), lambda bi, h: (bi, 0, h))
    return pl.pallas_call(
        kern,
        grid=(b, nh),
        in_specs=[seq,
                  pl.BlockSpec((1, t, HEAD_LANES), lambda bi, h: (bi, 0, nh + h)),
                  pl.BlockSpec((1, HEAD_LANES, t), lambda bi, h: (bi, h, 0)),
                  _const_spec(lam_params.shape), _const_spec((1, HEAD_LANES))],
        out_specs=seq,
        out_shape=jax.ShapeDtypeStruct((b, t, nh * HEAD_LANES), BF16),
        scratch_shapes=_attn_state(tq),
        compiler_params=_params(("arbitrary", "arbitrary")),
        name="diff_attention",
    )(qk, qk, vt, lam_params, norm_w)


def _gdn_prep_kernel(q_ref, k_ref, v_ref, gates_ref, u_ref, w_ref, qd_ref, kd_ref, qk_ref):
    c = GDN_CHUNK
    sub = GDN_PREP_SUBTILE
    units = [(slice(t0, t0 + sub), hd) for t0 in range(0, q_ref.shape[1], sub) for hd in range(GDN_HEADS)]
    ids = range(len(units))
    col = lambda hd: slice(hd * HEAD_LANES, (hd + 1) * HEAD_LANES)
    lane = lax.broadcasted_iota(jnp.int32, (sub, LANES), 1)
    ri = lax.broadcasted_iota(jnp.int32, (sub, sub), 0)
    ci = lax.broadcasted_iota(jnp.int32, (sub, sub), 1)
    chunk_start = ri - ri % c
    incl = lambda a: jnp.where(ci <= ri, jnp.where(ci >= chunk_start, a, 0.0), 0.0)
    strict = lambda a: jnp.where(ci < ri, jnp.where(ci >= chunk_start, a, 0.0), 0.0)
    ident = jnp.where(ri == ci, 1.0, 0.0)
    ones = jnp.ones((sub, LANES), BF16)
    kt = [k_ref[0, rows, col(hd)] for rows, hd in units]
    beta = [gates_ref[0, rows, hd:hd + 1] for rows, hd in units]
    gcc = [gates_ref[0, rows, GDN_HEADS + hd:GDN_HEADS + hd + 1] for rows, hd in units]
    k16 = [kt[i].astype(BF16) for i in ids]
    kb = [kt[i] * beta[i] for i in ids]

    gc_row = []
    for i in ids:
        g_hi, g_mid, g_lo = (piece.astype(F32) for piece in _split_bf16(gcc[i], 3))
        pieces = jnp.where(lane == 0, g_hi, jnp.where(lane == 1, g_mid, jnp.where(lane == 2, g_lo, 0.0)))
        gc_row.append(_dot_nt(ones, pieces.astype(BF16)))
    kk = [_dot_nt(kb[i].astype(BF16), k16[i]) for i in ids]
    qk_raw = [_dot_nt(q_ref[0, rows, col(hd)].astype(BF16), k16[i]) for i, (rows, hd) in enumerate(units)]
    decay = [incl(jnp.exp(incl(gcc[i] - gc_row[i]))) for i in ids]
    lower = [strict(kk[i] * decay[i]) for i in ids]

    span = ri ^ ci
    inv = [ident - jnp.where(span == 1, lower[i], 0.0) for i in ids]
    s_blk = 2
    while s_blk < c:
        shift = int(math.log2(s_blk))
        inv16 = [inv[i].astype(BF16) for i in ids]
        coupled = [_dot(jnp.where((span >> shift) == 1, lower[i], 0.0).astype(BF16), inv16[i]) for i in ids]
        inv = [inv[i] - _dot(inv16[i], coupled[i].astype(BF16)) for i in ids]
        s_blk *= 2
    inv16 = [inv[i].astype(BF16) for i in ids]

    eg = [jnp.exp(gcc[i]) for i in ids]
    rhs = [jnp.concatenate([v_ref[0, rows, col(hd)] * beta[i], kb[i] * eg[i]], axis=1)
           for i, (rows, hd) in enumerate(units)]
    sol = [_dot(inv16[i], rhs[i].astype(BF16)) for i in ids]
    a_hi, a_lo, s_hi, s_lo = [], [], [], []
    for i in ids:
        hi, lo = _split_bf16(ident + lower[i], 2)
        a_hi.append(hi)
        a_lo.append(lo)
        hi, lo = _split_bf16(sol[i], 2)
        s_hi.append(hi)
        s_lo.append(lo)
    prod = [_dot(a_hi[i], s_hi[i]) + (_dot(a_hi[i], s_lo[i]) + _dot(a_lo[i], s_hi[i])) for i in ids]
    corr = [_dot(inv16[i], (rhs[i] - prod[i]).astype(BF16)) for i in ids]
    sol = [sol[i] + corr[i] for i in ids]
    for i, (rows, hd) in enumerate(units):
        u_ref[0, rows, col(hd)] = sol[i][:, :HEAD_LANES]
        w_ref[0, rows, col(hd)] = sol[i][:, HEAD_LANES:].astype(BF16)
        qk = incl(qk_raw[i] * decay[i])
        qd_ref[0, rows, col(hd)] = (q_ref[0, rows, col(hd)] * eg[i]).astype(BF16)
        for n in range(sub // c):
            blk = slice(n * c, (n + 1) * c)
            out_rows = slice(rows.start + n * c, rows.start + (n + 1) * c)
            qk_ref[0, hd, out_rows, :] = qk[blk, blk].astype(BF16)
            gl = gcc[i][(n + 1) * c - 1:(n + 1) * c, :]
            kd_ref[0, out_rows, col(hd)] = (kt[i][blk] * jnp.exp(gl - gcc[i][blk])).astype(BF16)


def _gdn_prep(qkv, gates, *, tt=GDN_PREP_TILE):
    b, t, _ = qkv.shape
    nh = GDN_HEADS
    dm = nh * HEAD_LANES
    blk = lambda part: pl.BlockSpec((1, tt, dm), lambda bi, i: (bi, i, part))
    return pl.pallas_call(
        _gdn_prep_kernel,
        grid=(b, t // tt),
        in_specs=[blk(0), blk(1), blk(2), pl.BlockSpec((1, tt, LANES), lambda bi, i: (bi, i, 0))],
        out_specs=[blk(0)] * 4 + [pl.BlockSpec((1, nh, tt, GDN_CHUNK), lambda bi, i: (bi, 0, i, 0))],
        out_shape=[jax.ShapeDtypeStruct((b, t, dm), F32)]
        + [jax.ShapeDtypeStruct((b, t, dm), BF16)] * 3
        + [jax.ShapeDtypeStruct((b, nh, t, GDN_CHUNK), BF16)],
        compiler_params=_params(("arbitrary", "arbitrary")),
        name="gdn_prep",
    )(qkv, qkv, qkv, gates)


def _gdn_scan_kernel(u_ref, w_ref, qd_ref, kd_ref, qk_ref, gates_ref, z_ref, nw_ref, o_ref, s_ref):
    c = GDN_CHUNK
    nb, tt = u_ref.shape[0], u_ref.shape[1]

    @pl.when(pl.program_id(1) == 0)
    def _():
        s_ref[...] = jnp.zeros_like(s_ref)

    chains = [(bi, hd) for bi in range(nb) for hd in range(GDN_HEADS)]
    cols = [slice(hd * HEAD_LANES, (hd + 1) * HEAD_LANES) for hd in range(GDN_HEADS)]
    state = [s_ref[bi, hd] for bi, hd in chains]
    for n in range(tt // c):
        rows = slice(n * c, (n + 1) * c)
        r = [_dot(jnp.concatenate([w_ref[bi, rows, cols[hd]], qd_ref[bi, rows, cols[hd]]], axis=0),
                  state[i].astype(BF16)) for i, (bi, hd) in enumerate(chains)]
        v_new = [(u_ref[bi, rows, cols[hd]] - r[i][:c]).astype(BF16) for i, (bi, hd) in enumerate(chains)]
        intra = [_dot(qk_ref[bi, hd, rows, :], v_new[i]) for i, (bi, hd) in enumerate(chains)]
        upd = [_dot_tn(kd_ref[bi, rows, cols[hd]], v_new[i]) for i, (bi, hd) in enumerate(chains)]
        for i, (bi, hd) in enumerate(chains):
            last = (n + 1) * c - 1
            decay_last = jnp.exp(gates_ref[bi, last:last + 1, GDN_HEADS + hd:GDN_HEADS + hd + 1])
            state[i] = state[i] * decay_last + upd[i]
            zt = z_ref[bi, rows, cols[hd]].astype(F32)
            o = r[i][c:] + intra[i]
            o_ref[bi, rows, cols[hd]] = (_rms(o, nw_ref[...]) * (zt * _sigmoid(zt))).astype(o_ref.dtype)
    for i, (bi, hd) in enumerate(chains):
        s_ref[bi, hd] = state[i]


def _gdn_scan(u, w, qd, kd, qk, gates, z, norm_w, *, tt=GDN_SCAN_TILE, nb=GDN_SCAN_BATCH):
    b, t, dm = u.shape
    nh = GDN_HEADS
    assert b % nb == 0 and t % tt == 0
    blk = pl.BlockSpec((nb, tt, dm), lambda bi, i: (bi, i, 0))
    return pl.pallas_call(
        _gdn_scan_kernel,
        grid=(b // nb, t // tt),
        in_specs=[blk, blk, blk, blk,
                  pl.BlockSpec((nb, nh, tt, GDN_CHUNK), lambda bi, i: (bi, 0, i, 0)),
                  pl.BlockSpec((nb, tt, LANES), lambda bi, i: (bi, i, 0)),
                  blk, _const_spec((1, HEAD_LANES))],
        out_specs=blk,
        out_shape=jax.ShapeDtypeStruct((b, t, dm), BF16),
        scratch_shapes=[pltpu.VMEM((nb, nh, GDN_HEAD_DIM, GDN_HEAD_DIM), F32)],
        compiler_params=_params(("arbitrary", "arbitrary")),
        name="gdn_scan",
    )(u, w, qd, kd, qk, gates, z, norm_w)


def _post_kernel(*refs, n_mix, final_norm):
    x_ref = refs[0]
    mix_refs = refs[1:1 + n_mix]
    wout_ref, g_ref, wup_ref, wdn_ref, p_ref, wpp_ref, wpg_ref = refs[1 + n_mix:8 + n_mix]
    rest = refs[8 + n_mix:]
    if final_norm:
        gf_ref, o_ref = rest
    else:
        (o_ref,) = rest
    mix = mix_refs[0][...] if n_mix == 1 else jnp.concatenate([r[...] for r in mix_refs], axis=1)
    x = x_ref[...] + _dot(mix, wout_ref[...])
    h = _rms(x, g_ref[...]).astype(BF16)
    d_ff = wup_ref.shape[1]
    acc = x
    for s in range(d_ff // FF_SEG):
        a = jnp.maximum(_dot(h, wup_ref[:, s * FF_SEG:(s + 1) * FF_SEG]), 0.0)
        acc = acc + _dot((a * a).astype(BF16), wdn_ref[s * FF_SEG:(s + 1) * FF_SEG, :])
    x = acc
    gate = _sigmoid(_dot(x.astype(BF16), wpg_ref[...]))
    x = x + _dot(p_ref[...].astype(BF16), wpp_ref[...]) * gate
    if final_norm:
        x = _rms(x, gf_ref[...])
    o_ref[...] = x


def _post(x2d, mixes, wout, g, wup, wdn, p2d, wpp, wpg, gf=None):
    m, d = x2d.shape
    tm = TOKEN_TILE
    row = lambda i: (i, 0)
    single = pl.Buffered(1)
    const = lambda a: pl.BlockSpec(a.shape, lambda i: (0, 0), pipeline_mode=single)
    args = [x2d, *mixes, wout, g, wup, wdn, p2d, wpp, wpg]
    in_specs = ([pl.BlockSpec((tm, d), row)]
                + [pl.BlockSpec((tm, a.shape[1]), row) for a in mixes]
                + [const(wout), const(g), const(wup), const(wdn), pl.BlockSpec((tm, p2d.shape[1]), row), const(wpp), const(wpg)])
    if gf is not None:
        args.append(gf)
        in_specs.append(const(gf))
    kern = functools.partial(_post_kernel, n_mix=len(mixes), final_norm=gf is not None)
    return pl.pallas_call(
        kern,
        grid=(m // tm,),
        in_specs=in_specs,
        out_specs=pl.BlockSpec((tm, d), row),
        out_shape=jax.ShapeDtypeStruct((m, d), F32),
        compiler_params=_params(("arbitrary",)),
        name="out_proj_mlp_ple",
    )(*args)


def _pad_lanes(a):
    return jnp.pad(a, ((0, 0), (0, LANES - a.shape[1])))


def kernel(x, p, positions, norm_mix, norm_mlp, norm_final, w_in_even, conv_w, a_log, dt_bias, gdn_norm,
           lam_q1, lam_k1, lam_q2, lam_k2, diff_norm, w_out_even, w_in_odd, b_forget, w_out_odd,
           w_mlp_up, w_mlp_down, w_ple_proj, w_ple_gate):
    b, t, d = x.shape
    depth = p.shape[0]
    m = b * t
    assert t % TOKEN_TILE == 0 and d % PROJ_SEG == 0
    nh = GDN_HEADS
    gdn_w = 3 * nh * GDN_HEAD_DIM + nh * GDN_HEAD_DIM
    assert w_in_even.shape[2] == gdn_w + 2 * nh + 3 * DIFF_HEADS * 2 * DIFF_QK_DIM

    inv_freq = ROPE_THETA ** (-jnp.arange(0, DIFF_QK_DIM, 2, dtype=F32) / DIFF_QK_DIM)
    ang = positions.astype(F32)[..., None] * inv_freq
    cos, sin = jnp.cos(ang), jnp.sin(ang)
    cos_t = jnp.concatenate([cos, cos, cos, cos], axis=-1).reshape(m, LANES)
    sin_t = jnp.concatenate([-sin, sin, -sin, sin], axis=-1).reshape(m, LANES)

    x2d = x.reshape(m, d)
    for i in range(depth):
        j = i // 2
        g_mix = norm_mix[i].reshape(1, d)
        if i % 2 == 0:
            lambda_init = 0.8 - 0.6 * math.exp(-0.3 * i)
            w = w_in_even[j]
            wm = jnp.concatenate([w[:, :gdn_w], w[:, gdn_w + 2 * nh:]], axis=1).astype(BF16)
            wg = _pad_lanes(w[:, gdn_w:gdn_w + 2 * nh]).astype(BF16)
            alog_row = _pad_lanes(jnp.concatenate([jnp.zeros((nh,), F32), a_log[j]]).reshape(1, 2 * nh))
            dt_row = _pad_lanes(jnp.concatenate([jnp.zeros((nh,), F32), dt_bias[j]]).reshape(1, 2 * nh))
            qkv, z, qkb, vbt, gates = _even_in(x2d, g_mix, wm, wg, conv_w[j], alog_row, dt_row, cos_t, sin_t, t)
            qkv, z, qkb, gates = (a.reshape(b, t, -1) for a in (qkv, z, qkb, gates))
            u, wy, qd, kd, qk = _gdn_prep(qkv, gates)
            o_a = _gdn_scan(u, wy, qd, kd, qk, gates, z, gdn_norm[j].reshape(1, HEAD_LANES))
            lam_params = jnp.stack([lam_q1[j], lam_k1[j], lam_q2[j], lam_k2[j]])
            o_b = _diff_attention(qkb, vbt, lam_params, diff_norm[j].reshape(1, HEAD_LANES), lambda_init)
            mixes = [o_a.reshape(m, -1), o_b.reshape(m, -1)]
            wout = w_out_even[j].astype(BF16)
        else:
            w = w_in_odd[j]
            d_mix = (w.shape[1] - FOX_HEADS) // 4
            wm = w[:, :4 * d_mix].astype(BF16)
            wf = _pad_lanes(w[:, 4 * d_mix:]).astype(BF16)
            bf_row = _pad_lanes(b_forget[j].reshape(1, FOX_HEADS))
            qt, k, vt, gate, qbt, kb = _odd_in(x2d, g_mix, wm, wf, bf_row, t)
            k, gate, kb = (a.reshape(b, t, -1) for a in (k, gate, kb))
            o = _fox_attention(qt, k, vt, gate, qbt, kb)
            mixes = [o.reshape(m, -1)]
            wout = w_out_odd[j].astype(BF16)
        x2d = _post(x2d, mixes, wout, norm_mlp[i].reshape(1, d), w_mlp_up[i].astype(BF16),
                    w_mlp_down[i].astype(BF16), p[i].reshape(m, -1), w_ple_proj[i].astype(BF16),
                    w_ple_gate[i].astype(BF16), norm_final.reshape(1, d) if i == depth - 1 else None)
    return x2d.reshape(b, t, d)
```

```python
import functools
import math

import jax
import jax.numpy as jnp
import numpy as np
from jax import lax
from jax.experimental import pallas as pl
from jax.experimental.pallas import tpu as pltpu

F32 = jnp.float32
BF16 = jnp.bfloat16

GDN_HEADS = 4
GDN_HEAD_DIM = 128
GDN_CHUNK = 64
CONV_WIDTH = 4
DIFF_HEADS = 4
DIFF_QK_DIM = 64
FOX_HEADS = 8
HEAD_LANES = 128
ROPE_THETA = 10000.0
EPS = 1e-6
NEG_INF = -1e30
LOG2E = 1.4426950408889634
LANES = 128
SUBLANES = 8
VMEM_LIMIT_BYTES = 56 * 1024 * 1024

TOKEN_TILE = 1024
EVEN_TOKEN_TILE = 512
PROJ_SEG = 512
FF_SEG = 1024
GDN_PREP_SUBTILE = 256
GDN_PREP_TILE = 512
GDN_SCAN_TILE = 256
GDN_SCAN_BATCH = 4
ATTN_TILE = 512
ATTN_COLUMN_SPLIT = 2
BIAS_LANES_PER_HEAD = 16
ONES_ROWS = 16


def _dot(a, b):
    return jnp.dot(a, b, preferred_element_type=F32)


def _split_bf16(x, parts):
    out = []
    for _ in range(parts):
        piece = x.astype(BF16)
        out.append(piece)
        x = x - piece.astype(F32)
    return out


def _dot_nt(a, b):
    return lax.dot_general(a, b, (((1,), (1,)), ((), ())), preferred_element_type=F32)


def _dot_tn(a, b):
    return lax.dot_general(a, b, (((0,), (0,)), ((), ())), preferred_element_type=F32)


def _rms(x, g):
    return x * lax.rsqrt(jnp.mean(x * x, axis=-1, keepdims=True) + EPS) * g


def _sigmoid(x):
    return 1.0 / (1.0 + jnp.exp(-x))


def _softplus(x):
    return jnp.maximum(x, 0.0) + jnp.log1p(jnp.exp(-jnp.abs(x)))


def _row_scan(x, period):
    rows = lax.broadcasted_iota(jnp.int32, x.shape, 0) % period
    s = 1
    while s < period:
        x = x + jnp.where(rows >= s, pltpu.roll(x, s, 0), 0.0)
        s *= 2
    return x


def _const_spec(shape):
    return pl.BlockSpec(shape, lambda *_: (0,) * len(shape))


def _params(sem):
    return pltpu.CompilerParams(dimension_semantics=sem, vmem_limit_bytes=VMEM_LIMIT_BYTES)


def _even_in_kernel(x_ref, g_ref, wm_ref, wg_ref, conv_ref, alog_ref, dt_ref, cos_ref, sin_ref,
                    qkv_ref, z_ref, qkb_ref, vbt_ref, gates_ref, h_ref, carry_ref, tr_ref, pad_ref, *, tiles_per_seq):
    tm = x_ref.shape[0]
    i = pl.program_id(0)
    h_ref[...] = _rms(x_ref[...], g_ref[...]).astype(BF16)
    seq_start = (i % tiles_per_seq) == 0
    seg = lambda s: slice(s * PROJ_SEG, (s + 1) * PROJ_SEG)
    project = lambda s: _dot(h_ref[...], wm_ref[:, seg(s)])

    def gdn_qkv(s, y):
        cols = seg(s)
        pad_ref[0:SUBLANES, :] = jnp.where(seq_start, 0.0, carry_ref[:, cols])
        pad_ref[SUBLANES:, :] = y
        carry_ref[:, cols] = y[tm - SUBLANES:, :]
        w = conv_ref[:, cols]
        a = y * w[CONV_WIDTH - 1:CONV_WIDTH, :]
        for k in range(1, CONV_WIDTH):
            a = a + pad_ref[SUBLANES - k:SUBLANES - k + tm, :] * w[CONV_WIDTH - 1 - k:CONV_WIDTH - k, :]
        a = a * _sigmoid(a)
        if s < 2:
            outs = []
            for hd in range(GDN_HEADS):
                blk = a[:, hd * HEAD_LANES:(hd + 1) * HEAD_LANES]
                n = blk * lax.rsqrt(jnp.sum(blk * blk, axis=-1, keepdims=True) + EPS)
                outs.append(n * (GDN_HEAD_DIM ** -0.5) if s == 0 else n)
            a = jnp.concatenate(outs, axis=1)
        qkv_ref[:, cols] = a

    def gdn_gate(s, y):
        z_ref[...] = y.astype(BF16)

    def diff_qk(s, y):
        cos = jnp.concatenate([cos_ref[...]] * (PROJ_SEG // LANES), axis=1)
        sin = jnp.concatenate([sin_ref[...]] * (PROJ_SEG // LANES), axis=1)
        lane = lax.broadcasted_iota(jnp.int32, (tm, PROJ_SEG), 1)
        first_half = (lane % DIFF_QK_DIM) < (DIFF_QK_DIM // 2)
        swapped = jnp.where(first_half, pltpu.roll(y, PROJ_SEG - DIFF_QK_DIM // 2, 1),
                            pltpu.roll(y, DIFF_QK_DIM // 2, 1))
        scale = DIFF_QK_DIM ** -0.5 * LOG2E if s == 4 else 1.0
        qkb_ref[:, seg(s - 4)] = ((y * cos + swapped * sin) * scale).astype(BF16)

    def diff_v(s, y):
        tr_ref[...] = y
        vbt_ref[0] = tr_ref[...].T.astype(BF16)

    stages = ((0, gdn_qkv), (3, gdn_gate), (1, gdn_qkv), (6, diff_v), (2, gdn_qkv), (4, diff_qk), (5, diff_qk))
    pending = project(stages[0][0])
    for n, (s, epilogue) in enumerate(stages):
        upcoming = project(stages[n + 1][0]) if n + 1 < len(stages) else _dot(h_ref[...], wg_ref[...])
        epilogue(s, pending)
        pending = upcoming

    graw = pending
    beta = _sigmoid(graw)
    g = -jnp.exp(alog_ref[...]) * _softplus(graw + dt_ref[...])
    gc = _row_scan(g, GDN_CHUNK)
    lane_g = lax.broadcasted_iota(jnp.int32, (tm, LANES), 1)
    gates_ref[...] = jnp.where(lane_g < GDN_HEADS, beta, gc)


def _even_in(x2d, g, wm, wg, conv_w, alog_row, dt_row, cos_t, sin_t, seq_len):
    m, d = x2d.shape
    tm = EVEN_TOKEN_TILE
    n_main = wm.shape[1]
    tps = seq_len // tm
    kern = functools.partial(_even_in_kernel, tiles_per_seq=tps)
    row = lambda i: (i, 0)
    return pl.pallas_call(
        kern,
        grid=(m // tm,),
        in_specs=[
            pl.BlockSpec((tm, d), row),
            _const_spec((1, d)),
            _const_spec((d, n_main)),
            _const_spec((d, LANES)),
            _const_spec(conv_w.shape),
            _const_spec((1, LANES)),
            _const_spec((1, LANES)),
            pl.BlockSpec((tm, LANES), row),
            pl.BlockSpec((tm, LANES), row),
        ],
        out_specs=[
            pl.BlockSpec((tm, 3 * PROJ_SEG), row),
            pl.BlockSpec((tm, PROJ_SEG), row),
            pl.BlockSpec((tm, 2 * PROJ_SEG), row),
            pl.BlockSpec((1, PROJ_SEG, tm), lambda i: (i // tps, 0, i % tps)),
            pl.BlockSpec((tm, LANES), row),
        ],
        out_shape=[
            jax.ShapeDtypeStruct((m, 3 * PROJ_SEG), F32),
            jax.ShapeDtypeStruct((m, PROJ_SEG), BF16),
            jax.ShapeDtypeStruct((m, 2 * PROJ_SEG), BF16),
            jax.ShapeDtypeStruct((m // seq_len, PROJ_SEG, seq_len), BF16),
            jax.ShapeDtypeStruct((m, LANES), F32),
        ],
        scratch_shapes=[pltpu.VMEM((tm, d), BF16), pltpu.VMEM((SUBLANES, 3 * PROJ_SEG), F32), pltpu.VMEM((tm, PROJ_SEG), F32),
                        pltpu.VMEM((tm + SUBLANES, PROJ_SEG), F32)],
        compiler_params=_params(("arbitrary",)),
        name="even_in_proj",
    )(x2d, g, wm, wg, conv_w, alog_row, dt_row, cos_t, sin_t)


def _odd_in_kernel(x_ref, g_ref, wm_ref, wf_ref, bf_ref, sel_ref, ones_ref, qt_ref, k_ref, vt_ref, gate_ref, qbt_ref, kb_ref,
                   h_ref, carry_ref, tr_ref, *, tiles_per_seq, d_mix):
    tm = x_ref.shape[0]
    i = pl.program_id(0)
    h_ref[...] = _rms(x_ref[...], g_ref[...]).astype(BF16)
    f = _dot(h_ref[...], wf_ref[...]) + bf_ref[...]
    log_f = jnp.minimum(f, 0.0) - jnp.log1p(jnp.exp(-jnp.abs(f)))
    prev = jnp.where((i % tiles_per_seq) == 0, 0.0, carry_ref[0:1, :])
    cum = _row_scan(log_f, tm) + prev
    carry_ref[...] = jnp.broadcast_to(cum[tm - 1:tm, :], carry_ref.shape)
    pieces = jnp.concatenate(_split_bf16(LOG2E * cum, 3), axis=1)

    head_dim = d_mix // FOX_HEADS
    for o_ref, base, scale in ((qt_ref, 0, head_dim ** -0.5 * LOG2E), (k_ref, d_mix, 1.0),
                               (vt_ref, 2 * d_mix, 1.0), (gate_ref, 3 * d_mix, 1.0)):
        for s in range(d_mix // PROJ_SEG):
            cols = slice(s * PROJ_SEG, (s + 1) * PROJ_SEG)
            y = _dot(h_ref[...], wm_ref[:, base + s * PROJ_SEG:base + (s + 1) * PROJ_SEG])
            if o_ref is qt_ref or o_ref is vt_ref:
                tr_ref[...] = y * scale
                o_ref[0, cols, :] = tr_ref[...].T.astype(BF16)
            else:
                o_ref[:, cols] = y.astype(BF16)

    lanes = _dot(pieces, sel_ref[...]) + ones_ref[...]
    tr_ref[:, :LANES] = lanes[:, :LANES]
    qbt_ref[0] = tr_ref[:, :LANES].T.astype(BF16)
    kb_ref[...] = lanes[:, LANES:].astype(BF16)


def _bias_lane_tables():
    sel = np.zeros((3 * LANES, 2 * LANES), np.float32)
    ones = np.zeros((1, 2 * LANES), np.float32)
    for h in range(FOX_HEADS):
        base = BIAS_LANES_PER_HEAD * h
        for piece in range(3):
            sel[LANES * piece + h, base + 3 + piece] = 1.0
            sel[LANES * piece + h, LANES + base + piece] = -1.0
            ones[0, base + piece] = 1.0
            ones[0, LANES + base + 3 + piece] = 1.0
    return jnp.asarray(sel, BF16), jnp.asarray(ones, F32)


def _odd_in(x2d, g, wm, wf, bf_row, seq_len):
    m, d = x2d.shape
    sel, ones_row = _bias_lane_tables()
    tm = TOKEN_TILE
    d_mix = wm.shape[1] // 4
    tps = seq_len // tm
    kern = functools.partial(_odd_in_kernel, tiles_per_seq=tps, d_mix=d_mix)
    row = lambda i: (i, 0)
    row_blk = pl.BlockSpec((tm, d_mix), row)
    row_shape = jax.ShapeDtypeStruct((m, d_mix), BF16)
    col_blk = lambda width: pl.BlockSpec((1, width, tm), lambda i: (i // tps, 0, i % tps))
    col_shape = lambda width: jax.ShapeDtypeStruct((m // seq_len, width, seq_len), BF16)
    return pl.pallas_call(
        kern,
        grid=(m // tm,),
        in_specs=[
            pl.BlockSpec((tm, d), row),
            _const_spec((1, d)),
            _const_spec(wm.shape),
            _const_spec((d, LANES)),
            _const_spec((1, LANES)),
            _const_spec(sel.shape),
            _const_spec(ones_row.shape),
        ],
        out_specs=[col_blk(d_mix), row_blk, col_blk(d_mix), row_blk, col_blk(LANES), pl.BlockSpec((tm, LANES), row)],
        out_shape=[col_shape(d_mix), row_shape, col_shape(d_mix), row_shape, col_shape(LANES),
                   jax.ShapeDtypeStruct((m, LANES), BF16)],
        scratch_shapes=[pltpu.VMEM((tm, d), BF16), pltpu.VMEM((SUBLANES, LANES), F32), pltpu.VMEM((tm, PROJ_SEG), F32)],
        compiler_params=_params(("arbitrary",)),
        name="odd_in_proj",
    )(x2d, g, wm, wf, bf_row, sel, ones_row)


def _attn_kernel(*refs, tq, fox, lambda_init):
    if fox:
        q_ref, k_ref, vt_ref, gate_ref, qbt_ref, kball_ref, o_ref, st_ref, mx_ref, m_ref, acc_ref, kb_ref = refs
    else:
        q_ref, k_ref, vt_ref, lam_ref, nw_ref, o_ref, st_ref, mx_ref, m_ref, acc_ref = refs
    tk = tq
    hg = pl.program_id(1)
    n_tiles = k_ref.shape[1] // tq
    head = lambda g: slice(g * HEAD_LANES, (g + 1) * HEAD_LANES)
    kv = [head(0), head(1)] if fox else [head(0), head(0)]
    tile_rows = lambda tile: pl.ds(pl.multiple_of(tile * tq, tq), tq)

    def queries(g, tile):
        if fox:
            return jnp.concatenate([q_ref[0, head(g), tile_rows(tile)], qbt_ref[0, :, tile_rows(tile)]], axis=0)
        q = q_ref[0, tile_rows(tile), :]
        lane = lax.broadcasted_iota(jnp.int32, q.shape, 1)
        keep = (lane < DIFF_QK_DIM) if g == 0 else (lane >= DIFF_QK_DIM)
        return jnp.where(keep, q, jnp.zeros_like(q))

    def scores_of(qmat, g, j):
        k0 = pl.multiple_of(j * tk, tk)
        kj = k_ref[0, pl.ds(k0, tk), kv[g]]
        if fox:
            return _dot(jnp.concatenate([kj, kb_ref[pl.ds(k0, tk), kv[g]]], axis=1), qmat)
        return _dot_nt(kj, qmat)

    if fox:
        kb_all = kball_ref[0]
        owner = lax.broadcasted_iota(jnp.int32, kb_all.shape, 1) // BIAS_LANES_PER_HEAD
        for g in range(2):
            kb_ref[:, head(g)] = jnp.where(owner == hg * 2 + g, kb_all, jnp.zeros_like(kb_all))
    else:
        lam_p = lam_ref[...]
        lam = (jnp.exp(jnp.sum(lam_p[0:1] * lam_p[1:2], axis=1, keepdims=True))
               - jnp.exp(jnp.sum(lam_p[2:3] * lam_p[3:4], axis=1, keepdims=True)) + lambda_init)
    ones = jnp.ones((ONES_ROWS, tk), BF16)

    def lookahead(g, qmat, tile, j, diagonal, slot):
        st = scores_of(qmat, g, j)
        if diagonal:
            kpos = j * tk + lax.broadcasted_iota(jnp.int32, (tk, tq), 0)
            qpos = tile * tq + lax.broadcasted_iota(jnp.int32, (tk, tq), 1)
            st = jnp.where(qpos >= kpos, st, NEG_INF)
        st_ref[g, slot] = st
        mx_ref[g, slot] = jnp.max(st, axis=0, keepdims=True)

    def absorb(g, j, slot):
        k0 = pl.multiple_of(j * tk, tk)
        vt = jnp.concatenate([vt_ref[0, kv[g], pl.ds(k0, tk)], ones], axis=0)
        width = tq // ATTN_COLUMN_SPLIT
        for part in range(ATTN_COLUMN_SPLIT):
            cs = slice(part * width, (part + 1) * width)
            m = m_ref[g, :, cs]
            m_new = jnp.maximum(m, mx_ref[g, slot, :, cs])
            p = jnp.exp2(st_ref[g, slot, :, cs] - m_new).astype(BF16)
            acc_ref[g, :, cs] = jnp.exp2(m - m_new) * acc_ref[g, :, cs] + _dot(vt, p)
            m_ref[g, :, cs] = m_new

    for g in range(2):
        lookahead(g, queries(g, 0), 0, 0, True, 0)

    def tile(qi, carry):
        qs = [queries(g, qi) for g in range(2)]

        def pair(jj, c):
            for u in range(2):
                for g in range(2):
                    lookahead(g, qs[g], qi, 2 * jj + u + 1, False, 1 - u)
                    absorb(g, 2 * jj + u, u)
            return c

        def even_step(j, c):
            for g in range(2):
                lookahead(g, qs[g], qi, j + 1, False, 1)
                absorb(g, j, 0)
            return c

        m_ref[...] = jnp.full(m_ref.shape, NEG_INF, F32)
        acc_ref[...] = jnp.zeros(acc_ref.shape, F32)
        n_plain = jnp.maximum(qi - 1, 0)
        n_pairs = n_plain // 2
        lax.fori_loop(0, n_pairs, pair, 0)
        lax.fori_loop(2 * n_pairs, n_plain, even_step, 0)
        nxt = jnp.minimum(qi + 1, n_tiles - 1)

        def finish(diag_slot):
            def last_step(j, c):
                for g in range(2):
                    lookahead(g, qs[g], qi, j + 1, True, diag_slot)
                    absorb(g, j, 1 - diag_slot)
                return c

            lax.fori_loop(n_plain, qi, last_step, 0)
            for g in range(2):
                if diag_slot == 0:
                    absorb(g, qi, 0)
                    lookahead(g, queries(g, nxt), nxt, 0, False, 0)
                else:
                    lookahead(g, queries(g, nxt), nxt, 0, False, 0)
                    absorb(g, qi, 1)
            outs = [acc_ref[g, :HEAD_LANES, :] / acc_ref[g, HEAD_LANES:HEAD_LANES + 1, :] for g in range(2)]
            rows = tile_rows(qi)
            if fox:
                for g in range(2):
                    gate = _sigmoid(gate_ref[0, rows, head(g)].astype(F32))
                    o_ref[0, rows, head(g)] = (outs[g].T * gate).astype(o_ref.dtype)
            else:
                o = (outs[0] - lam * outs[1]).T
                o_ref[0, rows, :] = (_rms(o, nw_ref[...]) * (1.0 - lambda_init)).astype(o_ref.dtype)

        for parity in range(2):
            pl.when(qi % 2 == parity)(functools.partial(finish, parity))
        return carry

    lax.fori_loop(0, n_tiles, tile, 0)


def _attn_state(tq):
    return [pltpu.VMEM((2, 2, tq, tq), F32), pltpu.VMEM((2, 2, 1, tq), F32), pltpu.VMEM((2, 1, tq), F32),
            pltpu.VMEM((2, HEAD_LANES + ONES_ROWS, tq), F32)]


def _fox_attention(qt, k, vt, gate, qbt, kb, *, tq=ATTN_TILE):
    b, t, dm = k.shape
    width = 2 * HEAD_LANES
    kern = functools.partial(_attn_kernel, tq=tq, fox=True, lambda_init=0.0)
    seq = pl.BlockSpec((1, t, width), lambda bi, h: (bi, 0, h))
    seq_t = pl.BlockSpec((1, width, t), lambda bi, h: (bi, h, 0))
    seq_bias = pl.BlockSpec((1, t, LANES), lambda bi, h: (bi, 0, 0))
    return pl.pallas_call(
        kern,
        grid=(b, dm // width),
        in_specs=[seq_t, seq, seq_t, seq, pl.BlockSpec((1, LANES, t), lambda bi, h: (bi, 0, 0)), seq_bias],
        out_specs=seq,
        out_shape=jax.ShapeDtypeStruct((b, t, dm), BF16),
        scratch_shapes=_attn_state(tq) + [pltpu.VMEM((t, width), BF16)],
        compiler_params=_params(("arbitrary", "arbitrary")),
        name="fox_attention",
    )(qt, k, vt, gate, qbt, kb)


def _diff_attention(qk, vt, lam_params, norm_w, lambda_init, *, tq=ATTN_TILE):
    b, t, _ = qk.shape
    nh = DIFF_HEADS
    kern = functools.partial(_attn_kernel, tq=tq, fox=False, lambda_init=lambda_init)
    seq = pl.BlockSpec((1, t, HEAD_LANES), lambda bi, h: (bi, 0, h))
    return pl.pallas_call(
        kern,
        grid=(b, nh),
        in_specs=[seq,
                  pl.BlockSpec((1, t, HEAD_LANES), lambda bi, h: (bi, 0, nh + h)),
                  pl.BlockSpec((1, HEAD_LANES, t), lambda bi, h: (bi, h, 0)),
                  _const_spec(lam_params.shape), _const_spec((1, HEAD_LANES))],
        out_specs=seq,
        out_shape=jax.ShapeDtypeStruct((b, t, nh * HEAD_LANES), BF16),
        scratch_shapes=_attn_state(tq),
        compiler_params=_params(("arbitrary", "arbitrary")),
        name="diff_attention",
    )(qk, qk, vt, lam_params, norm_w)


def _gdn_prep_kernel(q_ref, k_ref, v_ref, gates_ref, u_ref, w_ref, qd_ref, kd_ref, qk_ref):
    c = GDN_CHUNK
    sub = GDN_PREP_SUBTILE
    units = [(slice(t0, t0 + sub), hd) for t0 in range(0, q_ref.shape[1], sub) for hd in range(GDN_HEADS)]
    ids = range(len(units))
    col = lambda hd: slice(hd * HEAD_LANES, (hd + 1) * HEAD_LANES)
    lane = lax.broadcasted_iota(jnp.int32, (sub, LANES), 1)
    ri = lax.broadcasted_iota(jnp.int32, (sub, sub), 0)
    ci = lax.broadcasted_iota(jnp.int32, (sub, sub), 1)
    chunk_start = ri - ri % c
    incl = lambda a: jnp.where(ci <= ri, jnp.where(ci >= chunk_start, a, 0.0), 0.0)
    strict = lambda a: jnp.where(ci < ri, jnp.where(ci >= chunk_start, a, 0.0), 0.0)
    ident = jnp.where(ri == ci, 1.0, 0.0)
    ones = jnp.ones((sub, LANES), BF16)
    kt = [k_ref[0, rows, col(hd)] for rows, hd in units]
    beta = [gates_ref[0, rows, hd:hd + 1] for rows, hd in units]
    gcc = [gates_ref[0, rows, GDN_HEADS + hd:GDN_HEADS + hd + 1] for rows, hd in units]
    k16 = [kt[i].astype(BF16) for i in ids]
    kb = [kt[i] * beta[i] for i in ids]

    qk_raw = [_dot_nt(q_ref[0, rows, col(hd)].astype(BF16), k16[i]) for i, (rows, hd) in enumerate(units)]
    kk = [_dot_nt(kb[i].astype(BF16), k16[i]) for i in ids]
    gc_row = []
    for i in ids:
        g_hi, g_mid, g_lo = (piece.astype(F32) for piece in _split_bf16(gcc[i], 3))
        pieces = jnp.where(lane == 0, g_hi, jnp.where(lane == 1, g_mid, jnp.where(lane == 2, g_lo, 0.0)))
        gc_row.append(_dot_nt(ones, pieces.astype(BF16)))
    decay = [incl(jnp.exp(incl(gcc[i] - gc_row[i]))) for i in ids]
    lower = [strict(kk[i] * decay[i]) for i in ids]

    span = ri ^ ci
    inv = [ident - jnp.where(span == 1, lower[i], 0.0) for i in ids]
    s_blk = 2
    while s_blk < c:
        shift = int(math.log2(s_blk))
        inv16 = [inv[i].astype(BF16) for i in ids]
        coupled = [_dot(jnp.where((span >> shift) == 1, lower[i], 0.0).astype(BF16), inv16[i]) for i in ids]
        inv = [inv[i] - _dot(inv16[i], coupled[i].astype(BF16)) for i in ids]
        s_blk *= 2
    inv16 = [inv[i].astype(BF16) for i in ids]

    eg = [jnp.exp(gcc[i]) for i in ids]
    rhs = [jnp.concatenate([v_ref[0, rows, col(hd)] * beta[i], kb[i] * eg[i]], axis=1)
           for i, (rows, hd) in enumerate(units)]
    sol = [_dot(inv16[i], rhs[i].astype(BF16)) for i in ids]
    a_hi, a_lo, s_hi, s_lo = [], [], [], []
    for i in ids:
        hi, lo = _split_bf16(ident + lower[i], 2)
        a_hi.append(hi)
        a_lo.append(lo)
        hi, lo = _split_bf16(sol[i], 2)
        s_hi.append(hi)
        s_lo.append(lo)
    prod = [_dot(a_hi[i], s_hi[i]) + (_dot(a_hi[i], s_lo[i]) + _dot(a_lo[i], s_hi[i])) for i in ids]
    corr = [_dot(inv16[i], (rhs[i] - prod[i]).astype(BF16)) for i in ids]
    sol = [sol[i] + corr[i] for i in ids]
    for i, (rows, hd) in enumerate(units):
        u_ref[0, rows, col(hd)] = sol[i][:, :HEAD_LANES]
        w_ref[0, rows, col(hd)] = sol[i][:, HEAD_LANES:].astype(BF16)
        qk = incl(qk_raw[i] * decay[i])
        qd_ref[0, rows, col(hd)] = (q_ref[0, rows, col(hd)] * eg[i]).astype(BF16)
        for n in range(sub // c):
            blk = slice(n * c, (n + 1) * c)
            out_rows = slice(rows.start + n * c, rows.start + (n + 1) * c)
            qk_ref[0, hd, out_rows, :] = qk[blk, blk].astype(BF16)
            gl = gcc[i][(n + 1) * c - 1:(n + 1) * c, :]
            kd_ref[0, out_rows, col(hd)] = (kt[i][blk] * jnp.exp(gl - gcc[i][blk])).astype(BF16)


def _gdn_prep(qkv, gates, *, tt=GDN_PREP_TILE):
    b, t, _ = qkv.shape
    nh = GDN_HEADS
    dm = nh * HEAD_LANES
    blk = lambda part: pl.BlockSpec((1, tt, dm), lambda bi, i: (bi, i, part))
    return pl.pallas_call(
        _gdn_prep_kernel,
        grid=(b, t // tt),
        in_specs=[blk(0), blk(1), blk(2), pl.BlockSpec((1, tt, LANES), lambda bi, i: (bi, i, 0))],
        out_specs=[blk(0)] * 4 + [pl.BlockSpec((1, nh, tt, GDN_CHUNK), lambda bi, i: (bi, 0, i, 0))],
        out_shape=[jax.ShapeDtypeStruct((b, t, dm), F32)]
        + [jax.ShapeDtypeStruct((b, t, dm), BF16)] * 3
        + [jax.ShapeDtypeStruct((b, nh, t, GDN_CHUNK), BF16)],
        compiler_params=_params(("arbitrary", "arbitrary")),
        name="gdn_prep",
    )(qkv, qkv, qkv, gates)


def _gdn_scan_kernel(u_ref, w_ref, qd_ref, kd_ref, qk_ref, gates_ref, z_ref, nw_ref, o_ref, s_ref):
    c = GDN_CHUNK
    nb, tt = u_ref.shape[0], u_ref.shape[1]

    @pl.when(pl.program_id(1) == 0)
    def _():
        s_ref[...] = jnp.zeros_like(s_ref)

    chains = [(bi, hd) for bi in range(nb) for hd in range(GDN_HEADS)]
    cols = [slice(hd * HEAD_LANES, (hd + 1) * HEAD_LANES) for hd in range(GDN_HEADS)]
    state = [s_ref[bi, hd] for bi, hd in chains]
    for n in range(tt // c):
        rows = slice(n * c, (n + 1) * c)
        r = [_dot(jnp.concatenate([w_ref[bi, rows, cols[hd]], qd_ref[bi, rows, cols[hd]]], axis=0),
                  state[i].astype(BF16)) for i, (bi, hd) in enumerate(chains)]
        v_new = [(u_ref[bi, rows, cols[hd]] - r[i][:c]).astype(BF16) for i, (bi, hd) in enumerate(chains)]
        intra = [_dot(qk_ref[bi, hd, rows, :], v_new[i]) for i, (bi, hd) in enumerate(chains)]
        upd = [_dot_tn(kd_ref[bi, rows, cols[hd]], v_new[i]) for i, (bi, hd) in enumerate(chains)]
        for i, (bi, hd) in enumerate(chains):
            last = (n + 1) * c - 1
            decay_last = jnp.exp(gates_ref[bi, last:last + 1, GDN_HEADS + hd:GDN_HEADS + hd + 1])
            state[i] = state[i] * decay_last + upd[i]
            zt = z_ref[bi, rows, cols[hd]].astype(F32)
            o = r[i][c:] + intra[i]
            o_ref[bi, rows, cols[hd]] = (_rms(o, nw_ref[...]) * (zt * _sigmoid(zt))).astype(o_ref.dtype)
    for i, (bi, hd) in enumerate(chains):
        s_ref[bi, hd] = state[i]


def _gdn_scan(u, w, qd, kd, qk, gates, z, norm_w, *, tt=GDN_SCAN_TILE, nb=GDN_SCAN_BATCH):
    b, t, dm = u.shape
    nh = GDN_HEADS
    assert b % nb == 0 and t % tt == 0
    blk = pl.BlockSpec((nb, tt, dm), lambda bi, i: (bi, i, 0))
    return pl.pallas_call(
        _gdn_scan_kernel,
        grid=(b // nb, t // tt),
        in_specs=[blk, blk, blk, blk,
                  pl.BlockSpec((nb, nh, tt, GDN_CHUNK), lambda bi, i: (bi, 0, i, 0)),
                  pl.BlockSpec((nb, tt, LANES), lambda bi, i: (bi, i, 0)),
                  blk, _const_spec((1, HEAD_LANES))],
        out_specs=blk,
        out_shape=jax.ShapeDtypeStruct((b, t, dm), BF16),
        scratch_shapes=[pltpu.VMEM((nb, nh, GDN_HEAD_DIM, GDN_HEAD_DIM), F32)],
        compiler_params=_params(("arbitrary", "arbitrary")),
        name="gdn_scan",
    )(u, w, qd, kd, qk, gates, z, norm_w)


def _post_kernel(*refs, n_mix, final_norm):
    x_ref = refs[0]
    mix_refs = refs[1:1 + n_mix]
    wout_ref, g_ref, wup_ref, wdn_ref, p_ref, wpp_ref, wpg_ref = refs[1 + n_mix:8 + n_mix]
    rest = refs[8 + n_mix:]
    if final_norm:
        gf_ref, o_ref = rest
    else:
        (o_ref,) = rest
    mix = mix_refs[0][...] if n_mix == 1 else jnp.concatenate([r[...] for r in mix_refs], axis=1)
    x = x_ref[...] + _dot(mix, wout_ref[...])
    h = _rms(x, g_ref[...]).astype(BF16)
    d_ff = wup_ref.shape[1]
    acc = x
    for s in range(d_ff // FF_SEG):
        a = jnp.maximum(_dot(h, wup_ref[:, s * FF_SEG:(s + 1) * FF_SEG]), 0.0)
        acc = acc + _dot((a * a).astype(BF16), wdn_ref[s * FF_SEG:(s + 1) * FF_SEG, :])
    x = acc
    gate = _sigmoid(_dot(x.astype(BF16), wpg_ref[...]))
    x = x + _dot(p_ref[...].astype(BF16), wpp_ref[...]) * gate
    if final_norm:
        x = _rms(x, gf_ref[...])
    o_ref[...] = x


def _post(x2d, mixes, wout, g, wup, wdn, p2d, wpp, wpg, gf=None):
    m, d = x2d.shape
    tm = TOKEN_TILE
    row = lambda i: (i, 0)
    single = pl.Buffered(1)
    const = lambda a: pl.BlockSpec(a.shape, lambda i: (0, 0), pipeline_mode=single)
    args = [x2d, *mixes, wout, g, wup, wdn, p2d, wpp, wpg]
    in_specs = ([pl.BlockSpec((tm, d), row)]
                + [pl.BlockSpec((tm, a.shape[1]), row) for a in mixes]
                + [const(wout), const(g), const(wup), const(wdn), pl.BlockSpec((tm, p2d.shape[1]), row), const(wpp), const(wpg)])
    if gf is not None:
        args.append(gf)
        in_specs.append(const(gf))
    kern = functools.partial(_post_kernel, n_mix=len(mixes), final_norm=gf is not None)
    return pl.pallas_call(
        kern,
        grid=(m // tm,),
        in_specs=in_specs,
        out_specs=pl.BlockSpec((tm, d), row),
        out_shape=jax.ShapeDtypeStruct((m, d), F32),
        compiler_params=_params(("arbitrary",)),
        name="out_proj_mlp_ple",
    )(*args)


def _pad_lanes(a):
    return jnp.pad(a, ((0, 0), (0, LANES - a.shape[1])))


def kernel(x, p, positions, norm_mix, norm_mlp, norm_final, w_in_even, conv_w, a_log, dt_bias, gdn_norm,
           lam_q1, lam_k1, lam_q2, lam_k2, diff_norm, w_out_even, w_in_odd, b_forget, w_out_odd,
           w_mlp_up, w_mlp_down, w_ple_proj, w_ple_gate):
    b, t, d = x.shape
    depth = p.shape[0]
    m = b * t
    assert t % TOKEN_TILE == 0 and d % PROJ_SEG == 0
    nh = GDN_HEADS
    gdn_w = 3 * nh * GDN_HEAD_DIM + nh * GDN_HEAD_DIM
    assert w_in_even.shape[2] == gdn_w + 2 * nh + 3 * DIFF_HEADS * 2 * DIFF_QK_DIM

    inv_freq = ROPE_THETA ** (-jnp.arange(0, DIFF_QK_DIM, 2, dtype=F32) / DIFF_QK_DIM)
    ang = positions.astype(F32)[..., None] * inv_freq
    cos, sin = jnp.cos(ang), jnp.sin(ang)
    cos_t = jnp.concatenate([cos, cos, cos, cos], axis=-1).reshape(m, LANES)
    sin_t = jnp.concatenate([-sin, sin, -sin, sin], axis=-1).reshape(m, LANES)

    x2d = x.reshape(m, d)
    for i in range(depth):
        j = i // 2
        g_mix = norm_mix[i].reshape(1, d)
        if i % 2 == 0:
            lambda_init = 0.8 - 0.6 * math.exp(-0.3 * i)
            w = w_in_even[j]
            wm = jnp.concatenate([w[:, :gdn_w], w[:, gdn_w + 2 * nh:]], axis=1).astype(BF16)
            wg = _pad_lanes(w[:, gdn_w:gdn_w + 2 * nh]).astype(BF16)
            alog_row = _pad_lanes(jnp.concatenate([jnp.zeros((nh,), F32), a_log[j]]).reshape(1, 2 * nh))
            dt_row = _pad_lanes(jnp.concatenate([jnp.zeros((nh,), F32), dt_bias[j]]).reshape(1, 2 * nh))
            qkv, z, qkb, vbt, gates = _even_in(x2d, g_mix, wm, wg, conv_w[j], alog_row, dt_row, cos_t, sin_t, t)
            qkv, z, qkb, gates = (a.reshape(b, t, -1) for a in (qkv, z, qkb, gates))
            u, wy, qd, kd, qk = _gdn_prep(qkv, gates)
            o_a = _gdn_scan(u, wy, qd, kd, qk, gates, z, gdn_norm[j].reshape(1, HEAD_LANES))
            lam_params = jnp.stack([lam_q1[j], lam_k1[j], lam_q2[j], lam_k2[j]])
            o_b = _diff_attention(qkb, vbt, lam_params, diff_norm[j].reshape(1, HEAD_LANES), lambda_init)
            mixes = [o_a.reshape(m, -1), o_b.reshape(m, -1)]
            wout = w_out_even[j].astype(BF16)
        else:
            w = w_in_odd[j]
            d_mix = (w.shape[1] - FOX_HEADS) // 4
            wm = w[:, :4 * d_mix].astype(BF16)
            wf = _pad_lanes(w[:, 4 * d_mix:]).astype(BF16)
            bf_row = _pad_lanes(b_forget[j].reshape(1, FOX_HEADS))
            qt, k, vt, gate, qbt, kb = _odd_in(x2d, g_mix, wm, wf, bf_row, t)
            k, gate, kb = (a.reshape(b, t, -1) for a in (k, gate, kb))
            o = _fox_attention(qt, k, vt, gate, qbt, kb)
            mixes = [o.reshape(m, -1)]
            wout = w_out_odd[j].astype(BF16)
        x2d = _post(x2d, mixes, wout, norm_mlp[i].reshape(1, d), w_mlp_up[i].astype(BF16),
                    w_mlp_down[i].astype(BF16), p[i].reshape(m, -1), w_ple_proj[i].astype(BF16),
                    w_ple_gate[i].astype(BF16), norm_final.reshape(1, d) if i == depth - 1 else None)
    return x2d.reshape(b, t, d)
```

```python
import functools
import math

import jax
import jax.numpy as jnp
import numpy as np
from jax import lax
from jax.experimental import pallas as pl
from jax.experimental.pallas import tpu as pltpu

F32 = jnp.float32
BF16 = jnp.bfloat16

GDN_HEADS = 4
GDN_HEAD_DIM = 128
GDN_CHUNK = 64
CONV_WIDTH = 4
DIFF_HEADS = 4
DIFF_QK_DIM = 64
FOX_HEADS = 8
HEAD_LANES = 128
ROPE_THETA = 10000.0
EPS = 1e-6
NEG_INF = -1e30
LOG2E = 1.4426950408889634
LANES = 128
SUBLANES = 8
VMEM_LIMIT_BYTES = 56 * 1024 * 1024

TOKEN_TILE = 1024
EVEN_TOKEN_TILE = 512
PROJ_SEG = 512
FF_SEG = 1024
GDN_PREP_SUBTILE = 256
GDN_PREP_TILE = 512
GDN_SCAN_TILE = 256
GDN_SCAN_BATCH = 4
ATTN_TILE = 512
BIAS_LANES_PER_HEAD = 16
ONES_ROWS = 16


def _dot(a, b):
    return jnp.dot(a, b, preferred_element_type=F32)


def _split_bf16(x, parts):
    out = []
    for _ in range(parts):
        piece = x.astype(BF16)
        out.append(piece)
        x = x - piece.astype(F32)
    return out


def _dot_nt(a, b):
    return lax.dot_general(a, b, (((1,), (1,)), ((), ())), preferred_element_type=F32)


def _dot_tn(a, b):
    return lax.dot_general(a, b, (((0,), (0,)), ((), ())), preferred_element_type=F32)


def _rms(x, g):
    return x * lax.rsqrt(jnp.mean(x * x, axis=-1, keepdims=True) + EPS) * g


def _sigmoid(x):
    return 1.0 / (1.0 + jnp.exp(-x))


def _softplus(x):
    return jnp.maximum(x, 0.0) + jnp.log1p(jnp.exp(-jnp.abs(x)))


def _row_scan(x, period):
    rows = lax.broadcasted_iota(jnp.int32, x.shape, 0) % period
    s = 1
    while s < period:
        x = x + jnp.where(rows >= s, pltpu.roll(x, s, 0), 0.0)
        s *= 2
    return x


def _const_spec(shape):
    return pl.BlockSpec(shape, lambda *_: (0,) * len(shape))


def _params(sem):
    return pltpu.CompilerParams(dimension_semantics=sem, vmem_limit_bytes=VMEM_LIMIT_BYTES)


def _even_in_kernel(x_ref, g_ref, wm_ref, wg_ref, conv_ref, alog_ref, dt_ref, cos_ref, sin_ref,
                    qkv_ref, z_ref, qkb_ref, vbt_ref, gates_ref, h_ref, carry_ref, tr_ref, pad_ref, *, tiles_per_seq):
    tm = x_ref.shape[0]
    i = pl.program_id(0)
    h_ref[...] = _rms(x_ref[...], g_ref[...]).astype(BF16)
    seq_start = (i % tiles_per_seq) == 0
    seg = lambda s: slice(s * PROJ_SEG, (s + 1) * PROJ_SEG)
    project = lambda s: _dot(h_ref[...], wm_ref[:, seg(s)])

    def gdn_qkv(s, y):
        cols = seg(s)
        pad_ref[0:SUBLANES, :] = jnp.where(seq_start, 0.0, carry_ref[:, cols])
        pad_ref[SUBLANES:, :] = y
        carry_ref[:, cols] = y[tm - SUBLANES:, :]
        w = conv_ref[:, cols]
        a = y * w[CONV_WIDTH - 1:CONV_WIDTH, :]
        for k in range(1, CONV_WIDTH):
            a = a + pad_ref[SUBLANES - k:SUBLANES - k + tm, :] * w[CONV_WIDTH - 1 - k:CONV_WIDTH - k, :]
        a = a * _sigmoid(a)
        if s < 2:
            outs = []
            for hd in range(GDN_HEADS):
                blk = a[:, hd * HEAD_LANES:(hd + 1) * HEAD_LANES]
                n = blk * lax.rsqrt(jnp.sum(blk * blk, axis=-1, keepdims=True) + EPS)
                outs.append(n * (GDN_HEAD_DIM ** -0.5) if s == 0 else n)
            a = jnp.concatenate(outs, axis=1)
        qkv_ref[:, cols] = a

    def gdn_gate(s, y):
        z_ref[...] = y.astype(BF16)

    def diff_qk(s, y):
        cos = jnp.concatenate([cos_ref[...]] * (PROJ_SEG // LANES), axis=1)
        sin = jnp.concatenate([sin_ref[...]] * (PROJ_SEG // LANES), axis=1)
        lane = lax.broadcasted_iota(jnp.int32, (tm, PROJ_SEG), 1)
        first_half = (lane % DIFF_QK_DIM) < (DIFF_QK_DIM // 2)
        swapped = jnp.where(first_half, pltpu.roll(y, PROJ_SEG - DIFF_QK_DIM // 2, 1),
                            pltpu.roll(y, DIFF_QK_DIM // 2, 1))
        scale = DIFF_QK_DIM ** -0.5 * LOG2E if s == 4 else 1.0
        qkb_ref[:, seg(s - 4)] = ((y * cos + swapped * sin) * scale).astype(BF16)

    def diff_v(s, y):
        tr_ref[...] = y
        vbt_ref[0] = tr_ref[...].T.astype(BF16)

    stages = ((0, gdn_qkv), (3, gdn_gate), (1, gdn_qkv), (6, diff_v), (2, gdn_qkv), (4, diff_qk), (5, diff_qk))
    pending = project(stages[0][0])
    for n, (s, epilogue) in enumerate(stages):
        upcoming = project(stages[n + 1][0]) if n + 1 < len(stages) else _dot(h_ref[...], wg_ref[...])
        epilogue(s, pending)
        pending = upcoming

    graw = pending
    beta = _sigmoid(graw)
    g = -jnp.exp(alog_ref[...]) * _softplus(graw + dt_ref[...])
    gc = _row_scan(g, GDN_CHUNK)
    lane_g = lax.broadcasted_iota(jnp.int32, (tm, LANES), 1)
    gates_ref[...] = jnp.where(lane_g < GDN_HEADS, beta, gc)


def _even_in(x2d, g, wm, wg, conv_w, alog_row, dt_row, cos_t, sin_t, seq_len):
    m, d = x2d.shape
    tm = EVEN_TOKEN_TILE
    n_main = wm.shape[1]
    tps = seq_len // tm
    kern = functools.partial(_even_in_kernel, tiles_per_seq=tps)
    row = lambda i: (i, 0)
    return pl.pallas_call(
        kern,
        grid=(m // tm,),
        in_specs=[
            pl.BlockSpec((tm, d), row),
            _const_spec((1, d)),
            _const_spec((d, n_main)),
            _const_spec((d, LANES)),
            _const_spec(conv_w.shape),
            _const_spec((1, LANES)),
            _const_spec((1, LANES)),
            pl.BlockSpec((tm, LANES), row),
            pl.BlockSpec((tm, LANES), row),
        ],
        out_specs=[
            pl.BlockSpec((tm, 3 * PROJ_SEG), row),
            pl.BlockSpec((tm, PROJ_SEG), row),
            pl.BlockSpec((tm, 2 * PROJ_SEG), row),
            pl.BlockSpec((1, PROJ_SEG, tm), lambda i: (i // tps, 0, i % tps)),
            pl.BlockSpec((tm, LANES), row),
        ],
        out_shape=[
            jax.ShapeDtypeStruct((m, 3 * PROJ_SEG), F32),
            jax.ShapeDtypeStruct((m, PROJ_SEG), BF16),
            jax.ShapeDtypeStruct((m, 2 * PROJ_SEG), BF16),
            jax.ShapeDtypeStruct((m // seq_len, PROJ_SEG, seq_len), BF16),
            jax.ShapeDtypeStruct((m, LANES), F32),
        ],
        scratch_shapes=[pltpu.VMEM((tm, d), BF16), pltpu.VMEM((SUBLANES, 3 * PROJ_SEG), F32), pltpu.VMEM((tm, PROJ_SEG), F32),
                        pltpu.VMEM((tm + SUBLANES, PROJ_SEG), F32)],
        compiler_params=_params(("arbitrary",)),
        name="even_in_proj",
    )(x2d, g, wm, wg, conv_w, alog_row, dt_row, cos_t, sin_t)


def _odd_in_kernel(x_ref, g_ref, wm_ref, wf_ref, bf_ref, sel_ref, ones_ref, qt_ref, k_ref, vt_ref, gate_ref, qbt_ref, kb_ref,
                   h_ref, carry_ref, tr_ref, *, tiles_per_seq, d_mix):
    tm = x_ref.shape[0]
    i = pl.program_id(0)
    h_ref[...] = _rms(x_ref[...], g_ref[...]).astype(BF16)
    f = _dot(h_ref[...], wf_ref[...]) + bf_ref[...]
    log_f = jnp.minimum(f, 0.0) - jnp.log1p(jnp.exp(-jnp.abs(f)))
    prev = jnp.where((i % tiles_per_seq) == 0, 0.0, carry_ref[0:1, :])
    cum = _row_scan(log_f, tm) + prev
    carry_ref[...] = jnp.broadcast_to(cum[tm - 1:tm, :], carry_ref.shape)
    pieces = jnp.concatenate(_split_bf16(LOG2E * cum, 3), axis=1)

    head_dim = d_mix // FOX_HEADS
    for o_ref, base, scale in ((qt_ref, 0, head_dim ** -0.5 * LOG2E), (k_ref, d_mix, 1.0),
                               (vt_ref, 2 * d_mix, 1.0), (gate_ref, 3 * d_mix, 1.0)):
        for s in range(d_mix // PROJ_SEG):
            cols = slice(s * PROJ_SEG, (s + 1) * PROJ_SEG)
            y = _dot(h_ref[...], wm_ref[:, base + s * PROJ_SEG:base + (s + 1) * PROJ_SEG])
            if o_ref is qt_ref or o_ref is vt_ref:
                tr_ref[...] = y * scale
                o_ref[0, cols, :] = tr_ref[...].T.astype(BF16)
            else:
                o_ref[:, cols] = y.astype(BF16)

    lanes = _dot(pieces, sel_ref[...]) + ones_ref[...]
    tr_ref[:, :LANES] = lanes[:, :LANES]
    qbt_ref[0] = tr_ref[:, :LANES].T.astype(BF16)
    kb_ref[...] = lanes[:, LANES:].astype(BF16)


def _bias_lane_tables():
    sel = np.zeros((3 * LANES, 2 * LANES), np.float32)
    ones = np.zeros((1, 2 * LANES), np.float32)
    for h in range(FOX_HEADS):
        base = BIAS_LANES_PER_HEAD * h
        for piece in range(3):
            sel[LANES * piece + h, base + 3 + piece] = 1.0
            sel[LANES * piece + h, LANES + base + piece] = -1.0
            ones[0, base + piece] = 1.0
            ones[0, LANES + base + 3 + piece] = 1.0
    return jnp.asarray(sel, BF16), jnp.asarray(ones, F32)


def _odd_in(x2d, g, wm, wf, bf_row, seq_len):
    m, d = x2d.shape
    sel, ones_row = _bias_lane_tables()
    tm = TOKEN_TILE
    d_mix = wm.shape[1] // 4
    tps = seq_len // tm
    kern = functools.partial(_odd_in_kernel, tiles_per_seq=tps, d_mix=d_mix)
    row = lambda i: (i, 0)
    row_blk = pl.BlockSpec((tm, d_mix), row)
    row_shape = jax.ShapeDtypeStruct((m, d_mix), BF16)
    col_blk = lambda width: pl.BlockSpec((1, width, tm), lambda i: (i // tps, 0, i % tps))
    col_shape = lambda width: jax.ShapeDtypeStruct((m // seq_len, width, seq_len), BF16)
    return pl.pallas_call(
        kern,
        grid=(m // tm,),
        in_specs=[
            pl.BlockSpec((tm, d), row),
            _const_spec((1, d)),
            _const_spec(wm.shape),
            _const_spec((d, LANES)),
            _const_spec((1, LANES)),
            _const_spec(sel.shape),
            _const_spec(ones_row.shape),
        ],
        out_specs=[col_blk(d_mix), row_blk, col_blk(d_mix), row_blk, col_blk(LANES), pl.BlockSpec((tm, LANES), row)],
        out_shape=[col_shape(d_mix), row_shape, col_shape(d_mix), row_shape, col_shape(LANES),
                   jax.ShapeDtypeStruct((m, LANES), BF16)],
        scratch_shapes=[pltpu.VMEM((tm, d), BF16), pltpu.VMEM((SUBLANES, LANES), F32), pltpu.VMEM((tm, PROJ_SEG), F32)],
        compiler_params=_params(("arbitrary",)),
        name="odd_in_proj",
    )(x2d, g, wm, wf, bf_row, sel, ones_row)


def _attn_kernel(*refs, tq, fox, lambda_init):
    if fox:
        q_ref, k_ref, vt_ref, gate_ref, qbt_ref, kball_ref, o_ref, st_ref, mx_ref, m_ref, acc_ref, kb_ref = refs
    else:
        q_ref, k_ref, vt_ref, lam_ref, nw_ref, o_ref, st_ref, mx_ref, m_ref, acc_ref = refs
    tk = tq
    hg = pl.program_id(1)
    n_tiles = k_ref.shape[1] // tq
    head = lambda g: slice(g * HEAD_LANES, (g + 1) * HEAD_LANES)
    kv = [head(0), head(1)] if fox else [head(0), head(0)]
    tile_rows = lambda tile: pl.ds(pl.multiple_of(tile * tq, tq), tq)

    def queries(g, tile):
        if fox:
            return jnp.concatenate([q_ref[0, head(g), tile_rows(tile)], qbt_ref[0, :, tile_rows(tile)]], axis=0)
        q = q_ref[0, tile_rows(tile), :]
        lane = lax.broadcasted_iota(jnp.int32, q.shape, 1)
        keep = (lane < DIFF_QK_DIM) if g == 0 else (lane >= DIFF_QK_DIM)
        return jnp.where(keep, q, jnp.zeros_like(q))

    def scores_of(qmat, g, j):
        k0 = pl.multiple_of(j * tk, tk)
        kj = k_ref[0, pl.ds(k0, tk), kv[g]]
        if fox:
            return _dot(jnp.concatenate([kj, kb_ref[pl.ds(k0, tk), kv[g]]], axis=1), qmat)
        return _dot_nt(kj, qmat)

    if fox:
        kb_all = kball_ref[0]
        owner = lax.broadcasted_iota(jnp.int32, kb_all.shape, 1) // BIAS_LANES_PER_HEAD
        for g in range(2):
            kb_ref[:, head(g)] = jnp.where(owner == hg * 2 + g, kb_all, jnp.zeros_like(kb_all))
    else:
        lam_p = lam_ref[...]
        lam = (jnp.exp(jnp.sum(lam_p[0:1] * lam_p[1:2], axis=1, keepdims=True))
               - jnp.exp(jnp.sum(lam_p[2:3] * lam_p[3:4], axis=1, keepdims=True)) + lambda_init)
    ones = jnp.ones((ONES_ROWS, tk), BF16)

    def lookahead(g, qmat, tile, j, diagonal, slot):
        st = scores_of(qmat, g, j)
        if diagonal:
            kpos = j * tk + lax.broadcasted_iota(jnp.int32, (tk, tq), 0)
            qpos = tile * tq + lax.broadcasted_iota(jnp.int32, (tk, tq), 1)
            st = jnp.where(qpos >= kpos, st, NEG_INF)
        st_ref[g, slot] = st
        mx_ref[g, slot] = jnp.max(st, axis=0, keepdims=True)

    def absorb(g, j, slot):
        k0 = pl.multiple_of(j * tk, tk)
        vt = jnp.concatenate([vt_ref[0, kv[g], pl.ds(k0, tk)], ones], axis=0)
        m = m_ref[g]
        m_new = jnp.maximum(m, mx_ref[g, slot])
        p = jnp.exp2(st_ref[g, slot] - m_new).astype(BF16)
        acc_ref[g] = jnp.exp2(m - m_new) * acc_ref[g] + _dot(vt, p)
        m_ref[g] = m_new

    for g in range(2):
        lookahead(g, queries(g, 0), 0, 0, True, 0)

    def tile(qi, carry):
        qs = [queries(g, qi) for g in range(2)]

        def pair(jj, c):
            for u in range(2):
                for g in range(2):
                    lookahead(g, qs[g], qi, 2 * jj + u + 1, False, 1 - u)
                    absorb(g, 2 * jj + u, u)
            return c

        def even_step(j, c):
            for g in range(2):
                lookahead(g, qs[g], qi, j + 1, False, 1)
                absorb(g, j, 0)
            return c

        m_ref[...] = jnp.full(m_ref.shape, NEG_INF, F32)
        acc_ref[...] = jnp.zeros(acc_ref.shape, F32)
        n_plain = jnp.maximum(qi - 1, 0)
        n_pairs = n_plain // 2
        lax.fori_loop(0, n_pairs, pair, 0)
        lax.fori_loop(2 * n_pairs, n_plain, even_step, 0)
        nxt = jnp.minimum(qi + 1, n_tiles - 1)

        def finish(diag_slot):
            def last_step(j, c):
                for g in range(2):
                    lookahead(g, qs[g], qi, j + 1, True, diag_slot)
                    absorb(g, j, 1 - diag_slot)
                return c

            lax.fori_loop(n_plain, qi, last_step, 0)
            for g in range(2):
                if diag_slot == 0:
                    absorb(g, qi, 0)
                    lookahead(g, queries(g, nxt), nxt, 0, False, 0)
                else:
                    lookahead(g, queries(g, nxt), nxt, 0, False, 0)
                    absorb(g, qi, 1)
            outs = [acc_ref[g, :HEAD_LANES, :] / acc_ref[g, HEAD_LANES:HEAD_LANES + 1, :] for g in range(2)]
            rows = tile_rows(qi)
            if fox:
                for g in range(2):
                    gate = _sigmoid(gate_ref[0, rows, head(g)].astype(F32))
                    o_ref[0, rows, head(g)] = (outs[g].T * gate).astype(o_ref.dtype)
            else:
                o = (outs[0] - lam * outs[1]).T
                o_ref[0, rows, :] = (_rms(o, nw_ref[...]) * (1.0 - lambda_init)).astype(o_ref.dtype)

        for parity in range(2):
            pl.when(qi % 2 == parity)(functools.partial(finish, parity))
        return carry

    lax.fori_loop(0, n_tiles, tile, 0)


def _attn_state(tq):
    return [pltpu.VMEM((2, 2, tq, tq), F32), pltpu.VMEM((2, 2, 1, tq), F32), pltpu.VMEM((2, 1, tq), F32),
            pltpu.VMEM((2, HEAD_LANES + ONES_ROWS, tq), F32)]


def _fox_attention(qt, k, vt, gate, qbt, kb, *, tq=ATTN_TILE):
    b, t, dm = k.shape
    width = 2 * HEAD_LANES
    kern = functools.partial(_attn_kernel, tq=tq, fox=True, lambda_init=0.0)
    seq = pl.BlockSpec((1, t, width), lambda bi, h: (bi, 0, h))
    seq_t = pl.BlockSpec((1, width, t), lambda bi, h: (bi, h, 0))
    seq_bias = pl.BlockSpec((1, t, LANES), lambda bi, h: (bi, 0, 0))
    return pl.pallas_call(
        kern,
        grid=(b, dm // width),
        in_specs=[seq_t, seq, seq_t, seq, pl.BlockSpec((1, LANES, t), lambda bi, h: (bi, 0, 0)), seq_bias],
        out_specs=seq,
        out_shape=jax.ShapeDtypeStruct((b, t, dm), BF16),
        scratch_shapes=_attn_state(tq) + [pltpu.VMEM((t, width), BF16)],
        compiler_params=_params(("arbitrary", "arbitrary")),
        name="fox_attention",
    )(qt, k, vt, gate, qbt, kb)


def _diff_attention(qk, vt, lam_params, norm_w, lambda_init, *, tq=ATTN_TILE):
    b, t, _ = qk.shape
    nh = DIFF_HEADS
    kern = functools.partial(_attn_kernel, tq=tq, fox=False, lambda_init=lambda_init)
    seq = pl.BlockSpec((1, t, HEAD_LANES), lambda bi, h: (bi, 0, h))
    return pl.pallas_call(
        kern,
        grid=(b, nh),
        in_specs=[seq,
                  pl.BlockSpec((1, t, HEAD_LANES), lambda bi, h: (bi, 0, nh + h)),
                  pl.BlockSpec((1, HEAD_LANES, t), lambda bi, h: (bi, h, 0)),
                  _const_spec(lam_params.shape), _const_spec((1, HEAD_LANES))],
        out_specs=seq,
        out_shape=jax.ShapeDtypeStruct((b, t, nh * HEAD_LANES), BF16),
        scratch_shapes=_attn_state(tq),
        compiler_params=_params(("arbitrary", "arbitrary")),
        name="diff_attention",
    )(qk, qk, vt, lam_params, norm_w)


def _gdn_prep_kernel(q_ref, k_ref, v_ref, gates_ref, u_ref, w_ref, qd_ref, kd_ref, qk_ref):
    c = GDN_CHUNK
    sub = GDN_PREP_SUBTILE
    units = [(slice(t0, t0 + sub), hd) for t0 in range(0, q_ref.shape[1], sub) for hd in range(GDN_HEADS)]
    ids = range(len(units))
    col = lambda hd: slice(hd * HEAD_LANES, (hd + 1) * HEAD_LANES)
    lane = lax.broadcasted_iota(jnp.int32, (sub, LANES), 1)
    ri = lax.broadcasted_iota(jnp.int32, (sub, sub), 0)
    ci = lax.broadcasted_iota(jnp.int32, (sub, sub), 1)
    chunk_start = ri - ri % c
    incl = lambda a: jnp.where(ci <= ri, jnp.where(ci >= chunk_start, a, 0.0), 0.0)
    strict = lambda a: jnp.where(ci < ri, jnp.where(ci >= chunk_start, a, 0.0), 0.0)
    ident = jnp.where(ri == ci, 1.0, 0.0)
    ones = jnp.ones((sub, LANES), BF16)
    kt = [k_ref[0, rows, col(hd)] for rows, hd in units]
    beta = [gates_ref[0, rows, hd:hd + 1] for rows, hd in units]
    gcc = [gates_ref[0, rows, GDN_HEADS + hd:GDN_HEADS + hd + 1] for rows, hd in units]
    k16 = [kt[i].astype(BF16) for i in ids]
    kb = [kt[i] * beta[i] for i in ids]

    qk_raw = [_dot_nt(q_ref[0, rows, col(hd)].astype(BF16), k16[i]) for i, (rows, hd) in enumerate(units)]
    kk = [_dot_nt(kb[i].astype(BF16), k16[i]) for i in ids]
    gc_row = []
    for i in ids:
        g_hi, g_mid, g_lo = (piece.astype(F32) for piece in _split_bf16(gcc[i], 3))
        pieces = jnp.where(lane == 0, g_hi, jnp.where(lane == 1, g_mid, jnp.where(lane == 2, g_lo, 0.0)))
        gc_row.append(_dot_nt(ones, pieces.astype(BF16)))
    decay = [incl(jnp.exp(incl(gcc[i] - gc_row[i]))) for i in ids]
    lower = [strict(kk[i] * decay[i]) for i in ids]

    span = ri ^ ci
    inv = [ident - jnp.where(span == 1, lower[i], 0.0) for i in ids]
    s_blk = 2
    while s_blk < c:
        shift = int(math.log2(s_blk))
        inv16 = [inv[i].astype(BF16) for i in ids]
        coupled = [_dot(jnp.where((span >> shift) == 1, lower[i], 0.0).astype(BF16), inv16[i]) for i in ids]
        inv = [inv[i] - _dot(inv16[i], coupled[i].astype(BF16)) for i in ids]
        s_blk *= 2
    inv16 = [inv[i].astype(BF16) for i in ids]

    eg = [jnp.exp(gcc[i]) for i in ids]
    rhs = [jnp.concatenate([v_ref[0, rows, col(hd)] * beta[i], kb[i] * eg[i]], axis=1)
           for i, (rows, hd) in enumerate(units)]
    sol = [_dot(inv16[i], rhs[i].astype(BF16)) for i in ids]
    a_hi, a_lo, s_hi, s_lo = [], [], [], []
    for i in ids:
        hi, lo = _split_bf16(ident + lower[i], 2)
        a_hi.append(hi)
        a_lo.append(lo)
        hi, lo = _split_bf16(sol[i], 2)
        s_hi.append(hi)
        s_lo.append(lo)
    prod = [_dot(a_hi[i], s_hi[i]) + (_dot(a_hi[i], s_lo[i]) + _dot(a_lo[i], s_hi[i])) for i in ids]
    corr = [_dot(inv16[i], (rhs[i] - prod[i]).astype(BF16)) for i in ids]
    sol = [sol[i] + corr[i] for i in ids]
    for i, (rows, hd) in enumerate(units):
        u_ref[0, rows, col(hd)] = sol[i][:, :HEAD_LANES]
        w_ref[0, rows, col(hd)] = sol[i][:, HEAD_LANES:].astype(BF16)
        qk = incl(qk_raw[i] * decay[i])
        qd_ref[0, rows, col(hd)] = (q_ref[0, rows, col(hd)] * eg[i]).astype(BF16)
        for n in range(sub // c):
            blk = slice(n * c, (n + 1) * c)
            out_rows = slice(rows.start + n * c, rows.start + (n + 1) * c)
            qk_ref[0, hd, out_rows, :] = qk[blk, blk].astype(BF16)
            gl = gcc[i][(n + 1) * c - 1:(n + 1) * c, :]
            kd_ref[0, out_rows, col(hd)] = (kt[i][blk] * jnp.exp(gl - gcc[i][blk])).astype(BF16)


def _gdn_prep(qkv, gates, *, tt=GDN_PREP_TILE):
    b, t, _ = qkv.shape
    nh = GDN_HEADS
    dm = nh * HEAD_LANES
    blk = lambda part: pl.BlockSpec((1, tt, dm), lambda bi, i: (bi, i, part))
    return pl.pallas_call(
        _gdn_prep_kernel,
        grid=(b, t // tt),
        in_specs=[blk(0), blk(1), blk(2), pl.BlockSpec((1, tt, LANES), lambda bi, i: (bi, i, 0))],
        out_specs=[blk(0)] * 4 + [pl.BlockSpec((1, nh, tt, GDN_CHUNK), lambda bi, i: (bi, 0, i, 0))],
        out_shape=[jax.ShapeDtypeStruct((b, t, dm), F32)]
        + [jax.ShapeDtypeStruct((b, t, dm), BF16)] * 3
        + [jax.ShapeDtypeStruct((b, nh, t, GDN_CHUNK), BF16)],
        compiler_params=_params(("arbitrary", "arbitrary")),
        name="gdn_prep",
    )(qkv, qkv, qkv, gates)


def _gdn_scan_kernel(u_ref, w_ref, qd_ref, kd_ref, qk_ref, gates_ref, z_ref, nw_ref, o_ref, s_ref):
    c = GDN_CHUNK
    nb, tt = u_ref.shape[0], u_ref.shape[1]

    @pl.when(pl.program_id(1) == 0)
    def _():
        s_ref[...] = jnp.zeros_like(s_ref)

    chains = [(bi, hd) for bi in range(nb) for hd in range(GDN_HEADS)]
    cols = [slice(hd * HEAD_LANES, (hd + 1) * HEAD_LANES) for hd in range(GDN_HEADS)]
    state = [s_ref[bi, hd] for bi, hd in chains]
    for n in range(tt // c):
        rows = slice(n * c, (n + 1) * c)
        r = [_dot(jnp.concatenate([w_ref[bi, rows, cols[hd]], qd_ref[bi, rows, cols[hd]]], axis=0),
                  state[i].astype(BF16)) for i, (bi, hd) in enumerate(chains)]
        v_new = [(u_ref[bi, rows, cols[hd]] - r[i][:c]).astype(BF16) for i, (bi, hd) in enumerate(chains)]
        intra = [_dot(qk_ref[bi, hd, rows, :], v_new[i]) for i, (bi, hd) in enumerate(chains)]
        upd = [_dot_tn(kd_ref[bi, rows, cols[hd]], v_new[i]) for i, (bi, hd) in enumerate(chains)]
        for i, (bi, hd) in enumerate(chains):
            last = (n + 1) * c - 1
            decay_last = jnp.exp(gates_ref[bi, last:last + 1, GDN_HEADS + hd:GDN_HEADS + hd + 1])
            state[i] = state[i] * decay_last + upd[i]
            zt = z_ref[bi, rows, cols[hd]].astype(F32)
            o = r[i][c:] + intra[i]
            o_ref[bi, rows, cols[hd]] = (_rms(o, nw_ref[...]) * (zt * _sigmoid(zt))).astype(o_ref.dtype)
    for i, (bi, hd) in enumerate(chains):
        s_ref[bi, hd] = state[i]


def _gdn_scan(u, w, qd, kd, qk, gates, z, norm_w, *, tt=GDN_SCAN_TILE, nb=GDN_SCAN_BATCH):
    b, t, dm = u.shape
    nh = GDN_HEADS
    assert b % nb == 0 and t % tt == 0
    blk = pl.BlockSpec((nb, tt, dm), lambda bi, i: (bi, i, 0))
    return pl.pallas_call(
        _gdn_scan_kernel,
        grid=(b // nb, t // tt),
        in_specs=[blk, blk, blk, blk,
                  pl.BlockSpec((nb, nh, tt, GDN_CHUNK), lambda bi, i: (bi, 0, i, 0)),
                  pl.BlockSpec((nb, tt, LANES), lambda bi, i: (bi, i, 0)),
                  blk, _const_spec((1, HEAD_LANES))],
        out_specs=blk,
        out_shape=jax.ShapeDtypeStruct((b, t, dm), BF16),
        scratch_shapes=[pltpu.VMEM((nb, nh, GDN_HEAD_DIM, GDN_HEAD_DIM), F32)],
        compiler_params=_params(("arbitrary", "arbitrary")),
        name="gdn_scan",
    )(u, w, qd, kd, qk, gates, z, norm_w)


def _post_kernel(*refs, n_mix, final_norm):
    x_ref = refs[0]
    mix_refs = refs[1:1 + n_mix]
    wout_ref, g_ref, wup_ref, wdn_ref, p_ref, wpp_ref, wpg_ref = refs[1 + n_mix:8 + n_mix]
    rest = refs[8 + n_mix:]
    if final_norm:
        gf_ref, o_ref = rest
    else:
        (o_ref,) = rest
    mix = mix_refs[0][...] if n_mix == 1 else jnp.concatenate([r[...] for r in mix_refs], axis=1)
    x = x_ref[...] + _dot(mix, wout_ref[...])
    h = _rms(x, g_ref[...]).astype(BF16)
    d_ff = wup_ref.shape[1]
    acc = x
    for s in range(d_ff // FF_SEG):
        a = jnp.maximum(_dot(h, wup_ref[:, s * FF_SEG:(s + 1) * FF_SEG]), 0.0)
        acc = acc + _dot((a * a).astype(BF16), wdn_ref[s * FF_SEG:(s + 1) * FF_SEG, :])
    x = acc
    gate = _sigmoid(_dot(x.astype(BF16), wpg_ref[...]))
    x = x + _dot(p_ref[...].astype(BF16), wpp_ref[...]) * gate
    if final_norm:
        x = _rms(x, gf_ref[...])
    o_ref[...] = x


def _post(x2d, mixes, wout, g, wup, wdn, p2d, wpp, wpg, gf=None):
    m, d = x2d.shape
    tm = TOKEN_TILE
    row = lambda i: (i, 0)
    single = pl.Buffered(1)
    const = lambda a: pl.BlockSpec(a.shape, lambda i: (0, 0), pipeline_mode=single)
    args = [x2d, *mixes, wout, g, wup, wdn, p2d, wpp, wpg]
    in_specs = ([pl.BlockSpec((tm, d), row)]
                + [pl.BlockSpec((tm, a.shape[1]), row) for a in mixes]
                + [const(wout), const(g), const(wup), const(wdn), pl.BlockSpec((tm, p2d.shape[1]), row), const(wpp), const(wpg)])
    if gf is not None:
        args.append(gf)
        in_specs.append(const(gf))
    kern = functools.partial(_post_kernel, n_mix=len(mixes), final_norm=gf is not None)
    return pl.pallas_call(
        kern,
        grid=(m // tm,),
        in_specs=in_specs,
        out_specs=pl.BlockSpec((tm, d), row),
        out_shape=jax.ShapeDtypeStruct((m, d), F32),
        compiler_params=_params(("arbitrary",)),
        name="out_proj_mlp_ple",
    )(*args)


def _pad_lanes(a):
    return jnp.pad(a, ((0, 0), (0, LANES - a.shape[1])))


def kernel(x, p, positions, norm_mix, norm_mlp, norm_final, w_in_even, conv_w, a_log, dt_bias, gdn_norm,
           lam_q1, lam_k1, lam_q2, lam_k2, diff_norm, w_out_even, w_in_odd, b_forget, w_out_odd,
           w_mlp_up, w_mlp_down, w_ple_proj, w_ple_gate):
    b, t, d = x.shape
    depth = p.shape[0]
    m = b * t
    assert t % TOKEN_TILE == 0 and d % PROJ_SEG == 0
    nh = GDN_HEADS
    gdn_w = 3 * nh * GDN_HEAD_DIM + nh * GDN_HEAD_DIM
    assert w_in_even.shape[2] == gdn_w + 2 * nh + 3 * DIFF_HEADS * 2 * DIFF_QK_DIM

    inv_freq = ROPE_THETA ** (-jnp.arange(0, DIFF_QK_DIM, 2, dtype=F32) / DIFF_QK_DIM)
    ang = positions.astype(F32)[..., None] * inv_freq
    cos, sin = jnp.cos(ang), jnp.sin(ang)
    cos_t = jnp.concatenate([cos, cos, cos, cos], axis=-1).reshape(m, LANES)
    sin_t = jnp.concatenate([-sin, sin, -sin, sin], axis=-1).reshape(m, LANES)

    x2d = x.reshape(m, d)
    for i in range(depth):
        j = i // 2
        g_mix = norm_mix[i].reshape(1, d)
        if i % 2 == 0:
            lambda_init = 0.8 - 0.6 * math.exp(-0.3 * i)
            w = w_in_even[j]
            wm = jnp.concatenate([w[:, :gdn_w], w[:, gdn_w + 2 * nh:]], axis=1).astype(BF16)
            wg = _pad_lanes(w[:, gdn_w:gdn_w + 2 * nh]).astype(BF16)
            alog_row = _pad_lanes(jnp.concatenate([jnp.zeros((nh,), F32), a_log[j]]).reshape(1, 2 * nh))
            dt_row = _pad_lanes(jnp.concatenate([jnp.zeros((nh,), F32), dt_bias[j]]).reshape(1, 2 * nh))
            qkv, z, qkb, vbt, gates = _even_in(x2d, g_mix, wm, wg, conv_w[j], alog_row, dt_row, cos_t, sin_t, t)
            qkv, z, qkb, gates = (a.reshape(b, t, -1) for a in (qkv, z, qkb, gates))
            u, wy, qd, kd, qk = _gdn_prep(qkv, gates)
            o_a = _gdn_scan(u, wy, qd, kd, qk, gates, z, gdn_norm[j].reshape(1, HEAD_LANES))
            lam_params = jnp.stack([lam_q1[j], lam_k1[j], lam_q2[j], lam_k2[j]])
            o_b = _diff_attention(qkb, vbt, lam_params, diff_norm[j].reshape(1, HEAD_LANES), lambda_init)
            mixes = [o_a.reshape(m, -1), o_b.reshape(m, -1)]
            wout = w_out_even[j].astype(BF16)
        else:
            w = w_in_odd[j]
            d_mix = (w.shape[1] - FOX_HEADS) // 4
            wm = w[:, :4 * d_mix].astype(BF16)
            wf = _pad_lanes(w[:, 4 * d_mix:]).astype(BF16)
            bf_row = _pad_lanes(b_forget[j].reshape(1, FOX_HEADS))
            qt, k, vt, gate, qbt, kb = _odd_in(x2d, g_mix, wm, wf, bf_row, t)
            k, gate, kb = (a.reshape(b, t, -1) for a in (k, gate, kb))
            o = _fox_attention(qt, k, vt, gate, qbt, kb)
            mixes = [o.reshape(m, -1)]
            wout = w_out_odd[j].astype(BF16)
        x2d = _post(x2d, mixes, wout, norm_mlp[i].reshape(1, d), w_mlp_up[i].astype(BF16),
                    w_mlp_down[i].astype(BF16), p[i].reshape(m, -1), w_ple_proj[i].astype(BF16),
                    w_ple_gate[i].astype(BF16), norm_final.reshape(1, d) if i == depth - 1 else None)
    return x2d.reshape(b, t, d)
```

```python
import functools
import math

import jax
import jax.numpy as jnp
import numpy as np
from jax import lax
from jax.experimental import pallas as pl
from jax.experimental.pallas import tpu as pltpu

F32 = jnp.float32
BF16 = jnp.bfloat16

GDN_HEADS = 4
GDN_HEAD_DIM = 128
GDN_CHUNK = 64
CONV_WIDTH = 4
DIFF_HEADS = 4
DIFF_QK_DIM = 64
FOX_HEADS = 8
HEAD_LANES = 128
ROPE_THETA = 10000.0
EPS = 1e-6
NEG_INF = -1e30
LOG2E = 1.4426950408889634
LANES = 128
SUBLANES = 8
VMEM_LIMIT_BYTES = 56 * 1024 * 1024

TOKEN_TILE = 1024
EVEN_TOKEN_TILE = 512
PROJ_SEG = 512
FF_SEG = 1024
GDN_PREP_SUBTILE = 256
GDN_PREP_TILE = 512
GDN_SCAN_TILE = 256
GDN_SCAN_BATCH = 4
ATTN_TILE = 512
BIAS_LANES_PER_HEAD = 16
ONES_ROWS = 16


def _dot(a, b):
    return jnp.dot(a, b, preferred_element_type=F32)


def _split_bf16(x, parts):
    out = []
    for _ in range(parts):
        piece = x.astype(BF16)
        out.append(piece)
        x = x - piece.astype(F32)
    return out


def _dot_nt(a, b):
    return lax.dot_general(a, b, (((1,), (1,)), ((), ())), preferred_element_type=F32)


def _dot_tn(a, b):
    return lax.dot_general(a, b, (((0,), (0,)), ((), ())), preferred_element_type=F32)


def _rms(x, g):
    return x * lax.rsqrt(jnp.mean(x * x, axis=-1, keepdims=True) + EPS) * g


def _sigmoid(x):
    return 1.0 / (1.0 + jnp.exp(-x))


def _softplus(x):
    return jnp.maximum(x, 0.0) + jnp.log1p(jnp.exp(-jnp.abs(x)))


def _row_scan(x, period):
    rows = lax.broadcasted_iota(jnp.int32, x.shape, 0) % period
    s = 1
    while s < period:
        x = x + jnp.where(rows >= s, pltpu.roll(x, s, 0), 0.0)
        s *= 2
    return x


def _const_spec(shape):
    return pl.BlockSpec(shape, lambda *_: (0,) * len(shape))


def _params(sem):
    return pltpu.CompilerParams(dimension_semantics=sem, vmem_limit_bytes=VMEM_LIMIT_BYTES)


def _even_in_kernel(x_ref, g_ref, wm_ref, wg_ref, conv_ref, alog_ref, dt_ref, cos_ref, sin_ref,
                    qkv_ref, z_ref, qkb_ref, vbt_ref, gates_ref, h_ref, carry_ref, tr_ref, pad_ref, *, tiles_per_seq):
    tm = x_ref.shape[0]
    i = pl.program_id(0)
    h_ref[...] = _rms(x_ref[...], g_ref[...]).astype(BF16)
    seq_start = (i % tiles_per_seq) == 0
    seg = lambda s: slice(s * PROJ_SEG, (s + 1) * PROJ_SEG)
    project = lambda s: _dot(h_ref[...], wm_ref[:, seg(s)])

    def gdn_qkv(s, y):
        cols = seg(s)
        pad_ref[0:SUBLANES, :] = jnp.where(seq_start, 0.0, carry_ref[:, cols])
        pad_ref[SUBLANES:, :] = y
        carry_ref[:, cols] = y[tm - SUBLANES:, :]
        w = conv_ref[:, cols]
        a = y * w[CONV_WIDTH - 1:CONV_WIDTH, :]
        for k in range(1, CONV_WIDTH):
            a = a + pad_ref[SUBLANES - k:SUBLANES - k + tm, :] * w[CONV_WIDTH - 1 - k:CONV_WIDTH - k, :]
        a = a * _sigmoid(a)
        if s < 2:
            outs = []
            for hd in range(GDN_HEADS):
                blk = a[:, hd * HEAD_LANES:(hd + 1) * HEAD_LANES]
                n = blk * lax.rsqrt(jnp.sum(blk * blk, axis=-1, keepdims=True) + EPS)
                outs.append(n * (GDN_HEAD_DIM ** -0.5) if s == 0 else n)
            a = jnp.concatenate(outs, axis=1)
        qkv_ref[:, cols] = a

    def gdn_gate(s, y):
        z_ref[...] = y.astype(BF16)

    def diff_qk(s, y):
        cos = jnp.concatenate([cos_ref[...]] * (PROJ_SEG // LANES), axis=1)
        sin = jnp.concatenate([sin_ref[...]] * (PROJ_SEG // LANES), axis=1)
        lane = lax.broadcasted_iota(jnp.int32, (tm, PROJ_SEG), 1)
        first_half = (lane % DIFF_QK_DIM) < (DIFF_QK_DIM // 2)
        swapped = jnp.where(first_half, pltpu.roll(y, PROJ_SEG - DIFF_QK_DIM // 2, 1),
                            pltpu.roll(y, DIFF_QK_DIM // 2, 1))
        scale = DIFF_QK_DIM ** -0.5 * LOG2E if s == 4 else 1.0
        qkb_ref[:, seg(s - 4)] = ((y * cos + swapped * sin) * scale).astype(BF16)

    def diff_v(s, y):
        tr_ref[...] = y
        vbt_ref[0] = tr_ref[...].T.astype(BF16)

    stages = ((0, gdn_qkv), (3, gdn_gate), (1, gdn_qkv), (6, diff_v), (2, gdn_qkv), (4, diff_qk), (5, diff_qk))
    pending = project(stages[0][0])
    for n, (s, epilogue) in enumerate(stages):
        upcoming = project(stages[n + 1][0]) if n + 1 < len(stages) else _dot(h_ref[...], wg_ref[...])
        epilogue(s, pending)
        pending = upcoming

    graw = pending
    beta = _sigmoid(graw)
    g = -jnp.exp(alog_ref[...]) * _softplus(graw + dt_ref[...])
    gc = _row_scan(g, GDN_CHUNK)
    lane_g = lax.broadcasted_iota(jnp.int32, (tm, LANES), 1)
    gates_ref[...] = jnp.where(lane_g < GDN_HEADS, beta, gc)


def _even_in(x2d, g, wm, wg, conv_w, alog_row, dt_row, cos_t, sin_t, seq_len):
    m, d = x2d.shape
    tm = EVEN_TOKEN_TILE
    n_main = wm.shape[1]
    tps = seq_len // tm
    kern = functools.partial(_even_in_kernel, tiles_per_seq=tps)
    row = lambda i: (i, 0)
    return pl.pallas_call(
        kern,
        grid=(m // tm,),
        in_specs=[
            pl.BlockSpec((tm, d), row),
            _const_spec((1, d)),
            _const_spec((d, n_main)),
            _const_spec((d, LANES)),
            _const_spec(conv_w.shape),
            _const_spec((1, LANES)),
            _const_spec((1, LANES)),
            pl.BlockSpec((tm, LANES), row),
            pl.BlockSpec((tm, LANES), row),
        ],
        out_specs=[
            pl.BlockSpec((tm, 3 * PROJ_SEG), row),
            pl.BlockSpec((tm, PROJ_SEG), row),
            pl.BlockSpec((tm, 2 * PROJ_SEG), row),
            pl.BlockSpec((1, PROJ_SEG, tm), lambda i: (i // tps, 0, i % tps)),
            pl.BlockSpec((tm, LANES), row),
        ],
        out_shape=[
            jax.ShapeDtypeStruct((m, 3 * PROJ_SEG), F32),
            jax.ShapeDtypeStruct((m, PROJ_SEG), BF16),
            jax.ShapeDtypeStruct((m, 2 * PROJ_SEG), BF16),
            jax.ShapeDtypeStruct((m // seq_len, PROJ_SEG, seq_len), BF16),
            jax.ShapeDtypeStruct((m, LANES), F32),
        ],
        scratch_shapes=[pltpu.VMEM((tm, d), BF16), pltpu.VMEM((SUBLANES, 3 * PROJ_SEG), F32), pltpu.VMEM((tm, PROJ_SEG), F32),
                        pltpu.VMEM((tm + SUBLANES, PROJ_SEG), F32)],
        compiler_params=_params(("arbitrary",)),
        name="even_in_proj",
    )(x2d, g, wm, wg, conv_w, alog_row, dt_row, cos_t, sin_t)


def _odd_in_kernel(x_ref, g_ref, wm_ref, wf_ref, bf_ref, sel_ref, ones_ref, qt_ref, k_ref, vt_ref, gate_ref, qbt_ref, kb_ref,
                   h_ref, carry_ref, tr_ref, *, tiles_per_seq, d_mix):
    tm = x_ref.shape[0]
    i = pl.program_id(0)
    h_ref[...] = _rms(x_ref[...], g_ref[...]).astype(BF16)
    f = _dot(h_ref[...], wf_ref[...]) + bf_ref[...]
    log_f = jnp.minimum(f, 0.0) - jnp.log1p(jnp.exp(-jnp.abs(f)))
    prev = jnp.where((i % tiles_per_seq) == 0, 0.0, carry_ref[0:1, :])
    cum = _row_scan(log_f, tm) + prev
    carry_ref[...] = jnp.broadcast_to(cum[tm - 1:tm, :], carry_ref.shape)
    pieces = jnp.concatenate(_split_bf16(LOG2E * cum, 3), axis=1)

    head_dim = d_mix // FOX_HEADS
    for o_ref, base, scale in ((qt_ref, 0, head_dim ** -0.5 * LOG2E), (k_ref, d_mix, 1.0),
                               (vt_ref, 2 * d_mix, 1.0), (gate_ref, 3 * d_mix, 1.0)):
        for s in range(d_mix // PROJ_SEG):
            cols = slice(s * PROJ_SEG, (s + 1) * PROJ_SEG)
            y = _dot(h_ref[...], wm_ref[:, base + s * PROJ_SEG:base + (s + 1) * PROJ_SEG])
            if o_ref is qt_ref or o_ref is vt_ref:
                tr_ref[...] = y * scale
                o_ref[0, cols, :] = tr_ref[...].T.astype(BF16)
            else:
                o_ref[:, cols] = y.astype(BF16)

    lanes = _dot(pieces, sel_ref[...]) + ones_ref[...]
    tr_ref[:, :LANES] = lanes[:, :LANES]
    qbt_ref[0] = tr_ref[:, :LANES].T.astype(BF16)
    kb_ref[...] = lanes[:, LANES:].astype(BF16)


def _bias_lane_tables():
    sel = np.zeros((3 * LANES, 2 * LANES), np.float32)
    ones = np.zeros((1, 2 * LANES), np.float32)
    for h in range(FOX_HEADS):
        base = BIAS_LANES_PER_HEAD * h
        for piece in range(3):
            sel[LANES * piece + h, base + 3 + piece] = 1.0
            sel[LANES * piece + h, LANES + base + piece] = -1.0
            ones[0, base + piece] = 1.0
            ones[0, LANES + base + 3 + piece] = 1.0
    return jnp.asarray(sel, BF16), jnp.asarray(ones, F32)


def _odd_in(x2d, g, wm, wf, bf_row, seq_len):
    m, d = x2d.shape
    sel, ones_row = _bias_lane_tables()
    tm = TOKEN_TILE
    d_mix = wm.shape[1] // 4
    tps = seq_len // tm
    kern = functools.partial(_odd_in_kernel, tiles_per_seq=tps, d_mix=d_mix)
    row = lambda i: (i, 0)
    row_blk = pl.BlockSpec((tm, d_mix), row)
    row_shape = jax.ShapeDtypeStruct((m, d_mix), BF16)
    col_blk = lambda width: pl.BlockSpec((1, width, tm), lambda i: (i // tps, 0, i % tps))
    col_shape = lambda width: jax.ShapeDtypeStruct((m // seq_len, width, seq_len), BF16)
    return pl.pallas_call(
        kern,
        grid=(m // tm,),
        in_specs=[
            pl.BlockSpec((tm, d), row),
            _const_spec((1, d)),
            _const_spec(wm.shape),
            _const_spec((d, LANES)),
            _const_spec((1, LANES)),
            _const_spec(sel.shape),
            _const_spec(ones_row.shape),
        ],
        out_specs=[col_blk(d_mix), row_blk, col_blk(d_mix), row_blk, col_blk(LANES), pl.BlockSpec((tm, LANES), row)],
        out_shape=[col_shape(d_mix), row_shape, col_shape(d_mix), row_shape, col_shape(LANES),
                   jax.ShapeDtypeStruct((m, LANES), BF16)],
        scratch_shapes=[pltpu.VMEM((tm, d), BF16), pltpu.VMEM((SUBLANES, LANES), F32), pltpu.VMEM((tm, PROJ_SEG), F32)],
        compiler_params=_params(("arbitrary",)),
        name="odd_in_proj",
    )(x2d, g, wm, wf, bf_row, sel, ones_row)


def _attn_kernel(*refs, tq, fox, lambda_init):
    if fox:
        q_ref, k_ref, vt_ref, gate_ref, qbt_ref, kball_ref, o_ref, st_ref, mx_ref, m_ref, acc_ref, kb_ref = refs
    else:
        q_ref, k_ref, vt_ref, lam_ref, nw_ref, o_ref, st_ref, mx_ref, m_ref, acc_ref = refs
    tk = tq
    half = tk // 2
    hg = pl.program_id(1)
    n_tiles = k_ref.shape[1] // tq
    head = lambda g: slice(g * HEAD_LANES, (g + 1) * HEAD_LANES)
    kv = [head(0), head(1)] if fox else [head(0), head(0)]
    tile_rows = lambda tile: pl.ds(pl.multiple_of(tile * tq, tq), tq)

    def queries(g, tile):
        if fox:
            return jnp.concatenate([q_ref[0, head(g), tile_rows(tile)], qbt_ref[0, :, tile_rows(tile)]], axis=0)
        q = q_ref[0, tile_rows(tile), :]
        lane = lax.broadcasted_iota(jnp.int32, q.shape, 1)
        keep = (lane < DIFF_QK_DIM) if g == 0 else (lane >= DIFF_QK_DIM)
        return jnp.where(keep, q, jnp.zeros_like(q))

    def scores_of(qmat, g, k0, rows):
        kj = k_ref[0, pl.ds(k0, rows), kv[g]]
        if fox:
            return _dot(jnp.concatenate([kj, kb_ref[pl.ds(k0, rows), kv[g]]], axis=1), qmat)
        return _dot_nt(kj, qmat)

    def causal(st):
        return jnp.where(lax.broadcasted_iota(jnp.int32, st.shape, 1) >= lax.broadcasted_iota(jnp.int32, st.shape, 0), st, NEG_INF)

    if fox:
        kb_all = kball_ref[0]
        owner = lax.broadcasted_iota(jnp.int32, kb_all.shape, 1) // BIAS_LANES_PER_HEAD
        for g in range(2):
            kb_ref[:, head(g)] = jnp.where(owner == hg * 2 + g, kb_all, jnp.zeros_like(kb_all))
    else:
        lam_p = lam_ref[...]
        lam = (jnp.exp(jnp.sum(lam_p[0:1] * lam_p[1:2], axis=1, keepdims=True))
               - jnp.exp(jnp.sum(lam_p[2:3] * lam_p[3:4], axis=1, keepdims=True)) + lambda_init)
    ones = jnp.ones((ONES_ROWS, tk), BF16)

    def lookahead(g, qmat, tile, j, diagonal, slot):
        k0 = pl.multiple_of(j * tk, tk)
        if not diagonal:
            st = scores_of(qmat, g, k0, tk)
            st_ref[g, slot] = st
            mx_ref[g, slot] = jnp.max(st, axis=0, keepdims=True)
            return
        late_q = qmat[:, half:] if fox else qmat[half:, :]
        top = causal(scores_of(qmat, g, k0, half))
        bottom = causal(scores_of(late_q, g, pl.multiple_of(k0 + half, half), half))
        st_ref[g, slot, :half, :] = top
        st_ref[g, slot, half:, half:] = bottom
        mx_top = jnp.max(top, axis=0, keepdims=True)
        mx_late = jnp.maximum(mx_top[:, half:], jnp.max(bottom, axis=0, keepdims=True))
        mx_ref[g, slot] = jnp.concatenate([mx_top[:, :half], mx_late], axis=1)

    def absorb(g, j, slot, diagonal=False):
        k0 = pl.multiple_of(j * tk, tk)
        vt = jnp.concatenate([vt_ref[0, kv[g], pl.ds(k0, tk)], ones], axis=0)
        m = m_ref[g]
        m_new = jnp.maximum(m, mx_ref[g, slot])
        if not diagonal:
            p = jnp.exp2(st_ref[g, slot] - m_new).astype(BF16)
            acc_ref[g] = jnp.exp2(m - m_new) * acc_ref[g] + _dot(vt, p)
        else:
            p_top = jnp.exp2(st_ref[g, slot, :half, :] - m_new).astype(BF16)
            p_bottom = jnp.exp2(st_ref[g, slot, half:, half:] - m_new[:, half:]).astype(BF16)
            acc = jnp.exp2(m - m_new) * acc_ref[g] + _dot(vt[:, :half], p_top)
            acc_ref[g, :, :half] = acc[:, :half]
            acc_ref[g, :, half:] = acc[:, half:] + _dot(vt[:, half:], p_bottom)
        m_ref[g] = m_new

    for g in range(2):
        lookahead(g, queries(g, 0), 0, 0, True, 0)

    def tile(qi, carry):
        qs = [queries(g, qi) for g in range(2)]

        def pair(jj, c):
            for u in range(2):
                for g in range(2):
                    lookahead(g, qs[g], qi, 2 * jj + u + 1, False, 1 - u)
                    absorb(g, 2 * jj + u, u)
            return c

        def even_step(j, c):
            for g in range(2):
                lookahead(g, qs[g], qi, j + 1, False, 1)
                absorb(g, j, 0)
            return c

        m_ref[...] = jnp.full(m_ref.shape, NEG_INF, F32)
        acc_ref[...] = jnp.zeros(acc_ref.shape, F32)
        n_plain = jnp.maximum(qi - 1, 0)
        n_pairs = n_plain // 2
        lax.fori_loop(0, n_pairs, pair, 0)
        lax.fori_loop(2 * n_pairs, n_plain, even_step, 0)
        nxt = jnp.minimum(qi + 1, n_tiles - 1)

        def finish(diag_slot):
            def last_step(j, c):
                for g in range(2):
                    lookahead(g, qs[g], qi, j + 1, True, diag_slot)
                    absorb(g, j, 1 - diag_slot)
                return c

            lax.fori_loop(n_plain, qi, last_step, 0)
            for g in range(2):
                if diag_slot == 0:
                    absorb(g, qi, 0, diagonal=True)
                    lookahead(g, queries(g, nxt), nxt, 0, False, 0)
                else:
                    lookahead(g, queries(g, nxt), nxt, 0, False, 0)
                    absorb(g, qi, 1, diagonal=True)
            outs = [acc_ref[g, :HEAD_LANES, :] / acc_ref[g, HEAD_LANES:HEAD_LANES + 1, :] for g in range(2)]
            rows = tile_rows(qi)
            if fox:
                for g in range(2):
                    gate = _sigmoid(gate_ref[0, rows, head(g)].astype(F32))
                    o_ref[0, rows, head(g)] = (outs[g].T * gate).astype(o_ref.dtype)
            else:
                o = (outs[0] - lam * outs[1]).T
                o_ref[0, rows, :] = (_rms(o, nw_ref[...]) * (1.0 - lambda_init)).astype(o_ref.dtype)

        for parity in range(2):
            pl.when(qi % 2 == parity)(functools.partial(finish, parity))
        return carry

    lax.fori_loop(0, n_tiles, tile, 0)


def _attn_state(tq):
    return [pltpu.VMEM((2, 2, tq, tq), F32), pltpu.VMEM((2, 2, 1, tq), F32), pltpu.VMEM((2, 1, tq), F32),
            pltpu.VMEM((2, HEAD_LANES + ONES_ROWS, tq), F32)]


def _fox_attention(qt, k, vt, gate, qbt, kb, *, tq=ATTN_TILE):
    b, t, dm = k.shape
    width = 2 * HEAD_LANES
    kern = functools.partial(_attn_kernel, tq=tq, fox=True, lambda_init=0.0)
    seq = pl.BlockSpec((1, t, width), lambda bi, h: (bi, 0, h))
    seq_t = pl.BlockSpec((1, width, t), lambda bi, h: (bi, h, 0))
    seq_bias = pl.BlockSpec((1, t, LANES), lambda bi, h: (bi, 0, 0))
    return pl.pallas_call(
        kern,
        grid=(b, dm // width),
        in_specs=[seq_t, seq, seq_t, seq, pl.BlockSpec((1, LANES, t), lambda bi, h: (bi, 0, 0)), seq_bias],
        out_specs=seq,
        out_shape=jax.ShapeDtypeStruct((b, t, dm), BF16),
        scratch_shapes=_attn_state(tq) + [pltpu.VMEM((t, width), BF16)],
        compiler_params=_params(("arbitrary", "arbitrary")),
        name="fox_attention",
    )(qt, k, vt, gate, qbt, kb)


def _diff_attention(qk, vt, lam_params, norm_w, lambda_init, *, tq=ATTN_TILE):
    b, t, _ = qk.shape
    nh = DIFF_HEADS
    kern = functools.partial(_attn_kernel, tq=tq, fox=False, lambda_init=lambda_init)
    seq = pl.BlockSpec((1, t, HEAD_LANES), lambda bi, h: (bi, 0, h))
    return pl.pallas_call(
        kern,
        grid=(b, nh),
        in_specs=[seq,
                  pl.BlockSpec((1, t, HEAD_LANES), lambda bi, h: (bi, 0, nh + h)),
                  pl.BlockSpec((1, HEAD_LANES, t), lambda bi, h: (bi, h, 0)),
                  _const_spec(lam_params.shape), _const_spec((1, HEAD_LANES))],
        out_specs=seq,
        out_shape=jax.ShapeDtypeStruct((b, t, nh * HEAD_LANES), BF16),
        scratch_shapes=_attn_state(tq),
        compiler_params=_params(("arbitrary", "arbitrary")),
        name="diff_attention",
    )(qk, qk, vt, lam_params, norm_w)


def _gdn_prep_kernel(q_ref, k_ref, v_ref, gates_ref, u_ref, w_ref, qd_ref, kd_ref, qk_ref):
    c = GDN_CHUNK
    sub = GDN_PREP_SUBTILE
    units = [(slice(t0, t0 + sub), hd) for t0 in range(0, q_ref.shape[1], sub) for hd in range(GDN_HEADS)]
    ids = range(len(units))
    col = lambda hd: slice(hd * HEAD_LANES, (hd + 1) * HEAD_LANES)
    lane = lax.broadcasted_iota(jnp.int32, (sub, LANES), 1)
    ri = lax.broadcasted_iota(jnp.int32, (sub, sub), 0)
    ci = lax.broadcasted_iota(jnp.int32, (sub, sub), 1)
    chunk_start = ri - ri % c
    incl = lambda a: jnp.where(ci <= ri, jnp.where(ci >= chunk_start, a, 0.0), 0.0)
    strict = lambda a: jnp.where(ci < ri, jnp.where(ci >= chunk_start, a, 0.0), 0.0)
    ident = jnp.where(ri == ci, 1.0, 0.0)
    ones = jnp.ones((sub, LANES), BF16)
    kt = [k_ref[0, rows, col(hd)] for rows, hd in units]
    beta = [gates_ref[0, rows, hd:hd + 1] for rows, hd in units]
    gcc = [gates_ref[0, rows, GDN_HEADS + hd:GDN_HEADS + hd + 1] for rows, hd in units]
    k16 = [kt[i].astype(BF16) for i in ids]
    kb = [kt[i] * beta[i] for i in ids]

    qk_raw = [_dot_nt(q_ref[0, rows, col(hd)].astype(BF16), k16[i]) for i, (rows, hd) in enumerate(units)]
    kk = [_dot_nt(kb[i].astype(BF16), k16[i]) for i in ids]
    gc_row = []
    for i in ids:
        g_hi, g_mid, g_lo = (piece.astype(F32) for piece in _split_bf16(gcc[i], 3))
        pieces = jnp.where(lane == 0, g_hi, jnp.where(lane == 1, g_mid, jnp.where(lane == 2, g_lo, 0.0)))
        gc_row.append(_dot_nt(ones, pieces.astype(BF16)))
    decay = [incl(jnp.exp(incl(gcc[i] - gc_row[i]))) for i in ids]
    lower = [strict(kk[i] * decay[i]) for i in ids]

    span = ri ^ ci
    inv = [ident - jnp.where(span == 1, lower[i], 0.0) for i in ids]
    s_blk = 2
    while s_blk < c:
        shift = int(math.log2(s_blk))
        inv16 = [inv[i].astype(BF16) for i in ids]
        coupled = [_dot(jnp.where((span >> shift) == 1, lower[i], 0.0).astype(BF16), inv16[i]) for i in ids]
        inv = [inv[i] - _dot(inv16[i], coupled[i].astype(BF16)) for i in ids]
        s_blk *= 2
    inv16 = [inv[i].astype(BF16) for i in ids]

    eg = [jnp.exp(gcc[i]) for i in ids]
    rhs = [jnp.concatenate([v_ref[0, rows, col(hd)] * beta[i], kb[i] * eg[i]], axis=1)
           for i, (rows, hd) in enumerate(units)]
    sol = [_dot(inv16[i], rhs[i].astype(BF16)) for i in ids]
    a_hi, a_lo, s_hi, s_lo = [], [], [], []
    for i in ids:
        hi, lo = _split_bf16(ident + lower[i], 2)
        a_hi.append(hi)
        a_lo.append(lo)
        hi, lo = _split_bf16(sol[i], 2)
        s_hi.append(hi)
        s_lo.append(lo)
    prod = [_dot(a_hi[i], s_hi[i]) + (_dot(a_hi[i], s_lo[i]) + _dot(a_lo[i], s_hi[i])) for i in ids]
    corr = [_dot(inv16[i], (rhs[i] - prod[i]).astype(BF16)) for i in ids]
    sol = [sol[i] + corr[i] for i in ids]
    for i, (rows, hd) in enumerate(units):
        u_ref[0, rows, col(hd)] = sol[i][:, :HEAD_LANES]
        w_ref[0, rows, col(hd)] = sol[i][:, HEAD_LANES:].astype(BF16)
        qk = incl(qk_raw[i] * decay[i])
        qd_ref[0, rows, col(hd)] = (q_ref[0, rows, col(hd)] * eg[i]).astype(BF16)
        for n in range(sub // c):
            blk = slice(n * c, (n + 1) * c)
            out_rows = slice(rows.start + n * c, rows.start + (n + 1) * c)
            qk_ref[0, hd, out_rows, :] = qk[blk, blk].astype(BF16)
            gl = gcc[i][(n + 1) * c - 1:(n + 1) * c, :]
            kd_ref[0, out_rows, col(hd)] = (kt[i][blk] * jnp.exp(gl - gcc[i][blk])).astype(BF16)


def _gdn_prep(qkv, gates, *, tt=GDN_PREP_TILE):
    b, t, _ = qkv.shape
    nh = GDN_HEADS
    dm = nh * HEAD_LANES
    blk = lambda part: pl.BlockSpec((1, tt, dm), lambda bi, i: (bi, i, part))
    return pl.pallas_call(
        _gdn_prep_kernel,
        grid=(b, t // tt),
        in_specs=[blk(0), blk(1), blk(2), pl.BlockSpec((1, tt, LANES), lambda bi, i: (bi, i, 0))],
        out_specs=[blk(0)] * 4 + [pl.BlockSpec((1, nh, tt, GDN_CHUNK), lambda bi, i: (bi, 0, i, 0))],
        out_shape=[jax.ShapeDtypeStruct((b, t, dm), F32)]
        + [jax.ShapeDtypeStruct((b, t, dm), BF16)] * 3
        + [jax.ShapeDtypeStruct((b, nh, t, GDN_CHUNK), BF16)],
        compiler_params=_params(("arbitrary", "arbitrary")),
        name="gdn_prep",
    )(qkv, qkv, qkv, gates)


def _gdn_scan_kernel(u_ref, w_ref, qd_ref, kd_ref, qk_ref, gates_ref, z_ref, nw_ref, o_ref, s_ref):
    c = GDN_CHUNK
    nb, tt = u_ref.shape[0], u_ref.shape[1]

    @pl.when(pl.program_id(1) == 0)
    def _():
        s_ref[...] = jnp.zeros_like(s_ref)

    chains = [(bi, hd) for bi in range(nb) for hd in range(GDN_HEADS)]
    cols = [slice(hd * HEAD_LANES, (hd + 1) * HEAD_LANES) for hd in range(GDN_HEADS)]
    state = [s_ref[bi, hd] for bi, hd in chains]
    for n in range(tt // c):
        rows = slice(n * c, (n + 1) * c)
        r = [_dot(jnp.concatenate([w_ref[bi, rows, cols[hd]], qd_ref[bi, rows, cols[hd]]], axis=0),
                  state[i].astype(BF16)) for i, (bi, hd) in enumerate(chains)]
        v_new = [(u_ref[bi, rows, cols[hd]] - r[i][:c]).astype(BF16) for i, (bi, hd) in enumerate(chains)]
        intra = [_dot(qk_ref[bi, hd, rows, :], v_new[i]) for i, (bi, hd) in enumerate(chains)]
        upd = [_dot_tn(kd_ref[bi, rows, cols[hd]], v_new[i]) for i, (bi, hd) in enumerate(chains)]
        for i, (bi, hd) in enumerate(chains):
            last = (n + 1) * c - 1
            decay_last = jnp.exp(gates_ref[bi, last:last + 1, GDN_HEADS + hd:GDN_HEADS + hd + 1])
            state[i] = state[i] * decay_last + upd[i]
            zt = z_ref[bi, rows, cols[hd]].astype(F32)
            o = r[i][c:] + intra[i]
            o_ref[bi, rows, cols[hd]] = (_rms(o, nw_ref[...]) * (zt * _sigmoid(zt))).astype(o_ref.dtype)
    for i, (bi, hd) in enumerate(chains):
        s_ref[bi, hd] = state[i]


def _gdn_scan(u, w, qd, kd, qk, gates, z, norm_w, *, tt=GDN_SCAN_TILE, nb=GDN_SCAN_BATCH):
    b, t, dm = u.shape
    nh = GDN_HEADS
    assert b % nb == 0 and t % tt == 0
    blk = pl.BlockSpec((nb, tt, dm), lambda bi, i: (bi, i, 0))
    return pl.pallas_call(
        _gdn_scan_kernel,
        grid=(b // nb, t // tt),
        in_specs=[blk, blk, blk, blk,
                  pl.BlockSpec((nb, nh, tt, GDN_CHUNK), lambda bi, i: (bi, 0, i, 0)),
                  pl.BlockSpec((nb, tt, LANES), lambda bi, i: (bi, i, 0)),
                  blk, _const_spec((1, HEAD_LANES))],
        out_specs=blk,
        out_shape=jax.ShapeDtypeStruct((b, t, dm), BF16),
        scratch_shapes=[pltpu.VMEM((nb, nh, GDN_HEAD_DIM, GDN_HEAD_DIM), F32)],
        compiler_params=_params(("arbitrary", "arbitrary")),
        name="gdn_scan",
    )(u, w, qd, kd, qk, gates, z, norm_w)


def _post_kernel(*refs, n_mix, final_norm):
    x_ref = refs[0]
    mix_refs = refs[1:1 + n_mix]
    wout_ref, g_ref, wup_ref, wdn_ref, p_ref, wpp_ref, wpg_ref = refs[1 + n_mix:8 + n_mix]
    rest = refs[8 + n_mix:]
    if final_norm:
        gf_ref, o_ref = rest
    else:
        (o_ref,) = rest
    mix = mix_refs[0][...] if n_mix == 1 else jnp.concatenate([r[...] for r in mix_refs], axis=1)
    x = x_ref[...] + _dot(mix, wout_ref[...])
    h = _rms(x, g_ref[...]).astype(BF16)
    d_ff = wup_ref.shape[1]
    acc = x
    for s in range(d_ff // FF_SEG):
        a = jnp.maximum(_dot(h, wup_ref[:, s * FF_SEG:(s + 1) * FF_SEG]), 0.0)
        acc = acc + _dot((a * a).astype(BF16), wdn_ref[s * FF_SEG:(s + 1) * FF_SEG, :])
    x = acc
    gate = _sigmoid(_dot(x.astype(BF16), wpg_ref[...]))
    x = x + _dot(p_ref[...].astype(BF16), wpp_ref[...]) * gate
    if final_norm:
        x = _rms(x, gf_ref[...])
    o_ref[...] = x


def _post(x2d, mixes, wout, g, wup, wdn, p2d, wpp, wpg, gf=None):
    m, d = x2d.shape
    tm = TOKEN_TILE
    row = lambda i: (i, 0)
    single = pl.Buffered(1)
    const = lambda a: pl.BlockSpec(a.shape, lambda i: (0, 0), pipeline_mode=single)
    args = [x2d, *mixes, wout, g, wup, wdn, p2d, wpp, wpg]
    in_specs = ([pl.BlockSpec((tm, d), row)]
                + [pl.BlockSpec((tm, a.shape[1]), row) for a in mixes]
                + [const(wout), const(g), const(wup), const(wdn), pl.BlockSpec((tm, p2d.shape[1]), row), const(wpp), const(wpg)])
    if gf is not None:
        args.append(gf)
        in_specs.append(const(gf))
    kern = functools.partial(_post_kernel, n_mix=len(mixes), final_norm=gf is not None)
    return pl.pallas_call(
        kern,
        grid=(m // tm,),
        in_specs=in_specs,
        out_specs=pl.BlockSpec((tm, d), row),
        out_shape=jax.ShapeDtypeStruct((m, d), F32),
        compiler_params=_params(("arbitrary",)),
        name="out_proj_mlp_ple",
    )(*args)


def _pad_lanes(a):
    return jnp.pad(a, ((0, 0), (0, LANES - a.shape[1])))


def kernel(x, p, positions, norm_mix, norm_mlp, norm_final, w_in_even, conv_w, a_log, dt_bias, gdn_norm,
           lam_q1, lam_k1, lam_q2, lam_k2, diff_norm, w_out_even, w_in_odd, b_forget, w_out_odd,
           w_mlp_up, w_mlp_down, w_ple_proj, w_ple_gate):
    b, t, d = x.shape
    depth = p.shape[0]
    m = b * t
    assert t % TOKEN_TILE == 0 and d % PROJ_SEG == 0
    nh = GDN_HEADS
    gdn_w = 3 * nh * GDN_HEAD_DIM + nh * GDN_HEAD_DIM
    assert w_in_even.shape[2] == gdn_w + 2 * nh + 3 * DIFF_HEADS * 2 * DIFF_QK_DIM

    inv_freq = ROPE_THETA ** (-jnp.arange(0, DIFF_QK_DIM, 2, dtype=F32) / DIFF_QK_DIM)
    ang = positions.astype(F32)[..., None] * inv_freq
    cos, sin = jnp.cos(ang), jnp.sin(ang)
    cos_t = jnp.concatenate([cos, cos, cos, cos], axis=-1).reshape(m, LANES)
    sin_t = jnp.concatenate([-sin, sin, -sin, sin], axis=-1).reshape(m, LANES)

    x2d = x.reshape(m, d)
    for i in range(depth):
        j = i // 2
        g_mix = norm_mix[i].reshape(1, d)
        if i % 2 == 0:
            lambda_init = 0.8 - 0.6 * math.exp(-0.3 * i)
            w = w_in_even[j]
            wm = jnp.concatenate([w[:, :gdn_w], w[:, gdn_w + 2 * nh:]], axis=1).astype(BF16)
            wg = _pad_lanes(w[:, gdn_w:gdn_w + 2 * nh]).astype(BF16)
            alog_row = _pad_lanes(jnp.concatenate([jnp.zeros((nh,), F32), a_log[j]]).reshape(1, 2 * nh))
            dt_row = _pad_lanes(jnp.concatenate([jnp.zeros((nh,), F32), dt_bias[j]]).reshape(1, 2 * nh))
            qkv, z, qkb, vbt, gates = _even_in(x2d, g_mix, wm, wg, conv_w[j], alog_row, dt_row, cos_t, sin_t, t)
            qkv, z, qkb, gates = (a.reshape(b, t, -1) for a in (qkv, z, qkb, gates))
            u, wy, qd, kd, qk = _gdn_prep(qkv, gates)
            o_a = _gdn_scan(u, wy, qd, kd, qk, gates, z, gdn_norm[j].reshape(1, HEAD_LANES))
            lam_params = jnp.stack([lam_q1[j], lam_k1[j], lam_q2[j], lam_k2[j]])
            o_b = _diff_attention(qkb, vbt, lam_params, diff_norm[j].reshape(1, HEAD_LANES), lambda_init)
            mixes = [o_a.reshape(m, -1), o_b.reshape(m, -1)]
            wout = w_out_even[j].astype(BF16)
        else:
            w = w_in_odd[j]
            d_mix = (w.shape[1] - FOX_HEADS) // 4
            wm = w[:, :4 * d_mix].astype(BF16)
            wf = _pad_lanes(w[:, 4 * d_mix:]).astype(BF16)
            bf_row = _pad_lanes(b_forget[j].reshape(1, FOX_HEADS))
            qt, k, vt, gate, qbt, kb = _odd_in(x2d, g_mix, wm, wf, bf_row, t)
            k, gate, kb = (a.reshape(b, t, -1) for a in (k, gate, kb))
            o = _fox_attention(qt, k, vt, gate, qbt, kb)
            mixes = [o.reshape(m, -1)]
            wout = w_out_odd[j].astype(BF16)
        x2d = _post(x2d, mixes, wout, norm_mlp[i].reshape(1, d), w_mlp_up[i].astype(BF16),
                    w_mlp_down[i].astype(BF16), p[i].reshape(m, -1), w_ple_proj[i].astype(BF16),
                    w_ple_gate[i].astype(BF16), norm_final.reshape(1, d) if i == depth - 1 else None)
    return x2d.reshape(b, t, d)
```

```python
import functools
import math

import jax
import jax.numpy as jnp
import numpy as np
from jax import lax
from jax.experimental import pallas as pl
from jax.experimental.pallas import tpu as pltpu

F32 = jnp.float32
BF16 = jnp.bfloat16

GDN_HEADS = 4
GDN_HEAD_DIM = 128
GDN_CHUNK = 64
CONV_WIDTH = 4
DIFF_HEADS = 4
DIFF_QK_DIM = 64
FOX_HEADS = 8
HEAD_LANES = 128
ROPE_THETA = 10000.0
EPS = 1e-6
NEG_INF = -1e30
LOG2E = 1.4426950408889634
LANES = 128
SUBLANES = 8
VMEM_LIMIT_BYTES = 56 * 1024 * 1024

TOKEN_TILE = 1024
EVEN_TOKEN_TILE = 512
PROJ_SEG = 512
FF_SEG = 1024
GDN_PREP_SUBTILE = 256
GDN_PREP_TILE = 512
GDN_SCAN_TILE = 256
GDN_SCAN_BATCH = 4
ATTN_TILE = 512
BIAS_LANES_PER_HEAD = 16
ONES_ROWS = 16
DIAGONAL_SLOT = 2


def _dot(a, b):
    return jnp.dot(a, b, preferred_element_type=F32)


def _split_bf16(x, parts):
    out = []
    for _ in range(parts):
        piece = x.astype(BF16)
        out.append(piece)
        x = x - piece.astype(F32)
    return out


def _dot_nt(a, b):
    return lax.dot_general(a, b, (((1,), (1,)), ((), ())), preferred_element_type=F32)


def _dot_tn(a, b):
    return lax.dot_general(a, b, (((0,), (0,)), ((), ())), preferred_element_type=F32)


def _rms(x, g):
    return x * lax.rsqrt(jnp.mean(x * x, axis=-1, keepdims=True) + EPS) * g


def _sigmoid(x):
    return 1.0 / (1.0 + jnp.exp(-x))


def _softplus(x):
    return jnp.maximum(x, 0.0) + jnp.log1p(jnp.exp(-jnp.abs(x)))


def _row_scan(x, period):
    rows = lax.broadcasted_iota(jnp.int32, x.shape, 0) % period
    s = 1
    while s < period:
        x = x + jnp.where(rows >= s, pltpu.roll(x, s, 0), 0.0)
        s *= 2
    return x


def _const_spec(shape):
    return pl.BlockSpec(shape, lambda *_: (0,) * len(shape))


def _params(sem):
    return pltpu.CompilerParams(dimension_semantics=sem, vmem_limit_bytes=VMEM_LIMIT_BYTES)


def _even_in_kernel(x_ref, g_ref, wm_ref, wg_ref, conv_ref, alog_ref, dt_ref, cos_ref, sin_ref,
                    qkv_ref, z_ref, qkb_ref, vbt_ref, gates_ref, h_ref, carry_ref, tr_ref, pad_ref, *, tiles_per_seq):
    tm = x_ref.shape[0]
    i = pl.program_id(0)
    h_ref[...] = _rms(x_ref[...], g_ref[...]).astype(BF16)
    seq_start = (i % tiles_per_seq) == 0
    seg = lambda s: slice(s * PROJ_SEG, (s + 1) * PROJ_SEG)
    project = lambda s: _dot(h_ref[...], wm_ref[:, seg(s)])

    def gdn_qkv(s, y):
        cols = seg(s)
        pad_ref[0:SUBLANES, :] = jnp.where(seq_start, 0.0, carry_ref[:, cols])
        pad_ref[SUBLANES:, :] = y
        carry_ref[:, cols] = y[tm - SUBLANES:, :]
        w = conv_ref[:, cols]
        a = y * w[CONV_WIDTH - 1:CONV_WIDTH, :]
        for k in range(1, CONV_WIDTH):
            a = a + pad_ref[SUBLANES - k:SUBLANES - k + tm, :] * w[CONV_WIDTH - 1 - k:CONV_WIDTH - k, :]
        a = a * _sigmoid(a)
        if s < 2:
            outs = []
            for hd in range(GDN_HEADS):
                blk = a[:, hd * HEAD_LANES:(hd + 1) * HEAD_LANES]
                n = blk * lax.rsqrt(jnp.sum(blk * blk, axis=-1, keepdims=True) + EPS)
                outs.append(n * (GDN_HEAD_DIM ** -0.5) if s == 0 else n)
            a = jnp.concatenate(outs, axis=1)
        qkv_ref[:, cols] = a

    def gdn_gate(s, y):
        z_ref[...] = y.astype(BF16)

    def diff_qk(s, y):
        cos = jnp.concatenate([cos_ref[...]] * (PROJ_SEG // LANES), axis=1)
        sin = jnp.concatenate([sin_ref[...]] * (PROJ_SEG // LANES), axis=1)
        lane = lax.broadcasted_iota(jnp.int32, (tm, PROJ_SEG), 1)
        first_half = (lane % DIFF_QK_DIM) < (DIFF_QK_DIM // 2)
        swapped = jnp.where(first_half, pltpu.roll(y, PROJ_SEG - DIFF_QK_DIM // 2, 1),
                            pltpu.roll(y, DIFF_QK_DIM // 2, 1))
        scale = DIFF_QK_DIM ** -0.5 * LOG2E if s == 4 else 1.0
        qkb_ref[:, seg(s - 4)] = ((y * cos + swapped * sin) * scale).astype(BF16)

    def diff_v(s, y):
        tr_ref[...] = y
        vbt_ref[0] = tr_ref[...].T.astype(BF16)

    stages = ((0, gdn_qkv), (3, gdn_gate), (1, gdn_qkv), (6, diff_v), (2, gdn_qkv), (4, diff_qk), (5, diff_qk))
    pending = project(stages[0][0])
    for n, (s, epilogue) in enumerate(stages):
        upcoming = project(stages[n + 1][0]) if n + 1 < len(stages) else _dot(h_ref[...], wg_ref[...])
        epilogue(s, pending)
        pending = upcoming

    graw = pending
    beta = _sigmoid(graw)
    g = -jnp.exp(alog_ref[...]) * _softplus(graw + dt_ref[...])
    gc = _row_scan(g, GDN_CHUNK)
    lane_g = lax.broadcasted_iota(jnp.int32, (tm, LANES), 1)
    gates_ref[...] = jnp.where(lane_g < GDN_HEADS, beta, gc)


def _even_in(x2d, g, wm, wg, conv_w, alog_row, dt_row, cos_t, sin_t, seq_len):
    m, d = x2d.shape
    tm = EVEN_TOKEN_TILE
    n_main = wm.shape[1]
    tps = seq_len // tm
    kern = functools.partial(_even_in_kernel, tiles_per_seq=tps)
    row = lambda i: (i, 0)
    return pl.pallas_call(
        kern,
        grid=(m // tm,),
        in_specs=[
            pl.BlockSpec((tm, d), row),
            _const_spec((1, d)),
            _const_spec((d, n_main)),
            _const_spec((d, LANES)),
            _const_spec(conv_w.shape),
            _const_spec((1, LANES)),
            _const_spec((1, LANES)),
            pl.BlockSpec((tm, LANES), row),
            pl.BlockSpec((tm, LANES), row),
        ],
        out_specs=[
            pl.BlockSpec((tm, 3 * PROJ_SEG), row),
            pl.BlockSpec((tm, PROJ_SEG), row),
            pl.BlockSpec((tm, 2 * PROJ_SEG), row),
            pl.BlockSpec((1, PROJ_SEG, tm), lambda i: (i // tps, 0, i % tps)),
            pl.BlockSpec((tm, LANES), row),
        ],
        out_shape=[
            jax.ShapeDtypeStruct((m, 3 * PROJ_SEG), F32),
            jax.ShapeDtypeStruct((m, PROJ_SEG), BF16),
            jax.ShapeDtypeStruct((m, 2 * PROJ_SEG), BF16),
            jax.ShapeDtypeStruct((m // seq_len, PROJ_SEG, seq_len), BF16),
            jax.ShapeDtypeStruct((m, LANES), F32),
        ],
        scratch_shapes=[pltpu.VMEM((tm, d), BF16), pltpu.VMEM((SUBLANES, 3 * PROJ_SEG), F32), pltpu.VMEM((tm, PROJ_SEG), F32),
                        pltpu.VMEM((tm + SUBLANES, PROJ_SEG), F32)],
        compiler_params=_params(("arbitrary",)),
        name="even_in_proj",
    )(x2d, g, wm, wg, conv_w, alog_row, dt_row, cos_t, sin_t)


def _odd_in_kernel(x_ref, g_ref, wm_ref, wf_ref, bf_ref, sel_ref, ones_ref, qt_ref, k_ref, vt_ref, gate_ref, qbt_ref, kb_ref,
                   h_ref, carry_ref, tr_ref, *, tiles_per_seq, d_mix):
    tm = x_ref.shape[0]
    i = pl.program_id(0)
    h_ref[...] = _rms(x_ref[...], g_ref[...]).astype(BF16)
    f = _dot(h_ref[...], wf_ref[...]) + bf_ref[...]
    log_f = jnp.minimum(f, 0.0) - jnp.log1p(jnp.exp(-jnp.abs(f)))
    prev = jnp.where((i % tiles_per_seq) == 0, 0.0, carry_ref[0:1, :])
    cum = _row_scan(log_f, tm) + prev
    carry_ref[...] = jnp.broadcast_to(cum[tm - 1:tm, :], carry_ref.shape)
    pieces = jnp.concatenate(_split_bf16(LOG2E * cum, 3), axis=1)

    head_dim = d_mix // FOX_HEADS
    for o_ref, base, scale in ((qt_ref, 0, head_dim ** -0.5 * LOG2E), (k_ref, d_mix, 1.0),
                               (vt_ref, 2 * d_mix, 1.0), (gate_ref, 3 * d_mix, 1.0)):
        for s in range(d_mix // PROJ_SEG):
            cols = slice(s * PROJ_SEG, (s + 1) * PROJ_SEG)
            y = _dot(h_ref[...], wm_ref[:, base + s * PROJ_SEG:base + (s + 1) * PROJ_SEG])
            if o_ref is qt_ref or o_ref is vt_ref:
                tr_ref[...] = y * scale
                o_ref[0, cols, :] = tr_ref[...].T.astype(BF16)
            else:
                o_ref[:, cols] = y.astype(BF16)

    lanes = _dot(pieces, sel_ref[...]) + ones_ref[...]
    tr_ref[:, :LANES] = lanes[:, :LANES]
    qbt_ref[0] = tr_ref[:, :LANES].T.astype(BF16)
    kb_ref[...] = lanes[:, LANES:].astype(BF16)


def _bias_lane_tables():
    sel = np.zeros((3 * LANES, 2 * LANES), np.float32)
    ones = np.zeros((1, 2 * LANES), np.float32)
    for h in range(FOX_HEADS):
        base = BIAS_LANES_PER_HEAD * h
        for piece in range(3):
            sel[LANES * piece + h, base + 3 + piece] = 1.0
            sel[LANES * piece + h, LANES + base + piece] = -1.0
            ones[0, base + piece] = 1.0
            ones[0, LANES + base + 3 + piece] = 1.0
    return jnp.asarray(sel, BF16), jnp.asarray(ones, F32)


def _odd_in(x2d, g, wm, wf, bf_row, seq_len):
    m, d = x2d.shape
    sel, ones_row = _bias_lane_tables()
    tm = TOKEN_TILE
    d_mix = wm.shape[1] // 4
    tps = seq_len // tm
    kern = functools.partial(_odd_in_kernel, tiles_per_seq=tps, d_mix=d_mix)
    row = lambda i: (i, 0)
    row_blk = pl.BlockSpec((tm, d_mix), row)
    row_shape = jax.ShapeDtypeStruct((m, d_mix), BF16)
    col_blk = lambda width: pl.BlockSpec((1, width, tm), lambda i: (i // tps, 0, i % tps))
    col_shape = lambda width: jax.ShapeDtypeStruct((m // seq_len, width, seq_len), BF16)
    return pl.pallas_call(
        kern,
        grid=(m // tm,),
        in_specs=[
            pl.BlockSpec((tm, d), row),
            _const_spec((1, d)),
            _const_spec(wm.shape),
            _const_spec((d, LANES)),
            _const_spec((1, LANES)),
            _const_spec(sel.shape),
            _const_spec(ones_row.shape),
        ],
        out_specs=[col_blk(d_mix), row_blk, col_blk(d_mix), row_blk, col_blk(LANES), pl.BlockSpec((tm, LANES), row)],
        out_shape=[col_shape(d_mix), row_shape, col_shape(d_mix), row_shape, col_shape(LANES),
                   jax.ShapeDtypeStruct((m, LANES), BF16)],
        scratch_shapes=[pltpu.VMEM((tm, d), BF16), pltpu.VMEM((SUBLANES, LANES), F32), pltpu.VMEM((tm, PROJ_SEG), F32)],
        compiler_params=_params(("arbitrary",)),
        name="odd_in_proj",
    )(x2d, g, wm, wf, bf_row, sel, ones_row)


def _attn_kernel(*refs, tq, fox, lambda_init):
    if fox:
        q_ref, k_ref, vt_ref, gate_ref, qbt_ref, kball_ref, o_ref, st_ref, mx_ref, m_ref, acc_ref, kb_ref = refs
    else:
        q_ref, k_ref, vt_ref, lam_ref, nw_ref, o_ref, st_ref, mx_ref, m_ref, acc_ref = refs
    tk = tq
    half = tk // 2
    hg = pl.program_id(1)
    n_tiles = k_ref.shape[1] // tq
    head = lambda g: slice(g * HEAD_LANES, (g + 1) * HEAD_LANES)
    kv = [head(0), head(1)] if fox else [head(0), head(0)]
    tile_rows = lambda tile: pl.ds(pl.multiple_of(tile * tq, tq), tq)

    def queries(g, tile):
        if fox:
            return jnp.concatenate([q_ref[0, head(g), tile_rows(tile)], qbt_ref[0, :, tile_rows(tile)]], axis=0)
        q = q_ref[0, tile_rows(tile), :]
        lane = lax.broadcasted_iota(jnp.int32, q.shape, 1)
        keep = (lane < DIFF_QK_DIM) if g == 0 else (lane >= DIFF_QK_DIM)
        return jnp.where(keep, q, jnp.zeros_like(q))

    def scores_of(qmat, g, k0, rows):
        kj = k_ref[0, pl.ds(k0, rows), kv[g]]
        if fox:
            return _dot(jnp.concatenate([kj, kb_ref[pl.ds(k0, rows), kv[g]]], axis=1), qmat)
        return _dot_nt(kj, qmat)

    def causal(st):
        return jnp.where(lax.broadcasted_iota(jnp.int32, st.shape, 1) >= lax.broadcasted_iota(jnp.int32, st.shape, 0), st, NEG_INF)

    if fox:
        kb_all = kball_ref[0]
        owner = lax.broadcasted_iota(jnp.int32, kb_all.shape, 1) // BIAS_LANES_PER_HEAD
        for g in range(2):
            kb_ref[:, head(g)] = jnp.where(owner == hg * 2 + g, kb_all, jnp.zeros_like(kb_all))
    else:
        lam_p = lam_ref[...]
        lam = (jnp.exp(jnp.sum(lam_p[0:1] * lam_p[1:2], axis=1, keepdims=True))
               - jnp.exp(jnp.sum(lam_p[2:3] * lam_p[3:4], axis=1, keepdims=True)) + lambda_init)
    ones = jnp.ones((ONES_ROWS, tk), BF16)

    def lookahead(g, qmat, tile, j, diagonal, slot):
        k0 = pl.multiple_of(j * tk, tk)
        if not diagonal:
            st = scores_of(qmat, g, k0, tk)
            st_ref[g, slot] = st
            mx_ref[g, slot] = jnp.max(st, axis=0, keepdims=True)
            return
        late_q = qmat[:, half:] if fox else qmat[half:, :]
        top = causal(scores_of(qmat, g, k0, half))
        bottom = causal(scores_of(late_q, g, pl.multiple_of(k0 + half, half), half))
        st_ref[g, slot, :half, :] = top
        st_ref[g, slot, half:, half:] = bottom
        mx_top = jnp.max(top, axis=0, keepdims=True)
        mx_late = jnp.maximum(mx_top[:, half:], jnp.max(bottom, axis=0, keepdims=True))
        mx_ref[g, slot] = jnp.concatenate([mx_top[:, :half], mx_late], axis=1)

    def absorb(g, j, slot, diagonal=False):
        k0 = pl.multiple_of(j * tk, tk)
        vt = jnp.concatenate([vt_ref[0, kv[g], pl.ds(k0, tk)], ones], axis=0)
        m = m_ref[g]
        m_new = jnp.maximum(m, mx_ref[g, slot])
        if not diagonal:
            p = jnp.exp2(st_ref[g, slot] - m_new).astype(BF16)
            acc_ref[g] = jnp.exp2(m - m_new) * acc_ref[g] + _dot(vt, p)
        else:
            p_top = jnp.exp2(st_ref[g, slot, :half, :] - m_new).astype(BF16)
            p_bottom = jnp.exp2(st_ref[g, slot, half:, half:] - m_new[:, half:]).astype(BF16)
            acc = jnp.exp2(m - m_new) * acc_ref[g] + _dot(vt[:, :half], p_top)
            acc_ref[g, :, :half] = acc[:, :half]
            acc_ref[g, :, half:] = acc[:, half:] + _dot(vt[:, half:], p_bottom)
        m_ref[g] = m_new

    for g in range(2):
        lookahead(g, queries(g, 0), 0, 0, True, DIAGONAL_SLOT)

    def tile(qi, carry):
        qs = [queries(g, qi) for g in range(2)]

        def pair(jj, c):
            for u in range(2):
                for g in range(2):
                    lookahead(g, qs[g], qi, 2 * jj + u + 1, False, 1 - u)
                    absorb(g, 2 * jj + u, u)
            return c

        def even_step(j, c):
            for g in range(2):
                lookahead(g, qs[g], qi, j + 1, False, 1)
                absorb(g, j, 0)
            return c

        m_ref[...] = jnp.full(m_ref.shape, NEG_INF, F32)
        acc_ref[...] = jnp.zeros(acc_ref.shape, F32)
        n_plain = jnp.maximum(qi - 1, 0)
        n_pairs = n_plain // 2
        lax.fori_loop(0, n_pairs, pair, 0)
        lax.fori_loop(2 * n_pairs, n_plain, even_step, 0)
        nxt = jnp.minimum(qi + 1, n_tiles - 1)

        def finish(parity):
            def last_step(j, c):
                for g in range(2):
                    lookahead(g, qs[g], qi, j + 1, True, DIAGONAL_SLOT)
                    absorb(g, j, 1 - parity)
                return c

            lax.fori_loop(n_plain, qi, last_step, 0)
            for g in range(2):
                lookahead(g, queries(g, nxt), nxt, 0, False, 0)
                absorb(g, qi, DIAGONAL_SLOT, diagonal=True)
            outs = [acc_ref[g, :HEAD_LANES, :] / acc_ref[g, HEAD_LANES:HEAD_LANES + 1, :] for g in range(2)]
            rows = tile_rows(qi)
            if fox:
                for g in range(2):
                    gate = _sigmoid(gate_ref[0, rows, head(g)].astype(F32))
                    o_ref[0, rows, head(g)] = (outs[g].T * gate).astype(o_ref.dtype)
            else:
                o = (outs[0] - lam * outs[1]).T
                o_ref[0, rows, :] = (_rms(o, nw_ref[...]) * (1.0 - lambda_init)).astype(o_ref.dtype)

        for parity in range(2):
            pl.when(qi % 2 == parity)(functools.partial(finish, parity))
        return carry

    lax.fori_loop(0, n_tiles, tile, 0)


def _attn_state(tq):
    return [pltpu.VMEM((2, DIAGONAL_SLOT + 1, tq, tq), F32), pltpu.VMEM((2, DIAGONAL_SLOT + 1, 1, tq), F32), pltpu.VMEM((2, 1, tq), F32),
            pltpu.VMEM((2, HEAD_LANES + ONES_ROWS, tq), F32)]


def _fox_attention(qt, k, vt, gate, qbt, kb, *, tq=ATTN_TILE):
    b, t, dm = k.shape
    width = 2 * HEAD_LANES
    kern = functools.partial(_attn_kernel, tq=tq, fox=True, lambda_init=0.0)
    seq = pl.BlockSpec((1, t, width), lambda bi, h: (bi, 0, h))
    seq_t = pl.BlockSpec((1, width, t), lambda bi, h: (bi, h, 0))
    seq_bias = pl.BlockSpec((1, t, LANES), lambda bi, h: (bi, 0, 0))
    return pl.pallas_call(
        kern,
        grid=(b, dm // width),
        in_specs=[seq_t, seq, seq_t, seq, pl.BlockSpec((1, LANES, t), lambda bi, h: (bi, 0, 0)), seq_bias],
        out_specs=seq,
        out_shape=jax.ShapeDtypeStruct((b, t, dm), BF16),
        scratch_shapes=_attn_state(tq) + [pltpu.VMEM((t, width), BF16)],
        compiler_params=_params(("arbitrary", "arbitrary")),
        name="fox_attention",
    )(qt, k, vt, gate, qbt, kb)


def _diff_attention(qk, vt, lam_params, norm_w, lambda_init, *, tq=ATTN_TILE):
    b, t, _ = qk.shape
    nh = DIFF_HEADS
    kern = functools.partial(_attn_kernel, tq=tq, fox=False, lambda_init=lambda_init)
    seq = pl.BlockSpec((1, t, HEAD_LANES), lambda bi, h: (bi, 0, h))
    return pl.pallas_call(
        kern,
        grid=(b, nh),
        in_specs=[seq,
                  pl.BlockSpec((1, t, HEAD_LANES), lambda bi, h: (bi, 0, nh + h)),
                  pl.BlockSpec((1, HEAD_LANES, t), lambda bi, h: (bi, h, 0)),
                  _const_spec(lam_params.shape), _const_spec((1, HEAD_LANES))],
        out_specs=seq,
        out_shape=jax.ShapeDtypeStruct((b, t, nh * HEAD_LANES), BF16),
        scratch_shapes=_attn_state(tq),
        compiler_params=_params(("arbitrary", "arbitrary")),
        name="diff_attention",
    )(qk, qk, vt, lam_params, norm_w)


def _gdn_prep_kernel(q_ref, k_ref, v_ref, gates_ref, u_ref, w_ref, qd_ref, kd_ref, qk_ref):
    c = GDN_CHUNK
    sub = GDN_PREP_SUBTILE
    units = [(slice(t0, t0 + sub), hd) for t0 in range(0, q_ref.shape[1], sub) for hd in range(GDN_HEADS)]
    ids = range(len(units))
    col = lambda hd: slice(hd * HEAD_LANES, (hd + 1) * HEAD_LANES)
    lane = lax.broadcasted_iota(jnp.int32, (sub, LANES), 1)
    ri = lax.broadcasted_iota(jnp.int32, (sub, sub), 0)
    ci = lax.broadcasted_iota(jnp.int32, (sub, sub), 1)
    chunk_start = ri - ri % c
    incl = lambda a: jnp.where(ci <= ri, jnp.where(ci >= chunk_start, a, 0.0), 0.0)
    strict = lambda a: jnp.where(ci < ri, jnp.where(ci >= chunk_start, a, 0.0), 0.0)
    ident = jnp.where(ri == ci, 1.0, 0.0)
    ones = jnp.ones((sub, LANES), BF16)
    kt = [k_ref[0, rows, col(hd)] for rows, hd in units]
    beta = [gates_ref[0, rows, hd:hd + 1] for rows, hd in units]
    gcc = [gates_ref[0, rows, GDN_HEADS + hd:GDN_HEADS + hd + 1] for rows, hd in units]
    k16 = [kt[i].astype(BF16) for i in ids]
    kb = [kt[i] * beta[i] for i in ids]

    qk_raw = [_dot_nt(q_ref[0, rows, col(hd)].astype(BF16), k16[i]) for i, (rows, hd) in enumerate(units)]
    kk = [_dot_nt(kb[i].astype(BF16), k16[i]) for i in ids]
    gc_row = []
    for i in ids:
        g_hi, g_mid, g_lo = (piece.astype(F32) for piece in _split_bf16(gcc[i], 3))
        pieces = jnp.where(lane == 0, g_hi, jnp.where(lane == 1, g_mid, jnp.where(lane == 2, g_lo, 0.0)))
        gc_row.append(_dot_nt(ones, pieces.astype(BF16)))
    decay = [incl(jnp.exp(incl(gcc[i] - gc_row[i]))) for i in ids]
    lower = [strict(kk[i] * decay[i]) for i in ids]

    span = ri ^ ci
    inv = [ident - jnp.where(span == 1, lower[i], 0.0) for i in ids]
    s_blk = 2
    while s_blk < c:
        shift = int(math.log2(s_blk))
        inv16 = [inv[i].astype(BF16) for i in ids]
        coupled = [_dot(jnp.where((span >> shift) == 1, lower[i], 0.0).astype(BF16), inv16[i]) for i in ids]
        inv = [inv[i] - _dot(inv16[i], coupled[i].astype(BF16)) for i in ids]
        s_blk *= 2
    inv16 = [inv[i].astype(BF16) for i in ids]

    eg = [jnp.exp(gcc[i]) for i in ids]
    rhs = [jnp.concatenate([v_ref[0, rows, col(hd)] * beta[i], kb[i] * eg[i]], axis=1)
           for i, (rows, hd) in enumerate(units)]
    sol = [_dot(inv16[i], rhs[i].astype(BF16)) for i in ids]
    a_hi, a_lo, s_hi, s_lo = [], [], [], []
    for i in ids:
        hi, lo = _split_bf16(ident + lower[i], 2)
        a_hi.append(hi)
        a_lo.append(lo)
        hi, lo = _split_bf16(sol[i], 2)
        s_hi.append(hi)
        s_lo.append(lo)
    prod = [_dot(a_hi[i], s_hi[i]) + (_dot(a_hi[i], s_lo[i]) + _dot(a_lo[i], s_hi[i])) for i in ids]
    corr = [_dot(inv16[i], (rhs[i] - prod[i]).astype(BF16)) for i in ids]
    sol = [sol[i] + corr[i] for i in ids]
    for i, (rows, hd) in enumerate(units):
        u_ref[0, rows, col(hd)] = sol[i][:, :HEAD_LANES]
        w_ref[0, rows, col(hd)] = sol[i][:, HEAD_LANES:].astype(BF16)
        qk = incl(qk_raw[i] * decay[i])
        qd_ref[0, rows, col(hd)] = (q_ref[0, rows, col(hd)] * eg[i]).astype(BF16)
        for n in range(sub // c):
            blk = slice(n * c, (n + 1) * c)
            out_rows = slice(rows.start + n * c, rows.start + (n + 1) * c)
            qk_ref[0, hd, out_rows, :] = qk[blk, blk].astype(BF16)
            gl = gcc[i][(n + 1) * c - 1:(n + 1) * c, :]
            kd_ref[0, out_rows, col(hd)] = (kt[i][blk] * jnp.exp(gl - gcc[i][blk])).astype(BF16)


def _gdn_prep(qkv, gates, *, tt=GDN_PREP_TILE):
    b, t, _ = qkv.shape
    nh = GDN_HEADS
    dm = nh * HEAD_LANES
    blk = lambda part: pl.BlockSpec((1, tt, dm), lambda bi, i: (bi, i, part))
    return pl.pallas_call(
        _gdn_prep_kernel,
        grid=(b, t // tt),
        in_specs=[blk(0), blk(1), blk(2), pl.BlockSpec((1, tt, LANES), lambda bi, i: (bi, i, 0))],
        out_specs=[blk(0)] * 4 + [pl.BlockSpec((1, nh, tt, GDN_CHUNK), lambda bi, i: (bi, 0, i, 0))],
        out_shape=[jax.ShapeDtypeStruct((b, t, dm), F32)]
        + [jax.ShapeDtypeStruct((b, t, dm), BF16)] * 3
        + [jax.ShapeDtypeStruct((b, nh, t, GDN_CHUNK), BF16)],
        compiler_params=_params(("arbitrary", "arbitrary")),
        name="gdn_prep",
    )(qkv, qkv, qkv, gates)


def _gdn_scan_kernel(u_ref, w_ref, qd_ref, kd_ref, qk_ref, gates_ref, z_ref, nw_ref, o_ref, s_ref):
    c = GDN_CHUNK
    nb, tt = u_ref.shape[0], u_ref.shape[1]

    @pl.when(pl.program_id(1) == 0)
    def _():
        s_ref[...] = jnp.zeros_like(s_ref)

    chains = [(bi, hd) for bi in range(nb) for hd in range(GDN_HEADS)]
    cols = [slice(hd * HEAD_LANES, (hd + 1) * HEAD_LANES) for hd in range(GDN_HEADS)]
    state = [s_ref[bi, hd] for bi, hd in chains]
    for n in range(tt // c):
        rows = slice(n * c, (n + 1) * c)
        r = [_dot(jnp.concatenate([w_ref[bi, rows, cols[hd]], qd_ref[bi, rows, cols[hd]]], axis=0),
                  state[i].astype(BF16)) for i, (bi, hd) in enumerate(chains)]
        v_new = [(u_ref[bi, rows, cols[hd]] - r[i][:c]).astype(BF16) for i, (bi, hd) in enumerate(chains)]
        intra = [_dot(qk_ref[bi, hd, rows, :], v_new[i]) for i, (bi, hd) in enumerate(chains)]
        upd = [_dot_tn(kd_ref[bi, rows, cols[hd]], v_new[i]) for i, (bi, hd) in enumerate(chains)]
        for i, (bi, hd) in enumerate(chains):
            last = (n + 1) * c - 1
            decay_last = jnp.exp(gates_ref[bi, last:last + 1, GDN_HEADS + hd:GDN_HEADS + hd + 1])
            state[i] = state[i] * decay_last + upd[i]
            zt = z_ref[bi, rows, cols[hd]].astype(F32)
            o = r[i][c:] + intra[i]
            o_ref[bi, rows, cols[hd]] = (_rms(o, nw_ref[...]) * (zt * _sigmoid(zt))).astype(o_ref.dtype)
    for i, (bi, hd) in enumerate(chains):
        s_ref[bi, hd] = state[i]


def _gdn_scan(u, w, qd, kd, qk, gates, z, norm_w, *, tt=GDN_SCAN_TILE, nb=GDN_SCAN_BATCH):
    b, t, dm = u.shape
    nh = GDN_HEADS
    assert b % nb == 0 and t % tt == 0
    blk = pl.BlockSpec((nb, tt, dm), lambda bi, i: (bi, i, 0))
    return pl.pallas_call(
        _gdn_scan_kernel,
        grid=(b // nb, t // tt),
        in_specs=[blk, blk, blk, blk,
                  pl.BlockSpec((nb, nh, tt, GDN_CHUNK), lambda bi, i: (bi, 0, i, 0)),
                  pl.BlockSpec((nb, tt, LANES), lambda bi, i: (bi, i, 0)),
                  blk, _const_spec((1, HEAD_LANES))],
        out_specs=blk,
        out_shape=jax.ShapeDtypeStruct((b, t, dm), BF16),
        scratch_shapes=[pltpu.VMEM((nb, nh, GDN_HEAD_DIM, GDN_HEAD_DIM), F32)],
        compiler_params=_params(("arbitrary", "arbitrary")),
        name="gdn_scan",
    )(u, w, qd, kd, qk, gates, z, norm_w)


def _post_kernel(*refs, n_mix, final_norm):
    x_ref = refs[0]
    mix_refs = refs[1:1 + n_mix]
    wout_ref, g_ref, wup_ref, wdn_ref, p_ref, wpp_ref, wpg_ref = refs[1 + n_mix:8 + n_mix]
    rest = refs[8 + n_mix:]
    if final_norm:
        gf_ref, o_ref = rest
    else:
        (o_ref,) = rest
    mix = mix_refs[0][...] if n_mix == 1 else jnp.concatenate([r[...] for r in mix_refs], axis=1)
    x = x_ref[...] + _dot(mix, wout_ref[...])
    h = _rms(x, g_ref[...]).astype(BF16)
    d_ff = wup_ref.shape[1]
    acc = x
    for s in range(d_ff // FF_SEG):
        a = jnp.maximum(_dot(h, wup_ref[:, s * FF_SEG:(s + 1) * FF_SEG]), 0.0)
        acc = acc + _dot((a * a).astype(BF16), wdn_ref[s * FF_SEG:(s + 1) * FF_SEG, :])
    x = acc
    gate = _sigmoid(_dot(x.astype(BF16), wpg_ref[...]))
    x = x + _dot(p_ref[...].astype(BF16), wpp_ref[...]) * gate
    if final_norm:
        x = _rms(x, gf_ref[...])
    o_ref[...] = x


def _post(x2d, mixes, wout, g, wup, wdn, p2d, wpp, wpg, gf=None):
    m, d = x2d.shape
    tm = TOKEN_TILE
    row = lambda i: (i, 0)
    single = pl.Buffered(1)
    const = lambda a: pl.BlockSpec(a.shape, lambda i: (0, 0), pipeline_mode=single)
    args = [x2d, *mixes, wout, g, wup, wdn, p2d, wpp, wpg]
    in_specs = ([pl.BlockSpec((tm, d), row)]
                + [pl.BlockSpec((tm, a.shape[1]), row) for a in mixes]
                + [const(wout), const(g), const(wup), const(wdn), pl.BlockSpec((tm, p2d.shape[1]), row), const(wpp), const(wpg)])
    if gf is not None:
        args.append(gf)
        in_specs.append(const(gf))
    kern = functools.partial(_post_kernel, n_mix=len(mixes), final_norm=gf is not None)
    return pl.pallas_call(
        kern,
        grid=(m // tm,),
        in_specs=in_specs,
        out_specs=pl.BlockSpec((tm, d), row),
        out_shape=jax.ShapeDtypeStruct((m, d), F32),
        compiler_params=_params(("arbitrary",)),
        name="out_proj_mlp_ple",
    )(*args)


def _pad_lanes(a):
    return jnp.pad(a, ((0, 0), (0, LANES - a.shape[1])))


def kernel(x, p, positions, norm_mix, norm_mlp, norm_final, w_in_even, conv_w, a_log, dt_bias, gdn_norm,
           lam_q1, lam_k1, lam_q2, lam_k2, diff_norm, w_out_even, w_in_odd, b_forget, w_out_odd,
           w_mlp_up, w_mlp_down, w_ple_proj, w_ple_gate):
    b, t, d = x.shape
    depth = p.shape[0]
    m = b * t
    assert t % TOKEN_TILE == 0 and d % PROJ_SEG == 0
    nh = GDN_HEADS
    gdn_w = 3 * nh * GDN_HEAD_DIM + nh * GDN_HEAD_DIM
    assert w_in_even.shape[2] == gdn_w + 2 * nh + 3 * DIFF_HEADS * 2 * DIFF_QK_DIM

    inv_freq = ROPE_THETA ** (-jnp.arange(0, DIFF_QK_DIM, 2, dtype=F32) / DIFF_QK_DIM)
    ang = positions.astype(F32)[..., None] * inv_freq
    cos, sin = jnp.cos(ang), jnp.sin(ang)
    cos_t = jnp.concatenate([cos, cos, cos, cos], axis=-1).reshape(m, LANES)
    sin_t = jnp.concatenate([-sin, sin, -sin, sin], axis=-1).reshape(m, LANES)

    x2d = x.reshape(m, d)
    for i in range(depth):
        j = i // 2
        g_mix = norm_mix[i].reshape(1, d)
        if i % 2 == 0:
            lambda_init = 0.8 - 0.6 * math.exp(-0.3 * i)
            w = w_in_even[j]
            wm = jnp.concatenate([w[:, :gdn_w], w[:, gdn_w + 2 * nh:]], axis=1).astype(BF16)
            wg = _pad_lanes(w[:, gdn_w:gdn_w + 2 * nh]).astype(BF16)
            alog_row = _pad_lanes(jnp.concatenate([jnp.zeros((nh,), F32), a_log[j]]).reshape(1, 2 * nh))
            dt_row = _pad_lanes(jnp.concatenate([jnp.zeros((nh,), F32), dt_bias[j]]).reshape(1, 2 * nh))
            qkv, z, qkb, vbt, gates = _even_in(x2d, g_mix, wm, wg, conv_w[j], alog_row, dt_row, cos_t, sin_t, t)
            qkv, z, qkb, gates = (a.reshape(b, t, -1) for a in (qkv, z, qkb, gates))
            u, wy, qd, kd, qk = _gdn_prep(qkv, gates)
            o_a = _gdn_scan(u, wy, qd, kd, qk, gates, z, gdn_norm[j].reshape(1, HEAD_LANES))
            lam_params = jnp.stack([lam_q1[j], lam_k1[j], lam_q2[j], lam_k2[j]])
            o_b = _diff_attention(qkb, vbt, lam_params, diff_norm[j].reshape(1, HEAD_LANES), lambda_init)
            mixes = [o_a.reshape(m, -1), o_b.reshape(m, -1)]
            wout = w_out_even[j].astype(BF16)
        else:
            w = w_in_odd[j]
            d_mix = (w.shape[1] - FOX_HEADS) // 4
            wm = w[:, :4 * d_mix].astype(BF16)
            wf = _pad_lanes(w[:, 4 * d_mix:]).astype(BF16)
            bf_row = _pad_lanes(b_forget[j].reshape(1, FOX_HEADS))
            qt, k, vt, gate, qbt, kb = _odd_in(x2d, g_mix, wm, wf, bf_row, t)
            k, gate, kb = (a.reshape(b, t, -1) for a in (k, gate, kb))
            o = _fox_attention(qt, k, vt, gate, qbt, kb)
            mixes = [o.reshape(m, -1)]
            wout = w_out_odd[j].astype(BF16)
        x2d = _post(x2d, mixes, wout, norm_mlp[i].reshape(1, d), w_mlp_up[i].astype(BF16),
                    w_mlp_down[i].astype(BF16), p[i].reshape(m, -1), w_ple_proj[i].astype(BF16),
                    w_ple_gate[i].astype(BF16), norm_final.reshape(1, d) if i == depth - 1 else None)
    return x2d.reshape(b, t, d)
```
